```python
import math
import jax, jax.numpy as jnp
from jax import lax
import numpy as np

D_MODEL = 1024
BATCH = 4
SEQ = 4096
DEPTH = 1
DEC_BATCH = 128
DEC_SEQ = 8
PAST_LEN = 2048
PAGE_SIZE = 128

HEAD_DIM = 64
N_ATTN_HEADS = 8
N_KV_HEADS = 2
GROUP = N_ATTN_HEADS // N_KV_HEADS
N_RNN_HEADS = 8
RNN_DK = 64
RNN_DV = 64
ATTN_WIDTH = N_ATTN_HEADS * HEAD_DIM
RNN_WIDTH = N_RNN_HEADS * RNN_DV
MIX_WIDTH = ATTN_WIDTH + RNN_WIDTH
KV_WIDTH = N_KV_HEADS * HEAD_DIM
N_BRANCH = 3
CMP_BLOCK = 64
SEL_BLOCK = 64
SEL_TOPK = 16
WINDOW = 512
CMP_HIDDEN = 128
D_FF = 4 * D_MODEL
NSA_Q_BLOCK = 64
WIN_Q_BLOCK = 128
HGRN_CHUNK = 64
IN_WIDTH = ATTN_WIDTH + 6 * KV_WIDTH + N_ATTN_HEADS * N_BRANCH + 2 * N_RNN_HEADS * RNN_DK + 2 * RNN_WIDTH
SCALE = HEAD_DIM ** -0.5
EPS = 1e-6
NEG = -1e30
F32 = jnp.float32

kernel_name = 'nsa_hgrn2_hybrid_step'


def rms_norm(x, gain):
    xf = x.astype(F32)
    y = xf * lax.rsqrt(jnp.mean(xf * xf, axis=-1, keepdims=True) + EPS)
    return (y * gain.astype(F32)).astype(x.dtype)


def alibi_slopes():
    h = jnp.arange(1, N_ATTN_HEADS + 1, dtype=F32)
    return jnp.exp2(-8.0 * h / N_ATTN_HEADS).reshape(N_KV_HEADS, GROUP)


def split_points():
    sizes = (ATTN_WIDTH, 2 * KV_WIDTH, 2 * KV_WIDTH, 2 * KV_WIDTH, N_ATTN_HEADS * N_BRANCH,
             N_RNN_HEADS * RNN_DK, N_RNN_HEADS * RNN_DK, RNN_WIDTH, RNN_WIDTH)
    return [int(v) for v in np.cumsum(sizes)[:-1]]


def mix_projections(x, ln, w_in, q_gain, k_gain, lb):
    b, t, _ = x.shape
    z = rms_norm(x, ln) @ w_in
    q, kv_c, kv_s, kv_w, gates, rq, rf, ri, rg = jnp.split(z, split_points(), axis=-1)
    q = rms_norm(q.reshape(b, t, N_ATTN_HEADS, HEAD_DIM), q_gain)
    kv_c = kv_c.reshape(b, t, 2, N_KV_HEADS, HEAD_DIM)
    kv_s = kv_s.reshape(b, t, 2, N_KV_HEADS, HEAD_DIM)
    kv_w = kv_w.reshape(b, t, 2, N_KV_HEADS, HEAD_DIM)
    k_cmp, v_cmp = kv_c[:, :, 0], kv_c[:, :, 1]
    k_sel, v_sel = rms_norm(kv_s[:, :, 0], k_gain[1]), kv_s[:, :, 1]
    k_win, v_win = rms_norm(kv_w[:, :, 0], k_gain[2]), kv_w[:, :, 1]
    gates = jax.nn.sigmoid(gates.astype(F32)).reshape(b, t, N_ATTN_HEADS, N_BRANCH)
    rq = jax.nn.silu(rq.astype(F32)).reshape(b, t, N_RNN_HEADS, RNN_DK)
    f = (lb + (1.0 - lb) * jax.nn.sigmoid(rf.astype(F32))).reshape(b, t, N_RNN_HEADS, RNN_DK)
    rk = 1.0 - f
    g_log = jnp.log(f)
    rv = ri.astype(F32).reshape(b, t, N_RNN_HEADS, RNN_DV)
    return q, k_cmp, v_cmp, k_sel, v_sel, k_win, v_win, gates, rq, rk, rv, g_log, rg


def compress(rows, pe, w1, w2):
    b, l = rows.shape[:2]
    nc = l // CMP_BLOCK
    blk = rows[:, :nc * CMP_BLOCK].reshape(b, nc, CMP_BLOCK, N_KV_HEADS, HEAD_DIM) + pe[None, None, :, None, :]
    blk = blk.transpose(0, 1, 3, 2, 4).reshape(b, nc, N_KV_HEADS, CMP_BLOCK * HEAD_DIM)
    return jax.nn.silu(blk @ w1) @ w2


def nsa_memory(k_cmp, v_cmp, k_sel, v_sel, k_gain_cmp, pe, w1, w2):
    b, l = k_cmp.shape[:2]
    kc = rms_norm(compress(k_cmp, pe[0], w1[0], w2[0]), k_gain_cmp)
    vc = compress(v_cmp, pe[1], w1[1], w2[1])
    ns = -(-l // SEL_BLOCK)
    padlen = ns * SEL_BLOCK - l

    def blocks(x):
        return jnp.pad(x, ((0, 0), (0, padlen), (0, 0), (0, 0))).reshape(b, ns, SEL_BLOCK, N_KV_HEADS, HEAD_DIM)

    return kc, vc, blocks(k_sel), blocks(v_sel)


def cmp_sel_attend(q, q_pos, kc, vc, ksb, vsb, slopes):
    b, tq = q.shape[:2]
    nc, ns = kc.shape[1], ksb.shape[1]
    qg = q.reshape(b, tq, N_KV_HEADS, GROUP, HEAD_DIM).astype(F32)
    end = jnp.arange(nc, dtype=jnp.int32) * CMP_BLOCK + (CMP_BLOCK - 1)
    dist = (q_pos[:, None] - end[None, :]).astype(F32)
    valid = dist >= 0
    s = jnp.einsum('btgrd,bngd->bgrtn', qg, kc.astype(F32)) * SCALE - slopes[None, :, :, None, None] * dist
    s = jnp.where(valid, s, NEG)
    p = jax.nn.softmax(s, axis=-1) * valid
    o_cmp = jnp.einsum('bgrtn,bngd->btgrd', p, vc.astype(F32))
    imp = jnp.pad(p.sum(axis=2), ((0, 0), (0, 0), (0, 0), (0, ns - nc)))
    blk = jnp.arange(ns, dtype=jnp.int32)
    cur = q_pos // SEL_BLOCK
    force = (blk[None, :] == 0) | (blk[None, :] == cur[:, None])
    allowed = blk[None, :] <= cur[:, None]
    score = jnp.where(force, GROUP + 1.0, imp)
    score = jnp.where(allowed, score, -1.0)
    top_s, idx = lax.top_k(score, min(SEL_TOPK, ns))
    blk_ok = top_s >= 0
    gather = jax.vmap(jax.vmap(lambda bl, ix: bl[ix]))
    ks = gather(ksb.transpose(0, 3, 1, 2, 4), idx)
    vs = gather(vsb.transpose(0, 3, 1, 2, 4), idx)
    tok = idx[..., None] * SEL_BLOCK + jnp.arange(SEL_BLOCK, dtype=jnp.int32)
    dist_s = (q_pos[None, None, :, None, None] - tok).astype(F32)
    ok = blk_ok[..., None] & (dist_s >= 0)
    ss = jnp.einsum('btgrd,bgtksd->bgrtks', qg, ks.astype(F32)) * SCALE - slopes[None, :, :, None, None, None] * dist_s[:, :, None]
    ss = jnp.where(ok[:, :, None], ss, NEG)
    ps = jax.nn.softmax(ss.reshape(ss.shape[:4] + (-1,)), axis=-1).reshape(ss.shape)
    o_sel = jnp.einsum('bgrtks,bgtksd->btgrd', ps, vs.astype(F32))
    return (o_cmp.reshape(b, tq, N_ATTN_HEADS, HEAD_DIM), o_sel.reshape(b, tq, N_ATTN_HEADS, HEAD_DIM))


def nsa_prompt_cmp_sel(q, mem, slopes):
    b, t = q.shape[:2]
    nb = t // NSA_Q_BLOCK
    qb = q.reshape(b, nb, NSA_Q_BLOCK, N_ATTN_HEADS, HEAD_DIM).transpose(1, 0, 2, 3, 4)
    starts = jnp.arange(nb, dtype=jnp.int32) * NSA_Q_BLOCK
    kc, vc, ksb, vsb = mem

    def one(args):
        qblk, st = args
        return cmp_sel_attend(qblk, st + jnp.arange(NSA_Q_BLOCK, dtype=jnp.int32), kc, vc, ksb, vsb, slopes)

    o_c, o_s = lax.map(one, (qb, starts))
    back = lambda o: o.transpose(1, 0, 2, 3, 4).reshape(b, t, N_ATTN_HEADS, HEAD_DIM)
    return back(o_c), back(o_s)


def window_banded(q, k, v, slopes):
    b, t = q.shape[:2]
    qbl = WIN_Q_BLOCK
    nb, nprev = t // qbl, WINDOW // qbl
    pad = nprev * qbl

    def band(x):
        xp = jnp.pad(x, ((0, 0), (pad, 0), (0, 0), (0, 0))).reshape(b, nb + nprev, qbl, N_KV_HEADS, HEAD_DIM)
        return jnp.concatenate([xp[:, j:j + nb] for j in range(nprev + 1)], axis=2)

    kb, vb = band(k), band(v)
    qg = q.reshape(b, nb, qbl, N_KV_HEADS, GROUP, HEAD_DIM).astype(F32)
    start = jnp.arange(nb, dtype=jnp.int32)[:, None] * qbl
    q_pos = start + jnp.arange(qbl, dtype=jnp.int32)
    k_pos = start - pad + jnp.arange((nprev + 1) * qbl, dtype=jnp.int32)
    dist = (q_pos[:, :, None] - k_pos[:, None, :]).astype(F32)
    ok = (dist >= 0) & (dist < WINDOW) & (k_pos[:, None, :] >= 0)
    s = jnp.einsum('bnqgrd,bnkgd->bngrqk', qg, kb.astype(F32)) * SCALE - slopes[None, None, :, :, None, None] * dist[None, :, None, None]
    s = jnp.where(ok[None, :, None, None], s, NEG)
    p = jax.nn.softmax(s, axis=-1)
    o = jnp.einsum('bngrqk,bnkgd->bnqgrd', p, vb.astype(F32))
    return o.reshape(b, t, N_ATTN_HEADS, HEAD_DIM)


def window_dense(q, q_pos, k, v, k_pos, slopes):
    b, tq = q.shape[:2]
    qg = q.reshape(b, tq, N_KV_HEADS, GROUP, HEAD_DIM).astype(F32)
    dist = (q_pos[:, None] - k_pos[None, :]).astype(F32)
    ok = (dist >= 0) & (dist < WINDOW)
    s = jnp.einsum('btgrd,bkgd->bgrtk', qg, k.astype(F32)) * SCALE - slopes[None, :, :, None, None] * dist
    s = jnp.where(ok, s, NEG)
    p = jax.nn.softmax(s, axis=-1)
    o = jnp.einsum('bgrtk,bkgd->btgrd', p, v.astype(F32))
    return o.reshape(b, tq, N_ATTN_HEADS, HEAD_DIM)


def hgrn2_chunked(q, k, v, g_log, s0):
    b, t, nh = q.shape[:3]
    c = math.gcd(t, HGRN_CHUNK)
    n = t // c

    def chunks(x):
        return x.astype(F32).reshape(b, n, c, nh, x.shape[-1]).transpose(1, 0, 3, 2, 4)

    tri = jnp.tril(jnp.ones((c, c), dtype=bool))

    def step(S, inp):
        qc, kc, vc, gc = inp
        cum = jnp.cumsum(gc, axis=2)
        o_inter = jnp.einsum('bhtd,bhde->bhte', qc * jnp.exp(cum), S)
        decay = jnp.exp(jnp.where(tri[:, :, None], cum[:, :, :, None, :] - cum[:, :, None, :, :], -jnp.inf))
        a = jnp.einsum('bhtd,bhsd,bhtsd->bhts', qc, kc, decay)
        o = o_inter + jnp.einsum('bhts,bhse->bhte', a, vc)
        last = cum[:, :, -1:]
        S = jnp.exp(last[:, :, 0])[..., None] * S + jnp.einsum('bhsd,bhse->bhde', kc * jnp.exp(last - cum), vc)
        return S, o

    S, o = lax.scan(step, s0.astype(F32), (chunks(q), chunks(k), chunks(v), chunks(g_log)))
    return o.transpose(1, 0, 3, 2, 4).reshape(b, t, nh, v.shape[-1]), S


def finish(x, o_cmp, o_sel, o_win, gates, o_rnn, rg, attn_gain, rnn_gain, w_out, ln_mlp, w_up, w_down):
    b, t, _ = x.shape
    o_a = gates[..., 0:1] * o_cmp + gates[..., 1:2] * o_sel + gates[..., 2:3] * o_win
    o_a = rms_norm(o_a, attn_gain.reshape(N_ATTN_HEADS, HEAD_DIM)).reshape(b, t, ATTN_WIDTH)
    o_r = rms_norm(o_rnn, rnn_gain.reshape(N_RNN_HEADS, RNN_DV)).reshape(b, t, RNN_WIDTH) * jax.nn.silu(rg.astype(F32))
    h = x + jnp.concatenate([o_a, o_r], axis=-1).astype(x.dtype) @ w_out
    u = jax.nn.relu(rms_norm(h, ln_mlp) @ w_up)
    return h + (u * u) @ w_down


def setup_inputs(seed: int = 0) -> dict:
    key = jax.random.key(seed)
    k = jax.random.split(key, 20)
    n_pages = PAST_LEN // PAGE_SIZE
    n_used = DEC_BATCH * n_pages
    n_pool = n_used + (n_used + 3) // 4
    w_buf = min(WINDOW, PAST_LEN)

    def normal(kk, shape, scale):
        return jax.random.normal(kk, shape, F32) * scale

    def gain(kk, shape):
        return 1.0 + normal(kk, shape, 0.05)

    page_table = jax.random.permutation(k[5], n_pool)[:n_used].reshape(DEC_BATCH, n_pages).astype(jnp.int32)
    return {
        'x_prompt': normal(k[0], (BATCH, SEQ, D_MODEL), 1.0),
        'x_sample': normal(k[1], (DEC_BATCH, DEC_SEQ, D_MODEL), 1.0),
        'cache_kv': normal(k[2], (DEPTH, n_pool, PAGE_SIZE, 4, N_KV_HEADS, HEAD_DIM), 1.0),
        'cache_win': normal(k[3], (DEPTH, DEC_BATCH, w_buf, 2, N_KV_HEADS, HEAD_DIM), 1.0),
        'state_rnn': normal(k[4], (DEPTH, DEC_BATCH, N_RNN_HEADS, RNN_DK, RNN_DV), 0.3),
        'page_table': page_table,
        'ln_mix': gain(k[6], (DEPTH, D_MODEL)),
        'w_in': normal(k[7], (DEPTH, D_MODEL, IN_WIDTH), D_MODEL ** -0.5),
        'q_norm': gain(k[8], (DEPTH, HEAD_DIM)),
        'k_norm': gain(k[9], (DEPTH, N_BRANCH, HEAD_DIM)),
        'cmp_pe': normal(k[10], (DEPTH, 2, CMP_BLOCK, HEAD_DIM), 0.1),
        'cmp_w1': normal(k[11], (DEPTH, 2, CMP_BLOCK * HEAD_DIM, CMP_HIDDEN), (CMP_BLOCK * HEAD_DIM) ** -0.5),
        'cmp_w2': normal(k[12], (DEPTH, 2, CMP_HIDDEN, HEAD_DIM), CMP_HIDDEN ** -0.5),
        'attn_out_norm': gain(k[13], (DEPTH, ATTN_WIDTH)),
        'rnn_lb_logits': normal(k[14], (DEPTH + 1, N_RNN_HEADS * RNN_DK), 0.5),
        'rnn_out_norm': gain(k[15], (DEPTH, RNN_WIDTH)),
        'w_out': normal(k[16], (DEPTH, MIX_WIDTH, D_MODEL), MIX_WIDTH ** -0.5),
        'ln_mlp': gain(k[17], (DEPTH, D_MODEL)),
        'w_up': normal(k[18], (DEPTH, D_MODEL, D_FF), D_MODEL ** -0.5),
        'w_down': normal(k[19], (DEPTH, D_FF, D_MODEL), D_FF ** -0.5),
    }


def reference(x_prompt, x_sample, cache_kv, cache_win, state_rnn, page_table, ln_mix, w_in, q_norm, k_norm,
              cmp_pe, cmp_w1, cmp_w2, attn_out_norm, rnn_lb_logits, rnn_out_norm, w_out, ln_mlp, w_up, w_down):
    slopes = alibi_slopes()
    lb_all = jnp.cumsum(jax.nn.softmax(rnn_lb_logits.astype(F32), axis=0), axis=0)
    dec_b, dec_t = x_sample.shape[:2]
    past_len = page_table.shape[1] * PAGE_SIZE
    w_buf = cache_win.shape[2]
    xp, xs = x_prompt, x_sample
    kv_p, kv_s, win_p, win_s, rnn_p, rnn_s = [], [], [], [], [], []
    for l in range(DEPTH):
        lb = lb_all[l]
        b, t = xp.shape[:2]
        q, kc, vc, ksl, vsl, kw, vw, gates, rq, rk, rv, glog, rg = mix_projections(xp, ln_mix[l], w_in[l], q_norm[l], k_norm[l], lb)
        mem = nsa_memory(kc, vc, ksl, vsl, k_norm[l, 0], cmp_pe[l], cmp_w1[l], cmp_w2[l])
        o_c, o_s = nsa_prompt_cmp_sel(q, mem, slopes)
        o_w = window_banded(q, kw, vw, slopes)
        o_r, s_p = hgrn2_chunked(rq, rk, rv, glog, jnp.zeros((b, N_RNN_HEADS, RNN_DK, RNN_DV), F32))
        kv_p.append(jnp.stack([kc, vc, ksl, vsl], axis=2))
        win_p.append(jnp.stack([kw, vw], axis=2)[:, t - min(WINDOW, t):])
        rnn_p.append(s_p.astype(state_rnn.dtype))
        xp = finish(xp, o_c, o_s, o_w, gates, o_r, rg, attn_out_norm[l], rnn_out_norm[l], w_out[l], ln_mlp[l], w_up[l], w_down[l])
        q, kc, vc, ksl, vsl, kw, vw, gates, rq, rk, rv, glog, rg = mix_projections(xs, ln_mix[l], w_in[l], q_norm[l], k_norm[l], lb)
        new_kv = jnp.stack([kc, vc, ksl, vsl], axis=2)
        past = cache_kv[l][page_table].reshape(dec_b, past_len, 4, N_KV_HEADS, HEAD_DIM)
        all_kv = jnp.concatenate([past, new_kv.astype(past.dtype)], axis=1)
        mem = nsa_memory(all_kv[:, :, 0], all_kv[:, :, 1], all_kv[:, :, 2], all_kv[:, :, 3],
                         k_norm[l, 0], cmp_pe[l], cmp_w1[l], cmp_w2[l])
        q_pos = past_len + jnp.arange(dec_t, dtype=jnp.int32)
        o_c, o_s = cmp_sel_attend(q, q_pos, mem[0], mem[1], mem[2], mem[3], slopes)
        win_all = jnp.concatenate([cache_win[l], jnp.stack([kw, vw], axis=2).astype(cache_win.dtype)], axis=1)
        k_pos = past_len - w_buf + jnp.arange(w_buf + dec_t, dtype=jnp.int32)
        o_w = window_dense(q, q_pos, win_all[:, :, 0], win_all[:, :, 1], k_pos, slopes)
        o_r, s_s = hgrn2_chunked(rq, rk, rv, glog, state_rnn[l])
        kv_s.append(new_kv)
        win_s.append(win_all[:, dec_t:])
        rnn_s.append(s_s.astype(state_rnn.dtype))
        xs = finish(xs, o_c, o_s, o_w, gates, o_r, rg, attn_out_norm[l], rnn_out_norm[l], w_out[l], ln_mlp[l], w_up[l], w_down[l])
    return (xp, xs, jnp.stack(kv_p), jnp.stack(kv_s), jnp.stack(win_p), jnp.stack(win_s), jnp.stack(rnn_p), jnp.stack(rnn_s))
```

```python
import functools

import numpy as np
import jax
import jax.numpy as jnp
from jax import lax
from jax.experimental import pallas as pl
from jax.experimental.pallas import tpu as pltpu

F32 = jnp.float32
BF16 = jnp.bfloat16

HEAD_DIM = 64
N_ATTN_HEADS = 8
N_KV_HEADS = 2
GROUP = N_ATTN_HEADS // N_KV_HEADS
N_RNN_HEADS = 8
RNN_DK = 64
RNN_DV = 64
ATTN_WIDTH = N_ATTN_HEADS * HEAD_DIM
RNN_WIDTH = N_RNN_HEADS * RNN_DV
KV_WIDTH = N_KV_HEADS * HEAD_DIM
N_BRANCH = 3
CMP_BLOCK = 64
SEL_BLOCK = 64
SEL_TOPK = 16
WINDOW = 512
CMP_HIDDEN = 128
PAGE_SIZE = 128
SCALE = HEAD_DIM ** -0.5
EPS = 1e-6
NEG = -1e30
FORCED_SCORE = GROUP + 1.0
SLOPES = [[2.0 ** (-(g * GROUP + r + 1)) for r in range(GROUP)] for g in range(N_KV_HEADS)]

LANES = 128
VMEM_BYTES_V7X = 64 * 1024 * 1024

PROJ_ROWS = 256
NSA_Q_ROWS = 128
SEL_KEYS = 512
RNN_ROWS = 256
RNN_CHUNK = 16
FF_CHUNK = 1024
FINISH_ROWS = 512

NT = (((1,), (1,)), ((), ()))
TN = (((0,), (0,)), ((), ()))


def _vmem_limit(nbytes):
    return int(min(VMEM_BYTES_V7X - (8 << 20), max(nbytes, 16 << 20)))


def _dot(a, b):
    return jnp.dot(a, b, preferred_element_type=F32)


def _dot_nt(a, b):
    return lax.dot_general(a, b, NT, preferred_element_type=F32)


def _dot_tn(a, b):
    return lax.dot_general(a, b, TN, preferred_element_type=F32)


def _low_half(shape):
    lane = lax.broadcasted_iota(jnp.int32, shape, len(shape) - 1)
    return (lane & HEAD_DIM) == 0


def _head_mean_sq(x):
    outs = []
    for j in range(x.shape[-1] // LANES):
        blk = x[:, j * LANES:(j + 1) * LANES]
        sq = blk * blk
        low = _low_half(blk.shape)
        s_lo = jnp.sum(jnp.where(low, sq, 0.0), axis=-1, keepdims=True)
        s_hi = jnp.sum(jnp.where(low, 0.0, sq), axis=-1, keepdims=True)
        outs.append(jnp.where(low, s_lo, s_hi))
    return jnp.concatenate(outs, axis=-1) * (1.0 / HEAD_DIM)


def _row_rms(x):
    return x * lax.rsqrt(jnp.mean(x * x, axis=-1, keepdims=True) + EPS)


def _col_head_norm(rows, gain):
    ms = jnp.mean(rows * rows, axis=0, keepdims=True)
    return rows * lax.rsqrt(ms + EPS) * gain


def _silu(z):
    return z * jax.nn.sigmoid(z)


def _lower_bound(logits, axis):
    m = jnp.max(logits, axis=axis, keepdims=True)
    e = jnp.exp(logits - m)
    lb = e / jnp.sum(e, axis=axis, keepdims=True)
    return lb[0:1] if axis == 0 else lb


def _stack_group_queries(q, g):
    rows = q.shape[0]
    low = _low_half((rows, LANES))
    zero = jnp.zeros((rows, LANES), q.dtype)
    pa = q[:, g * 2 * LANES:g * 2 * LANES + LANES]
    pb = q[:, g * 2 * LANES + LANES:(g + 1) * 2 * LANES]
    return jnp.concatenate([jnp.where(low, pa, zero), jnp.where(low, zero, pa),
                            jnp.where(low, pb, zero), jnp.where(low, zero, pb)], axis=0)


def _unstack_group(o2, rows):
    low = _low_half((rows, LANES))
    return jnp.concatenate([jnp.where(low, o2[0:rows], o2[rows:2 * rows]),
                            jnp.where(low, o2[2 * rows:3 * rows], o2[3 * rows:4 * rows])], axis=1)


def _log2(n):
    assert n > 0 and n & (n - 1) == 0, n
    return n.bit_length() - 1


def _twice(x):
    return jnp.concatenate([x, x], axis=0)


def _row_slopes(g, row_head):
    s = jnp.full(row_head.shape, SLOPES[g][GROUP - 1], F32)
    for r in range(GROUP - 2, -1, -1):
        s = jnp.where(row_head == r, SLOPES[g][r], s)
    return s


def _block_ranks(score, ids, cand):
    rank = jnp.zeros(score.shape, F32)
    for row, id_i in cand:
        s_i = score[row:row + 1, :]
        rank = rank + jnp.where(ids > id_i, (s_i >= score).astype(F32), (s_i > score).astype(F32))
    return rank


def _expand_gates(gates, gexp_ref):
    hi = gates.astype(BF16)
    lo = (gates - hi.astype(F32)).astype(BF16)
    return [_dot(hi, gexp_ref[br]) + _dot(lo, gexp_ref[br]) for br in range(N_BRANCH)]


TOK_Q, TOK_RQ, TOK_RF, TOK_RI, TOK_RG, TOK_GATE, TOK_KVC, TOK_END = 0, 512, 1024, 1536, 2048, 2560, 2688, 2944


def _proj_prompt_body(x_ref, ln_ref, wtok_ref, wft_ref, qg_ref, lbl_ref, pe_ref, gsel_ref, gwin_ref,
                      q_ref, gates_ref, rq_ref, f_ref, rv_ref, rgs_ref, kvc_ref, kvt_ref, wint_ref, att_ref):
    xb = (_row_rms(x_ref[0]) * ln_ref[...]).astype(BF16)

    def tok(lo, hi):
        return _dot(xb, wtok_ref[:, lo:hi])

    zq = tok(TOK_Q, TOK_RQ)
    q_ref[0] = (zq * lax.rsqrt(_head_mean_sq(zq) + EPS) * qg_ref[...]).astype(BF16)
    rq_ref[0] = _silu(tok(TOK_RQ, TOK_RF))
    lb = _lower_bound(lbl_ref[...], 0)
    f_ref[0] = lb + (1.0 - lb) * jax.nn.sigmoid(tok(TOK_RF, TOK_RI))
    rv_ref[0] = tok(TOK_RI, TOK_RG)
    rgs_ref[0] = _silu(tok(TOK_RG, TOK_GATE))
    gates_ref[0] = jax.nn.sigmoid(tok(TOK_GATE, TOK_KVC))
    zc = tok(TOK_KVC, TOK_END) + pe_ref[...]
    kvc_ref[0, 0] = zc[:, 0:LANES]
    kvc_ref[1, 0] = zc[:, LANES:2 * LANES]

    zf = _dot_nt(wft_ref[...], xb)
    d = HEAD_DIM
    ksel = jnp.concatenate([_col_head_norm(zf[256 + g * d:256 + (g + 1) * d], gsel_ref[...])
                            for g in range(N_KV_HEADS)], axis=0)
    kwin = jnp.concatenate([_col_head_norm(zf[512 + g * d:512 + (g + 1) * d], gwin_ref[...])
                            for g in range(N_KV_HEADS)], axis=0)
    vsel = zf[384:512]
    vwin = zf[640:768]
    kvt_ref[0, 0:256] = zf[0:256]
    kvt_ref[0, 256:384] = ksel
    kvt_ref[0, 384:512] = vsel
    wint_ref[0, 0:128] = kwin
    wint_ref[0, 128:256] = vwin
    att_ref[0, 0:128] = ksel.astype(BF16)
    att_ref[0, 128:256] = vsel.astype(BF16)
    att_ref[0, 256:384] = kwin.astype(BF16)
    att_ref[0, 384:512] = vwin.astype(BF16)


def _proj_prompt(x, ln, wtok, wft, qg, lbl, pe_tok, gsel, gwin):
    b, t, dm = x.shape
    tm = PROJ_ROWS
    grid = (b, t // tm)
    row = lambda i, j: (i, j, 0)
    col = lambda i, j: (i, 0, j)
    const2 = lambda i, j: (0, 0)
    out_shape = (
        jax.ShapeDtypeStruct((b, t, ATTN_WIDTH), BF16),
        jax.ShapeDtypeStruct((b, t, LANES), F32),
        jax.ShapeDtypeStruct((b, t, RNN_WIDTH), F32),
        jax.ShapeDtypeStruct((b, t, RNN_WIDTH), F32),
        jax.ShapeDtypeStruct((b, t, RNN_WIDTH), F32),
        jax.ShapeDtypeStruct((b, t, RNN_WIDTH), F32),
        jax.ShapeDtypeStruct((2, b, t, LANES), F32),
        jax.ShapeDtypeStruct((b, 4 * KV_WIDTH, t), F32),
        jax.ShapeDtypeStruct((b, 2 * KV_WIDTH, t), F32),
        jax.ShapeDtypeStruct((b, 4 * KV_WIDTH, t), BF16),
    )
    out_specs = (
        pl.BlockSpec((1, tm, ATTN_WIDTH), row), pl.BlockSpec((1, tm, LANES), row),
        pl.BlockSpec((1, tm, RNN_WIDTH), row), pl.BlockSpec((1, tm, RNN_WIDTH), row),
        pl.BlockSpec((1, tm, RNN_WIDTH), row), pl.BlockSpec((1, tm, RNN_WIDTH), row),
        pl.BlockSpec((2, 1, tm, LANES), lambda i, j: (0, i, j, 0)),
        pl.BlockSpec((1, 4 * KV_WIDTH, tm), col), pl.BlockSpec((1, 2 * KV_WIDTH, tm), col),
        pl.BlockSpec((1, 4 * KV_WIDTH, tm), col),
    )
    in_specs = [
        pl.BlockSpec((1, tm, dm), row), pl.BlockSpec(ln.shape, const2),
        pl.BlockSpec(wtok.shape, const2), pl.BlockSpec(wft.shape, const2),
        pl.BlockSpec(qg.shape, const2), pl.BlockSpec(lbl.shape, const2), pl.BlockSpec(pe_tok.shape, const2),
        pl.BlockSpec(gsel.shape, const2), pl.BlockSpec(gwin.shape, const2),
    ]
    return pl.pallas_call(
        _proj_prompt_body, grid=grid, in_specs=in_specs, out_specs=out_specs, out_shape=out_shape,
        compiler_params=pltpu.CompilerParams(dimension_semantics=("parallel", "parallel"),
                                             vmem_limit_bytes=_vmem_limit(48 << 20)),
        name="proj_prompt",
    )(x, ln, wtok, wft, qg, lbl, pe_tok, gsel, gwin)


def _compress_prompt_body(x_ref, w1_ref, w2_ref, kg_ref, out_ref):
    c = pl.program_id(0)
    nb = out_ref.shape[2]
    acc = jnp.zeros((nb, 2 * CMP_HIDDEN), F32)
    for pos in range(CMP_BLOCK):
        xp = x_ref[0, 0, pl.ds(pos, nb, stride=CMP_BLOCK), :]
        acc = acc + _dot(xp.astype(BF16), w1_ref[0, pos])
    hb = _silu(acc).astype(BF16)
    outs = []
    for g in range(N_KV_HEADS):
        y = _dot(hb[:, g * CMP_HIDDEN:(g + 1) * CMP_HIDDEN], w2_ref[0])
        yn = _row_rms(y) * kg_ref[...]
        outs.append(jnp.where(c == 0, yn, y))
    out_ref[0, 0] = jnp.concatenate(outs, axis=1)


def _compress_prompt(kvc, w1bd, w2dup, kg_dup):
    _, b, t, _ = kvc.shape
    nb = t // CMP_BLOCK
    return pl.pallas_call(
        _compress_prompt_body, grid=(2, b),
        in_specs=[pl.BlockSpec((1, 1, t, LANES), lambda c, i: (c, i, 0, 0)),
                  pl.BlockSpec((1,) + w1bd.shape[1:], lambda c, i: (c, 0, 0, 0)),
                  pl.BlockSpec((1,) + w2dup.shape[1:], lambda c, i: (c, 0, 0)),
                  pl.BlockSpec(kg_dup.shape, lambda c, i: (0, 0))],
        out_specs=pl.BlockSpec((1, 1, nb, 2 * LANES), lambda c, i: (c, i, 0, 0)),
        out_shape=jax.ShapeDtypeStruct((2, b, nb, 2 * LANES), F32),
        compiler_params=pltpu.CompilerParams(dimension_semantics=("arbitrary", "arbitrary"),
                                             vmem_limit_bytes=_vmem_limit(32 << 20)),
        name="compress_prompt",
    )(kvc, w1bd, w2dup, kg_dup)


def _nsa_prompt_body(q_ref, gates_ref, cmp_ref, att_ref, e_ref, gexp_ref, oa_ref, *, seq, topk):
    tq = NSA_Q_ROWS
    tk = min(SEL_KEYS, seq)
    nb = cmp_ref.shape[2]
    t0 = pl.program_id(1) * tq
    q = q_ref[0]
    d = HEAD_DIM
    o_cmp, o_sel, o_win = [], [], []
    for g in range(N_KV_HEADS):
        qst = _stack_group_queries(q, g)

        kc = cmp_ref[0, 0][:, g * LANES:(g + 1) * LANES].astype(BF16)
        vc = cmp_ref[1, 0][:, g * LANES:(g + 1) * LANES].astype(BF16)
        st = _dot_nt(kc, qst)
        blk = lax.broadcasted_iota(jnp.int32, (nb, GROUP * tq), 0)
        colq = lax.broadcasted_iota(jnp.int32, (1, GROUP * tq), 1)
        tpos = t0 + (colq & (tq - 1))
        dist = (tpos - (blk * CMP_BLOCK + (CMP_BLOCK - 1))).astype(F32)
        valid = dist >= 0.0
        s = jnp.where(valid, st - _row_slopes(g, colq >> _log2(tq)) * dist, NEG)
        e = jnp.exp(s - jnp.max(s, axis=0, keepdims=True))
        p = jnp.where(valid, e / jnp.sum(e, axis=0, keepdims=True), 0.0)
        o_cmp.append(_unstack_group(_dot_tn(p.astype(BF16), vc), tq))
        imp = p[:, 0:tq] + p[:, tq:2 * tq] + p[:, 2 * tq:3 * tq] + p[:, 3 * tq:4 * tq]

        bj = lax.broadcasted_iota(jnp.int32, (nb, tq), 0)
        cur = (t0 + lax.broadcasted_iota(jnp.int32, (nb, tq), 1)) >> _log2(SEL_BLOCK)
        force = (bj == 0) | (bj == cur)
        score = jnp.where(bj <= cur, jnp.where(force, FORCED_SCORE, imp), -1.0)
        rank = _block_ranks(score, bj, [(i, i) for i in range(nb)])
        msel = jnp.where((rank < topk) & (score >= 0.0), 1.0, 0.0).astype(BF16)

        n_kt = (t0 + tq - 1) // tk + 1

        def kv_step(kt, carry, g=g, qst=qst, msel=msel):
            m_i, l_i, acc = carry
            s0 = pl.multiple_of(kt * tk, tk)
            kt_ = att_ref[0, g * d:(g + 1) * d, pl.ds(s0, tk)]
            vt_ = att_ref[0, KV_WIDTH + g * d:KV_WIDTH + (g + 1) * d, pl.ds(s0, tk)]
            sc = _dot(qst, _twice(kt_))
            mk = _dot_tn(msel, e_ref[:, pl.ds(s0, tk)])
            dd = (t0 + lax.broadcasted_iota(jnp.int32, (tq, tk), 0)) - (s0 + lax.broadcasted_iota(jnp.int32, (tq, tk), 1))
            ok = (mk > 0.5) & (dd >= 0)
            df = dd.astype(F32)
            ps, ms_, ls_, al_ = [], [], [], []
            for r in range(GROUP):
                lg = jnp.where(ok, sc[r * tq:(r + 1) * tq] - SLOPES[g][r] * df, NEG)
                m_old = m_i[r * tq:(r + 1) * tq]
                m_new = jnp.maximum(m_old, jnp.max(lg, axis=-1, keepdims=True))
                pr = jnp.exp(lg - m_new)
                alpha = jnp.exp(m_old - m_new)
                ps.append(pr.astype(BF16))
                ms_.append(m_new)
                al_.append(alpha)
                ls_.append(alpha * l_i[r * tq:(r + 1) * tq] + jnp.sum(pr, axis=-1, keepdims=True))
            pv = _dot_nt(jnp.concatenate(ps, axis=0), _twice(vt_))
            return (jnp.concatenate(ms_, axis=0), jnp.concatenate(ls_, axis=0),
                    jnp.concatenate(al_, axis=0) * acc + pv)

        init = (jnp.full((GROUP * tq, 1), NEG, F32), jnp.zeros((GROUP * tq, 1), F32),
                jnp.zeros((GROUP * tq, LANES), F32))
        _, l_f, acc_f = lax.fori_loop(0, n_kt, kv_step, init)
        o_sel.append(_unstack_group(acc_f / l_f, tq))

        wk = min(WINDOW + tq, seq)
        ws = pl.multiple_of(jnp.clip(t0 - WINDOW, 0, seq - wk), LANES)
        kw = att_ref[0, 2 * KV_WIDTH + g * d:2 * KV_WIDTH + (g + 1) * d, pl.ds(ws, wk)]
        vw = att_ref[0, 3 * KV_WIDTH + g * d:3 * KV_WIDTH + (g + 1) * d, pl.ds(ws, wk)]
        sw = _dot(qst, _twice(kw))
        dd = (t0 + lax.broadcasted_iota(jnp.int32, (tq, wk), 0)) - (ws + lax.broadcasted_iota(jnp.int32, (tq, wk), 1))
        okw = (dd >= 0) & (dd < WINDOW)
        dfw = dd.astype(F32)
        ps, ls_ = [], []
        for r in range(GROUP):
            lg = jnp.where(okw, sw[r * tq:(r + 1) * tq] - SLOPES[g][r] * dfw, NEG)
            pr = jnp.exp(lg - jnp.max(lg, axis=-1, keepdims=True))
            ps.append(pr.astype(BF16))
            ls_.append(jnp.sum(pr, axis=-1, keepdims=True))
        pw = _dot_nt(jnp.concatenate(ps, axis=0), _twice(vw))
        o_win.append(_unstack_group(pw / jnp.concatenate(ls_, axis=0), tq))

    ge = _expand_gates(gates_ref[0], gexp_ref)
    oa_ref[0] = (ge[0] * jnp.concatenate(o_cmp, axis=1) + ge[1] * jnp.concatenate(o_sel, axis=1)
                 + ge[2] * jnp.concatenate(o_win, axis=1))


def _nsa_prompt(q, gates, cmpkv, att, e_blocks, gexp):
    b, t, _ = q.shape
    tq = NSA_Q_ROWS
    nb = t // CMP_BLOCK
    body = functools.partial(_nsa_prompt_body, seq=t, topk=min(SEL_TOPK, nb))
    return pl.pallas_call(
        body, grid=(b, t // tq),
        in_specs=[pl.BlockSpec((1, tq, ATTN_WIDTH), lambda i, j: (i, j, 0)),
                  pl.BlockSpec((1, tq, LANES), lambda i, j: (i, j, 0)),
                  pl.BlockSpec((2, 1, nb, 2 * LANES), lambda i, j: (0, i, 0, 0)),
                  pl.BlockSpec((1, 4 * KV_WIDTH, t), lambda i, j: (i, 0, 0)),
                  pl.BlockSpec(e_blocks.shape, lambda i, j: (0, 0)),
                  pl.BlockSpec(gexp.shape, lambda i, j: (0, 0, 0))],
        out_specs=pl.BlockSpec((1, tq, ATTN_WIDTH), lambda i, j: (i, j, 0)),
        out_shape=jax.ShapeDtypeStruct((b, t, ATTN_WIDTH), F32),
        compiler_params=pltpu.CompilerParams(dimension_semantics=("parallel", "parallel"),
                                             vmem_limit_bytes=_vmem_limit(48 << 20)),
        name="nsa_prompt",
    )(q, gates, cmpkv, att, e_blocks, gexp)


def _split3(x):
    hi = x.astype(BF16)
    r1 = x - hi.astype(F32)
    mid = r1.astype(BF16)
    lo = (r1 - mid.astype(F32)).astype(BF16)
    return hi, mid, lo


def _hgrn_prompt_body(rq_ref, f_ref, rv_ref, ltri_ref, bd_ref, o_ref, st_ref, s_scr, cum_scr, k_scr):
    c16 = RNN_CHUNK
    half = RNN_WIDTH // 2

    @pl.when(pl.program_id(1) == 0)
    def _():
        s_scr[...] = jnp.zeros(s_scr.shape, F32)

    f = f_ref[0]
    k_scr[...] = 1.0 - f
    hi, mid, lo = _split3(jnp.log(f))
    ltri = ltri_ref[...]
    cum_scr[...] = _dot(ltri, hi) + _dot(ltri, mid) + _dot(ltri, lo)

    ti = lax.broadcasted_iota(jnp.int32, (c16, c16, RNN_WIDTH), 0)
    si = lax.broadcasted_iota(jnp.int32, (c16, c16, RNN_WIDTH), 1)
    causal = si <= ti
    low = _low_half((c16 * c16, LANES))

    def chunk(c, carry):
        r0 = pl.multiple_of(c * c16, c16)
        cc = cum_scr[pl.ds(r0, c16), :]
        qc = rq_ref[0, pl.ds(r0, c16), :]
        kc = k_scr[pl.ds(r0, c16), :]
        vc = rv_ref[0, pl.ds(r0, c16), :]
        last = cc[c16 - 1:c16, :]
        qd = (qc * jnp.exp(cc)).astype(BF16)
        kd = (kc * jnp.exp(last - cc)).astype(BF16)
        dl = jnp.exp(last)
        vb = vc.astype(BF16)
        o_int = jnp.concatenate([_dot_nt(qd[:, h * half:(h + 1) * half], s_scr[h].astype(BF16))
                                 for h in range(2)], axis=1)
        dec = jnp.exp(jnp.where(causal, cc[:, None, :] - cc[None, :, :], NEG))
        prod = (qc[:, None, :] * kc[None, :, :] * dec).reshape(c16 * c16, RNN_WIDTH)
        outs = []
        for j in range(RNN_WIDTH // LANES):
            pj = prod[:, j * LANES:(j + 1) * LANES]
            s_lo = jnp.sum(jnp.where(low, pj, 0.0), axis=-1, keepdims=True)
            s_hi = jnp.sum(jnp.where(low, 0.0, pj), axis=-1, keepdims=True)
            outs.append(jnp.where(low, s_lo, s_hi))
        a = jnp.concatenate(outs, axis=1).reshape(c16, c16, RNN_WIDTH)
        o_intra = jnp.sum(a * vc[None, :, :], axis=1)
        o_ref[0, pl.ds(r0, c16), :] = o_int + o_intra
        for h in range(2):
            u = _dot_tn(vb[:, h * half:(h + 1) * half], kd[:, h * half:(h + 1) * half])
            s_scr[h] = s_scr[h] * dl[:, h * half:(h + 1) * half] + u * bd_ref[...]
        return carry

    lax.fori_loop(0, rq_ref.shape[1] // c16, chunk, 0)
    st_ref[0] = s_scr[...]


def _hgrn_prompt(rq, f, rv, ltri, bd):
    b, t, w = rq.shape
    tc = RNN_ROWS
    half = w // 2
    row = lambda i, j: (i, j, 0)
    return pl.pallas_call(
        _hgrn_prompt_body, grid=(b, t // tc),
        in_specs=[pl.BlockSpec((1, tc, w), row), pl.BlockSpec((1, tc, w), row), pl.BlockSpec((1, tc, w), row),
                  pl.BlockSpec(ltri.shape, lambda i, j: (0, 0)), pl.BlockSpec(bd.shape, lambda i, j: (0, 0))],
        out_specs=(pl.BlockSpec((1, tc, w), row), pl.BlockSpec((1, 2, half, half), lambda i, j: (i, 0, 0, 0))),
        out_shape=(jax.ShapeDtypeStruct((b, t, w), F32), jax.ShapeDtypeStruct((b, 2, half, half), F32)),
        scratch_shapes=[pltpu.VMEM((2, half, half), F32), pltpu.VMEM((tc, w), F32), pltpu.VMEM((tc, w), F32)],
        compiler_params=pltpu.CompilerParams(dimension_semantics=("parallel", "arbitrary"),
                                             vmem_limit_bytes=_vmem_limit(32 << 20)),
        name="hgrn_prompt",
    )(rq, f, rv, ltri, bd)


def _finish_body(x_ref, oa_ref, orn_ref, rgs_ref, ag_ref, rg_ref, wout_ref, lnm_ref, wup_ref, wdn_ref, y_ref, hn_scr):
    @pl.when(pl.program_id(1) == 0)
    def _():
        oa = oa_ref[...]
        orn = orn_ref[...]
        a_n = oa * lax.rsqrt(_head_mean_sq(oa) + EPS) * ag_ref[...]
        r_n = orn * lax.rsqrt(_head_mean_sq(orn) + EPS) * rg_ref[...] * rgs_ref[...]
        h = (x_ref[...] + _dot(a_n.astype(BF16), wout_ref[0:ATTN_WIDTH, :])
             + _dot(r_n.astype(BF16), wout_ref[ATTN_WIDTH:ATTN_WIDTH + RNN_WIDTH, :]))
        y_ref[...] = h
        hn_scr[...] = (_row_rms(h) * lnm_ref[...]).astype(BF16)

    u = jnp.maximum(_dot(hn_scr[...], wup_ref[...]), 0.0)
    y_ref[...] += _dot((u * u).astype(BF16), wdn_ref[...])


def _finish(x, oa, orn, rgs, ag, rg, wout, lnm, wup, wdn):
    n, dm = x.shape
    tm = FINISH_ROWS
    dff = wup.shape[1]
    row = lambda i, j: (i, 0)
    const = lambda i, j: (0, 0)
    return pl.pallas_call(
        _finish_body, grid=(n // tm, dff // FF_CHUNK),
        in_specs=[pl.BlockSpec((tm, dm), row), pl.BlockSpec((tm, ATTN_WIDTH), row),
                  pl.BlockSpec((tm, RNN_WIDTH), row), pl.BlockSpec((tm, RNN_WIDTH), row),
                  pl.BlockSpec(ag.shape, const), pl.BlockSpec(rg.shape, const),
                  pl.BlockSpec(wout.shape, const), pl.BlockSpec(lnm.shape, const),
                  pl.BlockSpec((dm, FF_CHUNK), lambda i, j: (0, j)),
                  pl.BlockSpec((FF_CHUNK, dm), lambda i, j: (j, 0))],
        out_specs=pl.BlockSpec((tm, dm), row),
        out_shape=jax.ShapeDtypeStruct((n, dm), F32),
        scratch_shapes=[pltpu.VMEM((tm, dm), BF16)],
        compiler_params=pltpu.CompilerParams(dimension_semantics=("parallel", "arbitrary"),
                                             vmem_limit_bytes=_vmem_limit(48 << 20)),
        name="finish",
    )(x, oa, orn, rgs, ag, rg, wout, lnm, wup, wdn)


def _proj_sample_body(xbt_ref, xtb_ref, ln_ref, wtok_ref, wtb_ref, wbt_ref, qg_ref, lbl_ref, gsel_ref, gwin_ref,
                      q_ref, rgs_ref, gates_ref, ztb_ref, zbt_ref):
    xb = (_row_rms(xbt_ref[...]) * ln_ref[...]).astype(BF16)
    xt = (_row_rms(xtb_ref[...]) * ln_ref[...]).astype(BF16)
    zq = _dot(xb, wtok_ref[:, 0:ATTN_WIDTH])
    q_ref[...] = (zq * lax.rsqrt(_head_mean_sq(zq) + EPS) * qg_ref[...]).astype(BF16)
    rgs_ref[...] = _silu(_dot(xb, wtok_ref[:, ATTN_WIDTH:ATTN_WIDTH + RNN_WIDTH]))
    gates_ref[...] = jax.nn.sigmoid(_dot(xb, wtok_ref[:, ATTN_WIDTH + RNN_WIDTH:ATTN_WIDTH + RNN_WIDTH + LANES]))

    d = HEAD_DIM
    zt = _dot_nt(wtb_ref[...], xt)
    ztb_ref[0:256] = zt[0:256]
    for g in range(N_KV_HEADS):
        ztb_ref[256 + g * d:256 + (g + 1) * d] = _col_head_norm(zt[256 + g * d:256 + (g + 1) * d], gsel_ref[...])
    ztb_ref[384:512] = zt[384:512]
    ztb_ref[512:1024] = _silu(zt[512:1024])
    lb = _lower_bound(lbl_ref[...], 0)[0]
    ztb_ref[1024:1536] = lb + (1.0 - lb) * jax.nn.sigmoid(zt[1024:1536])
    ztb_ref[1536:2048] = zt[1536:2048]

    zb = _dot_nt(wbt_ref[...], xb)
    for g in range(N_KV_HEADS):
        zbt_ref[g * d:(g + 1) * d] = _col_head_norm(zb[g * d:(g + 1) * d], gsel_ref[...])
        zbt_ref[256 + g * d:256 + (g + 1) * d] = _col_head_norm(zb[256 + g * d:256 + (g + 1) * d], gwin_ref[...])
    zbt_ref[128:256] = zb[128:256]
    zbt_ref[384:512] = zb[384:512]


def _proj_sample(xbt, xtb, ln, wtok, wtb, wbt, qg, lblt, gsel, gwin):
    n, dm = xbt.shape
    tm = PROJ_ROWS
    row = lambda i: (i, 0)
    col = lambda i: (0, i)
    const = lambda i: (0, 0)
    return pl.pallas_call(
        _proj_sample_body, grid=(n // tm,),
        in_specs=[pl.BlockSpec((tm, dm), row), pl.BlockSpec((tm, dm), row), pl.BlockSpec(ln.shape, const),
                  pl.BlockSpec(wtok.shape, const), pl.BlockSpec(wtb.shape, const), pl.BlockSpec(wbt.shape, const),
                  pl.BlockSpec(qg.shape, const), pl.BlockSpec(lblt.shape, lambda i: (0, 0, 0)),
                  pl.BlockSpec(gsel.shape, const), pl.BlockSpec(gwin.shape, const)],
        out_specs=(pl.BlockSpec((tm, ATTN_WIDTH), row), pl.BlockSpec((tm, RNN_WIDTH), row),
                   pl.BlockSpec((tm, LANES), row), pl.BlockSpec((wtb.shape[0], tm), col),
                   pl.BlockSpec((wbt.shape[0], tm), col)),
        out_shape=(jax.ShapeDtypeStruct((n, ATTN_WIDTH), BF16), jax.ShapeDtypeStruct((n, RNN_WIDTH), F32),
                   jax.ShapeDtypeStruct((n, LANES), F32), jax.ShapeDtypeStruct((wtb.shape[0], n), F32),
                   jax.ShapeDtypeStruct((wbt.shape[0], n), F32)),
        compiler_params=pltpu.CompilerParams(dimension_semantics=("parallel",),
                                             vmem_limit_bytes=_vmem_limit(40 << 20)),
        name="proj_sample",
    )(xbt, xtb, ln, wtok, wtb, wbt, qg, lblt, gsel, gwin)


PAGE_ROWS = 4 * KV_WIDTH


def _nsa_sample_body(pt_ref, cache_ref, q_ref, nkv_ref, w1_ref, pe_ref, w2_ref, kg_ref, e_ref,
                     ocs_ref, buf, kk_scr, vv_scr, sem, *, n_pages, past_len, dec_t, topk):
    b = pl.program_id(0)
    nbatch = pl.num_programs(0)
    slot = b % 2
    d = HEAD_DIM
    nblk = 2 * n_pages

    def page_copy(bb, sl, j):
        return pltpu.make_async_copy(cache_ref.at[pt_ref[bb * n_pages + j]],
                                     buf.at[sl, pl.ds(j * PAGE_ROWS, PAGE_ROWS)], sem.at[sl])

    @pl.when(b == 0)
    def _():
        for j in range(n_pages):
            page_copy(0, 0, j).start()

    @pl.when(b + 1 < nbatch)
    def _():
        for j in range(n_pages):
            page_copy(b + 1, 1 - slot, j).start()

    for j in range(n_pages):
        page_copy(b, slot, j).wait()

    low = _low_half((n_pages, LANES))

    def compress(c):
        acc = jnp.zeros((4 * n_pages, CMP_HIDDEN), F32)
        for dd in range(d):
            rows = []
            for g in range(N_KV_HEADS):
                xg = buf[slot, pl.ds((c * N_KV_HEADS + g) * d + dd, n_pages, stride=PAGE_ROWS), :]
                xg = xg + pe_ref[c, dd]
                rows += [jnp.where(low, xg, 0.0), jnp.where(low, 0.0, xg)]
            lhs = jnp.concatenate(rows, axis=0).astype(BF16)
            acc = acc + _dot(lhs, w1_ref[c, dd])
        return _dot(_silu(acc).astype(BF16), w2_ref[c])

    kc_all = _row_rms(compress(0)) * kg_ref[...]
    vc_all = compress(1)

    q = q_ref[0]
    nq = GROUP * dec_t
    rho = lax.broadcasted_iota(jnp.int32, (nblk, dec_t), 0)
    bid = 2 * (rho & (n_pages - 1)) + (rho >> _log2(n_pages))
    colq = lax.broadcasted_iota(jnp.int32, (1, nq), 1)
    rowq = lax.broadcasted_iota(jnp.int32, (nq, 1), 0)
    cur_blk = past_len // SEL_BLOCK
    o_cmp, o_sel = [], []
    for g in range(N_KV_HEADS):
        qst = _stack_group_queries(q, g)
        kc = kc_all[g * nblk:(g + 1) * nblk].astype(BF16)
        vc = vc_all[g * nblk:(g + 1) * nblk].astype(BF16)
        st = _dot_nt(kc, qst)
        qpos = past_len + (colq & (dec_t - 1))
        rho_q = lax.broadcasted_iota(jnp.int32, (nblk, 1), 0)
        end = (2 * (rho_q & (n_pages - 1)) + (rho_q >> _log2(n_pages))) * CMP_BLOCK + (CMP_BLOCK - 1)
        dist = (qpos - end).astype(F32)
        valid = dist >= 0.0
        s = jnp.where(valid, st - _row_slopes(g, colq >> _log2(dec_t)) * dist, NEG)
        e = jnp.exp(s - jnp.max(s, axis=0, keepdims=True))
        p = jnp.where(valid, e / jnp.sum(e, axis=0, keepdims=True), 0.0)
        o_cmp.append(_unstack_group(_dot_tn(p.astype(BF16), vc), dec_t))
        imp = p[:, 0:dec_t]
        for r in range(1, GROUP):
            imp = imp + p[:, r * dec_t:(r + 1) * dec_t]

        force = (bid == 0) | (bid == cur_blk)
        score = jnp.where(force, FORCED_SCORE, imp)
        rank = _block_ranks(score, bid, [(i, 2 * (i % n_pages) + i // n_pages) for i in range(nblk)])
        rank = rank + jnp.where(bid > cur_blk, (FORCED_SCORE >= score).astype(F32),
                                (FORCED_SCORE > score).astype(F32))
        msel = jnp.where((rank < topk) & (score >= 0.0), 1.0, 0.0).astype(BF16)

        for pg in range(n_pages):
            kt_ = buf[slot, pl.ds(pg * PAGE_ROWS + (2 * N_KV_HEADS + g) * d, d), :].astype(BF16)
            vt_ = buf[slot, pl.ds(pg * PAGE_ROWS + (3 * N_KV_HEADS + g) * d, d), :].astype(BF16)
            kk_scr[0:d, pg * PAGE_SIZE:(pg + 1) * PAGE_SIZE] = kt_
            kk_scr[d:2 * d, pg * PAGE_SIZE:(pg + 1) * PAGE_SIZE] = kt_
            vv_scr[0:d, pg * PAGE_SIZE:(pg + 1) * PAGE_SIZE] = vt_
            vv_scr[d:2 * d, pg * PAGE_SIZE:(pg + 1) * PAGE_SIZE] = vt_
        nk = _twice(nkv_ref[0, g * d:(g + 1) * d, :].astype(BF16))
        nv = _twice(nkv_ref[0, KV_WIDTH + g * d:KV_WIDTH + (g + 1) * d, :].astype(BF16))
        sp = _dot(qst, kk_scr[...])
        sn = _dot(qst, nk)
        mk = _dot_tn(msel, e_ref[...])
        okp = jnp.concatenate([mk] * GROUP, axis=0) > 0.5
        step = rowq & (dec_t - 1)
        slope = _row_slopes(g, rowq >> _log2(dec_t))
        kpos = lax.broadcasted_iota(jnp.int32, (1, past_len), 1)
        dpast = (past_len + step - kpos).astype(F32)
        lgp = jnp.where(okp & (dpast >= 0.0), sp - slope * dpast, NEG)
        lane = lax.broadcasted_iota(jnp.int32, (1, LANES), 1)
        dnew = (step - lane).astype(F32)
        lgn = jnp.where((lane < dec_t) & (dnew >= 0.0), sn - slope * dnew, NEG)
        m = jnp.maximum(jnp.max(lgp, axis=-1, keepdims=True), jnp.max(lgn, axis=-1, keepdims=True))
        pp = jnp.exp(lgp - m)
        pn = jnp.exp(lgn - m)
        den = jnp.sum(pp, axis=-1, keepdims=True) + jnp.sum(pn, axis=-1, keepdims=True)
        o2 = _dot_nt(pp.astype(BF16), vv_scr[...]) + _dot_nt(pn.astype(BF16), nv)
        o_sel.append(_unstack_group(o2 / den, dec_t))

    ocs_ref[0, 0] = jnp.concatenate(o_cmp, axis=1)
    ocs_ref[0, 1] = jnp.concatenate(o_sel, axis=1)


def _nsa_sample(page_flat, cache, q, nkv, w1r, pe_t, w2dup, kg_dup, e_perm, past_len, topk):
    nbatch, dec_t, _ = q.shape
    n_pages = past_len // PAGE_SIZE
    body = functools.partial(_nsa_sample_body, n_pages=n_pages, past_len=past_len, dec_t=dec_t, topk=topk)
    grid_spec = pltpu.PrefetchScalarGridSpec(
        num_scalar_prefetch=1, grid=(nbatch,),
        in_specs=[pl.BlockSpec(memory_space=pl.ANY),
                  pl.BlockSpec((1, dec_t, ATTN_WIDTH), lambda i, pt: (i, 0, 0)),
                  pl.BlockSpec((1,) + nkv.shape[1:], lambda i, pt: (i, 0, 0)),
                  pl.BlockSpec(w1r.shape, lambda i, pt: (0, 0, 0, 0)),
                  pl.BlockSpec(pe_t.shape, lambda i, pt: (0, 0, 0, 0)),
                  pl.BlockSpec(w2dup.shape, lambda i, pt: (0, 0, 0)),
                  pl.BlockSpec(kg_dup.shape, lambda i, pt: (0, 0)),
                  pl.BlockSpec(e_perm.shape, lambda i, pt: (0, 0))],
        out_specs=pl.BlockSpec((1, 2, dec_t, ATTN_WIDTH), lambda i, pt: (i, 0, 0, 0)),
        scratch_shapes=[pltpu.VMEM((2, n_pages * PAGE_ROWS, PAGE_SIZE), F32),
                        pltpu.VMEM((2 * HEAD_DIM, past_len), BF16), pltpu.VMEM((2 * HEAD_DIM, past_len), BF16),
                        pltpu.SemaphoreType.DMA((2,))])
    return pl.pallas_call(
        body, grid_spec=grid_spec,
        out_shape=jax.ShapeDtypeStruct((nbatch, 2, dec_t, ATTN_WIDTH), F32),
        compiler_params=pltpu.CompilerParams(dimension_semantics=("arbitrary",),
                                             vmem_limit_bytes=_vmem_limit(40 << 20)),
        name="nsa_sample",
    )(page_flat, cache, q, nkv, w1r, pe_t, w2dup, kg_dup, e_perm)


def _win_sample_body(win_ref, q_ref, nw_ref, ocs_ref, gates_ref, gexp_ref, oa_ref, wout_ref, *, past_len, dec_t):
    d = HEAD_DIM
    wbuf = win_ref.shape[2]
    q = q_ref[0]
    nq = GROUP * dec_t
    rowq = lax.broadcasted_iota(jnp.int32, (nq, 1), 0)
    step = rowq & (dec_t - 1)
    kpos = past_len - wbuf + lax.broadcasted_iota(jnp.int32, (1, wbuf), 1)
    dpast = (past_len + step - kpos).astype(F32)
    okp = (dpast >= 0.0) & (dpast < WINDOW)
    lane = lax.broadcasted_iota(jnp.int32, (1, LANES), 1)
    dnew = (step - lane).astype(F32)
    okn = (lane < dec_t) & (dnew >= 0.0) & (dnew < WINDOW)
    o_win = []
    for g in range(N_KV_HEADS):
        qst = _stack_group_queries(q, g)
        kt_ = win_ref[0, g * d:(g + 1) * d, :].astype(BF16)
        vt_ = win_ref[0, KV_WIDTH + g * d:KV_WIDTH + (g + 1) * d, :].astype(BF16)
        nk = _twice(nw_ref[0, g * d:(g + 1) * d, :].astype(BF16))
        nv = _twice(nw_ref[0, KV_WIDTH + g * d:KV_WIDTH + (g + 1) * d, :].astype(BF16))
        slope = _row_slopes(g, rowq >> _log2(dec_t))
        lgp = jnp.where(okp, _dot(qst, _twice(kt_)) - slope * dpast, NEG)
        lgn = jnp.where(okn, _dot(qst, nk) - slope * dnew, NEG)
        m = jnp.maximum(jnp.max(lgp, axis=-1, keepdims=True), jnp.max(lgn, axis=-1, keepdims=True))
        pp = jnp.exp(lgp - m)
        pn = jnp.exp(lgn - m)
        den = jnp.sum(pp, axis=-1, keepdims=True) + jnp.sum(pn, axis=-1, keepdims=True)
        o2 = _dot_nt(pp.astype(BF16), _twice(vt_)) + _dot_nt(pn.astype(BF16), nv)
        o_win.append(_unstack_group(o2 / den, dec_t))
    ge = _expand_gates(gates_ref[0], gexp_ref)
    oa_ref[0] = ge[0] * ocs_ref[0, 0] + ge[1] * ocs_ref[0, 1] + ge[2] * jnp.concatenate(o_win, axis=1)

    old = win_ref[0]
    rolled = pltpu.roll(old, wbuf - dec_t, 1)
    newr = pltpu.roll(nw_ref[0], LANES - dec_t, 1)
    wout_ref[0, :, 0:wbuf - LANES] = rolled[:, 0:wbuf - LANES]
    wout_ref[0, :, wbuf - LANES:wbuf] = jnp.where(lane >= LANES - dec_t, newr, rolled[:, wbuf - LANES:wbuf])


def _win_sample(win, q, nw, ocs, gates, gexp, past_len):
    nbatch, feat, wbuf = win.shape
    dec_t = q.shape[1]
    body = functools.partial(_win_sample_body, past_len=past_len, dec_t=dec_t)
    b3 = lambda i: (i, 0, 0)
    return pl.pallas_call(
        body, grid=(nbatch,),
        in_specs=[pl.BlockSpec((1, feat, wbuf), b3), pl.BlockSpec((1, dec_t, ATTN_WIDTH), b3),
                  pl.BlockSpec((1, feat, LANES), b3), pl.BlockSpec((1, 2, dec_t, ATTN_WIDTH), lambda i: (i, 0, 0, 0)),
                  pl.BlockSpec((1, dec_t, LANES), b3), pl.BlockSpec(gexp.shape, lambda i: (0, 0, 0))],
        out_specs=(pl.BlockSpec((1, dec_t, ATTN_WIDTH), b3), pl.BlockSpec((1, feat, wbuf), b3)),
        out_shape=(jax.ShapeDtypeStruct((nbatch, dec_t, ATTN_WIDTH), F32),
                   jax.ShapeDtypeStruct((nbatch, feat, wbuf), F32)),
        compiler_params=pltpu.CompilerParams(dimension_semantics=("parallel",),
                                             vmem_limit_bytes=_vmem_limit(24 << 20)),
        name="win_sample",
    )(win, q, nw, ocs, gates, gexp)


def _hgrn_sample_body(q_ref, f_ref, v_ref, s_ref, o_ref, so_ref, *, dec_t):
    nb = s_ref.shape[3]
    o_ref[...] = jnp.zeros(o_ref.shape, F32)

    sub = 8

    def per_tile(i, carry):
        r0 = pl.multiple_of(i * sub, sub)
        f_t = [f_ref[pl.ds(r0, sub), pl.ds(t * nb, nb)] for t in range(dec_t)]
        q_t = [q_ref[pl.ds(r0, sub), pl.ds(t * nb, nb)] for t in range(dec_t)]
        for j in range(sub):
            s = s_ref[0, r0 + j]
            for t in range(dec_t):
                cols = pl.ds(t * nb, nb)
                fr = f_t[t][j:j + 1, :]
                s = fr * s + (1.0 - fr) * v_ref[:, cols]
                o_ref[:, cols] = o_ref[:, cols] + s * q_t[t][j:j + 1, :]
            so_ref[0, r0 + j] = s
        return carry

    lax.fori_loop(0, s_ref.shape[1] // sub, per_tile, 0)


def _hgrn_sample(ztb, state, dec_t):
    nh, dk, dv, nb = state.shape
    n = ztb.shape[1]
    body = functools.partial(_hgrn_sample_body, dec_t=dec_t)
    q0, f0, v0 = 512 // dk, 1024 // dk, 1536 // dk
    return pl.pallas_call(
        body, grid=(nh,),
        in_specs=[pl.BlockSpec((dk, n), lambda h: (q0 + h, 0)), pl.BlockSpec((dk, n), lambda h: (f0 + h, 0)),
                  pl.BlockSpec((dv, n), lambda h: (v0 + h, 0)),
                  pl.BlockSpec((1, dk, dv, nb), lambda h: (h, 0, 0, 0))],
        out_specs=(pl.BlockSpec((dv, n), lambda h: (h, 0)), pl.BlockSpec((1, dk, dv, nb), lambda h: (h, 0, 0, 0))),
        out_shape=(jax.ShapeDtypeStruct((nh * dv, n), F32), jax.ShapeDtypeStruct(state.shape, F32)),
        compiler_params=pltpu.CompilerParams(dimension_semantics=("parallel",),
                                             vmem_limit_bytes=_vmem_limit(24 << 20)),
        name="hgrn_sample",
    )(ztb, ztb, ztb, state)


def _gate_expander():
    m = np.zeros((N_BRANCH, LANES, ATTN_WIDTH), np.float32)
    for br in range(N_BRANCH):
        for h in range(N_ATTN_HEADS):
            m[br, h * N_BRANCH + br, h * HEAD_DIM:(h + 1) * HEAD_DIM] = 1.0
    return jnp.asarray(m, BF16)


def _block_expander(block_ids, n_keys):
    key_blk = np.arange(n_keys) // SEL_BLOCK
    return jnp.asarray((np.asarray(block_ids)[:, None] == key_blk[None, :]).astype(np.float32), BF16)


def _chunk_lower_tri(n, c):
    i = np.arange(n)
    return jnp.asarray(((i[:, None] // c == i[None, :] // c) & (i[None, :] <= i[:, None])).astype(np.float32), BF16)


def _head_block_diag(n):
    i = np.arange(n)
    return jnp.asarray((i[:, None] // RNN_DV == i[None, :] // RNN_DK).astype(np.float32), F32)


def kernel(x_prompt, x_sample, cache_kv, cache_win, state_rnn, page_table, ln_mix, w_in, q_norm, k_norm, cmp_pe,
           cmp_w1, cmp_w2, attn_out_norm, rnn_lb_logits, rnn_out_norm, w_out, ln_mlp, w_up, w_down):
    assert w_in.shape[0] == 1, "single layer"
    b, t, dm = x_prompt.shape
    nbatch, dec_t, _ = x_sample.shape
    n_pool = cache_kv.shape[1]
    n_pages = page_table.shape[1]
    past_len = n_pages * PAGE_SIZE
    wbuf = cache_win.shape[2]
    assert t % PROJ_ROWS == 0 and t % RNN_ROWS == 0 and t % min(SEL_KEYS, t) == 0 and t >= WINDOW
    assert (b * t) % FINISH_ROWS == 0 and (nbatch * dec_t) % FINISH_ROWS == 0 and w_up.shape[2] % FF_CHUNK == 0
    assert (nbatch * dec_t) % PROJ_ROWS == 0 and nbatch == LANES and dec_t <= 8
    assert past_len % SEL_BLOCK == 0 and wbuf == WINDOW and wbuf >= LANES

    w = w_in[0]
    c_kv, c_gate, c_rq, c_rf, c_ri, c_rg = ATTN_WIDTH, ATTN_WIDTH + 6 * KV_WIDTH, 1304, 1816, 2328, 2840
    gate_cols = jnp.pad(w[:, c_gate:c_rq], ((0, 0), (0, LANES - N_ATTN_HEADS * N_BRANCH)))
    wtok = jnp.concatenate([w[:, 0:ATTN_WIDTH], w[:, c_rq:], gate_cols, w[:, c_kv:c_kv + 2 * KV_WIDTH]],
                           axis=1).astype(BF16)
    wft = w[:, c_kv:c_gate].T.astype(BF16)
    wtok_s = jnp.concatenate([w[:, 0:ATTN_WIDTH], w[:, c_rg:], gate_cols], axis=1).astype(BF16)
    wtb_s = jnp.concatenate([w[:, c_kv:c_kv + 4 * KV_WIDTH], w[:, c_rq:c_rg]], axis=1).T.astype(BF16)
    wbt_s = w[:, c_kv + 2 * KV_WIDTH:c_gate].T.astype(BF16)
    ln = ln_mix[0][None, :]
    qg = (jnp.tile(q_norm[0], N_ATTN_HEADS) * SCALE)[None, :]
    lbl = rnn_lb_logits.astype(F32)
    lblt = jnp.broadcast_to(lbl[:, :, None], lbl.shape + (PROJ_ROWS,))
    gsel = jnp.broadcast_to(k_norm[0, 1][:, None], (HEAD_DIM, PROJ_ROWS))
    gwin = jnp.broadcast_to(k_norm[0, 2][:, None], (HEAD_DIM, PROJ_ROWS))
    kg_dup = jnp.tile(k_norm[0, 0], 2)[None, :]
    pe = cmp_pe[0]
    pe_tok = jnp.tile(jnp.concatenate([jnp.tile(pe[0], (1, N_KV_HEADS)), jnp.tile(pe[1], (1, N_KV_HEADS))], axis=1),
                      (PROJ_ROWS // CMP_BLOCK, 1))
    w1 = cmp_w1[0].reshape(2, CMP_BLOCK, HEAD_DIM, CMP_HIDDEN)
    zeros = jnp.zeros_like(w1)
    w1bd = jnp.concatenate([jnp.concatenate([w1, zeros], axis=3), jnp.concatenate([zeros, w1], axis=3)],
                           axis=2).astype(BF16)
    w1r = jnp.tile(w1.transpose(0, 2, 1, 3), (1, 1, 2, 1)).astype(BF16)
    pe_t = jnp.tile(pe.transpose(0, 2, 1), (1, 1, 2))[:, :, None, :]
    w2dup = jnp.tile(cmp_w2[0], (1, 1, 2)).astype(BF16)
    ag = attn_out_norm[0][None, :]
    rg = rnn_out_norm[0][None, :]
    wout = w_out[0].astype(BF16)
    lnm = ln_mlp[0][None, :]
    wup = w_up[0].astype(BF16)
    wdn = w_down[0].astype(BF16)
    gexp = _gate_expander()

    (q_p, gates_p, rq_p, f_p, rv_p, rgs_p, kvc_p, kvt_p, wint_p, att_p) = _proj_prompt(
        x_prompt, ln, wtok, wft, qg, lbl, pe_tok, gsel, gwin)
    cmp_p = _compress_prompt(kvc_p, w1bd, w2dup, kg_dup)
    nb_p = t // CMP_BLOCK
    oa_p = _nsa_prompt(q_p, gates_p, cmp_p, att_p, _block_expander(np.arange(nb_p), t), gexp)
    orn_p, st_p = _hgrn_prompt(rq_p, f_p, rv_p, _chunk_lower_tri(RNN_ROWS, RNN_CHUNK), _head_block_diag(RNN_WIDTH // 2))
    y_p = _finish(x_prompt.reshape(b * t, dm), oa_p.reshape(b * t, ATTN_WIDTH), orn_p.reshape(b * t, RNN_WIDTH),
                  rgs_p.reshape(b * t, RNN_WIDTH), ag, rg, wout, lnm, wup, wdn).reshape(b, t, dm)
    kv_prompt = kvt_p.reshape(1, b, 4, N_KV_HEADS, HEAD_DIM, t).transpose(0, 1, 5, 2, 3, 4)
    wlen = min(WINDOW, t)
    win_prompt = wint_p[:, :, t - wlen:].reshape(1, b, 2, N_KV_HEADS, HEAD_DIM, wlen).transpose(0, 1, 5, 2, 3, 4)
    hh = RNN_WIDTH // 2 // RNN_DV
    st5 = st_p.reshape(b, 2, hh, RNN_DV, hh, RNN_DK)
    rnn_prompt = jnp.stack([st5[:, :, i, :, i, :] for i in range(hh)], axis=2)
    rnn_prompt = rnn_prompt.reshape(b, N_RNN_HEADS, RNN_DV, RNN_DK).transpose(0, 1, 3, 2)[None]

    n_s = nbatch * dec_t
    xbt = x_sample.reshape(n_s, dm)
    xtb = x_sample.transpose(1, 0, 2).reshape(n_s, dm)
    q_s, rgs_s, gates_s, ztb, zbt = _proj_sample(xbt, xtb, ln, wtok_s, wtb_s, wbt_s, qg, lblt, gsel, gwin)
    kv_sample = ztb[0:4 * KV_WIDTH].reshape(4, N_KV_HEADS, HEAD_DIM, dec_t, nbatch).transpose(4, 3, 0, 1, 2)[None]
    new_rows = jnp.pad(zbt.reshape(4 * KV_WIDTH, nbatch, dec_t).transpose(1, 0, 2),
                       ((0, 0), (0, 0), (0, LANES - dec_t)))
    cache = cache_kv[0].transpose(0, 2, 3, 4, 1).reshape(n_pool, PAGE_ROWS, PAGE_SIZE)
    nblk_s = past_len // CMP_BLOCK
    rho = np.arange(nblk_s)
    e_perm = _block_expander(2 * (rho % n_pages) + rho // n_pages, past_len)
    ns_s = -(-(past_len + dec_t) // SEL_BLOCK)
    q_s3 = q_s.reshape(nbatch, dec_t, ATTN_WIDTH)
    ocs = _nsa_sample(page_table.reshape(-1), cache, q_s3, new_rows[:, 0:2 * KV_WIDTH], w1r, pe_t, w2dup, kg_dup,
                      e_perm, past_len, min(SEL_TOPK, ns_s))
    win = cache_win[0].transpose(0, 2, 3, 4, 1).reshape(nbatch, 2 * KV_WIDTH, wbuf)
    oa_s, win_new = _win_sample(win, q_s3, new_rows[:, 2 * KV_WIDTH:], ocs,
                                gates_s.reshape(nbatch, dec_t, LANES), gexp, past_len)
    win_sample = win_new.reshape(1, nbatch, 2, N_KV_HEADS, HEAD_DIM, wbuf).transpose(0, 1, 5, 2, 3, 4)
    state = state_rnn[0].transpose(1, 2, 3, 0)
    orn_t, state_new = _hgrn_sample(ztb, state, dec_t)
    rnn_sample = state_new.transpose(3, 0, 1, 2)[None]
    orn_s = orn_t.reshape(RNN_WIDTH, dec_t, nbatch).transpose(2, 1, 0).reshape(n_s, RNN_WIDTH)
    y_s = _finish(xbt, oa_s.reshape(n_s, ATTN_WIDTH), orn_s, rgs_s, ag, rg, wout, lnm, wup, wdn).reshape(nbatch, dec_t, dm)

    return (y_p, y_s, kv_prompt, kv_sample, win_prompt, win_sample, rnn_prompt, rnn_sample)
```

```python
import functools

import numpy as np
import jax
import jax.numpy as jnp
from jax import lax
from jax.experimental import pallas as pl
from jax.experimental.pallas import tpu as pltpu

F32 = jnp.float32
BF16 = jnp.bfloat16

HEAD_DIM = 64
N_ATTN_HEADS = 8
N_KV_HEADS = 2
GROUP = N_ATTN_HEADS // N_KV_HEADS
N_RNN_HEADS = 8
RNN_DK = 64
RNN_DV = 64
ATTN_WIDTH = N_ATTN_HEADS * HEAD_DIM
RNN_WIDTH = N_RNN_HEADS * RNN_DV
KV_WIDTH = N_KV_HEADS * HEAD_DIM
N_BRANCH = 3
CMP_BLOCK = 64
SEL_BLOCK = 64
SEL_TOPK = 16
WINDOW = 512
CMP_HIDDEN = 128
PAGE_SIZE = 128
SCALE = HEAD_DIM ** -0.5
EPS = 1e-6
NEG = -1e30
FORCED_SCORE = GROUP + 1.0
SLOPES = [[2.0 ** (-(g * GROUP + r + 1)) for r in range(GROUP)] for g in range(N_KV_HEADS)]

LANES = 128
VMEM_BYTES_V7X = 64 * 1024 * 1024

PROJ_ROWS = 256
NSA_Q_ROWS = 128
SEL_KEYS = 512
ROW_BLOCK = 16
RNN_ROWS = 256
RNN_CHUNK = 16
FF_CHUNK = 1024
FINISH_ROWS = 512

NT = (((1,), (1,)), ((), ()))
TN = (((0,), (0,)), ((), ()))


def _vmem_limit(nbytes):
    return int(min(VMEM_BYTES_V7X - (8 << 20), max(nbytes, 16 << 20)))


def _dot(a, b):
    return jnp.dot(a, b, preferred_element_type=F32)


def _dot_nt(a, b):
    return lax.dot_general(a, b, NT, preferred_element_type=F32)


def _dot_tn(a, b):
    return lax.dot_general(a, b, TN, preferred_element_type=F32)


def _low_half(shape):
    lane = lax.broadcasted_iota(jnp.int32, shape, len(shape) - 1)
    return (lane & HEAD_DIM) == 0


def _head_mean_sq(x):
    outs = []
    for j in range(x.shape[-1] // LANES):
        blk = x[:, j * LANES:(j + 1) * LANES]
        sq = blk * blk
        low = _low_half(blk.shape)
        s_lo = jnp.sum(jnp.where(low, sq, 0.0), axis=-1, keepdims=True)
        s_hi = jnp.sum(jnp.where(low, 0.0, sq), axis=-1, keepdims=True)
        outs.append(jnp.where(low, s_lo, s_hi))
    return jnp.concatenate(outs, axis=-1) * (1.0 / HEAD_DIM)


def _row_rms(x):
    return x * lax.rsqrt(jnp.mean(x * x, axis=-1, keepdims=True) + EPS)


def _col_head_norm(rows, gain):
    ms = jnp.mean(rows * rows, axis=0, keepdims=True)
    return rows * lax.rsqrt(ms + EPS) * gain


def _silu(z):
    return z * jax.nn.sigmoid(z)


def _lower_bound(logits, axis):
    m = jnp.max(logits, axis=axis, keepdims=True)
    e = jnp.exp(logits - m)
    lb = e / jnp.sum(e, axis=axis, keepdims=True)
    return lb[0:1] if axis == 0 else lb


def _stack_group_queries(q, g):
    rows = q.shape[0]
    low = _low_half((rows, LANES))
    zero = jnp.zeros((rows, LANES), q.dtype)
    pa = q[:, g * 2 * LANES:g * 2 * LANES + LANES]
    pb = q[:, g * 2 * LANES + LANES:(g + 1) * 2 * LANES]
    return jnp.concatenate([jnp.where(low, pa, zero), jnp.where(low, zero, pa),
                            jnp.where(low, pb, zero), jnp.where(low, zero, pb)], axis=0)


def _unstack_group(o2, rows):
    low = _low_half((rows, LANES))
    return jnp.concatenate([jnp.where(low, o2[0:rows], o2[rows:2 * rows]),
                            jnp.where(low, o2[2 * rows:3 * rows], o2[3 * rows:4 * rows])], axis=1)


def _log2(n):
    assert n > 0 and n & (n - 1) == 0, n
    return n.bit_length() - 1


def _twice(x):
    return jnp.concatenate([x, x], axis=0)


def _row_slopes(g, row_head):
    s = jnp.full(row_head.shape, SLOPES[g][GROUP - 1], F32)
    for r in range(GROUP - 2, -1, -1):
        s = jnp.where(row_head == r, SLOPES[g][r], s)
    return s


def _block_ranks(score, ids, cand):
    rank = jnp.zeros(score.shape, F32)
    for row, id_i in cand:
        s_i = score[row:row + 1, :]
        rank = rank + jnp.where(ids > id_i, (s_i >= score).astype(F32), (s_i > score).astype(F32))
    return rank


def _expand_gates(gates, gexp_ref):
    hi = gates.astype(BF16)
    lo = (gates - hi.astype(F32)).astype(BF16)
    return [_dot(hi, gexp_ref[br]) + _dot(lo, gexp_ref[br]) for br in range(N_BRANCH)]


TOK_Q, TOK_RQ, TOK_RF, TOK_RI, TOK_RG, TOK_GATE, TOK_KVC, TOK_END = 0, 512, 1024, 1536, 2048, 2560, 2688, 2944


def _proj_prompt_body(x_ref, ln_ref, wtok_ref, wft_ref, qg_ref, lbl_ref, pe_ref, gsel_ref, gwin_ref,
                      q_ref, gates_ref, rq_ref, f_ref, rv_ref, rgs_ref, kvc_ref, kvt_ref, wint_ref, att_ref):
    xb = (_row_rms(x_ref[0]) * ln_ref[...]).astype(BF16)

    def tok(lo, hi):
        return _dot(xb, wtok_ref[:, lo:hi])

    zq = tok(TOK_Q, TOK_RQ)
    q_ref[0] = (zq * lax.rsqrt(_head_mean_sq(zq) + EPS) * qg_ref[...]).astype(BF16)
    rq_ref[0] = _silu(tok(TOK_RQ, TOK_RF))
    lb = _lower_bound(lbl_ref[...], 0)
    f_ref[0] = lb + (1.0 - lb) * jax.nn.sigmoid(tok(TOK_RF, TOK_RI))
    rv_ref[0] = tok(TOK_RI, TOK_RG)
    rgs_ref[0] = _silu(tok(TOK_RG, TOK_GATE))
    gates_ref[0] = jax.nn.sigmoid(tok(TOK_GATE, TOK_KVC))
    zc = tok(TOK_KVC, TOK_END) + pe_ref[...]
    kvc_ref[0, 0] = zc[:, 0:LANES]
    kvc_ref[1, 0] = zc[:, LANES:2 * LANES]

    zf = _dot_nt(wft_ref[...], xb)
    d = HEAD_DIM
    ksel = jnp.concatenate([_col_head_norm(zf[256 + g * d:256 + (g + 1) * d], gsel_ref[...])
                            for g in range(N_KV_HEADS)], axis=0)
    kwin = jnp.concatenate([_col_head_norm(zf[512 + g * d:512 + (g + 1) * d], gwin_ref[...])
                            for g in range(N_KV_HEADS)], axis=0)
    vsel = zf[384:512]
    vwin = zf[640:768]
    kvt_ref[0, 0:256] = zf[0:256]
    kvt_ref[0, 256:384] = ksel
    kvt_ref[0, 384:512] = vsel
    wint_ref[0, 0:128] = kwin
    wint_ref[0, 128:256] = vwin
    att_ref[0, 0:128] = ksel.astype(BF16)
    att_ref[0, 128:256] = vsel.astype(BF16)
    att_ref[0, 256:384] = kwin.astype(BF16)
    att_ref[0, 384:512] = vwin.astype(BF16)


def _proj_prompt(x, ln, wtok, wft, qg, lbl, pe_tok, gsel, gwin):
    b, t, dm = x.shape
    tm = PROJ_ROWS
    grid = (b, t // tm)
    row = lambda i, j: (i, j, 0)
    col = lambda i, j: (i, 0, j)
    const2 = lambda i, j: (0, 0)
    out_shape = (
        jax.ShapeDtypeStruct((b, t, ATTN_WIDTH), BF16),
        jax.ShapeDtypeStruct((b, t, LANES), F32),
        jax.ShapeDtypeStruct((b, t, RNN_WIDTH), F32),
        jax.ShapeDtypeStruct((b, t, RNN_WIDTH), F32),
        jax.ShapeDtypeStruct((b, t, RNN_WIDTH), F32),
        jax.ShapeDtypeStruct((b, t, RNN_WIDTH), F32),
        jax.ShapeDtypeStruct((2, b, t, LANES), F32),
        jax.ShapeDtypeStruct((b, 4 * KV_WIDTH, t), F32),
        jax.ShapeDtypeStruct((b, 2 * KV_WIDTH, t), F32),
        jax.ShapeDtypeStruct((b, 4 * KV_WIDTH, t), BF16),
    )
    out_specs = (
        pl.BlockSpec((1, tm, ATTN_WIDTH), row), pl.BlockSpec((1, tm, LANES), row),
        pl.BlockSpec((1, tm, RNN_WIDTH), row), pl.BlockSpec((1, tm, RNN_WIDTH), row),
        pl.BlockSpec((1, tm, RNN_WIDTH), row), pl.BlockSpec((1, tm, RNN_WIDTH), row),
        pl.BlockSpec((2, 1, tm, LANES), lambda i, j: (0, i, j, 0)),
        pl.BlockSpec((1, 4 * KV_WIDTH, tm), col), pl.BlockSpec((1, 2 * KV_WIDTH, tm), col),
        pl.BlockSpec((1, 4 * KV_WIDTH, tm), col),
    )
    in_specs = [
        pl.BlockSpec((1, tm, dm), row), pl.BlockSpec(ln.shape, const2),
        pl.BlockSpec(wtok.shape, const2), pl.BlockSpec(wft.shape, const2),
        pl.BlockSpec(qg.shape, const2), pl.BlockSpec(lbl.shape, const2), pl.BlockSpec(pe_tok.shape, const2),
        pl.BlockSpec(gsel.shape, const2), pl.BlockSpec(gwin.shape, const2),
    ]
    return pl.pallas_call(
        _proj_prompt_body, grid=grid, in_specs=in_specs, out_specs=out_specs, out_shape=out_shape,
        compiler_params=pltpu.CompilerParams(dimension_semantics=("parallel", "parallel"),
                                             vmem_limit_bytes=_vmem_limit(48 << 20)),
        name="proj_prompt",
    )(x, ln, wtok, wft, qg, lbl, pe_tok, gsel, gwin)


def _compress_prompt_body(x_ref, w1_ref, w2_ref, kg_ref, out_ref):
    c = pl.program_id(0)
    nb = out_ref.shape[2]
    acc = jnp.zeros((nb, 2 * CMP_HIDDEN), F32)
    for pos in range(CMP_BLOCK):
        xp = x_ref[0, 0, pl.ds(pos, nb, stride=CMP_BLOCK), :]
        acc = acc + _dot(xp.astype(BF16), w1_ref[0, pos])
    hb = _silu(acc).astype(BF16)
    outs = []
    for g in range(N_KV_HEADS):
        y = _dot(hb[:, g * CMP_HIDDEN:(g + 1) * CMP_HIDDEN], w2_ref[0])
        yn = _row_rms(y) * kg_ref[...]
        outs.append(jnp.where(c == 0, yn, y))
    out_ref[0, 0] = jnp.concatenate(outs, axis=1)


def _compress_prompt(kvc, w1bd, w2dup, kg_dup):
    _, b, t, _ = kvc.shape
    nb = t // CMP_BLOCK
    return pl.pallas_call(
        _compress_prompt_body, grid=(2, b),
        in_specs=[pl.BlockSpec((1, 1, t, LANES), lambda c, i: (c, i, 0, 0)),
                  pl.BlockSpec((1,) + w1bd.shape[1:], lambda c, i: (c, 0, 0, 0)),
                  pl.BlockSpec((1,) + w2dup.shape[1:], lambda c, i: (c, 0, 0)),
                  pl.BlockSpec(kg_dup.shape, lambda c, i: (0, 0))],
        out_specs=pl.BlockSpec((1, 1, nb, 2 * LANES), lambda c, i: (c, i, 0, 0)),
        out_shape=jax.ShapeDtypeStruct((2, b, nb, 2 * LANES), F32),
        compiler_params=pltpu.CompilerParams(dimension_semantics=("arbitrary", "arbitrary"),
                                             vmem_limit_bytes=_vmem_limit(32 << 20)),
        name="compress_prompt",
    )(kvc, w1bd, w2dup, kg_dup)


def _softmax_update(sc_scr, p_scr, m_scr, mn_scr, l_scr, a_scr, ncol, tq):
    rb = ROW_BLOCK
    for i in range(GROUP * tq // rb):
        rows = slice(i * rb, (i + 1) * rb)
        mn = mn_scr[rows, :]
        tot = jnp.zeros((rb, LANES), F32)
        for j in range(ncol):
            cols = slice(j * LANES, (j + 1) * LANES)
            p = jnp.exp(sc_scr[rows, cols] - mn)
            tot = tot + p
            p_scr[rows, cols] = p.astype(BF16)
        alpha = jnp.exp(m_scr[rows, :] - mn)
        l_scr[rows, :] = alpha * l_scr[rows, :] + jnp.sum(tot, axis=-1, keepdims=True)
        a_scr[rows, :] = alpha
        m_scr[rows, :] = mn


def _nsa_prompt_body(q_ref, gates_ref, cmp_ref, att_ref, kfeat_ref, qfeat_ref, gexp_ref, oa_ref,
                     sc_scr, p_scr, m_scr, mn_scr, l_scr, a_scr, acc_scr, *, seq, topk):
    tq = NSA_Q_ROWS
    tk = min(SEL_KEYS, seq)
    nb = cmp_ref.shape[2]
    t0 = pl.program_id(1) * tq
    wk = min(WINDOW + tq, seq)
    ws = pl.multiple_of(jnp.clip(t0 - WINDOW, 0, seq - wk), LANES)
    q = q_ref[0]
    d = HEAD_DIM
    groups = range(N_KV_HEADS)
    o_cmp, o_win, qsel, qwin = [], [], [], []
    for g in groups:
        qst = _stack_group_queries(q, g)

        kc = cmp_ref[0, 0][:, g * LANES:(g + 1) * LANES].astype(BF16)
        vc = cmp_ref[1, 0][:, g * LANES:(g + 1) * LANES].astype(BF16)
        st = _dot_nt(kc, qst)
        blk = lax.broadcasted_iota(jnp.int32, (nb, GROUP * tq), 0)
        colq = lax.broadcasted_iota(jnp.int32, (1, GROUP * tq), 1)
        tpos = t0 + (colq & (tq - 1))
        dist = (tpos - (blk * CMP_BLOCK + (CMP_BLOCK - 1))).astype(F32)
        valid = dist >= 0.0
        s = jnp.where(valid, st - _row_slopes(g, colq >> _log2(tq)) * dist, NEG)
        e = jnp.exp(s - jnp.max(s, axis=0, keepdims=True))
        p = jnp.where(valid, e / jnp.sum(e, axis=0, keepdims=True), 0.0)
        o_cmp.append(_unstack_group(_dot_tn(p.astype(BF16), vc), tq))
        imp = p[:, 0:tq] + p[:, tq:2 * tq] + p[:, 2 * tq:3 * tq] + p[:, 3 * tq:4 * tq]

        bj = lax.broadcasted_iota(jnp.int32, (nb, tq), 0)
        cur = (t0 + lax.broadcasted_iota(jnp.int32, (nb, tq), 1)) >> _log2(SEL_BLOCK)
        force = (bj == 0) | (bj == cur)
        score = jnp.where(bj <= cur, jnp.where(force, FORCED_SCORE, imp), -1.0)
        rank = _block_ranks(score, bj, [(i, i) for i in range(nb)])
        mneg = jnp.where((rank < topk) & (score >= 0.0), 0.0, NEG)
        mtok = jnp.concatenate([mneg, jnp.zeros((LANES - nb, tq), F32)], axis=0).T
        qf = qfeat_ref[g]
        qwin.append(jnp.concatenate([qst, qf.astype(BF16)], axis=1))
        qsel.append(jnp.concatenate([qst, (qf + jnp.concatenate([mtok] * GROUP, axis=0)).astype(BF16)], axis=1))

    def reset(stats):
        for g in groups:
            m_scr[g] = jnp.full(m_scr.shape[1:], NEG, F32)
            for ref in stats:
                ref[g] = jnp.zeros(ref.shape[1:], F32)

    def logits(g, qa, kt_, kf, width, extra):
        sc = _dot(qa, jnp.concatenate([kt_, kt_, kf], axis=0))
        if extra is not None:
            sc = sc + extra
        sc_scr[g, :, 0:width] = sc
        mn_scr[g] = jnp.maximum(m_scr[g], jnp.max(sc, axis=-1, keepdims=True))

    def softmax_tiles(ncol):
        for g in groups:
            _softmax_update(sc_scr.at[g], p_scr.at[g], m_scr.at[g], mn_scr.at[g], l_scr.at[g], a_scr.at[g], ncol, tq)

    reset((l_scr, acc_scr))
    n_kt = (t0 + tq - 1) // tk + 1

    def kv_step(kt, causal):
        s0 = pl.multiple_of(kt * tk, tk)
        kf = kfeat_ref[:, pl.ds(s0, tk)]
        extra = None
        if causal:
            dd = (t0 - s0) + lax.broadcasted_iota(jnp.int32, (tq, tk), 0) - lax.broadcasted_iota(jnp.int32, (tq, tk), 1)
            extra = jnp.concatenate([jnp.where(dd >= 0, 0.0, NEG)] * GROUP, axis=0)
        for g in groups:
            logits(g, qsel[g], att_ref[0, g * d:(g + 1) * d, pl.ds(s0, tk)], kf, tk, extra)
        softmax_tiles(tk // LANES)
        for g in groups:
            vt_ = att_ref[0, KV_WIDTH + g * d:KV_WIDTH + (g + 1) * d, pl.ds(s0, tk)]
            acc_scr[g] = a_scr[g] * acc_scr[g] + _dot_nt(p_scr[g, :, 0:tk], _twice(vt_))

    def full_tile(kt, carry):
        kv_step(kt, False)
        return carry

    lax.fori_loop(0, n_kt - 1, full_tile, 0)
    kv_step(n_kt - 1, True)
    o_sel = [_unstack_group(acc_scr[g] / l_scr[g], tq) for g in groups]

    reset((l_scr,))
    kfw = kfeat_ref[:, pl.ds(ws, wk)]
    ddw = (t0 - ws) + lax.broadcasted_iota(jnp.int32, (tq, wk), 0) - lax.broadcasted_iota(jnp.int32, (tq, wk), 1)
    nmw = jnp.concatenate([jnp.where((ddw >= 0) & (ddw < WINDOW), 0.0, NEG)] * GROUP, axis=0)
    for g in groups:
        logits(g, qwin[g], att_ref[0, 2 * KV_WIDTH + g * d:2 * KV_WIDTH + (g + 1) * d, pl.ds(ws, wk)], kfw, wk, nmw)
    softmax_tiles(wk // LANES)
    for g in groups:
        vw = att_ref[0, 3 * KV_WIDTH + g * d:3 * KV_WIDTH + (g + 1) * d, pl.ds(ws, wk)]
        o_win.append(_unstack_group(_dot_nt(p_scr[g, :, 0:wk], _twice(vw)) / l_scr[g], tq))

    ge = _expand_gates(gates_ref[0], gexp_ref)
    oa_ref[0] = (ge[0] * jnp.concatenate(o_cmp, axis=1) + ge[1] * jnp.concatenate(o_sel, axis=1)
                 + ge[2] * jnp.concatenate(o_win, axis=1))


def _nsa_prompt(q, gates, cmpkv, att, k_feat, q_feat, gexp):
    b, t, _ = q.shape
    tq = NSA_Q_ROWS
    nb = t // CMP_BLOCK
    assert nb <= HEAD_DIM, "the block mask uses 64 feature lanes"
    cw = max(min(SEL_KEYS, t), min(WINDOW + tq, t))
    body = functools.partial(_nsa_prompt_body, seq=t, topk=min(SEL_TOPK, nb))
    ng = N_KV_HEADS
    stat = pltpu.VMEM((ng, GROUP * tq, LANES), F32)
    return pl.pallas_call(
        body, grid=(b, t // tq),
        in_specs=[pl.BlockSpec((1, tq, ATTN_WIDTH), lambda i, j: (i, j, 0)),
                  pl.BlockSpec((1, tq, LANES), lambda i, j: (i, j, 0)),
                  pl.BlockSpec((2, 1, nb, 2 * LANES), lambda i, j: (0, i, 0, 0)),
                  pl.BlockSpec((1, 4 * KV_WIDTH, t), lambda i, j: (i, 0, 0)),
                  pl.BlockSpec(k_feat.shape, lambda i, j: (0, 0)),
                  pl.BlockSpec(q_feat.shape, lambda i, j: (0, 0, 0)),
                  pl.BlockSpec(gexp.shape, lambda i, j: (0, 0, 0))],
        out_specs=pl.BlockSpec((1, tq, ATTN_WIDTH), lambda i, j: (i, j, 0)),
        out_shape=jax.ShapeDtypeStruct((b, t, ATTN_WIDTH), F32),
        scratch_shapes=[pltpu.VMEM((ng, GROUP * tq, cw), F32),
                        pltpu.VMEM((ng, GROUP * tq, cw), BF16), stat, stat, stat, stat, stat],
        compiler_params=pltpu.CompilerParams(dimension_semantics=("parallel", "parallel"),
                                             vmem_limit_bytes=_vmem_limit(48 << 20)),
        name="nsa_prompt",
    )(q, gates, cmpkv, att, k_feat, q_feat, gexp)


def _split3(x):
    hi = x.astype(BF16)
    r1 = x - hi.astype(F32)
    mid = r1.astype(BF16)
    lo = (r1 - mid.astype(F32)).astype(BF16)
    return hi, mid, lo


def _hgrn_prompt_body(rq_ref, f_ref, rv_ref, ltri_ref, bd_ref, o_ref, st_ref, s_scr, cum_scr, k_scr):
    c16 = RNN_CHUNK
    half = RNN_WIDTH // 2

    @pl.when(pl.program_id(1) == 0)
    def _():
        s_scr[...] = jnp.zeros(s_scr.shape, F32)

    f = f_ref[0]
    k_scr[...] = 1.0 - f
    hi, mid, lo = _split3(jnp.log(f))
    ltri = ltri_ref[...]
    cum_scr[...] = _dot(ltri, hi) + _dot(ltri, mid) + _dot(ltri, lo)

    ti = lax.broadcasted_iota(jnp.int32, (c16, c16, RNN_WIDTH), 0)
    si = lax.broadcasted_iota(jnp.int32, (c16, c16, RNN_WIDTH), 1)
    causal = si <= ti
    low = _low_half((c16 * c16, LANES))

    def chunk(c, carry):
        r0 = pl.multiple_of(c * c16, c16)
        cc = cum_scr[pl.ds(r0, c16), :]
        qc = rq_ref[0, pl.ds(r0, c16), :]
        kc = k_scr[pl.ds(r0, c16), :]
        vc = rv_ref[0, pl.ds(r0, c16), :]
        last = cc[c16 - 1:c16, :]
        qd = (qc * jnp.exp(cc)).astype(BF16)
        kd = (kc * jnp.exp(last - cc)).astype(BF16)
        dl = jnp.exp(last)
        vb = vc.astype(BF16)
        o_int = jnp.concatenate([_dot_nt(qd[:, h * half:(h + 1) * half], s_scr[h].astype(BF16))
                                 for h in range(2)], axis=1)
        dec = jnp.exp(jnp.where(causal, cc[:, None, :] - cc[None, :, :], NEG))
        prod = (qc[:, None, :] * kc[None, :, :] * dec).reshape(c16 * c16, RNN_WIDTH)
        outs = []
        for j in range(RNN_WIDTH // LANES):
            pj = prod[:, j * LANES:(j + 1) * LANES]
            s_lo = jnp.sum(jnp.where(low, pj, 0.0), axis=-1, keepdims=True)
            s_hi = jnp.sum(jnp.where(low, 0.0, pj), axis=-1, keepdims=True)
            outs.append(jnp.where(low, s_lo, s_hi))
        a = jnp.concatenate(outs, axis=1).reshape(c16, c16, RNN_WIDTH)
        o_intra = jnp.sum(a * vc[None, :, :], axis=1)
        o_ref[0, pl.ds(r0, c16), :] = o_int + o_intra
        for h in range(2):
            u = _dot_tn(vb[:, h * half:(h + 1) * half], kd[:, h * half:(h + 1) * half])
            s_scr[h] = s_scr[h] * dl[:, h * half:(h + 1) * half] + u * bd_ref[...]
        return carry

    lax.fori_loop(0, rq_ref.shape[1] // c16, chunk, 0)
    st_ref[0] = s_scr[...]


def _hgrn_prompt(rq, f, rv, ltri, bd):
    b, t, w = rq.shape
    tc = RNN_ROWS
    half = w // 2
    row = lambda i, j: (i, j, 0)
    return pl.pallas_call(
        _hgrn_prompt_body, grid=(b, t // tc),
        in_specs=[pl.BlockSpec((1, tc, w), row), pl.BlockSpec((1, tc, w), row), pl.BlockSpec((1, tc, w), row),
                  pl.BlockSpec(ltri.shape, lambda i, j: (0, 0)), pl.BlockSpec(bd.shape, lambda i, j: (0, 0))],
        out_specs=(pl.BlockSpec((1, tc, w), row), pl.BlockSpec((1, 2, half, half), lambda i, j: (i, 0, 0, 0))),
        out_shape=(jax.ShapeDtypeStruct((b, t, w), F32), jax.ShapeDtypeStruct((b, 2, half, half), F32)),
        scratch_shapes=[pltpu.VMEM((2, half, half), F32), pltpu.VMEM((tc, w), F32), pltpu.VMEM((tc, w), F32)],
        compiler_params=pltpu.CompilerParams(dimension_semantics=("parallel", "arbitrary"),
                                             vmem_limit_bytes=_vmem_limit(32 << 20)),
        name="hgrn_prompt",
    )(rq, f, rv, ltri, bd)


def _finish_body(x_ref, oa_ref, orn_ref, rgs_ref, ag_ref, rg_ref, wout_ref, lnm_ref, wup_ref, wdn_ref, y_ref, hn_scr):
    @pl.when(pl.program_id(1) == 0)
    def _():
        oa = oa_ref[...]
        orn = orn_ref[...]
        a_n = oa * lax.rsqrt(_head_mean_sq(oa) + EPS) * ag_ref[...]
        r_n = orn * lax.rsqrt(_head_mean_sq(orn) + EPS) * rg_ref[...] * rgs_ref[...]
        h = (x_ref[...] + _dot(a_n.astype(BF16), wout_ref[0:ATTN_WIDTH, :])
             + _dot(r_n.astype(BF16), wout_ref[ATTN_WIDTH:ATTN_WIDTH + RNN_WIDTH, :]))
        y_ref[...] = h
        hn_scr[...] = (_row_rms(h) * lnm_ref[...]).astype(BF16)

    u = jnp.maximum(_dot(hn_scr[...], wup_ref[...]), 0.0)
    y_ref[...] += _dot((u * u).astype(BF16), wdn_ref[...])


def _finish(x, oa, orn, rgs, ag, rg, wout, lnm, wup, wdn):
    n, dm = x.shape
    tm = FINISH_ROWS
    dff = wup.shape[1]
    row = lambda i, j: (i, 0)
    const = lambda i, j: (0, 0)
    return pl.pallas_call(
        _finish_body, grid=(n // tm, dff // FF_CHUNK),
        in_specs=[pl.BlockSpec((tm, dm), row), pl.BlockSpec((tm, ATTN_WIDTH), row),
                  pl.BlockSpec((tm, RNN_WIDTH), row), pl.BlockSpec((tm, RNN_WIDTH), row),
                  pl.BlockSpec(ag.shape, const), pl.BlockSpec(rg.shape, const),
                  pl.BlockSpec(wout.shape, const), pl.BlockSpec(lnm.shape, const),
                  pl.BlockSpec((dm, FF_CHUNK), lambda i, j: (0, j)),
                  pl.BlockSpec((FF_CHUNK, dm), lambda i, j: (j, 0))],
        out_specs=pl.BlockSpec((tm, dm), row),
        out_shape=jax.ShapeDtypeStruct((n, dm), F32),
        scratch_shapes=[pltpu.VMEM((tm, dm), BF16)],
        compiler_params=pltpu.CompilerParams(dimension_semantics=("parallel", "arbitrary"),
                                             vmem_limit_bytes=_vmem_limit(48 << 20)),
        name="finish",
    )(x, oa, orn, rgs, ag, rg, wout, lnm, wup, wdn)


def _proj_sample_body(xbt_ref, xtb_ref, ln_ref, wtok_ref, wtb_ref, wbt_ref, qg_ref, lbl_ref, gsel_ref, gwin_ref,
                      q_ref, rgs_ref, gates_ref, ztb_ref, zbt_ref):
    xb = (_row_rms(xbt_ref[...]) * ln_ref[...]).astype(BF16)
    xt = (_row_rms(xtb_ref[...]) * ln_ref[...]).astype(BF16)
    zq = _dot(xb, wtok_ref[:, 0:ATTN_WIDTH])
    q_ref[...] = (zq * lax.rsqrt(_head_mean_sq(zq) + EPS) * qg_ref[...]).astype(BF16)
    rgs_ref[...] = _silu(_dot(xb, wtok_ref[:, ATTN_WIDTH:ATTN_WIDTH + RNN_WIDTH]))
    gates_ref[...] = jax.nn.sigmoid(_dot(xb, wtok_ref[:, ATTN_WIDTH + RNN_WIDTH:ATTN_WIDTH + RNN_WIDTH + LANES]))

    d = HEAD_DIM
    zt = _dot_nt(wtb_ref[...], xt)
    ztb_ref[0:256] = zt[0:256]
    for g in range(N_KV_HEADS):
        ztb_ref[256 + g * d:256 + (g + 1) * d] = _col_head_norm(zt[256 + g * d:256 + (g + 1) * d], gsel_ref[...])
    ztb_ref[384:512] = zt[384:512]
    ztb_ref[512:1024] = _silu(zt[512:1024])
    lb = _lower_bound(lbl_ref[...], 0)[0]
    ztb_ref[1024:1536] = lb + (1.0 - lb) * jax.nn.sigmoid(zt[1024:1536])
    ztb_ref[1536:2048] = zt[1536:2048]

    zb = _dot_nt(wbt_ref[...], xb)
    for g in range(N_KV_HEADS):
        zbt_ref[g * d:(g + 1) * d] = _col_head_norm(zb[g * d:(g + 1) * d], gsel_ref[...])
        zbt_ref[256 + g * d:256 + (g + 1) * d] = _col_head_norm(zb[256 + g * d:256 + (g + 1) * d], gwin_ref[...])
    zbt_ref[128:256] = zb[128:256]
    zbt_ref[384:512] = zb[384:512]


def _proj_sample(xbt, xtb, ln, wtok, wtb, wbt, qg, lblt, gsel, gwin):
    n, dm = xbt.shape
    tm = PROJ_ROWS
    row = lambda i: (i, 0)
    col = lambda i: (0, i)
    const = lambda i: (0, 0)
    return pl.pallas_call(
        _proj_sample_body, grid=(n // tm,),
        in_specs=[pl.BlockSpec((tm, dm), row), pl.BlockSpec((tm, dm), row), pl.BlockSpec(ln.shape, const),
                  pl.BlockSpec(wtok.shape, const), pl.BlockSpec(wtb.shape, const), pl.BlockSpec(wbt.shape, const),
                  pl.BlockSpec(qg.shape, const), pl.BlockSpec(lblt.shape, lambda i: (0, 0, 0)),
                  pl.BlockSpec(gsel.shape, const), pl.BlockSpec(gwin.shape, const)],
        out_specs=(pl.BlockSpec((tm, ATTN_WIDTH), row), pl.BlockSpec((tm, RNN_WIDTH), row),
                   pl.BlockSpec((tm, LANES), row), pl.BlockSpec((wtb.shape[0], tm), col),
                   pl.BlockSpec((wbt.shape[0], tm), col)),
        out_shape=(jax.ShapeDtypeStruct((n, ATTN_WIDTH), BF16), jax.ShapeDtypeStruct((n, RNN_WIDTH), F32),
                   jax.ShapeDtypeStruct((n, LANES), F32), jax.ShapeDtypeStruct((wtb.shape[0], n), F32),
                   jax.ShapeDtypeStruct((wbt.shape[0], n), F32)),
        compiler_params=pltpu.CompilerParams(dimension_semantics=("parallel",),
                                             vmem_limit_bytes=_vmem_limit(40 << 20)),
        name="proj_sample",
    )(xbt, xtb, ln, wtok, wtb, wbt, qg, lblt, gsel, gwin)


PAGE_ROWS = 4 * KV_WIDTH


def _nsa_sample_body(pt_ref, cache_ref, q_ref, nkv_ref, w1_ref, pe_ref, w2_ref, kg_ref, e_ref,
                     ocs_ref, buf, kk_scr, vv_scr, sem, *, n_pages, past_len, dec_t, topk):
    b = pl.program_id(0)
    nbatch = pl.num_programs(0)
    slot = b % 2
    d = HEAD_DIM
    nblk = 2 * n_pages

    def page_copy(bb, sl, j):
        return pltpu.make_async_copy(cache_ref.at[pt_ref[bb * n_pages + j]],
                                     buf.at[sl, pl.ds(j * PAGE_ROWS, PAGE_ROWS)], sem.at[sl])

    @pl.when(b == 0)
    def _():
        for j in range(n_pages):
            page_copy(0, 0, j).start()

    @pl.when(b + 1 < nbatch)
    def _():
        for j in range(n_pages):
            page_copy(b + 1, 1 - slot, j).start()

    for j in range(n_pages):
        page_copy(b, slot, j).wait()

    low = _low_half((n_pages, LANES))

    def compress(c):
        acc = jnp.zeros((4 * n_pages, CMP_HIDDEN), F32)
        for dd in range(d):
            rows = []
            for g in range(N_KV_HEADS):
                xg = buf[slot, pl.ds((c * N_KV_HEADS + g) * d + dd, n_pages, stride=PAGE_ROWS), :]
                xg = xg + pe_ref[c, dd]
                rows += [jnp.where(low, xg, 0.0), jnp.where(low, 0.0, xg)]
            lhs = jnp.concatenate(rows, axis=0).astype(BF16)
            acc = acc + _dot(lhs, w1_ref[c, dd])
        return _dot(_silu(acc).astype(BF16), w2_ref[c])

    kc_all = _row_rms(compress(0)) * kg_ref[...]
    vc_all = compress(1)

    q = q_ref[0]
    nq = GROUP * dec_t
    rho = lax.broadcasted_iota(jnp.int32, (nblk, dec_t), 0)
    bid = 2 * (rho & (n_pages - 1)) + (rho >> _log2(n_pages))
    colq = lax.broadcasted_iota(jnp.int32, (1, nq), 1)
    rowq = lax.broadcasted_iota(jnp.int32, (nq, 1), 0)
    cur_blk = past_len // SEL_BLOCK
    o_cmp, o_sel = [], []
    for g in range(N_KV_HEADS):
        qst = _stack_group_queries(q, g)
        kc = kc_all[g * nblk:(g + 1) * nblk].astype(BF16)
        vc = vc_all[g * nblk:(g + 1) * nblk].astype(BF16)
        st = _dot_nt(kc, qst)
        qpos = past_len + (colq & (dec_t - 1))
        rho_q = lax.broadcasted_iota(jnp.int32, (nblk, 1), 0)
        end = (2 * (rho_q & (n_pages - 1)) + (rho_q >> _log2(n_pages))) * CMP_BLOCK + (CMP_BLOCK - 1)
        dist = (qpos - end).astype(F32)
        valid = dist >= 0.0
        s = jnp.where(valid, st - _row_slopes(g, colq >> _log2(dec_t)) * dist, NEG)
        e = jnp.exp(s - jnp.max(s, axis=0, keepdims=True))
        p = jnp.where(valid, e / jnp.sum(e, axis=0, keepdims=True), 0.0)
        o_cmp.append(_unstack_group(_dot_tn(p.astype(BF16), vc), dec_t))
        imp = p[:, 0:dec_t]
        for r in range(1, GROUP):
            imp = imp + p[:, r * dec_t:(r + 1) * dec_t]

        force = (bid == 0) | (bid == cur_blk)
        score = jnp.where(force, FORCED_SCORE, imp)
        rank = _block_ranks(score, bid, [(i, 2 * (i % n_pages) + i // n_pages) for i in range(nblk)])
        rank = rank + jnp.where(bid > cur_blk, (FORCED_SCORE >= score).astype(F32),
                                (FORCED_SCORE > score).astype(F32))
        msel = jnp.where((rank < topk) & (score >= 0.0), 1.0, 0.0).astype(BF16)

        for pg in range(n_pages):
            kt_ = buf[slot, pl.ds(pg * PAGE_ROWS + (2 * N_KV_HEADS + g) * d, d), :].astype(BF16)
            vt_ = buf[slot, pl.ds(pg * PAGE_ROWS + (3 * N_KV_HEADS + g) * d, d), :].astype(BF16)
            kk_scr[0:d, pg * PAGE_SIZE:(pg + 1) * PAGE_SIZE] = kt_
            kk_scr[d:2 * d, pg * PAGE_SIZE:(pg + 1) * PAGE_SIZE] = kt_
            vv_scr[0:d, pg * PAGE_SIZE:(pg + 1) * PAGE_SIZE] = vt_
            vv_scr[d:2 * d, pg * PAGE_SIZE:(pg + 1) * PAGE_SIZE] = vt_
        nk = _twice(nkv_ref[0, g * d:(g + 1) * d, :].astype(BF16))
        nv = _twice(nkv_ref[0, KV_WIDTH + g * d:KV_WIDTH + (g + 1) * d, :].astype(BF16))
        sp = _dot(qst, kk_scr[...])
        sn = _dot(qst, nk)
        mk = _dot_tn(msel, e_ref[...])
        okp = jnp.concatenate([mk] * GROUP, axis=0) > 0.5
        step = rowq & (dec_t - 1)
        slope = _row_slopes(g, rowq >> _log2(dec_t))
        kpos = lax.broadcasted_iota(jnp.int32, (1, past_len), 1)
        dpast = (past_len + step - kpos).astype(F32)
        lgp = jnp.where(okp & (dpast >= 0.0), sp - slope * dpast, NEG)
        lane = lax.broadcasted_iota(jnp.int32, (1, LANES), 1)
        dnew = (step - lane).astype(F32)
        lgn = jnp.where((lane < dec_t) & (dnew >= 0.0), sn - slope * dnew, NEG)
        m = jnp.maximum(jnp.max(lgp, axis=-1, keepdims=True), jnp.max(lgn, axis=-1, keepdims=True))
        pp = jnp.exp(lgp - m)
        pn = jnp.exp(lgn - m)
        den = jnp.sum(pp, axis=-1, keepdims=True) + jnp.sum(pn, axis=-1, keepdims=True)
        o2 = _dot_nt(pp.astype(BF16), vv_scr[...]) + _dot_nt(pn.astype(BF16), nv)
        o_sel.append(_unstack_group(o2 / den, dec_t))

    ocs_ref[0, 0] = jnp.concatenate(o_cmp, axis=1)
    ocs_ref[0, 1] = jnp.concatenate(o_sel, axis=1)


def _nsa_sample(page_flat, cache, q, nkv, w1r, pe_t, w2dup, kg_dup, e_perm, past_len, topk):
    nbatch, dec_t, _ = q.shape
    n_pages = past_len // PAGE_SIZE
    body = functools.partial(_nsa_sample_body, n_pages=n_pages, past_len=past_len, dec_t=dec_t, topk=topk)
    grid_spec = pltpu.PrefetchScalarGridSpec(
        num_scalar_prefetch=1, grid=(nbatch,),
        in_specs=[pl.BlockSpec(memory_space=pl.ANY),
                  pl.BlockSpec((1, dec_t, ATTN_WIDTH), lambda i, pt: (i, 0, 0)),
                  pl.BlockSpec((1,) + nkv.shape[1:], lambda i, pt: (i, 0, 0)),
                  pl.BlockSpec(w1r.shape, lambda i, pt: (0, 0, 0, 0)),
                  pl.BlockSpec(pe_t.shape, lambda i, pt: (0, 0, 0, 0)),
                  pl.BlockSpec(w2dup.shape, lambda i, pt: (0, 0, 0)),
                  pl.BlockSpec(kg_dup.shape, lambda i, pt: (0, 0)),
                  pl.BlockSpec(e_perm.shape, lambda i, pt: (0, 0))],
        out_specs=pl.BlockSpec((1, 2, dec_t, ATTN_WIDTH), lambda i, pt: (i, 0, 0, 0)),
        scratch_shapes=[pltpu.VMEM((2, n_pages * PAGE_ROWS, PAGE_SIZE), F32),
                        pltpu.VMEM((2 * HEAD_DIM, past_len), BF16), pltpu.VMEM((2 * HEAD_DIM, past_len), BF16),
                        pltpu.SemaphoreType.DMA((2,))])
    return pl.pallas_call(
        body, grid_spec=grid_spec,
        out_shape=jax.ShapeDtypeStruct((nbatch, 2, dec_t, ATTN_WIDTH), F32),
        compiler_params=pltpu.CompilerParams(dimension_semantics=("arbitrary",),
                                             vmem_limit_bytes=_vmem_limit(40 << 20)),
        name="nsa_sample",
    )(page_flat, cache, q, nkv, w1r, pe_t, w2dup, kg_dup, e_perm)


def _win_sample_body(win_ref, q_ref, nw_ref, ocs_ref, gates_ref, gexp_ref, oa_ref, wout_ref, *, past_len, dec_t):
    d = HEAD_DIM
    wbuf = win_ref.shape[2]
    q = q_ref[0]
    nq = GROUP * dec_t
    rowq = lax.broadcasted_iota(jnp.int32, (nq, 1), 0)
    step = rowq & (dec_t - 1)
    kpos = past_len - wbuf + lax.broadcasted_iota(jnp.int32, (1, wbuf), 1)
    dpast = (past_len + step - kpos).astype(F32)
    okp = (dpast >= 0.0) & (dpast < WINDOW)
    lane = lax.broadcasted_iota(jnp.int32, (1, LANES), 1)
    dnew = (step - lane).astype(F32)
    okn = (lane < dec_t) & (dnew >= 0.0) & (dnew < WINDOW)
    o_win = []
    for g in range(N_KV_HEADS):
        qst = _stack_group_queries(q, g)
        kt_ = win_ref[0, g * d:(g + 1) * d, :].astype(BF16)
        vt_ = win_ref[0, KV_WIDTH + g * d:KV_WIDTH + (g + 1) * d, :].astype(BF16)
        nk = _twice(nw_ref[0, g * d:(g + 1) * d, :].astype(BF16))
        nv = _twice(nw_ref[0, KV_WIDTH + g * d:KV_WIDTH + (g + 1) * d, :].astype(BF16))
        slope = _row_slopes(g, rowq >> _log2(dec_t))
        lgp = jnp.where(okp, _dot(qst, _twice(kt_)) - slope * dpast, NEG)
        lgn = jnp.where(okn, _dot(qst, nk) - slope * dnew, NEG)
        m = jnp.maximum(jnp.max(lgp, axis=-1, keepdims=True), jnp.max(lgn, axis=-1, keepdims=True))
        pp = jnp.exp(lgp - m)
        pn = jnp.exp(lgn - m)
        den = jnp.sum(pp, axis=-1, keepdims=True) + jnp.sum(pn, axis=-1, keepdims=True)
        o2 = _dot_nt(pp.astype(BF16), _twice(vt_)) + _dot_nt(pn.astype(BF16), nv)
        o_win.append(_unstack_group(o2 / den, dec_t))
    ge = _expand_gates(gates_ref[0], gexp_ref)
    oa_ref[0] = ge[0] * ocs_ref[0, 0] + ge[1] * ocs_ref[0, 1] + ge[2] * jnp.concatenate(o_win, axis=1)

    old = win_ref[0]
    rolled = pltpu.roll(old, wbuf - dec_t, 1)
    newr = pltpu.roll(nw_ref[0], LANES - dec_t, 1)
    wout_ref[0, :, 0:wbuf - LANES] = rolled[:, 0:wbuf - LANES]
    wout_ref[0, :, wbuf - LANES:wbuf] = jnp.where(lane >= LANES - dec_t, newr, rolled[:, wbuf - LANES:wbuf])


def _win_sample(win, q, nw, ocs, gates, gexp, past_len):
    nbatch, feat, wbuf = win.shape
    dec_t = q.shape[1]
    body = functools.partial(_win_sample_body, past_len=past_len, dec_t=dec_t)
    b3 = lambda i: (i, 0, 0)
    return pl.pallas_call(
        body, grid=(nbatch,),
        in_specs=[pl.BlockSpec((1, feat, wbuf), b3), pl.BlockSpec((1, dec_t, ATTN_WIDTH), b3),
                  pl.BlockSpec((1, feat, LANES), b3), pl.BlockSpec((1, 2, dec_t, ATTN_WIDTH), lambda i: (i, 0, 0, 0)),
                  pl.BlockSpec((1, dec_t, LANES), b3), pl.BlockSpec(gexp.shape, lambda i: (0, 0, 0))],
        out_specs=(pl.BlockSpec((1, dec_t, ATTN_WIDTH), b3), pl.BlockSpec((1, feat, wbuf), b3)),
        out_shape=(jax.ShapeDtypeStruct((nbatch, dec_t, ATTN_WIDTH), F32),
                   jax.ShapeDtypeStruct((nbatch, feat, wbuf), F32)),
        compiler_params=pltpu.CompilerParams(dimension_semantics=("parallel",),
                                             vmem_limit_bytes=_vmem_limit(24 << 20)),
        name="win_sample",
    )(win, q, nw, ocs, gates, gexp)


def _hgrn_sample_body(q_ref, f_ref, v_ref, s_ref, o_ref, so_ref, *, dec_t):
    nb = s_ref.shape[3]
    o_ref[...] = jnp.zeros(o_ref.shape, F32)

    sub = 8

    def per_tile(i, carry):
        r0 = pl.multiple_of(i * sub, sub)
        f_t = [f_ref[pl.ds(r0, sub), pl.ds(t * nb, nb)] for t in range(dec_t)]
        q_t = [q_ref[pl.ds(r0, sub), pl.ds(t * nb, nb)] for t in range(dec_t)]
        for j in range(sub):
            s = s_ref[0, r0 + j]
            for t in range(dec_t):
                cols = pl.ds(t * nb, nb)
                fr = f_t[t][j:j + 1, :]
                s = fr * s + (1.0 - fr) * v_ref[:, cols]
                o_ref[:, cols] = o_ref[:, cols] + s * q_t[t][j:j + 1, :]
            so_ref[0, r0 + j] = s
        return carry

    lax.fori_loop(0, s_ref.shape[1] // sub, per_tile, 0)


def _hgrn_sample(ztb, state, dec_t):
    nh, dk, dv, nb = state.shape
    n = ztb.shape[1]
    body = functools.partial(_hgrn_sample_body, dec_t=dec_t)
    q0, f0, v0 = 512 // dk, 1024 // dk, 1536 // dk
    return pl.pallas_call(
        body, grid=(nh,),
        in_specs=[pl.BlockSpec((dk, n), lambda h: (q0 + h, 0)), pl.BlockSpec((dk, n), lambda h: (f0 + h, 0)),
                  pl.BlockSpec((dv, n), lambda h: (v0 + h, 0)),
                  pl.BlockSpec((1, dk, dv, nb), lambda h: (h, 0, 0, 0))],
        out_specs=(pl.BlockSpec((dv, n), lambda h: (h, 0)), pl.BlockSpec((1, dk, dv, nb), lambda h: (h, 0, 0, 0))),
        out_shape=(jax.ShapeDtypeStruct((nh * dv, n), F32), jax.ShapeDtypeStruct(state.shape, F32)),
        compiler_params=pltpu.CompilerParams(dimension_semantics=("parallel",),
                                             vmem_limit_bytes=_vmem_limit(24 << 20)),
        name="hgrn_sample",
    )(ztb, ztb, ztb, state)


def _gate_expander():
    m = np.zeros((N_BRANCH, LANES, ATTN_WIDTH), np.float32)
    for br in range(N_BRANCH):
        for h in range(N_ATTN_HEADS):
            m[br, h * N_BRANCH + br, h * HEAD_DIM:(h + 1) * HEAD_DIM] = 1.0
    return jnp.asarray(m, BF16)


def _block_expander(block_ids, n_keys):
    key_blk = np.arange(n_keys) // SEL_BLOCK
    return jnp.asarray((np.asarray(block_ids)[:, None] == key_blk[None, :]).astype(np.float32), BF16)


def _key_features(n_keys):
    s = np.arange(n_keys)
    m = np.zeros((LANES, n_keys), np.float32)
    m[0:HEAD_DIM] = (np.arange(HEAD_DIM)[:, None] == (s // SEL_BLOCK)[None, :])
    m[HEAD_DIM] = s // SEL_BLOCK
    m[HEAD_DIM + 1] = s % SEL_BLOCK
    return jnp.asarray(m, BF16)


def _query_slope_features(rows):
    m = np.zeros((N_KV_HEADS, GROUP * rows, LANES), np.float32)
    for g in range(N_KV_HEADS):
        for r in range(GROUP):
            m[g, r * rows:(r + 1) * rows, HEAD_DIM] = SLOPES[g][r] * SEL_BLOCK
            m[g, r * rows:(r + 1) * rows, HEAD_DIM + 1] = SLOPES[g][r]
    return jnp.asarray(m, F32)


def _chunk_lower_tri(n, c):
    i = np.arange(n)
    return jnp.asarray(((i[:, None] // c == i[None, :] // c) & (i[None, :] <= i[:, None])).astype(np.float32), BF16)


def _head_block_diag(n):
    i = np.arange(n)
    return jnp.asarray((i[:, None] // RNN_DV == i[None, :] // RNN_DK).astype(np.float32), F32)


def kernel(x_prompt, x_sample, cache_kv, cache_win, state_rnn, page_table, ln_mix, w_in, q_norm, k_norm, cmp_pe,
           cmp_w1, cmp_w2, attn_out_norm, rnn_lb_logits, rnn_out_norm, w_out, ln_mlp, w_up, w_down):
    assert w_in.shape[0] == 1, "single layer"
    b, t, dm = x_prompt.shape
    nbatch, dec_t, _ = x_sample.shape
    n_pool = cache_kv.shape[1]
    n_pages = page_table.shape[1]
    past_len = n_pages * PAGE_SIZE
    wbuf = cache_win.shape[2]
    assert t % PROJ_ROWS == 0 and t % RNN_ROWS == 0 and t % min(SEL_KEYS, t) == 0 and t >= WINDOW
    assert (b * t) % FINISH_ROWS == 0 and (nbatch * dec_t) % FINISH_ROWS == 0 and w_up.shape[2] % FF_CHUNK == 0
    assert (nbatch * dec_t) % PROJ_ROWS == 0 and nbatch == LANES and dec_t <= 8
    assert past_len % SEL_BLOCK == 0 and wbuf == WINDOW and wbuf >= LANES

    w = w_in[0]
    c_kv, c_gate, c_rq, c_rf, c_ri, c_rg = ATTN_WIDTH, ATTN_WIDTH + 6 * KV_WIDTH, 1304, 1816, 2328, 2840
    gate_cols = jnp.pad(w[:, c_gate:c_rq], ((0, 0), (0, LANES - N_ATTN_HEADS * N_BRANCH)))
    wtok = jnp.concatenate([w[:, 0:ATTN_WIDTH], w[:, c_rq:], gate_cols, w[:, c_kv:c_kv + 2 * KV_WIDTH]],
                           axis=1).astype(BF16)
    wft = w[:, c_kv:c_gate].T.astype(BF16)
    wtok_s = jnp.concatenate([w[:, 0:ATTN_WIDTH], w[:, c_rg:], gate_cols], axis=1).astype(BF16)
    wtb_s = jnp.concatenate([w[:, c_kv:c_kv + 4 * KV_WIDTH], w[:, c_rq:c_rg]], axis=1).T.astype(BF16)
    wbt_s = w[:, c_kv + 2 * KV_WIDTH:c_gate].T.astype(BF16)
    ln = ln_mix[0][None, :]
    qg = (jnp.tile(q_norm[0], N_ATTN_HEADS) * SCALE)[None, :]
    lbl = rnn_lb_logits.astype(F32)
    lblt = jnp.broadcast_to(lbl[:, :, None], lbl.shape + (PROJ_ROWS,))
    gsel = jnp.broadcast_to(k_norm[0, 1][:, None], (HEAD_DIM, PROJ_ROWS))
    gwin = jnp.broadcast_to(k_norm[0, 2][:, None], (HEAD_DIM, PROJ_ROWS))
    kg_dup = jnp.tile(k_norm[0, 0], 2)[None, :]
    pe = cmp_pe[0]
    pe_tok = jnp.tile(jnp.concatenate([jnp.tile(pe[0], (1, N_KV_HEADS)), jnp.tile(pe[1], (1, N_KV_HEADS))], axis=1),
                      (PROJ_ROWS // CMP_BLOCK, 1))
    w1 = cmp_w1[0].reshape(2, CMP_BLOCK, HEAD_DIM, CMP_HIDDEN)
    zeros = jnp.zeros_like(w1)
    w1bd = jnp.concatenate([jnp.concatenate([w1, zeros], axis=3), jnp.concatenate([zeros, w1], axis=3)],
                           axis=2).astype(BF16)
    w1r = jnp.tile(w1.transpose(0, 2, 1, 3), (1, 1, 2, 1)).astype(BF16)
    pe_t = jnp.tile(pe.transpose(0, 2, 1), (1, 1, 2))[:, :, None, :]
    w2dup = jnp.tile(cmp_w2[0], (1, 1, 2)).astype(BF16)
    ag = attn_out_norm[0][None, :]
    rg = rnn_out_norm[0][None, :]
    wout = w_out[0].astype(BF16)
    lnm = ln_mlp[0][None, :]
    wup = w_up[0].astype(BF16)
    wdn = w_down[0].astype(BF16)
    gexp = _gate_expander()

    (q_p, gates_p, rq_p, f_p, rv_p, rgs_p, kvc_p, kvt_p, wint_p, att_p) = _proj_prompt(
        x_prompt, ln, wtok, wft, qg, lbl, pe_tok, gsel, gwin)
    cmp_p = _compress_prompt(kvc_p, w1bd, w2dup, kg_dup)
    nb_p = t // CMP_BLOCK
    oa_p = _nsa_prompt(q_p, gates_p, cmp_p, att_p, _key_features(t), _query_slope_features(NSA_Q_ROWS), gexp)
    orn_p, st_p = _hgrn_prompt(rq_p, f_p, rv_p, _chunk_lower_tri(RNN_ROWS, RNN_CHUNK), _head_block_diag(RNN_WIDTH // 2))
    y_p = _finish(x_prompt.reshape(b * t, dm), oa_p.reshape(b * t, ATTN_WIDTH), orn_p.reshape(b * t, RNN_WIDTH),
                  rgs_p.reshape(b * t, RNN_WIDTH), ag, rg, wout, lnm, wup, wdn).reshape(b, t, dm)
    kv_prompt = kvt_p.reshape(1, b, 4, N_KV_HEADS, HEAD_DIM, t).transpose(0, 1, 5, 2, 3, 4)
    wlen = min(WINDOW, t)
    win_prompt = wint_p[:, :, t - wlen:].reshape(1, b, 2, N_KV_HEADS, HEAD_DIM, wlen).transpose(0, 1, 5, 2, 3, 4)
    hh = RNN_WIDTH // 2 // RNN_DV
    st5 = st_p.reshape(b, 2, hh, RNN_DV, hh, RNN_DK)
    rnn_prompt = jnp.stack([st5[:, :, i, :, i, :] for i in range(hh)], axis=2)
    rnn_prompt = rnn_prompt.reshape(b, N_RNN_HEADS, RNN_DV, RNN_DK).transpose(0, 1, 3, 2)[None]

    n_s = nbatch * dec_t
    xbt = x_sample.reshape(n_s, dm)
    xtb = x_sample.transpose(1, 0, 2).reshape(n_s, dm)
    q_s, rgs_s, gates_s, ztb, zbt = _proj_sample(xbt, xtb, ln, wtok_s, wtb_s, wbt_s, qg, lblt, gsel, gwin)
    kv_sample = ztb[0:4 * KV_WIDTH].reshape(4, N_KV_HEADS, HEAD_DIM, dec_t, nbatch).transpose(4, 3, 0, 1, 2)[None]
    new_rows = jnp.pad(zbt.reshape(4 * KV_WIDTH, nbatch, dec_t).transpose(1, 0, 2),
                       ((0, 0), (0, 0), (0, LANES - dec_t)))
    cache = cache_kv[0].transpose(0, 2, 3, 4, 1).reshape(n_pool, PAGE_ROWS, PAGE_SIZE)
    nblk_s = past_len // CMP_BLOCK
    rho = np.arange(nblk_s)
    e_perm = _block_expander(2 * (rho % n_pages) + rho // n_pages, past_len)
    ns_s = -(-(past_len + dec_t) // SEL_BLOCK)
    q_s3 = q_s.reshape(nbatch, dec_t, ATTN_WIDTH)
    ocs = _nsa_sample(page_table.reshape(-1), cache, q_s3, new_rows[:, 0:2 * KV_WIDTH], w1r, pe_t, w2dup, kg_dup,
                      e_perm, past_len, min(SEL_TOPK, ns_s))
    win = cache_win[0].transpose(0, 2, 3, 4, 1).reshape(nbatch, 2 * KV_WIDTH, wbuf)
    oa_s, win_new = _win_sample(win, q_s3, new_rows[:, 2 * KV_WIDTH:], ocs,
                                gates_s.reshape(nbatch, dec_t, LANES), gexp, past_len)
    win_sample = win_new.reshape(1, nbatch, 2, N_KV_HEADS, HEAD_DIM, wbuf).transpose(0, 1, 5, 2, 3, 4)
    state = state_rnn[0].transpose(1, 2, 3, 0)
    orn_t, state_new = _hgrn_sample(ztb, state, dec_t)
    rnn_sample = state_new.transpose(3, 0, 1, 2)[None]
    orn_s = orn_t.reshape(RNN_WIDTH, dec_t, nbatch).transpose(2, 1, 0).reshape(n_s, RNN_WIDTH)
    y_s = _finish(xbt, oa_s.reshape(n_s, ATTN_WIDTH), orn_s, rgs_s, ag, rg, wout, lnm, wup, wdn).reshape(nbatch, dec_t, dm)

    return (y_p, y_s, kv_prompt, kv_sample, win_prompt, win_sample, rnn_prompt, rnn_sample)
```

```python
import functools

import numpy as np
import jax
import jax.numpy as jnp
from jax import lax
from jax.experimental import pallas as pl
from jax.experimental.pallas import tpu as pltpu

F32 = jnp.float32
BF16 = jnp.bfloat16

HEAD_DIM = 64
N_ATTN_HEADS = 8
N_KV_HEADS = 2
GROUP = N_ATTN_HEADS // N_KV_HEADS
N_RNN_HEADS = 8
RNN_DK = 64
RNN_DV = 64
ATTN_WIDTH = N_ATTN_HEADS * HEAD_DIM
RNN_WIDTH = N_RNN_HEADS * RNN_DV
KV_WIDTH = N_KV_HEADS * HEAD_DIM
N_BRANCH = 3
CMP_BLOCK = 64
SEL_BLOCK = 64
SEL_TOPK = 16
WINDOW = 512
CMP_HIDDEN = 128
PAGE_SIZE = 128
SCALE = HEAD_DIM ** -0.5
EPS = 1e-6
NEG = -1e30
LOG2E = 1.4426950408889634
FORCED_SCORE = GROUP + 1.0
SLOPES = [[2.0 ** (-(g * GROUP + r + 1)) for r in range(GROUP)] for g in range(N_KV_HEADS)]

LANES = 128
VMEM_BYTES_V7X = 64 * 1024 * 1024

PROJ_ROWS = 256
NSA_Q_ROWS = 128
SEL_KEYS = 512
ROW_BLOCK = 16
RNN_ROWS = 256
RNN_CHUNK = 16
FF_CHUNK = 1024
FINISH_ROWS = 512

NT = (((1,), (1,)), ((), ()))
TN = (((0,), (0,)), ((), ()))


def _vmem_limit(nbytes):
    return int(min(VMEM_BYTES_V7X - (8 << 20), max(nbytes, 16 << 20)))


def _dot(a, b):
    return jnp.dot(a, b, preferred_element_type=F32)


def _dot_nt(a, b):
    return lax.dot_general(a, b, NT, preferred_element_type=F32)


def _dot_tn(a, b):
    return lax.dot_general(a, b, TN, preferred_element_type=F32)


def _low_half(shape):
    lane = lax.broadcasted_iota(jnp.int32, shape, len(shape) - 1)
    return (lane & HEAD_DIM) == 0


def _head_mean_sq(x):
    outs = []
    for j in range(x.shape[-1] // LANES):
        blk = x[:, j * LANES:(j + 1) * LANES]
        sq = blk * blk
        low = _low_half(blk.shape)
        s_lo = jnp.sum(jnp.where(low, sq, 0.0), axis=-1, keepdims=True)
        s_hi = jnp.sum(jnp.where(low, 0.0, sq), axis=-1, keepdims=True)
        outs.append(jnp.where(low, s_lo, s_hi))
    return jnp.concatenate(outs, axis=-1) * (1.0 / HEAD_DIM)


def _row_rms(x):
    return x * lax.rsqrt(jnp.mean(x * x, axis=-1, keepdims=True) + EPS)


def _col_head_norm(rows, gain):
    ms = jnp.mean(rows * rows, axis=0, keepdims=True)
    return rows * lax.rsqrt(ms + EPS) * gain


def _silu(z):
    return z * jax.nn.sigmoid(z)


def _lower_bound(logits, axis):
    m = jnp.max(logits, axis=axis, keepdims=True)
    e = jnp.exp(logits - m)
    lb = e / jnp.sum(e, axis=axis, keepdims=True)
    return lb[0:1] if axis == 0 else lb


def _stack_group_queries(q, g):
    rows = q.shape[0]
    low = _low_half((rows, LANES))
    zero = jnp.zeros((rows, LANES), q.dtype)
    pa = q[:, g * 2 * LANES:g * 2 * LANES + LANES]
    pb = q[:, g * 2 * LANES + LANES:(g + 1) * 2 * LANES]
    return jnp.concatenate([jnp.where(low, pa, zero), jnp.where(low, zero, pa),
                            jnp.where(low, pb, zero), jnp.where(low, zero, pb)], axis=0)


def _unstack_group(o2, rows):
    low = _low_half((rows, LANES))
    return jnp.concatenate([jnp.where(low, o2[0:rows], o2[rows:2 * rows]),
                            jnp.where(low, o2[2 * rows:3 * rows], o2[3 * rows:4 * rows])], axis=1)


def _log2(n):
    assert n > 0 and n & (n - 1) == 0, n
    return n.bit_length() - 1


def _twice(x):
    return jnp.concatenate([x, x], axis=0)


def _row_slopes(g, row_head):
    s = jnp.full(row_head.shape, SLOPES[g][GROUP - 1], F32)
    for r in range(GROUP - 2, -1, -1):
        s = jnp.where(row_head == r, SLOPES[g][r], s)
    return s


def _block_ranks(score, ids, cand):
    rank = jnp.zeros(score.shape, F32)
    for row, id_i in cand:
        s_i = score[row:row + 1, :]
        rank = rank + jnp.where(ids > id_i, (s_i >= score).astype(F32), (s_i > score).astype(F32))
    return rank


def _expand_gates(gates, gexp_ref):
    hi = gates.astype(BF16)
    lo = (gates - hi.astype(F32)).astype(BF16)
    return [_dot(hi, gexp_ref[br]) + _dot(lo, gexp_ref[br]) for br in range(N_BRANCH)]


TOK_Q, TOK_RQ, TOK_RF, TOK_RI, TOK_RG, TOK_GATE, TOK_KVC, TOK_END = 0, 512, 1024, 1536, 2048, 2560, 2688, 2944


def _proj_prompt_body(x_ref, ln_ref, wtok_ref, wft_ref, qg_ref, lbl_ref, pe_ref, gsel_ref, gwin_ref,
                      q_ref, gates_ref, rq_ref, f_ref, rv_ref, rgs_ref, kvc_ref, kvt_ref, wint_ref, att_ref):
    xb = (_row_rms(x_ref[0]) * ln_ref[...]).astype(BF16)

    def tok(lo, hi):
        return _dot(xb, wtok_ref[:, lo:hi])

    zq = tok(TOK_Q, TOK_RQ)
    q_ref[0] = (zq * lax.rsqrt(_head_mean_sq(zq) + EPS) * qg_ref[...]).astype(BF16)
    rq_ref[0] = _silu(tok(TOK_RQ, TOK_RF))
    lb = _lower_bound(lbl_ref[...], 0)
    f_ref[0] = lb + (1.0 - lb) * jax.nn.sigmoid(tok(TOK_RF, TOK_RI))
    rv_ref[0] = tok(TOK_RI, TOK_RG)
    rgs_ref[0] = _silu(tok(TOK_RG, TOK_GATE))
    gates_ref[0] = jax.nn.sigmoid(tok(TOK_GATE, TOK_KVC))
    zc = tok(TOK_KVC, TOK_END) + pe_ref[...]
    kvc_ref[0, 0] = zc[:, 0:LANES]
    kvc_ref[1, 0] = zc[:, LANES:2 * LANES]

    zf = _dot_nt(wft_ref[...], xb)
    d = HEAD_DIM
    ksel = jnp.concatenate([_col_head_norm(zf[256 + g * d:256 + (g + 1) * d], gsel_ref[...])
                            for g in range(N_KV_HEADS)], axis=0)
    kwin = jnp.concatenate([_col_head_norm(zf[512 + g * d:512 + (g + 1) * d], gwin_ref[...])
                            for g in range(N_KV_HEADS)], axis=0)
    vsel = zf[384:512]
    vwin = zf[640:768]
    kvt_ref[0, 0:256] = zf[0:256]
    kvt_ref[0, 256:384] = ksel
    kvt_ref[0, 384:512] = vsel
    wint_ref[0, 0:128] = kwin
    wint_ref[0, 128:256] = vwin
    att_ref[0, 0:128] = ksel.astype(BF16)
    att_ref[0, 128:256] = vsel.astype(BF16)
    att_ref[0, 256:384] = kwin.astype(BF16)
    att_ref[0, 384:512] = vwin.astype(BF16)


def _proj_prompt(x, ln, wtok, wft, qg, lbl, pe_tok, gsel, gwin):
    b, t, dm = x.shape
    tm = PROJ_ROWS
    grid = (b, t // tm)
    row = lambda i, j: (i, j, 0)
    col = lambda i, j: (i, 0, j)
    const2 = lambda i, j: (0, 0)
    out_shape = (
        jax.ShapeDtypeStruct((b, t, ATTN_WIDTH), BF16),
        jax.ShapeDtypeStruct((b, t, LANES), F32),
        jax.ShapeDtypeStruct((b, t, RNN_WIDTH), F32),
        jax.ShapeDtypeStruct((b, t, RNN_WIDTH), F32),
        jax.ShapeDtypeStruct((b, t, RNN_WIDTH), F32),
        jax.ShapeDtypeStruct((b, t, RNN_WIDTH), F32),
        jax.ShapeDtypeStruct((2, b, t, LANES), F32),
        jax.ShapeDtypeStruct((b, 4 * KV_WIDTH, t), F32),
        jax.ShapeDtypeStruct((b, 2 * KV_WIDTH, t), F32),
        jax.ShapeDtypeStruct((b, 4 * KV_WIDTH, t), BF16),
    )
    out_specs = (
        pl.BlockSpec((1, tm, ATTN_WIDTH), row), pl.BlockSpec((1, tm, LANES), row),
        pl.BlockSpec((1, tm, RNN_WIDTH), row), pl.BlockSpec((1, tm, RNN_WIDTH), row),
        pl.BlockSpec((1, tm, RNN_WIDTH), row), pl.BlockSpec((1, tm, RNN_WIDTH), row),
        pl.BlockSpec((2, 1, tm, LANES), lambda i, j: (0, i, j, 0)),
        pl.BlockSpec((1, 4 * KV_WIDTH, tm), col), pl.BlockSpec((1, 2 * KV_WIDTH, tm), col),
        pl.BlockSpec((1, 4 * KV_WIDTH, tm), col),
    )
    in_specs = [
        pl.BlockSpec((1, tm, dm), row), pl.BlockSpec(ln.shape, const2),
        pl.BlockSpec(wtok.shape, const2), pl.BlockSpec(wft.shape, const2),
        pl.BlockSpec(qg.shape, const2), pl.BlockSpec(lbl.shape, const2), pl.BlockSpec(pe_tok.shape, const2),
        pl.BlockSpec(gsel.shape, const2), pl.BlockSpec(gwin.shape, const2),
    ]
    return pl.pallas_call(
        _proj_prompt_body, grid=grid, in_specs=in_specs, out_specs=out_specs, out_shape=out_shape,
        compiler_params=pltpu.CompilerParams(dimension_semantics=("parallel", "parallel"),
                                             vmem_limit_bytes=_vmem_limit(48 << 20)),
        name="proj_prompt",
    )(x, ln, wtok, wft, qg, lbl, pe_tok, gsel, gwin)


def _compress_prompt_body(x_ref, w1_ref, w2_ref, kg_ref, out_ref):
    c = pl.program_id(0)
    nb = out_ref.shape[2]
    acc = jnp.zeros((nb, 2 * CMP_HIDDEN), F32)
    for pos in range(CMP_BLOCK):
        xp = x_ref[0, 0, pl.ds(pos, nb, stride=CMP_BLOCK), :]
        acc = acc + _dot(xp.astype(BF16), w1_ref[0, pos])
    hb = _silu(acc).astype(BF16)
    outs = []
    for g in range(N_KV_HEADS):
        y = _dot(hb[:, g * CMP_HIDDEN:(g + 1) * CMP_HIDDEN], w2_ref[0])
        yn = _row_rms(y) * kg_ref[...]
        outs.append(jnp.where(c == 0, yn, y))
    out_ref[0, 0] = jnp.concatenate(outs, axis=1)


def _compress_prompt(kvc, w1bd, w2dup, kg_dup):
    _, b, t, _ = kvc.shape
    nb = t // CMP_BLOCK
    return pl.pallas_call(
        _compress_prompt_body, grid=(2, b),
        in_specs=[pl.BlockSpec((1, 1, t, LANES), lambda c, i: (c, i, 0, 0)),
                  pl.BlockSpec((1,) + w1bd.shape[1:], lambda c, i: (c, 0, 0, 0)),
                  pl.BlockSpec((1,) + w2dup.shape[1:], lambda c, i: (c, 0, 0)),
                  pl.BlockSpec(kg_dup.shape, lambda c, i: (0, 0))],
        out_specs=pl.BlockSpec((1, 1, nb, 2 * LANES), lambda c, i: (c, i, 0, 0)),
        out_shape=jax.ShapeDtypeStruct((2, b, nb, 2 * LANES), F32),
        compiler_params=pltpu.CompilerParams(dimension_semantics=("arbitrary", "arbitrary"),
                                             vmem_limit_bytes=_vmem_limit(32 << 20)),
        name="compress_prompt",
    )(kvc, w1bd, w2dup, kg_dup)


def _softmax_update(sc_scr, p_scr, m_scr, mn_scr, l_scr, a_scr, ncol, tq):
    rb = ROW_BLOCK
    for i in range(GROUP * tq // rb):
        rows = slice(i * rb, (i + 1) * rb)
        mn = mn_scr[rows, :]
        tot = jnp.zeros((rb, LANES), F32)
        for j in range(ncol):
            cols = slice(j * LANES, (j + 1) * LANES)
            p = jnp.exp(sc_scr[rows, cols] - mn)
            tot = tot + p
            p_scr[rows, cols] = p.astype(BF16)
        alpha = jnp.exp(m_scr[rows, :] - mn)
        l_scr[rows, :] = alpha * l_scr[rows, :] + jnp.sum(tot, axis=-1, keepdims=True)
        a_scr[rows, :] = alpha
        m_scr[rows, :] = mn


def _nsa_prompt_body(q_ref, gates_ref, cmp_ref, att_ref, kfeat_ref, qfeat_ref, gexp_ref, oa_ref,
                     sc_scr, p_scr, m_scr, mn_scr, l_scr, a_scr, acc_scr, *, seq, topk):
    tq = NSA_Q_ROWS
    tk = min(SEL_KEYS, seq)
    nb = cmp_ref.shape[2]
    t0 = pl.program_id(1) * tq
    wk = min(WINDOW + tq, seq)
    ws = pl.multiple_of(jnp.clip(t0 - WINDOW, 0, seq - wk), LANES)
    q = q_ref[0]
    d = HEAD_DIM
    groups = range(N_KV_HEADS)
    o_cmp, o_win, qsel, qwin = [], [], [], []
    for g in groups:
        qst = _stack_group_queries(q, g)

        kc = cmp_ref[0, 0][:, g * LANES:(g + 1) * LANES].astype(BF16)
        vc = cmp_ref[1, 0][:, g * LANES:(g + 1) * LANES].astype(BF16)
        st = _dot_nt(kc, qst)
        blk = lax.broadcasted_iota(jnp.int32, (nb, GROUP * tq), 0)
        colq = lax.broadcasted_iota(jnp.int32, (1, GROUP * tq), 1)
        tpos = t0 + (colq & (tq - 1))
        dist = (tpos - (blk * CMP_BLOCK + (CMP_BLOCK - 1))).astype(F32)
        valid = dist >= 0.0
        s = jnp.where(valid, st - _row_slopes(g, colq >> _log2(tq)) * dist, NEG)
        e = jnp.exp(s - jnp.max(s, axis=0, keepdims=True))
        p = jnp.where(valid, e / jnp.sum(e, axis=0, keepdims=True), 0.0)
        o_cmp.append(_unstack_group(_dot_tn(p.astype(BF16), vc), tq))
        imp = p[:, 0:tq] + p[:, tq:2 * tq] + p[:, 2 * tq:3 * tq] + p[:, 3 * tq:4 * tq]

        bj = lax.broadcasted_iota(jnp.int32, (nb, tq), 0)
        cur = (t0 + lax.broadcasted_iota(jnp.int32, (nb, tq), 1)) >> _log2(SEL_BLOCK)
        force = (bj == 0) | (bj == cur)
        score = jnp.where(bj <= cur, jnp.where(force, FORCED_SCORE, imp), -1.0)
        rank = _block_ranks(score, bj, [(i, i) for i in range(nb)])
        mneg = jnp.where((rank < topk) & (score >= 0.0), 0.0, NEG)
        mtok = jnp.concatenate([mneg, jnp.zeros((LANES - nb, tq), F32)], axis=0).T
        qf = qfeat_ref[g]
        qwin.append(jnp.concatenate([qst, qf.astype(BF16)], axis=1))
        qsel.append(jnp.concatenate([qst, (qf + jnp.concatenate([mtok] * GROUP, axis=0)).astype(BF16)], axis=1))

    def reset(stats):
        for g in groups:
            m_scr[g] = jnp.full(m_scr.shape[1:], NEG, F32)
            for ref in stats:
                ref[g] = jnp.zeros(ref.shape[1:], F32)

    def logits(g, qa, kt_, kf, width, extra):
        sc = _dot(qa, jnp.concatenate([kt_, kt_, kf], axis=0))
        if extra is not None:
            sc = sc + extra
        sc_scr[g, :, 0:width] = sc
        mn_scr[g] = jnp.maximum(m_scr[g], jnp.max(sc, axis=-1, keepdims=True))

    def softmax_tiles(ncol):
        for g in groups:
            _softmax_update(sc_scr.at[g], p_scr.at[g], m_scr.at[g], mn_scr.at[g], l_scr.at[g], a_scr.at[g], ncol, tq)

    reset((l_scr, acc_scr))
    n_kt = (t0 + tq - 1) // tk + 1

    def kv_step(kt, causal):
        s0 = pl.multiple_of(kt * tk, tk)
        kf = kfeat_ref[:, pl.ds(s0, tk)]
        extra = None
        if causal:
            dd = (t0 - s0) + lax.broadcasted_iota(jnp.int32, (tq, tk), 0) - lax.broadcasted_iota(jnp.int32, (tq, tk), 1)
            extra = jnp.concatenate([jnp.where(dd >= 0, 0.0, NEG)] * GROUP, axis=0)
        for g in groups:
            logits(g, qsel[g], att_ref[0, g * d:(g + 1) * d, pl.ds(s0, tk)], kf, tk, extra)
        softmax_tiles(tk // LANES)
        for g in groups:
            vt_ = att_ref[0, KV_WIDTH + g * d:KV_WIDTH + (g + 1) * d, pl.ds(s0, tk)]
            acc_scr[g] = a_scr[g] * acc_scr[g] + _dot_nt(p_scr[g, :, 0:tk], _twice(vt_))

    def full_tile(kt, carry):
        kv_step(kt, False)
        return carry

    lax.fori_loop(0, n_kt - 1, full_tile, 0)
    kv_step(n_kt - 1, True)
    o_sel = [_unstack_group(acc_scr[g] / l_scr[g], tq) for g in groups]

    reset((l_scr,))
    kfw = kfeat_ref[:, pl.ds(ws, wk)]
    ddw = (t0 - ws) + lax.broadcasted_iota(jnp.int32, (tq, wk), 0) - lax.broadcasted_iota(jnp.int32, (tq, wk), 1)
    nmw = jnp.concatenate([jnp.where((ddw >= 0) & (ddw < WINDOW), 0.0, NEG)] * GROUP, axis=0)
    for g in groups:
        logits(g, qwin[g], att_ref[0, 2 * KV_WIDTH + g * d:2 * KV_WIDTH + (g + 1) * d, pl.ds(ws, wk)], kfw, wk, nmw)
    softmax_tiles(wk // LANES)
    for g in groups:
        vw = att_ref[0, 3 * KV_WIDTH + g * d:3 * KV_WIDTH + (g + 1) * d, pl.ds(ws, wk)]
        o_win.append(_unstack_group(_dot_nt(p_scr[g, :, 0:wk], _twice(vw)) / l_scr[g], tq))

    ge = _expand_gates(gates_ref[0], gexp_ref)
    oa_ref[0] = (ge[0] * jnp.concatenate(o_cmp, axis=1) + ge[1] * jnp.concatenate(o_sel, axis=1)
                 + ge[2] * jnp.concatenate(o_win, axis=1))


def _nsa_prompt(q, gates, cmpkv, att, k_feat, q_feat, gexp):
    b, t, _ = q.shape
    tq = NSA_Q_ROWS
    nb = t // CMP_BLOCK
    assert nb <= HEAD_DIM, "the block mask uses 64 feature lanes"
    cw = max(min(SEL_KEYS, t), min(WINDOW + tq, t))
    body = functools.partial(_nsa_prompt_body, seq=t, topk=min(SEL_TOPK, nb))
    ng = N_KV_HEADS
    stat = pltpu.VMEM((ng, GROUP * tq, LANES), F32)
    return pl.pallas_call(
        body, grid=(b, t // tq),
        in_specs=[pl.BlockSpec((1, tq, ATTN_WIDTH), lambda i, j: (i, j, 0)),
                  pl.BlockSpec((1, tq, LANES), lambda i, j: (i, j, 0)),
                  pl.BlockSpec((2, 1, nb, 2 * LANES), lambda i, j: (0, i, 0, 0)),
                  pl.BlockSpec((1, 4 * KV_WIDTH, t), lambda i, j: (i, 0, 0)),
                  pl.BlockSpec(k_feat.shape, lambda i, j: (0, 0)),
                  pl.BlockSpec(q_feat.shape, lambda i, j: (0, 0, 0)),
                  pl.BlockSpec(gexp.shape, lambda i, j: (0, 0, 0))],
        out_specs=pl.BlockSpec((1, tq, ATTN_WIDTH), lambda i, j: (i, j, 0)),
        out_shape=jax.ShapeDtypeStruct((b, t, ATTN_WIDTH), F32),
        scratch_shapes=[pltpu.VMEM((ng, GROUP * tq, cw), F32),
                        pltpu.VMEM((ng, GROUP * tq, cw), BF16), stat, stat, stat, stat, stat],
        compiler_params=pltpu.CompilerParams(dimension_semantics=("parallel", "parallel"),
                                             vmem_limit_bytes=_vmem_limit(48 << 20)),
        name="nsa_prompt",
    )(q, gates, cmpkv, att, k_feat, q_feat, gexp)


def _split3(x):
    hi = x.astype(BF16)
    r1 = x - hi.astype(F32)
    mid = r1.astype(BF16)
    lo = (r1 - mid.astype(F32)).astype(BF16)
    return hi, mid, lo


def _hgrn_prompt_body(rq_ref, f_ref, rv_ref, ltri_ref, bd_ref, ind_ref, indt_ref, o_ref, st_ref,
                      s_scr, cum_scr, k_scr):
    c16 = RNN_CHUNK

    @pl.when(pl.program_id(1) == 0)
    def _():
        s_scr[...] = jnp.zeros(s_scr.shape, F32)

    f = f_ref[0]
    hi, mid, lo = _split3(jnp.log(f))
    ltri = ltri_ref[...]
    cum = (_dot(ltri, hi) + _dot(ltri, mid) + _dot(ltri, lo)) * LOG2E
    cum_scr[...] = cum
    k_scr[...] = cum - jnp.log2(1.0 - f)

    si = lax.broadcasted_iota(jnp.int32, (c16, c16, RNN_WIDTH), 0)
    ti = lax.broadcasted_iota(jnp.int32, (c16, c16, RNN_WIDTH), 1)
    causal = si <= ti
    npair = RNN_WIDTH // LANES

    def chunk(c, carry):
        r0 = pl.multiple_of(c * c16, c16)
        cc = cum_scr[pl.ds(r0, c16), :]
        qc = rq_ref[0, pl.ds(r0, c16), :]
        lk = k_scr[pl.ds(r0, c16), :]
        vc = rv_ref[0, pl.ds(r0, c16), :]
        last = cc[c16 - 1:c16, :]
        qd = (qc * jnp.exp2(cc)).astype(BF16)
        kd = jnp.exp2(last - lk).astype(BF16)
        dl = jnp.exp2(last)
        vb = vc.astype(BF16)
        o_int = jnp.concatenate([_dot_nt(qd[:, p * LANES:(p + 1) * LANES], s_scr[p].astype(BF16))
                                 for p in range(npair)], axis=1)
        dec = jnp.exp2(jnp.where(causal, cc[None, :, :] - lk[:, None, :], NEG))
        prod = (qc[None, :, :] * dec).reshape(c16 * c16, RNN_WIDTH)
        a = _dot(prod.astype(BF16), ind_ref[...])
        a = _dot(a.astype(BF16), indt_ref[...]).reshape(c16, c16, RNN_WIDTH)
        o_intra = jnp.sum(a * vc[:, None, :], axis=0)
        o_ref[0, pl.ds(r0, c16), :] = o_int + o_intra
        for p in range(npair):
            u = _dot_tn(vb[:, p * LANES:(p + 1) * LANES], kd[:, p * LANES:(p + 1) * LANES])
            s_scr[p] = s_scr[p] * dl[:, p * LANES:(p + 1) * LANES] + u * bd_ref[...]
        return carry

    lax.fori_loop(0, rq_ref.shape[1] // c16, chunk, 0, unroll=4)
    st_ref[0] = s_scr[...]


def _hgrn_prompt(rq, f, rv, ltri, bd, ind):
    b, t, w = rq.shape
    tc = RNN_ROWS
    npair = w // LANES
    row = lambda i, j: (i, j, 0)
    const = lambda i, j: (0, 0)
    indt = ind.T
    return pl.pallas_call(
        _hgrn_prompt_body, grid=(b, t // tc),
        in_specs=[pl.BlockSpec((1, tc, w), row), pl.BlockSpec((1, tc, w), row), pl.BlockSpec((1, tc, w), row),
                  pl.BlockSpec(ltri.shape, const), pl.BlockSpec(bd.shape, const),
                  pl.BlockSpec(ind.shape, const), pl.BlockSpec(indt.shape, const)],
        out_specs=(pl.BlockSpec((1, tc, w), row), pl.BlockSpec((1, npair, LANES, LANES), lambda i, j: (i, 0, 0, 0))),
        out_shape=(jax.ShapeDtypeStruct((b, t, w), F32), jax.ShapeDtypeStruct((b, npair, LANES, LANES), F32)),
        scratch_shapes=[pltpu.VMEM((npair, LANES, LANES), F32), pltpu.VMEM((tc, w), F32), pltpu.VMEM((tc, w), F32)],
        compiler_params=pltpu.CompilerParams(dimension_semantics=("parallel", "arbitrary"),
                                             vmem_limit_bytes=_vmem_limit(32 << 20)),
        name="hgrn_prompt",
    )(rq, f, rv, ltri, bd, ind, indt)


def _finish_body(x_ref, oa_ref, orn_ref, rgs_ref, ag_ref, rg_ref, wout_ref, lnm_ref, wup_ref, wdn_ref, y_ref, hn_scr):
    @pl.when(pl.program_id(1) == 0)
    def _():
        oa = oa_ref[...]
        orn = orn_ref[...]
        a_n = oa * lax.rsqrt(_head_mean_sq(oa) + EPS) * ag_ref[...]
        r_n = orn * lax.rsqrt(_head_mean_sq(orn) + EPS) * rg_ref[...] * rgs_ref[...]
        h = (x_ref[...] + _dot(a_n.astype(BF16), wout_ref[0:ATTN_WIDTH, :])
             + _dot(r_n.astype(BF16), wout_ref[ATTN_WIDTH:ATTN_WIDTH + RNN_WIDTH, :]))
        y_ref[...] = h
        hn_scr[...] = (_row_rms(h) * lnm_ref[...]).astype(BF16)

    u = jnp.maximum(_dot(hn_scr[...], wup_ref[...]), 0.0)
    y_ref[...] += _dot((u * u).astype(BF16), wdn_ref[...])


def _finish(x, oa, orn, rgs, ag, rg, wout, lnm, wup, wdn):
    n, dm = x.shape
    tm = FINISH_ROWS
    dff = wup.shape[1]
    row = lambda i, j: (i, 0)
    const = lambda i, j: (0, 0)
    return pl.pallas_call(
        _finish_body, grid=(n // tm, dff // FF_CHUNK),
        in_specs=[pl.BlockSpec((tm, dm), row), pl.BlockSpec((tm, ATTN_WIDTH), row),
                  pl.BlockSpec((tm, RNN_WIDTH), row), pl.BlockSpec((tm, RNN_WIDTH), row),
                  pl.BlockSpec(ag.shape, const), pl.BlockSpec(rg.shape, const),
                  pl.BlockSpec(wout.shape, const), pl.BlockSpec(lnm.shape, const),
                  pl.BlockSpec((dm, FF_CHUNK), lambda i, j: (0, j)),
                  pl.BlockSpec((FF_CHUNK, dm), lambda i, j: (j, 0))],
        out_specs=pl.BlockSpec((tm, dm), row),
        out_shape=jax.ShapeDtypeStruct((n, dm), F32),
        scratch_shapes=[pltpu.VMEM((tm, dm), BF16)],
        compiler_params=pltpu.CompilerParams(dimension_semantics=("parallel", "arbitrary"),
                                             vmem_limit_bytes=_vmem_limit(48 << 20)),
        name="finish",
    )(x, oa, orn, rgs, ag, rg, wout, lnm, wup, wdn)


def _proj_sample_body(xbt_ref, xtb_ref, ln_ref, wtok_ref, wtb_ref, wbt_ref, qg_ref, lbl_ref, gsel_ref, gwin_ref,
                      q_ref, rgs_ref, gates_ref, ztb_ref, zbt_ref):
    xb = (_row_rms(xbt_ref[...]) * ln_ref[...]).astype(BF16)
    xt = (_row_rms(xtb_ref[...]) * ln_ref[...]).astype(BF16)
    zq = _dot(xb, wtok_ref[:, 0:ATTN_WIDTH])
    q_ref[...] = (zq * lax.rsqrt(_head_mean_sq(zq) + EPS) * qg_ref[...]).astype(BF16)
    rgs_ref[...] = _silu(_dot(xb, wtok_ref[:, ATTN_WIDTH:ATTN_WIDTH + RNN_WIDTH]))
    gates_ref[...] = jax.nn.sigmoid(_dot(xb, wtok_ref[:, ATTN_WIDTH + RNN_WIDTH:ATTN_WIDTH + RNN_WIDTH + LANES]))

    d = HEAD_DIM
    zt = _dot_nt(wtb_ref[...], xt)
    ztb_ref[0:256] = zt[0:256]
    for g in range(N_KV_HEADS):
        ztb_ref[256 + g * d:256 + (g + 1) * d] = _col_head_norm(zt[256 + g * d:256 + (g + 1) * d], gsel_ref[...])
    ztb_ref[384:512] = zt[384:512]
    ztb_ref[512:1024] = _silu(zt[512:1024])
    lb = _lower_bound(lbl_ref[...], 0)[0]
    ztb_ref[1024:1536] = lb + (1.0 - lb) * jax.nn.sigmoid(zt[1024:1536])
    ztb_ref[1536:2048] = zt[1536:2048]

    zb = _dot_nt(wbt_ref[...], xb)
    for g in range(N_KV_HEADS):
        zbt_ref[g * d:(g + 1) * d] = _col_head_norm(zb[g * d:(g + 1) * d], gsel_ref[...])
        zbt_ref[256 + g * d:256 + (g + 1) * d] = _col_head_norm(zb[256 + g * d:256 + (g + 1) * d], gwin_ref[...])
    zbt_ref[128:256] = zb[128:256]
    zbt_ref[384:512] = zb[384:512]


def _proj_sample(xbt, xtb, ln, wtok, wtb, wbt, qg, lblt, gsel, gwin):
    n, dm = xbt.shape
    tm = PROJ_ROWS
    row = lambda i: (i, 0)
    col = lambda i: (0, i)
    const = lambda i: (0, 0)
    return pl.pallas_call(
        _proj_sample_body, grid=(n // tm,),
        in_specs=[pl.BlockSpec((tm, dm), row), pl.BlockSpec((tm, dm), row), pl.BlockSpec(ln.shape, const),
                  pl.BlockSpec(wtok.shape, const), pl.BlockSpec(wtb.shape, const), pl.BlockSpec(wbt.shape, const),
                  pl.BlockSpec(qg.shape, const), pl.BlockSpec(lblt.shape, lambda i: (0, 0, 0)),
                  pl.BlockSpec(gsel.shape, const), pl.BlockSpec(gwin.shape, const)],
        out_specs=(pl.BlockSpec((tm, ATTN_WIDTH), row), pl.BlockSpec((tm, RNN_WIDTH), row),
                   pl.BlockSpec((tm, LANES), row), pl.BlockSpec((wtb.shape[0], tm), col),
                   pl.BlockSpec((wbt.shape[0], tm), col)),
        out_shape=(jax.ShapeDtypeStruct((n, ATTN_WIDTH), BF16), jax.ShapeDtypeStruct((n, RNN_WIDTH), F32),
                   jax.ShapeDtypeStruct((n, LANES), F32), jax.ShapeDtypeStruct((wtb.shape[0], n), F32),
                   jax.ShapeDtypeStruct((wbt.shape[0], n), F32)),
        compiler_params=pltpu.CompilerParams(dimension_semantics=("parallel",),
                                             vmem_limit_bytes=_vmem_limit(40 << 20)),
        name="proj_sample",
    )(xbt, xtb, ln, wtok, wtb, wbt, qg, lblt, gsel, gwin)


PAGE_ROWS = 4 * KV_WIDTH
SAMPLE_BATCH_PER_STEP = 2


def _nsa_sample_body(pt_ref, cache_ref, q_ref, nkv_ref, w1_ref, pe_ref, w2_ref, kg_ref, e_ref,
                     ocs_ref, cbuf, sbuf, lhs_scr, kk_scr, vv_scr, sem, *, n_pages, past_len, dec_t, topk):
    step_id = pl.program_id(0)
    nsteps = pl.num_programs(0)
    slot = step_id % 2
    d = HEAD_DIM
    nblk = 2 * n_pages
    half_rows = PAGE_ROWS // 2
    bps = q_ref.shape[0]

    def page_copies(st, sl, bi, j):
        pg = pt_ref[(st * bps + bi) * n_pages + j]
        return (pltpu.make_async_copy(cache_ref.at[pg, pl.ds(0, half_rows)], cbuf.at[sl, bi, :, j, :], sem.at[sl]),
                pltpu.make_async_copy(cache_ref.at[pg, pl.ds(half_rows, half_rows)],
                                      sbuf.at[sl, bi, pl.ds(j * half_rows, half_rows)], sem.at[sl]))

    def all_copies(st, sl):
        return [cp for bi in range(bps) for j in range(n_pages) for cp in page_copies(st, sl, bi, j)]

    @pl.when(step_id == 0)
    def _():
        for cp in all_copies(0, 0):
            cp.start()

    @pl.when(step_id + 1 < nsteps)
    def _():
        for cp in all_copies(step_id + 1, 1 - slot):
            cp.start()

    for cp in all_copies(step_id, slot):
        cp.wait()

    low = _low_half((n_pages, LANES))
    rows_b = 4 * n_pages

    def compress(c):
        for dd in range(d):
            rows = []
            for bi in range(bps):
                for g in range(N_KV_HEADS):
                    xg = cbuf[slot, bi, (c * N_KV_HEADS + g) * d + dd]
                    xg = xg + pe_ref[c, dd]
                    rows += [jnp.where(low, xg, 0.0), jnp.where(low, 0.0, xg)]
            lhs_scr[:, dd * LANES:(dd + 1) * LANES] = jnp.concatenate(rows, axis=0).astype(BF16)
        acc = _dot(lhs_scr[...], w1_ref[c])
        return _dot(_silu(acc).astype(BF16), w2_ref[c])

    kc_all = _row_rms(compress(0)) * kg_ref[...]
    vc_all = compress(1)

    nq = GROUP * dec_t
    rho = lax.broadcasted_iota(jnp.int32, (nblk, dec_t), 0)
    bid = 2 * (rho & (n_pages - 1)) + (rho >> _log2(n_pages))
    colq = lax.broadcasted_iota(jnp.int32, (1, nq), 1)
    rowq = lax.broadcasted_iota(jnp.int32, (nq, 1), 0)
    cur_blk = past_len // SEL_BLOCK
    o_cmp, o_sel = [], []
    for bi, g in [(bi, g) for bi in range(bps) for g in range(N_KV_HEADS)]:
        ch = bi * N_KV_HEADS + g
        qst = _stack_group_queries(q_ref[bi], g)
        kc = kc_all[bi * rows_b + g * nblk:bi * rows_b + (g + 1) * nblk].astype(BF16)
        vc = vc_all[bi * rows_b + g * nblk:bi * rows_b + (g + 1) * nblk].astype(BF16)
        st = _dot_nt(kc, qst)
        qpos = past_len + (colq & (dec_t - 1))
        rho_q = lax.broadcasted_iota(jnp.int32, (nblk, 1), 0)
        end = (2 * (rho_q & (n_pages - 1)) + (rho_q >> _log2(n_pages))) * CMP_BLOCK + (CMP_BLOCK - 1)
        dist = (qpos - end).astype(F32)
        valid = dist >= 0.0
        s = jnp.where(valid, st - _row_slopes(g, colq >> _log2(dec_t)) * dist, NEG)
        e = jnp.exp(s - jnp.max(s, axis=0, keepdims=True))
        p = jnp.where(valid, e / jnp.sum(e, axis=0, keepdims=True), 0.0)
        o_cmp.append(_unstack_group(_dot_tn(p.astype(BF16), vc), dec_t))
        imp = p[:, 0:dec_t]
        for r in range(1, GROUP):
            imp = imp + p[:, r * dec_t:(r + 1) * dec_t]

        force = (bid == 0) | (bid == cur_blk)
        score = jnp.where(force, FORCED_SCORE, imp)
        rank = _block_ranks(score, bid, [(i, 2 * (i % n_pages) + i // n_pages) for i in range(nblk)])
        rank = rank + jnp.where(bid > cur_blk, (FORCED_SCORE >= score).astype(F32),
                                (FORCED_SCORE > score).astype(F32))
        msel = jnp.where((rank < topk) & (score >= 0.0), 1.0, 0.0).astype(BF16)

        for pg in range(n_pages):
            kt_ = sbuf[slot, bi, pl.ds(pg * half_rows + g * d, d), :].astype(BF16)
            vt_ = sbuf[slot, bi, pl.ds(pg * half_rows + (N_KV_HEADS + g) * d, d), :].astype(BF16)
            kk_scr[ch, 0:d, pg * PAGE_SIZE:(pg + 1) * PAGE_SIZE] = kt_
            kk_scr[ch, d:2 * d, pg * PAGE_SIZE:(pg + 1) * PAGE_SIZE] = kt_
            vv_scr[ch, 0:d, pg * PAGE_SIZE:(pg + 1) * PAGE_SIZE] = vt_
            vv_scr[ch, d:2 * d, pg * PAGE_SIZE:(pg + 1) * PAGE_SIZE] = vt_
        nk = _twice(nkv_ref[bi, g * d:(g + 1) * d, :].astype(BF16))
        nv = _twice(nkv_ref[bi, KV_WIDTH + g * d:KV_WIDTH + (g + 1) * d, :].astype(BF16))
        sp = _dot(qst, kk_scr[ch])
        sn = _dot(qst, nk)
        mk = _dot_tn(msel, e_ref[...])
        okp = jnp.concatenate([mk] * GROUP, axis=0) > 0.5
        step = rowq & (dec_t - 1)
        slope = _row_slopes(g, rowq >> _log2(dec_t))
        kpos = lax.broadcasted_iota(jnp.int32, (1, past_len), 1)
        dpast = (past_len + step - kpos).astype(F32)
        lgp = jnp.where(okp & (dpast >= 0.0), sp - slope * dpast, NEG)
        lane = lax.broadcasted_iota(jnp.int32, (1, LANES), 1)
        dnew = (step - lane).astype(F32)
        lgn = jnp.where((lane < dec_t) & (dnew >= 0.0), sn - slope * dnew, NEG)
        m = jnp.maximum(jnp.max(lgp, axis=-1, keepdims=True), jnp.max(lgn, axis=-1, keepdims=True))
        pp = jnp.exp(lgp - m)
        pn = jnp.exp(lgn - m)
        den = jnp.sum(pp, axis=-1, keepdims=True) + jnp.sum(pn, axis=-1, keepdims=True)
        o2 = _dot_nt(pp.astype(BF16), vv_scr[ch]) + _dot_nt(pn.astype(BF16), nv)
        o_sel.append(_unstack_group(o2 / den, dec_t))

    for bi in range(bps):
        ocs_ref[bi, 0] = jnp.concatenate(o_cmp[bi * N_KV_HEADS:(bi + 1) * N_KV_HEADS], axis=1)
        ocs_ref[bi, 1] = jnp.concatenate(o_sel[bi * N_KV_HEADS:(bi + 1) * N_KV_HEADS], axis=1)


def _nsa_sample(page_flat, cache, q, nkv, w1r, pe_t, w2dup, kg_dup, e_perm, past_len, topk):
    nbatch, dec_t, _ = q.shape
    n_pages = past_len // PAGE_SIZE
    bps = SAMPLE_BATCH_PER_STEP
    nch = bps * N_KV_HEADS
    body = functools.partial(_nsa_sample_body, n_pages=n_pages, past_len=past_len, dec_t=dec_t, topk=topk)
    grid_spec = pltpu.PrefetchScalarGridSpec(
        num_scalar_prefetch=1, grid=(nbatch // bps,),
        in_specs=[pl.BlockSpec(memory_space=pl.ANY),
                  pl.BlockSpec((bps, dec_t, ATTN_WIDTH), lambda i, pt: (i, 0, 0)),
                  pl.BlockSpec((bps,) + nkv.shape[1:], lambda i, pt: (i, 0, 0)),
                  pl.BlockSpec(w1r.shape, lambda i, pt: (0, 0, 0)),
                  pl.BlockSpec(pe_t.shape, lambda i, pt: (0, 0, 0, 0)),
                  pl.BlockSpec(w2dup.shape, lambda i, pt: (0, 0, 0)),
                  pl.BlockSpec(kg_dup.shape, lambda i, pt: (0, 0)),
                  pl.BlockSpec(e_perm.shape, lambda i, pt: (0, 0))],
        out_specs=pl.BlockSpec((bps, 2, dec_t, ATTN_WIDTH), lambda i, pt: (i, 0, 0, 0)),
        scratch_shapes=[pltpu.VMEM((2, bps, PAGE_ROWS // 2, n_pages, PAGE_SIZE), F32),
                        pltpu.VMEM((2, bps, n_pages * PAGE_ROWS // 2, PAGE_SIZE), F32),
                        pltpu.VMEM((bps * 4 * n_pages, HEAD_DIM * LANES), BF16),
                        pltpu.VMEM((nch, 2 * HEAD_DIM, past_len), BF16),
                        pltpu.VMEM((nch, 2 * HEAD_DIM, past_len), BF16),
                        pltpu.SemaphoreType.DMA((2,))])
    return pl.pallas_call(
        body, grid_spec=grid_spec,
        out_shape=jax.ShapeDtypeStruct((nbatch, 2, dec_t, ATTN_WIDTH), F32),
        compiler_params=pltpu.CompilerParams(dimension_semantics=("arbitrary",),
                                             vmem_limit_bytes=_vmem_limit(40 << 20)),
        name="nsa_sample",
    )(page_flat, cache, q, nkv, w1r, pe_t, w2dup, kg_dup, e_perm)


def _win_sample_body(win_ref, q_ref, nw_ref, ocs_ref, gates_ref, gexp_ref, oa_ref, wout_ref, *, past_len, dec_t):
    d = HEAD_DIM
    wbuf = win_ref.shape[2]
    q = q_ref[0]
    nq = GROUP * dec_t
    rowq = lax.broadcasted_iota(jnp.int32, (nq, 1), 0)
    step = rowq & (dec_t - 1)
    kpos = past_len - wbuf + lax.broadcasted_iota(jnp.int32, (1, wbuf), 1)
    dpast = (past_len + step - kpos).astype(F32)
    okp = (dpast >= 0.0) & (dpast < WINDOW)
    lane = lax.broadcasted_iota(jnp.int32, (1, LANES), 1)
    dnew = (step - lane).astype(F32)
    okn = (lane < dec_t) & (dnew >= 0.0) & (dnew < WINDOW)
    o_win = []
    for g in range(N_KV_HEADS):
        qst = _stack_group_queries(q, g)
        kt_ = win_ref[0, g * d:(g + 1) * d, :].astype(BF16)
        vt_ = win_ref[0, KV_WIDTH + g * d:KV_WIDTH + (g + 1) * d, :].astype(BF16)
        nk = _twice(nw_ref[0, g * d:(g + 1) * d, :].astype(BF16))
        nv = _twice(nw_ref[0, KV_WIDTH + g * d:KV_WIDTH + (g + 1) * d, :].astype(BF16))
        slope = _row_slopes(g, rowq >> _log2(dec_t))
        lgp = jnp.where(okp, _dot(qst, _twice(kt_)) - slope * dpast, NEG)
        lgn = jnp.where(okn, _dot(qst, nk) - slope * dnew, NEG)
        m = jnp.maximum(jnp.max(lgp, axis=-1, keepdims=True), jnp.max(lgn, axis=-1, keepdims=True))
        pp = jnp.exp(lgp - m)
        pn = jnp.exp(lgn - m)
        den = jnp.sum(pp, axis=-1, keepdims=True) + jnp.sum(pn, axis=-1, keepdims=True)
        o2 = _dot_nt(pp.astype(BF16), _twice(vt_)) + _dot_nt(pn.astype(BF16), nv)
        o_win.append(_unstack_group(o2 / den, dec_t))
    ge = _expand_gates(gates_ref[0], gexp_ref)
    oa_ref[0] = ge[0] * ocs_ref[0, 0] + ge[1] * ocs_ref[0, 1] + ge[2] * jnp.concatenate(o_win, axis=1)

    old = win_ref[0]
    rolled = pltpu.roll(old, wbuf - dec_t, 1)
    newr = pltpu.roll(nw_ref[0], LANES - dec_t, 1)
    wout_ref[0, :, 0:wbuf - LANES] = rolled[:, 0:wbuf - LANES]
    wout_ref[0, :, wbuf - LANES:wbuf] = jnp.where(lane >= LANES - dec_t, newr, rolled[:, wbuf - LANES:wbuf])


def _win_sample(win, q, nw, ocs, gates, gexp, past_len):
    nbatch, feat, wbuf = win.shape
    dec_t = q.shape[1]
    body = functools.partial(_win_sample_body, past_len=past_len, dec_t=dec_t)
    b3 = lambda i: (i, 0, 0)
    return pl.pallas_call(
        body, grid=(nbatch,),
        in_specs=[pl.BlockSpec((1, feat, wbuf), b3), pl.BlockSpec((1, dec_t, ATTN_WIDTH), b3),
                  pl.BlockSpec((1, feat, LANES), b3), pl.BlockSpec((1, 2, dec_t, ATTN_WIDTH), lambda i: (i, 0, 0, 0)),
                  pl.BlockSpec((1, dec_t, LANES), b3), pl.BlockSpec(gexp.shape, lambda i: (0, 0, 0))],
        out_specs=(pl.BlockSpec((1, dec_t, ATTN_WIDTH), b3), pl.BlockSpec((1, feat, wbuf), b3)),
        out_shape=(jax.ShapeDtypeStruct((nbatch, dec_t, ATTN_WIDTH), F32),
                   jax.ShapeDtypeStruct((nbatch, feat, wbuf), F32)),
        compiler_params=pltpu.CompilerParams(dimension_semantics=("parallel",),
                                             vmem_limit_bytes=_vmem_limit(24 << 20)),
        name="win_sample",
    )(win, q, nw, ocs, gates, gexp)


def _hgrn_sample_body(q_ref, f_ref, v_ref, s_ref, o_ref, so_ref, *, dec_t):
    nb = s_ref.shape[3]
    o_ref[...] = jnp.zeros(o_ref.shape, F32)

    sub = 8

    def per_tile(i, carry):
        r0 = pl.multiple_of(i * sub, sub)
        f_t = [f_ref[pl.ds(r0, sub), pl.ds(t * nb, nb)] for t in range(dec_t)]
        q_t = [q_ref[pl.ds(r0, sub), pl.ds(t * nb, nb)] for t in range(dec_t)]
        for j in range(sub):
            s = s_ref[0, r0 + j]
            for t in range(dec_t):
                cols = pl.ds(t * nb, nb)
                fr = f_t[t][j:j + 1, :]
                s = fr * s + (1.0 - fr) * v_ref[:, cols]
                o_ref[:, cols] = o_ref[:, cols] + s * q_t[t][j:j + 1, :]
            so_ref[0, r0 + j] = s
        return carry

    lax.fori_loop(0, s_ref.shape[1] // sub, per_tile, 0)


def _hgrn_sample(ztb, state, dec_t):
    nh, dk, dv, nb = state.shape
    n = ztb.shape[1]
    body = functools.partial(_hgrn_sample_body, dec_t=dec_t)
    q0, f0, v0 = 512 // dk, 1024 // dk, 1536 // dk
    return pl.pallas_call(
        body, grid=(nh,),
        in_specs=[pl.BlockSpec((dk, n), lambda h: (q0 + h, 0)), pl.BlockSpec((dk, n), lambda h: (f0 + h, 0)),
                  pl.BlockSpec((dv, n), lambda h: (v0 + h, 0)),
                  pl.BlockSpec((1, dk, dv, nb), lambda h: (h, 0, 0, 0))],
        out_specs=(pl.BlockSpec((dv, n), lambda h: (h, 0)), pl.BlockSpec((1, dk, dv, nb), lambda h: (h, 0, 0, 0))),
        out_shape=(jax.ShapeDtypeStruct((nh * dv, n), F32), jax.ShapeDtypeStruct(state.shape, F32)),
        compiler_params=pltpu.CompilerParams(dimension_semantics=("parallel",),
                                             vmem_limit_bytes=_vmem_limit(24 << 20)),
        name="hgrn_sample",
    )(ztb, ztb, ztb, state)


def _gate_expander():
    m = np.zeros((N_BRANCH, LANES, ATTN_WIDTH), np.float32)
    for br in range(N_BRANCH):
        for h in range(N_ATTN_HEADS):
            m[br, h * N_BRANCH + br, h * HEAD_DIM:(h + 1) * HEAD_DIM] = 1.0
    return jnp.asarray(m, BF16)


def _block_expander(block_ids, n_keys):
    key_blk = np.arange(n_keys) // SEL_BLOCK
    return jnp.asarray((np.asarray(block_ids)[:, None] == key_blk[None, :]).astype(np.float32), BF16)


def _key_features(n_keys):
    s = np.arange(n_keys)
    m = np.zeros((LANES, n_keys), np.float32)
    m[0:HEAD_DIM] = (np.arange(HEAD_DIM)[:, None] == (s // SEL_BLOCK)[None, :])
    m[HEAD_DIM] = s // SEL_BLOCK
    m[HEAD_DIM + 1] = s % SEL_BLOCK
    return jnp.asarray(m, BF16)


def _query_slope_features(rows):
    m = np.zeros((N_KV_HEADS, GROUP * rows, LANES), np.float32)
    for g in range(N_KV_HEADS):
        for r in range(GROUP):
            m[g, r * rows:(r + 1) * rows, HEAD_DIM] = SLOPES[g][r] * SEL_BLOCK
            m[g, r * rows:(r + 1) * rows, HEAD_DIM + 1] = SLOPES[g][r]
    return jnp.asarray(m, F32)


def _chunk_lower_tri(n, c):
    i = np.arange(n)
    return jnp.asarray(((i[:, None] // c == i[None, :] // c) & (i[None, :] <= i[:, None])).astype(np.float32), BF16)


def _head_indicator():
    m = np.zeros((RNN_WIDTH, LANES), np.float32)
    m[np.arange(RNN_WIDTH), np.arange(RNN_WIDTH) // RNN_DK] = 1.0
    return jnp.asarray(m, BF16)


def _head_block_diag(n):
    i = np.arange(n)
    return jnp.asarray((i[:, None] // RNN_DV == i[None, :] // RNN_DK).astype(np.float32), F32)


def kernel(x_prompt, x_sample, cache_kv, cache_win, state_rnn, page_table, ln_mix, w_in, q_norm, k_norm, cmp_pe,
           cmp_w1, cmp_w2, attn_out_norm, rnn_lb_logits, rnn_out_norm, w_out, ln_mlp, w_up, w_down):
    assert w_in.shape[0] == 1, "single layer"
    b, t, dm = x_prompt.shape
    nbatch, dec_t, _ = x_sample.shape
    n_pool = cache_kv.shape[1]
    n_pages = page_table.shape[1]
    past_len = n_pages * PAGE_SIZE
    wbuf = cache_win.shape[2]
    assert t % PROJ_ROWS == 0 and t % RNN_ROWS == 0 and t % min(SEL_KEYS, t) == 0 and t >= WINDOW
    assert (b * t) % FINISH_ROWS == 0 and (nbatch * dec_t) % FINISH_ROWS == 0 and w_up.shape[2] % FF_CHUNK == 0
    assert (nbatch * dec_t) % PROJ_ROWS == 0 and nbatch == LANES and dec_t <= 8
    assert past_len % SEL_BLOCK == 0 and wbuf == WINDOW and wbuf >= LANES

    w = w_in[0]
    c_kv, c_gate, c_rq, c_rf, c_ri, c_rg = ATTN_WIDTH, ATTN_WIDTH + 6 * KV_WIDTH, 1304, 1816, 2328, 2840
    gate_cols = jnp.pad(w[:, c_gate:c_rq], ((0, 0), (0, LANES - N_ATTN_HEADS * N_BRANCH)))
    wtok = jnp.concatenate([w[:, 0:ATTN_WIDTH], w[:, c_rq:], gate_cols, w[:, c_kv:c_kv + 2 * KV_WIDTH]],
                           axis=1).astype(BF16)
    wft = w[:, c_kv:c_gate].T.astype(BF16)
    wtok_s = jnp.concatenate([w[:, 0:ATTN_WIDTH], w[:, c_rg:], gate_cols], axis=1).astype(BF16)
    wtb_s = jnp.concatenate([w[:, c_kv:c_kv + 4 * KV_WIDTH], w[:, c_rq:c_rg]], axis=1).T.astype(BF16)
    wbt_s = w[:, c_kv + 2 * KV_WIDTH:c_gate].T.astype(BF16)
    ln = ln_mix[0][None, :]
    qg = (jnp.tile(q_norm[0], N_ATTN_HEADS) * SCALE)[None, :]
    lbl = rnn_lb_logits.astype(F32)
    lblt = jnp.broadcast_to(lbl[:, :, None], lbl.shape + (PROJ_ROWS,))
    gsel = jnp.broadcast_to(k_norm[0, 1][:, None], (HEAD_DIM, PROJ_ROWS))
    gwin = jnp.broadcast_to(k_norm[0, 2][:, None], (HEAD_DIM, PROJ_ROWS))
    kg_dup = jnp.tile(k_norm[0, 0], 2)[None, :]
    pe = cmp_pe[0]
    pe_tok = jnp.tile(jnp.concatenate([jnp.tile(pe[0], (1, N_KV_HEADS)), jnp.tile(pe[1], (1, N_KV_HEADS))], axis=1),
                      (PROJ_ROWS // CMP_BLOCK, 1))
    w1 = cmp_w1[0].reshape(2, CMP_BLOCK, HEAD_DIM, CMP_HIDDEN)
    zeros = jnp.zeros_like(w1)
    w1bd = jnp.concatenate([jnp.concatenate([w1, zeros], axis=3), jnp.concatenate([zeros, w1], axis=3)],
                           axis=2).astype(BF16)
    w1r = jnp.tile(w1.transpose(0, 2, 1, 3), (1, 1, 2, 1)).astype(BF16)
    w1r = w1r.reshape(2, HEAD_DIM * LANES, CMP_HIDDEN)
    pe_t = jnp.tile(pe.transpose(0, 2, 1), (1, 1, 2))[:, :, None, :]
    w2dup = jnp.tile(cmp_w2[0], (1, 1, 2)).astype(BF16)
    ag = attn_out_norm[0][None, :]
    rg = rnn_out_norm[0][None, :]
    wout = w_out[0].astype(BF16)
    lnm = ln_mlp[0][None, :]
    wup = w_up[0].astype(BF16)
    wdn = w_down[0].astype(BF16)
    gexp = _gate_expander()

    (q_p, gates_p, rq_p, f_p, rv_p, rgs_p, kvc_p, kvt_p, wint_p, att_p) = _proj_prompt(
        x_prompt, ln, wtok, wft, qg, lbl, pe_tok, gsel, gwin)
    cmp_p = _compress_prompt(kvc_p, w1bd, w2dup, kg_dup)
    nb_p = t // CMP_BLOCK
    oa_p = _nsa_prompt(q_p, gates_p, cmp_p, att_p, _key_features(t), _query_slope_features(NSA_Q_ROWS), gexp)
    orn_p, st_p = _hgrn_prompt(rq_p, f_p, rv_p, _chunk_lower_tri(RNN_ROWS, RNN_CHUNK),
                               _head_block_diag(LANES), _head_indicator())
    y_p = _finish(x_prompt.reshape(b * t, dm), oa_p.reshape(b * t, ATTN_WIDTH), orn_p.reshape(b * t, RNN_WIDTH),
                  rgs_p.reshape(b * t, RNN_WIDTH), ag, rg, wout, lnm, wup, wdn).reshape(b, t, dm)
    kv_prompt = kvt_p.reshape(1, b, 4, N_KV_HEADS, HEAD_DIM, t).transpose(0, 1, 5, 2, 3, 4)
    wlen = min(WINDOW, t)
    win_prompt = wint_p[:, :, t - wlen:].reshape(1, b, 2, N_KV_HEADS, HEAD_DIM, wlen).transpose(0, 1, 5, 2, 3, 4)
    hh = LANES // RNN_DV
    st5 = st_p.reshape(b, RNN_WIDTH // LANES, hh, RNN_DV, hh, RNN_DK)
    rnn_prompt = jnp.stack([st5[:, :, i, :, i, :] for i in range(hh)], axis=2)
    rnn_prompt = rnn_prompt.reshape(b, N_RNN_HEADS, RNN_DV, RNN_DK).transpose(0, 1, 3, 2)[None]

    n_s = nbatch * dec_t
    xbt = x_sample.reshape(n_s, dm)
    xtb = x_sample.transpose(1, 0, 2).reshape(n_s, dm)
    q_s, rgs_s, gates_s, ztb, zbt = _proj_sample(xbt, xtb, ln, wtok_s, wtb_s, wbt_s, qg, lblt, gsel, gwin)
    kv_sample = ztb[0:4 * KV_WIDTH].reshape(4, N_KV_HEADS, HEAD_DIM, dec_t, nbatch).transpose(4, 3, 0, 1, 2)[None]
    new_rows = jnp.pad(zbt.reshape(4 * KV_WIDTH, nbatch, dec_t).transpose(1, 0, 2),
                       ((0, 0), (0, 0), (0, LANES - dec_t)))
    cache = cache_kv[0].transpose(0, 2, 3, 4, 1).reshape(n_pool, PAGE_ROWS, PAGE_SIZE)
    nblk_s = past_len // CMP_BLOCK
    rho = np.arange(nblk_s)
    e_perm = _block_expander(2 * (rho % n_pages) + rho // n_pages, past_len)
    ns_s = -(-(past_len + dec_t) // SEL_BLOCK)
    q_s3 = q_s.reshape(nbatch, dec_t, ATTN_WIDTH)
    ocs = _nsa_sample(page_table.reshape(-1), cache, q_s3, new_rows[:, 0:2 * KV_WIDTH], w1r, pe_t, w2dup, kg_dup,
                      e_perm, past_len, min(SEL_TOPK, ns_s))
    win = cache_win[0].transpose(0, 2, 3, 4, 1).reshape(nbatch, 2 * KV_WIDTH, wbuf)
    oa_s, win_new = _win_sample(win, q_s3, new_rows[:, 2 * KV_WIDTH:], ocs,
                                gates_s.reshape(nbatch, dec_t, LANES), gexp, past_len)
    win_sample = win_new.reshape(1, nbatch, 2, N_KV_HEADS, HEAD_DIM, wbuf).transpose(0, 1, 5, 2, 3, 4)
    state = state_rnn[0].transpose(1, 2, 3, 0)
    orn_t, state_new = _hgrn_sample(ztb, state, dec_t)
    rnn_sample = state_new.transpose(3, 0, 1, 2)[None]
    orn_s = orn_t.reshape(RNN_WIDTH, dec_t, nbatch).transpose(2, 1, 0).reshape(n_s, RNN_WIDTH)
    y_s = _finish(xbt, oa_s.reshape(n_s, ATTN_WIDTH), orn_s, rgs_s, ag, rg, wout, lnm, wup, wdn).reshape(nbatch, dec_t, dm)

    return (y_p, y_s, kv_prompt, kv_sample, win_prompt, win_sample, rnn_prompt, rnn_sample)
```

```python
import functools

import numpy as np
import jax
import jax.numpy as jnp
from jax import lax
from jax.experimental import pallas as pl
from jax.experimental.pallas import tpu as pltpu

F32 = jnp.float32
BF16 = jnp.bfloat16

HEAD_DIM = 64
N_ATTN_HEADS = 8
N_KV_HEADS = 2
GROUP = N_ATTN_HEADS // N_KV_HEADS
N_RNN_HEADS = 8
RNN_DK = 64
RNN_DV = 64
ATTN_WIDTH = N_ATTN_HEADS * HEAD_DIM
RNN_WIDTH = N_RNN_HEADS * RNN_DV
KV_WIDTH = N_KV_HEADS * HEAD_DIM
N_BRANCH = 3
CMP_BLOCK = 64
SEL_BLOCK = 64
SEL_TOPK = 16
WINDOW = 512
CMP_HIDDEN = 128
PAGE_SIZE = 128
SCALE = HEAD_DIM ** -0.5
EPS = 1e-6
NEG = -1e30
LOG2E = 1.4426950408889634
FORCED_SCORE = GROUP + 1.0
SLOPES = [[2.0 ** (-(g * GROUP + r + 1)) for r in range(GROUP)] for g in range(N_KV_HEADS)]

LANES = 128
VMEM_BYTES_V7X = 64 * 1024 * 1024

PROJ_ROWS = 256
NSA_Q_ROWS = 128
SEL_KEYS = 512
ROW_BLOCK = 16
RNN_ROWS = 256
RNN_CHUNK = 16
FF_CHUNK = 1024
FINISH_ROWS = 512

NT = (((1,), (1,)), ((), ()))
TN = (((0,), (0,)), ((), ()))


def _vmem_limit(nbytes):
    return int(min(VMEM_BYTES_V7X - (8 << 20), max(nbytes, 16 << 20)))


def _dot(a, b):
    return jnp.dot(a, b, preferred_element_type=F32)


def _dot_nt(a, b):
    return lax.dot_general(a, b, NT, preferred_element_type=F32)


def _dot_tn(a, b):
    return lax.dot_general(a, b, TN, preferred_element_type=F32)


def _low_half(shape):
    lane = lax.broadcasted_iota(jnp.int32, shape, len(shape) - 1)
    return (lane & HEAD_DIM) == 0


def _head_mean_sq(x):
    outs = []
    for j in range(x.shape[-1] // LANES):
        blk = x[:, j * LANES:(j + 1) * LANES]
        sq = blk * blk
        low = _low_half(blk.shape)
        s_lo = jnp.sum(jnp.where(low, sq, 0.0), axis=-1, keepdims=True)
        s_hi = jnp.sum(jnp.where(low, 0.0, sq), axis=-1, keepdims=True)
        outs.append(jnp.where(low, s_lo, s_hi))
    return jnp.concatenate(outs, axis=-1) * (1.0 / HEAD_DIM)


def _row_rms(x):
    return x * lax.rsqrt(jnp.mean(x * x, axis=-1, keepdims=True) + EPS)


def _col_head_norm(rows, gain):
    ms = jnp.mean(rows * rows, axis=0, keepdims=True)
    return rows * lax.rsqrt(ms + EPS) * gain


def _silu(z):
    return z * jax.nn.sigmoid(z)


def _lower_bound(logits, axis):
    m = jnp.max(logits, axis=axis, keepdims=True)
    e = jnp.exp(logits - m)
    lb = e / jnp.sum(e, axis=axis, keepdims=True)
    return lb[0:1] if axis == 0 else lb


def _stack_group_queries(q, g):
    rows = q.shape[0]
    low = _low_half((rows, LANES))
    zero = jnp.zeros((rows, LANES), q.dtype)
    pa = q[:, g * 2 * LANES:g * 2 * LANES + LANES]
    pb = q[:, g * 2 * LANES + LANES:(g + 1) * 2 * LANES]
    return jnp.concatenate([jnp.where(low, pa, zero), jnp.where(low, zero, pa),
                            jnp.where(low, pb, zero), jnp.where(low, zero, pb)], axis=0)


def _unstack_group(o2, rows):
    low = _low_half((rows, LANES))
    return jnp.concatenate([jnp.where(low, o2[0:rows], o2[rows:2 * rows]),
                            jnp.where(low, o2[2 * rows:3 * rows], o2[3 * rows:4 * rows])], axis=1)


def _log2(n):
    assert n > 0 and n & (n - 1) == 0, n
    return n.bit_length() - 1


def _twice(x):
    return jnp.concatenate([x, x], axis=0)


def _row_slopes(g, row_head):
    s = jnp.full(row_head.shape, SLOPES[g][GROUP - 1], F32)
    for r in range(GROUP - 2, -1, -1):
        s = jnp.where(row_head == r, SLOPES[g][r], s)
    return s


def _block_ranks(score, ids, cand):
    sub = 8
    nrow = score.shape[0]
    in_order = all(r == i for r, i in cand) and nrow % sub == 0
    rank = jnp.zeros(score.shape, F32)
    for row, id_i in cand:
        s_i = score[row:row + 1, :]
        if in_order:
            parts = []
            for v in range(nrow // sub):
                blk = score[v * sub:(v + 1) * sub, :]
                if (v + 1) * sub - 1 <= id_i:
                    parts.append((s_i > blk).astype(F32))
                elif v * sub > id_i:
                    parts.append((s_i >= blk).astype(F32))
                else:
                    parts.append(jnp.where(ids[v * sub:(v + 1) * sub, :] > id_i,
                                           (s_i >= blk).astype(F32), (s_i > blk).astype(F32)))
            rank = rank + jnp.concatenate(parts, axis=0)
        else:
            rank = rank + jnp.where(ids > id_i, (s_i >= score).astype(F32), (s_i > score).astype(F32))
    return rank


def _expand_gates(gates, gexp_ref):
    hi = gates.astype(BF16)
    lo = (gates - hi.astype(F32)).astype(BF16)
    return [_dot(hi, gexp_ref[br]) + _dot(lo, gexp_ref[br]) for br in range(N_BRANCH)]


TOK_Q, TOK_RQ, TOK_RF, TOK_RI, TOK_RG, TOK_GATE, TOK_KVC, TOK_END = 0, 512, 1024, 1536, 2048, 2560, 2688, 2944


def _proj_prompt_body(x_ref, ln_ref, wtok_ref, wft_ref, qg_ref, lbl_ref, pe_ref, gsel_ref, gwin_ref,
                      q_ref, gates_ref, rq_ref, f_ref, rv_ref, rgs_ref, kvc_ref, kvt_ref, wint_ref, att_ref):
    xb = (_row_rms(x_ref[0]) * ln_ref[...]).astype(BF16)

    def tok(lo, hi):
        return _dot(xb, wtok_ref[:, lo:hi])

    zq = tok(TOK_Q, TOK_RQ)
    q_ref[0] = (zq * lax.rsqrt(_head_mean_sq(zq) + EPS) * qg_ref[...]).astype(BF16)
    rq_ref[0] = _silu(tok(TOK_RQ, TOK_RF))
    lb = _lower_bound(lbl_ref[...], 0)
    f_ref[0] = lb + (1.0 - lb) * jax.nn.sigmoid(tok(TOK_RF, TOK_RI))
    rv_ref[0] = tok(TOK_RI, TOK_RG)
    rgs_ref[0] = _silu(tok(TOK_RG, TOK_GATE))
    gates_ref[0] = jax.nn.sigmoid(tok(TOK_GATE, TOK_KVC))
    zc = tok(TOK_KVC, TOK_END) + pe_ref[...]
    kvc_ref[0, 0] = zc[:, 0:LANES]
    kvc_ref[1, 0] = zc[:, LANES:2 * LANES]

    zf = _dot_nt(wft_ref[...], xb)
    d = HEAD_DIM
    ksel = jnp.concatenate([_col_head_norm(zf[256 + g * d:256 + (g + 1) * d], gsel_ref[...])
                            for g in range(N_KV_HEADS)], axis=0)
    kwin = jnp.concatenate([_col_head_norm(zf[512 + g * d:512 + (g + 1) * d], gwin_ref[...])
                            for g in range(N_KV_HEADS)], axis=0)
    vsel = zf[384:512]
    vwin = zf[640:768]
    kvt_ref[0, 0:256] = zf[0:256]
    kvt_ref[0, 256:384] = ksel
    kvt_ref[0, 384:512] = vsel
    wint_ref[0, 0:128] = kwin
    wint_ref[0, 128:256] = vwin
    att_ref[0, 0:128] = ksel.astype(BF16)
    att_ref[0, 128:256] = vsel.astype(BF16)
    att_ref[0, 256:384] = kwin.astype(BF16)
    att_ref[0, 384:512] = vwin.astype(BF16)


def _proj_prompt(x, ln, wtok, wft, qg, lbl, pe_tok, gsel, gwin):
    b, t, dm = x.shape
    tm = PROJ_ROWS
    grid = (b, t // tm)
    row = lambda i, j: (i, j, 0)
    col = lambda i, j: (i, 0, j)
    const2 = lambda i, j: (0, 0)
    out_shape = (
        jax.ShapeDtypeStruct((b, t, ATTN_WIDTH), BF16),
        jax.ShapeDtypeStruct((b, t, LANES), F32),
        jax.ShapeDtypeStruct((b, t, RNN_WIDTH), F32),
        jax.ShapeDtypeStruct((b, t, RNN_WIDTH), F32),
        jax.ShapeDtypeStruct((b, t, RNN_WIDTH), F32),
        jax.ShapeDtypeStruct((b, t, RNN_WIDTH), F32),
        jax.ShapeDtypeStruct((2, b, t, LANES), F32),
        jax.ShapeDtypeStruct((b, 4 * KV_WIDTH, t), F32),
        jax.ShapeDtypeStruct((b, 2 * KV_WIDTH, t), F32),
        jax.ShapeDtypeStruct((b, 4 * KV_WIDTH, t), BF16),
    )
    out_specs = (
        pl.BlockSpec((1, tm, ATTN_WIDTH), row), pl.BlockSpec((1, tm, LANES), row),
        pl.BlockSpec((1, tm, RNN_WIDTH), row), pl.BlockSpec((1, tm, RNN_WIDTH), row),
        pl.BlockSpec((1, tm, RNN_WIDTH), row), pl.BlockSpec((1, tm, RNN_WIDTH), row),
        pl.BlockSpec((2, 1, tm, LANES), lambda i, j: (0, i, j, 0)),
        pl.BlockSpec((1, 4 * KV_WIDTH, tm), col), pl.BlockSpec((1, 2 * KV_WIDTH, tm), col),
        pl.BlockSpec((1, 4 * KV_WIDTH, tm), col),
    )
    in_specs = [
        pl.BlockSpec((1, tm, dm), row), pl.BlockSpec(ln.shape, const2),
        pl.BlockSpec(wtok.shape, const2), pl.BlockSpec(wft.shape, const2),
        pl.BlockSpec(qg.shape, const2), pl.BlockSpec(lbl.shape, const2), pl.BlockSpec(pe_tok.shape, const2),
        pl.BlockSpec(gsel.shape, const2), pl.BlockSpec(gwin.shape, const2),
    ]
    return pl.pallas_call(
        _proj_prompt_body, grid=grid, in_specs=in_specs, out_specs=out_specs, out_shape=out_shape,
        compiler_params=pltpu.CompilerParams(dimension_semantics=("parallel", "parallel"),
                                             vmem_limit_bytes=_vmem_limit(48 << 20)),
        name="proj_prompt",
    )(x, ln, wtok, wft, qg, lbl, pe_tok, gsel, gwin)


def _compress_prompt_body(x_ref, w1_ref, w2_ref, kg_ref, out_ref):
    c = pl.program_id(0)
    nb = out_ref.shape[2]
    acc = jnp.zeros((nb, 2 * CMP_HIDDEN), F32)
    for pos in range(CMP_BLOCK):
        xp = x_ref[0, 0, pl.ds(pos, nb, stride=CMP_BLOCK), :]
        acc = acc + _dot(xp.astype(BF16), w1_ref[0, pos])
    hb = _silu(acc).astype(BF16)
    outs = []
    for g in range(N_KV_HEADS):
        y = _dot(hb[:, g * CMP_HIDDEN:(g + 1) * CMP_HIDDEN], w2_ref[0])
        yn = _row_rms(y) * kg_ref[...]
        outs.append(jnp.where(c == 0, yn, y))
    out_ref[0, 0] = jnp.concatenate(outs, axis=1)


def _compress_prompt(kvc, w1bd, w2dup, kg_dup):
    _, b, t, _ = kvc.shape
    nb = t // CMP_BLOCK
    return pl.pallas_call(
        _compress_prompt_body, grid=(2, b),
        in_specs=[pl.BlockSpec((1, 1, t, LANES), lambda c, i: (c, i, 0, 0)),
                  pl.BlockSpec((1,) + w1bd.shape[1:], lambda c, i: (c, 0, 0, 0)),
                  pl.BlockSpec((1,) + w2dup.shape[1:], lambda c, i: (c, 0, 0)),
                  pl.BlockSpec(kg_dup.shape, lambda c, i: (0, 0))],
        out_specs=pl.BlockSpec((1, 1, nb, 2 * LANES), lambda c, i: (c, i, 0, 0)),
        out_shape=jax.ShapeDtypeStruct((2, b, nb, 2 * LANES), F32),
        compiler_params=pltpu.CompilerParams(dimension_semantics=("arbitrary", "arbitrary"),
                                             vmem_limit_bytes=_vmem_limit(32 << 20)),
        name="compress_prompt",
    )(kvc, w1bd, w2dup, kg_dup)


def _row_max_update(sc_scr, nm_scr, m_scr, mn_scr, ncol, tq):
    rb = ROW_BLOCK
    for i in range(GROUP * tq // rb):
        rows = slice(i * rb, (i + 1) * rb)
        qrows = slice((i * rb) % tq, (i * rb) % tq + rb)
        mx = jnp.full((rb, LANES), NEG, F32)
        for j in range(ncol):
            cols = slice(j * LANES, (j + 1) * LANES)
            v = sc_scr[rows, cols]
            if nm_scr is not None:
                v = v + nm_scr[qrows, cols]
                sc_scr[rows, cols] = v
            mx = jnp.maximum(mx, v)
        mn_scr[rows, :] = jnp.maximum(m_scr[rows, :], jnp.max(mx, axis=-1, keepdims=True))


def _softmax_update(sc_scr, p_scr, m_scr, mn_scr, l_scr, a_scr, ncol, tq):
    rb = ROW_BLOCK
    for i in range(GROUP * tq // rb):
        rows = slice(i * rb, (i + 1) * rb)
        mn = mn_scr[rows, :]
        tot = jnp.zeros((rb, LANES), F32)
        for j in range(ncol):
            cols = slice(j * LANES, (j + 1) * LANES)
            p = jnp.exp(sc_scr[rows, cols] - mn)
            tot = tot + p
            p_scr[rows, cols] = p.astype(BF16)
        alpha = jnp.exp(m_scr[rows, :] - mn)
        l_scr[rows, :] = alpha * l_scr[rows, :] + jnp.sum(tot, axis=-1, keepdims=True)
        a_scr[rows, :] = alpha
        m_scr[rows, :] = mn


def _nsa_prompt_body(q_ref, gates_ref, cmp_ref, att_ref, kfeat_ref, qfeat_ref, gexp_ref, oa_ref,
                     sc_scr, nm_scr, p_scr, m_scr, mn_scr, l_scr, a_scr, acc_scr, *, seq, topk):
    tq = NSA_Q_ROWS
    tk = min(SEL_KEYS, seq)
    nb = cmp_ref.shape[2]
    t0 = pl.program_id(1) * tq
    wk = min(WINDOW + tq, seq)
    ws = pl.multiple_of(jnp.clip(t0 - WINDOW, 0, seq - wk), LANES)
    q = q_ref[0]
    d = HEAD_DIM
    groups = range(N_KV_HEADS)
    o_cmp, o_win, qsel, qwin = [], [], [], []
    for g in groups:
        qst = _stack_group_queries(q, g)

        kc = cmp_ref[0, 0][:, g * LANES:(g + 1) * LANES].astype(BF16)
        vc = cmp_ref[1, 0][:, g * LANES:(g + 1) * LANES].astype(BF16)
        st = _dot_nt(kc, qst)
        blk = lax.broadcasted_iota(jnp.int32, (nb, GROUP * tq), 0)
        colq = lax.broadcasted_iota(jnp.int32, (1, GROUP * tq), 1)
        tpos = t0 + (colq & (tq - 1))
        dist = (tpos - (blk * CMP_BLOCK + (CMP_BLOCK - 1))).astype(F32)
        valid = dist >= 0.0
        s = jnp.where(valid, st - _row_slopes(g, colq >> _log2(tq)) * dist, NEG)
        e = jnp.exp(s - jnp.max(s, axis=0, keepdims=True))
        p = jnp.where(valid, e / jnp.sum(e, axis=0, keepdims=True), 0.0)
        o_cmp.append(_unstack_group(_dot_tn(p.astype(BF16), vc), tq))
        imp = p[:, 0:tq] + p[:, tq:2 * tq] + p[:, 2 * tq:3 * tq] + p[:, 3 * tq:4 * tq]

        bj = lax.broadcasted_iota(jnp.int32, (nb, tq), 0)
        cur = (t0 + lax.broadcasted_iota(jnp.int32, (nb, tq), 1)) >> _log2(SEL_BLOCK)
        force = (bj == 0) | (bj == cur)
        score = jnp.where(bj <= cur, jnp.where(force, FORCED_SCORE, imp), -1.0)
        rank = _block_ranks(score, bj, [(i, i) for i in range(nb)])
        mneg = jnp.where((rank < topk) & (score >= 0.0), 0.0, NEG)
        mtok = jnp.concatenate([mneg, jnp.zeros((LANES - nb, tq), F32)], axis=0).T
        qf = qfeat_ref[g]
        qwin.append(jnp.concatenate([qst, qf.astype(BF16)], axis=1))
        qsel.append(jnp.concatenate([qst, (qf + jnp.concatenate([mtok] * GROUP, axis=0)).astype(BF16)], axis=1))

    def reset(stats):
        for g in groups:
            m_scr[g] = jnp.full(m_scr.shape[1:], NEG, F32)
            for ref in stats:
                ref[g] = jnp.zeros(ref.shape[1:], F32)

    def logits(g, qa, kt_, kf, width):
        sc_scr[g, :, 0:width] = _dot(qa, jnp.concatenate([kt_, kt_, kf], axis=0))

    def softmax_tiles(ncol, masked):
        for g in groups:
            _row_max_update(sc_scr.at[g], nm_scr if masked else None, m_scr.at[g], mn_scr.at[g], ncol, tq)
        for g in groups:
            _softmax_update(sc_scr.at[g], p_scr.at[g], m_scr.at[g], mn_scr.at[g], l_scr.at[g], a_scr.at[g], ncol, tq)

    reset((l_scr, acc_scr))
    n_kt = (t0 + tq - 1) // tk + 1

    def kv_step(kt, causal):
        s0 = pl.multiple_of(kt * tk, tk)
        kf = kfeat_ref[:, pl.ds(s0, tk)]
        if causal:
            dd = (t0 - s0) + lax.broadcasted_iota(jnp.int32, (tq, tk), 0) - lax.broadcasted_iota(jnp.int32, (tq, tk), 1)
            nm_scr[:, 0:tk] = jnp.where(dd >= 0, 0.0, NEG)
        for g in groups:
            logits(g, qsel[g], att_ref[0, g * d:(g + 1) * d, pl.ds(s0, tk)], kf, tk)
        softmax_tiles(tk // LANES, causal)
        for g in groups:
            vt_ = att_ref[0, KV_WIDTH + g * d:KV_WIDTH + (g + 1) * d, pl.ds(s0, tk)]
            acc_scr[g] = a_scr[g] * acc_scr[g] + _dot_nt(p_scr[g, :, 0:tk], _twice(vt_))

    def full_tile(kt, carry):
        kv_step(kt, False)
        return carry

    lax.fori_loop(0, n_kt - 1, full_tile, 0)
    kv_step(n_kt - 1, True)
    o_sel = [_unstack_group(acc_scr[g] / l_scr[g], tq) for g in groups]

    reset((l_scr,))
    kfw = kfeat_ref[:, pl.ds(ws, wk)]
    ddw = (t0 - ws) + lax.broadcasted_iota(jnp.int32, (tq, wk), 0) - lax.broadcasted_iota(jnp.int32, (tq, wk), 1)
    nm_scr[:, 0:wk] = jnp.where((ddw >= 0) & (ddw < WINDOW), 0.0, NEG)
    for g in groups:
        logits(g, qwin[g], att_ref[0, 2 * KV_WIDTH + g * d:2 * KV_WIDTH + (g + 1) * d, pl.ds(ws, wk)], kfw, wk)
    softmax_tiles(wk // LANES, True)
    for g in groups:
        vw = att_ref[0, 3 * KV_WIDTH + g * d:3 * KV_WIDTH + (g + 1) * d, pl.ds(ws, wk)]
        o_win.append(_unstack_group(_dot_nt(p_scr[g, :, 0:wk], _twice(vw)) / l_scr[g], tq))

    ge = _expand_gates(gates_ref[0], gexp_ref)
    oa_ref[0] = (ge[0] * jnp.concatenate(o_cmp, axis=1) + ge[1] * jnp.concatenate(o_sel, axis=1)
                 + ge[2] * jnp.concatenate(o_win, axis=1))


def _nsa_prompt(q, gates, cmpkv, att, k_feat, q_feat, gexp):
    b, t, _ = q.shape
    tq = NSA_Q_ROWS
    nb = t // CMP_BLOCK
    assert nb <= HEAD_DIM, "the block mask uses 64 feature lanes"
    cw = max(min(SEL_KEYS, t), min(WINDOW + tq, t))
    body = functools.partial(_nsa_prompt_body, seq=t, topk=min(SEL_TOPK, nb))
    ng = N_KV_HEADS
    stat = pltpu.VMEM((ng, GROUP * tq, LANES), F32)
    return pl.pallas_call(
        body, grid=(b, t // tq),
        in_specs=[pl.BlockSpec((1, tq, ATTN_WIDTH), lambda i, j: (i, j, 0)),
                  pl.BlockSpec((1, tq, LANES), lambda i, j: (i, j, 0)),
                  pl.BlockSpec((2, 1, nb, 2 * LANES), lambda i, j: (0, i, 0, 0)),
                  pl.BlockSpec((1, 4 * KV_WIDTH, t), lambda i, j: (i, 0, 0)),
                  pl.BlockSpec(k_feat.shape, lambda i, j: (0, 0)),
                  pl.BlockSpec(q_feat.shape, lambda i, j: (0, 0, 0)),
                  pl.BlockSpec(gexp.shape, lambda i, j: (0, 0, 0))],
        out_specs=pl.BlockSpec((1, tq, ATTN_WIDTH), lambda i, j: (i, j, 0)),
        out_shape=jax.ShapeDtypeStruct((b, t, ATTN_WIDTH), F32),
        scratch_shapes=[pltpu.VMEM((ng, GROUP * tq, cw), F32), pltpu.VMEM((tq, cw), F32),
                        pltpu.VMEM((ng, GROUP * tq, cw), BF16), stat, stat, stat, stat, stat],
        compiler_params=pltpu.CompilerParams(dimension_semantics=("parallel", "parallel"),
                                             vmem_limit_bytes=_vmem_limit(48 << 20)),
        name="nsa_prompt",
    )(q, gates, cmpkv, att, k_feat, q_feat, gexp)


def _split3(x):
    hi = x.astype(BF16)
    r1 = x - hi.astype(F32)
    mid = r1.astype(BF16)
    lo = (r1 - mid.astype(F32)).astype(BF16)
    return hi, mid, lo


def _hgrn_prompt_body(rq_ref, f_ref, rv_ref, ltri_ref, bd_ref, ind_ref, indt_ref, o_ref, st_ref,
                      s_scr, cum_scr, k_scr):
    c16 = RNN_CHUNK
    nbat = rq_ref.shape[0]

    @pl.when(pl.program_id(0) == 0)
    def _():
        s_scr[...] = jnp.zeros(s_scr.shape, F32)

    ltri = ltri_ref[...]
    for bi in range(nbat):
        f = f_ref[bi]
        hi, mid, lo = _split3(jnp.log(f))
        cum = (_dot(ltri, hi) + _dot(ltri, mid) + _dot(ltri, lo)) * LOG2E
        cum_scr[bi] = cum
        k_scr[bi] = cum - jnp.log2(1.0 - f)

    si = lax.broadcasted_iota(jnp.int32, (c16, c16, RNN_WIDTH), 0)
    ti = lax.broadcasted_iota(jnp.int32, (c16, c16, RNN_WIDTH), 1)
    causal = si <= ti
    npair = RNN_WIDTH // LANES

    def chunk(c, carry):
        r0 = pl.multiple_of(c * c16, c16)
        for bi in range(nbat):
            cc = cum_scr[bi, pl.ds(r0, c16), :]
            qc = rq_ref[bi, pl.ds(r0, c16), :]
            lk = k_scr[bi, pl.ds(r0, c16), :]
            vc = rv_ref[bi, pl.ds(r0, c16), :]
            last = cc[c16 - 1:c16, :]
            qd = (qc * jnp.exp2(cc)).astype(BF16)
            kd = jnp.exp2(last - lk).astype(BF16)
            dl = jnp.exp2(last)
            vb = vc.astype(BF16)
            o_int = jnp.concatenate([_dot_nt(qd[:, p * LANES:(p + 1) * LANES], s_scr[bi, p].astype(BF16))
                                     for p in range(npair)], axis=1)
            dec = jnp.exp2(jnp.where(causal, cc[None, :, :] - lk[:, None, :], NEG))
            prod = (qc[None, :, :] * dec).reshape(c16 * c16, RNN_WIDTH)
            a = _dot(prod.astype(BF16), ind_ref[...])
            a = _dot(a.astype(BF16), indt_ref[...]).reshape(c16, c16, RNN_WIDTH)
            o_intra = jnp.sum(a * vc[:, None, :], axis=0)
            o_ref[bi, pl.ds(r0, c16), :] = o_int + o_intra
            for p in range(npair):
                u = _dot_tn(vb[:, p * LANES:(p + 1) * LANES], kd[:, p * LANES:(p + 1) * LANES])
                s_scr[bi, p] = s_scr[bi, p] * dl[:, p * LANES:(p + 1) * LANES] + u * bd_ref[...]
        return carry

    lax.fori_loop(0, rq_ref.shape[1] // c16, chunk, 0)
    st_ref[...] = s_scr[...]


def _hgrn_prompt(rq, f, rv, ltri, bd, ind):
    b, t, w = rq.shape
    tc = RNN_ROWS
    npair = w // LANES
    row = lambda j: (0, j, 0)
    const = lambda j: (0, 0)
    indt = ind.T
    return pl.pallas_call(
        _hgrn_prompt_body, grid=(t // tc,),
        in_specs=[pl.BlockSpec((b, tc, w), row), pl.BlockSpec((b, tc, w), row), pl.BlockSpec((b, tc, w), row),
                  pl.BlockSpec(ltri.shape, const), pl.BlockSpec(bd.shape, const),
                  pl.BlockSpec(ind.shape, const), pl.BlockSpec(indt.shape, const)],
        out_specs=(pl.BlockSpec((b, tc, w), row), pl.BlockSpec((b, npair, LANES, LANES), lambda j: (0, 0, 0, 0))),
        out_shape=(jax.ShapeDtypeStruct((b, t, w), F32), jax.ShapeDtypeStruct((b, npair, LANES, LANES), F32)),
        scratch_shapes=[pltpu.VMEM((b, npair, LANES, LANES), F32), pltpu.VMEM((b, tc, w), F32),
                        pltpu.VMEM((b, tc, w), F32)],
        compiler_params=pltpu.CompilerParams(dimension_semantics=("arbitrary",),
                                             vmem_limit_bytes=_vmem_limit(40 << 20)),
        name="hgrn_prompt",
    )(rq, f, rv, ltri, bd, ind, indt)


def _finish_body(x_ref, oa_ref, orn_ref, rgs_ref, ag_ref, rg_ref, wout_ref, lnm_ref, wup_ref, wdn_ref, y_ref, hn_scr):
    @pl.when(pl.program_id(1) == 0)
    def _():
        oa = oa_ref[...]
        orn = orn_ref[...]
        a_n = oa * lax.rsqrt(_head_mean_sq(oa) + EPS) * ag_ref[...]
        r_n = orn * lax.rsqrt(_head_mean_sq(orn) + EPS) * rg_ref[...] * rgs_ref[...]
        h = (x_ref[...] + _dot(a_n.astype(BF16), wout_ref[0:ATTN_WIDTH, :])
             + _dot(r_n.astype(BF16), wout_ref[ATTN_WIDTH:ATTN_WIDTH + RNN_WIDTH, :]))
        y_ref[...] = h
        hn_scr[...] = (_row_rms(h) * lnm_ref[...]).astype(BF16)

    u = jnp.maximum(_dot(hn_scr[...], wup_ref[...]), 0.0)
    y_ref[...] += _dot((u * u).astype(BF16), wdn_ref[...])


def _finish(x, oa, orn, rgs, ag, rg, wout, lnm, wup, wdn):
    n, dm = x.shape
    tm = FINISH_ROWS
    dff = wup.shape[1]
    row = lambda i, j: (i, 0)
    const = lambda i, j: (0, 0)
    return pl.pallas_call(
        _finish_body, grid=(n // tm, dff // FF_CHUNK),
        in_specs=[pl.BlockSpec((tm, dm), row), pl.BlockSpec((tm, ATTN_WIDTH), row),
                  pl.BlockSpec((tm, RNN_WIDTH), row), pl.BlockSpec((tm, RNN_WIDTH), row),
                  pl.BlockSpec(ag.shape, const), pl.BlockSpec(rg.shape, const),
                  pl.BlockSpec(wout.shape, const), pl.BlockSpec(lnm.shape, const),
                  pl.BlockSpec((dm, FF_CHUNK), lambda i, j: (0, j)),
                  pl.BlockSpec((FF_CHUNK, dm), lambda i, j: (j, 0))],
        out_specs=pl.BlockSpec((tm, dm), row),
        out_shape=jax.ShapeDtypeStruct((n, dm), F32),
        scratch_shapes=[pltpu.VMEM((tm, dm), BF16)],
        compiler_params=pltpu.CompilerParams(dimension_semantics=("parallel", "arbitrary"),
                                             vmem_limit_bytes=_vmem_limit(48 << 20)),
        name="finish",
    )(x, oa, orn, rgs, ag, rg, wout, lnm, wup, wdn)


def _proj_sample_body(xbt_ref, xtb_ref, ln_ref, wtok_ref, wtb_ref, wbt_ref, qg_ref, lbl_ref, gsel_ref, gwin_ref,
                      q_ref, rgs_ref, gates_ref, ztb_ref, zbt_ref):
    xb = (_row_rms(xbt_ref[...]) * ln_ref[...]).astype(BF16)
    xt = (_row_rms(xtb_ref[...]) * ln_ref[...]).astype(BF16)
    zq = _dot(xb, wtok_ref[:, 0:ATTN_WIDTH])
    q_ref[...] = (zq * lax.rsqrt(_head_mean_sq(zq) + EPS) * qg_ref[...]).astype(BF16)
    rgs_ref[...] = _silu(_dot(xb, wtok_ref[:, ATTN_WIDTH:ATTN_WIDTH + RNN_WIDTH]))
    gates_ref[...] = jax.nn.sigmoid(_dot(xb, wtok_ref[:, ATTN_WIDTH + RNN_WIDTH:ATTN_WIDTH + RNN_WIDTH + LANES]))

    d = HEAD_DIM
    zt = _dot_nt(wtb_ref[...], xt)
    ztb_ref[0:256] = zt[0:256]
    for g in range(N_KV_HEADS):
        ztb_ref[256 + g * d:256 + (g + 1) * d] = _col_head_norm(zt[256 + g * d:256 + (g + 1) * d], gsel_ref[...])
    ztb_ref[384:512] = zt[384:512]
    ztb_ref[512:1024] = _silu(zt[512:1024])
    lb = _lower_bound(lbl_ref[...], 0)[0]
    ztb_ref[1024:1536] = lb + (1.0 - lb) * jax.nn.sigmoid(zt[1024:1536])
    ztb_ref[1536:2048] = zt[1536:2048]

    zb = _dot_nt(wbt_ref[...], xb)
    for g in range(N_KV_HEADS):
        zbt_ref[g * d:(g + 1) * d] = _col_head_norm(zb[g * d:(g + 1) * d], gsel_ref[...])
        zbt_ref[256 + g * d:256 + (g + 1) * d] = _col_head_norm(zb[256 + g * d:256 + (g + 1) * d], gwin_ref[...])
    zbt_ref[128:256] = zb[128:256]
    zbt_ref[384:512] = zb[384:512]


def _proj_sample(xbt, xtb, ln, wtok, wtb, wbt, qg, lblt, gsel, gwin):
    n, dm = xbt.shape
    tm = PROJ_ROWS
    row = lambda i: (i, 0)
    col = lambda i: (0, i)
    const = lambda i: (0, 0)
    return pl.pallas_call(
        _proj_sample_body, grid=(n // tm,),
        in_specs=[pl.BlockSpec((tm, dm), row), pl.BlockSpec((tm, dm), row), pl.BlockSpec(ln.shape, const),
                  pl.BlockSpec(wtok.shape, const), pl.BlockSpec(wtb.shape, const), pl.BlockSpec(wbt.shape, const),
                  pl.BlockSpec(qg.shape, const), pl.BlockSpec(lblt.shape, lambda i: (0, 0, 0)),
                  pl.BlockSpec(gsel.shape, const), pl.BlockSpec(gwin.shape, const)],
        out_specs=(pl.BlockSpec((tm, ATTN_WIDTH), row), pl.BlockSpec((tm, RNN_WIDTH), row),
                   pl.BlockSpec((tm, LANES), row), pl.BlockSpec((wtb.shape[0], tm), col),
                   pl.BlockSpec((wbt.shape[0], tm), col)),
        out_shape=(jax.ShapeDtypeStruct((n, ATTN_WIDTH), BF16), jax.ShapeDtypeStruct((n, RNN_WIDTH), F32),
                   jax.ShapeDtypeStruct((n, LANES), F32), jax.ShapeDtypeStruct((wtb.shape[0], n), F32),
                   jax.ShapeDtypeStruct((wbt.shape[0], n), F32)),
        compiler_params=pltpu.CompilerParams(dimension_semantics=("parallel",),
                                             vmem_limit_bytes=_vmem_limit(40 << 20)),
        name="proj_sample",
    )(xbt, xtb, ln, wtok, wtb, wbt, qg, lblt, gsel, gwin)


PAGE_ROWS = 4 * KV_WIDTH
SAMPLE_BATCH_PER_STEP = 2
WINDOW_BATCH_PER_STEP = 4


def _nsa_sample_body(pt_ref, cache_ref, q_ref, nkv_ref, w1_ref, pe_ref, w2_ref, kg_ref, e_ref,
                     ocs_ref, cbuf, sbuf, lhs_scr, kk_scr, vv_scr, sem, *, n_pages, past_len, dec_t, topk):
    step_id = pl.program_id(0)
    nsteps = pl.num_programs(0)
    slot = step_id % 2
    d = HEAD_DIM
    nblk = 2 * n_pages
    half_rows = PAGE_ROWS // 2
    bps = q_ref.shape[0]

    def page_copies(st, sl, bi, j):
        pg = pt_ref[(st * bps + bi) * n_pages + j]
        return (pltpu.make_async_copy(cache_ref.at[pg, pl.ds(0, half_rows)], cbuf.at[sl, bi, :, j, :], sem.at[sl]),
                pltpu.make_async_copy(cache_ref.at[pg, pl.ds(half_rows, half_rows)],
                                      sbuf.at[sl, bi, pl.ds(j * half_rows, half_rows)], sem.at[sl]))

    def all_copies(st, sl):
        return [cp for bi in range(bps) for j in range(n_pages) for cp in page_copies(st, sl, bi, j)]

    @pl.when(step_id == 0)
    def _():
        for cp in all_copies(0, 0):
            cp.start()

    @pl.when(step_id + 1 < nsteps)
    def _():
        for cp in all_copies(step_id + 1, 1 - slot):
            cp.start()

    for cp in all_copies(step_id, slot):
        cp.wait()

    low = _low_half((n_pages, LANES))
    rows_b = 4 * n_pages

    def compress(c):
        for dd in range(d):
            rows = []
            for bi in range(bps):
                for g in range(N_KV_HEADS):
                    xg = cbuf[slot, bi, (c * N_KV_HEADS + g) * d + dd]
                    xg = xg + pe_ref[c, dd]
                    rows += [jnp.where(low, xg, 0.0), jnp.where(low, 0.0, xg)]
            lhs_scr[:, dd * LANES:(dd + 1) * LANES] = jnp.concatenate(rows, axis=0).astype(BF16)
        acc = _dot(lhs_scr[...], w1_ref[c])
        return _dot(_silu(acc).astype(BF16), w2_ref[c])

    kc_all = _row_rms(compress(0)) * kg_ref[...]
    vc_all = compress(1)

    nq = GROUP * dec_t
    rho = lax.broadcasted_iota(jnp.int32, (nblk, dec_t), 0)
    bid = 2 * (rho & (n_pages - 1)) + (rho >> _log2(n_pages))
    colq = lax.broadcasted_iota(jnp.int32, (1, nq), 1)
    rowq = lax.broadcasted_iota(jnp.int32, (nq, 1), 0)
    cur_blk = past_len // SEL_BLOCK
    o_cmp, o_sel = [], []
    for bi, g in [(bi, g) for bi in range(bps) for g in range(N_KV_HEADS)]:
        ch = bi * N_KV_HEADS + g
        qst = _stack_group_queries(q_ref[bi], g)
        kc = kc_all[bi * rows_b + g * nblk:bi * rows_b + (g + 1) * nblk].astype(BF16)
        vc = vc_all[bi * rows_b + g * nblk:bi * rows_b + (g + 1) * nblk].astype(BF16)
        st = _dot_nt(kc, qst)
        qpos = past_len + (colq & (dec_t - 1))
        rho_q = lax.broadcasted_iota(jnp.int32, (nblk, 1), 0)
        end = (2 * (rho_q & (n_pages - 1)) + (rho_q >> _log2(n_pages))) * CMP_BLOCK + (CMP_BLOCK - 1)
        dist = (qpos - end).astype(F32)
        valid = dist >= 0.0
        s = jnp.where(valid, st - _row_slopes(g, colq >> _log2(dec_t)) * dist, NEG)
        e = jnp.exp(s - jnp.max(s, axis=0, keepdims=True))
        p = jnp.where(valid, e / jnp.sum(e, axis=0, keepdims=True), 0.0)
        o_cmp.append(_unstack_group(_dot_tn(p.astype(BF16), vc), dec_t))
        imp = p[:, 0:dec_t]
        for r in range(1, GROUP):
            imp = imp + p[:, r * dec_t:(r + 1) * dec_t]

        force = (bid == 0) | (bid == cur_blk)
        score = jnp.where(force, FORCED_SCORE, imp)
        rank = _block_ranks(score, bid, [(i, 2 * (i % n_pages) + i // n_pages) for i in range(nblk)])
        rank = rank + jnp.where(bid > cur_blk, (FORCED_SCORE >= score).astype(F32),
                                (FORCED_SCORE > score).astype(F32))
        msel = jnp.where((rank < topk) & (score >= 0.0), 1.0, 0.0).astype(BF16)

        for pg in range(n_pages):
            kt_ = sbuf[slot, bi, pl.ds(pg * half_rows + g * d, d), :].astype(BF16)
            vt_ = sbuf[slot, bi, pl.ds(pg * half_rows + (N_KV_HEADS + g) * d, d), :].astype(BF16)
            kk_scr[ch, 0:d, pg * PAGE_SIZE:(pg + 1) * PAGE_SIZE] = kt_
            kk_scr[ch, d:2 * d, pg * PAGE_SIZE:(pg + 1) * PAGE_SIZE] = kt_
            vv_scr[ch, 0:d, pg * PAGE_SIZE:(pg + 1) * PAGE_SIZE] = vt_
            vv_scr[ch, d:2 * d, pg * PAGE_SIZE:(pg + 1) * PAGE_SIZE] = vt_
        nk = _twice(nkv_ref[bi, g * d:(g + 1) * d, :].astype(BF16))
        nv = _twice(nkv_ref[bi, KV_WIDTH + g * d:KV_WIDTH + (g + 1) * d, :].astype(BF16))
        sp = _dot(qst, kk_scr[ch])
        sn = _dot(qst, nk)
        mk = _dot_tn(msel, e_ref[...])
        okp = jnp.concatenate([mk] * GROUP, axis=0) > 0.5
        step = rowq & (dec_t - 1)
        slope = _row_slopes(g, rowq >> _log2(dec_t))
        kpos = lax.broadcasted_iota(jnp.int32, (1, past_len), 1)
        dpast = (past_len + step - kpos).astype(F32)
        lgp = jnp.where(okp & (dpast >= 0.0), sp - slope * dpast, NEG)
        lane = lax.broadcasted_iota(jnp.int32, (1, LANES), 1)
        dnew = (step - lane).astype(F32)
        lgn = jnp.where((lane < dec_t) & (dnew >= 0.0), sn - slope * dnew, NEG)
        m = jnp.maximum(jnp.max(lgp, axis=-1, keepdims=True), jnp.max(lgn, axis=-1, keepdims=True))
        pp = jnp.exp(lgp - m)
        pn = jnp.exp(lgn - m)
        den = jnp.sum(pp, axis=-1, keepdims=True) + jnp.sum(pn, axis=-1, keepdims=True)
        o2 = _dot_nt(pp.astype(BF16), vv_scr[ch]) + _dot_nt(pn.astype(BF16), nv)
        o_sel.append(_unstack_group(o2 / den, dec_t))

    for bi in range(bps):
        ocs_ref[bi, 0] = jnp.concatenate(o_cmp[bi * N_KV_HEADS:(bi + 1) * N_KV_HEADS], axis=1)
        ocs_ref[bi, 1] = jnp.concatenate(o_sel[bi * N_KV_HEADS:(bi + 1) * N_KV_HEADS], axis=1)


def _nsa_sample(page_flat, cache, q, nkv, w1r, pe_t, w2dup, kg_dup, e_perm, past_len, topk):
    nbatch, dec_t, _ = q.shape
    n_pages = past_len // PAGE_SIZE
    bps = SAMPLE_BATCH_PER_STEP
    nch = bps * N_KV_HEADS
    body = functools.partial(_nsa_sample_body, n_pages=n_pages, past_len=past_len, dec_t=dec_t, topk=topk)
    grid_spec = pltpu.PrefetchScalarGridSpec(
        num_scalar_prefetch=1, grid=(nbatch // bps,),
        in_specs=[pl.BlockSpec(memory_space=pl.ANY),
                  pl.BlockSpec((bps, dec_t, ATTN_WIDTH), lambda i, pt: (i, 0, 0)),
                  pl.BlockSpec((bps,) + nkv.shape[1:], lambda i, pt: (i, 0, 0)),
                  pl.BlockSpec(w1r.shape, lambda i, pt: (0, 0, 0)),
                  pl.BlockSpec(pe_t.shape, lambda i, pt: (0, 0, 0, 0)),
                  pl.BlockSpec(w2dup.shape, lambda i, pt: (0, 0, 0)),
                  pl.BlockSpec(kg_dup.shape, lambda i, pt: (0, 0)),
                  pl.BlockSpec(e_perm.shape, lambda i, pt: (0, 0))],
        out_specs=pl.BlockSpec((bps, 2, dec_t, ATTN_WIDTH), lambda i, pt: (i, 0, 0, 0)),
        scratch_shapes=[pltpu.VMEM((2, bps, PAGE_ROWS // 2, n_pages, PAGE_SIZE), F32),
                        pltpu.VMEM((2, bps, n_pages * PAGE_ROWS // 2, PAGE_SIZE), F32),
                        pltpu.VMEM((bps * 4 * n_pages, HEAD_DIM * LANES), BF16),
                        pltpu.VMEM((nch, 2 * HEAD_DIM, past_len), BF16),
                        pltpu.VMEM((nch, 2 * HEAD_DIM, past_len), BF16),
                        pltpu.SemaphoreType.DMA((2,))])
    return pl.pallas_call(
        body, grid_spec=grid_spec,
        out_shape=jax.ShapeDtypeStruct((nbatch, 2, dec_t, ATTN_WIDTH), F32),
        compiler_params=pltpu.CompilerParams(dimension_semantics=("arbitrary",),
                                             vmem_limit_bytes=_vmem_limit(40 << 20)),
        name="nsa_sample",
    )(page_flat, cache, q, nkv, w1r, pe_t, w2dup, kg_dup, e_perm)


def _win_sample_body(win_ref, q_ref, nw_ref, ocs_ref, gates_ref, gexp_ref, oa_ref, wout_ref, *, past_len, dec_t):
    d = HEAD_DIM
    wbuf = win_ref.shape[2]
    nq = GROUP * dec_t
    rowq = lax.broadcasted_iota(jnp.int32, (nq, 1), 0)
    step = rowq & (dec_t - 1)
    kpos = past_len - wbuf + lax.broadcasted_iota(jnp.int32, (1, wbuf), 1)
    dpast = (past_len + step - kpos).astype(F32)
    okp = (dpast >= 0.0) & (dpast < WINDOW)
    lane = lax.broadcasted_iota(jnp.int32, (1, LANES), 1)
    dnew = (step - lane).astype(F32)
    okn = (lane < dec_t) & (dnew >= 0.0) & (dnew < WINDOW)
    for bi in range(win_ref.shape[0]):
        q = q_ref[bi]
        o_win = []
        for g in range(N_KV_HEADS):
            qst = _stack_group_queries(q, g)
            kt_ = win_ref[bi, g * d:(g + 1) * d, :].astype(BF16)
            vt_ = win_ref[bi, KV_WIDTH + g * d:KV_WIDTH + (g + 1) * d, :].astype(BF16)
            nk = _twice(nw_ref[bi, g * d:(g + 1) * d, :].astype(BF16))
            nv = _twice(nw_ref[bi, KV_WIDTH + g * d:KV_WIDTH + (g + 1) * d, :].astype(BF16))
            slope = _row_slopes(g, rowq >> _log2(dec_t))
            lgp = jnp.where(okp, _dot(qst, _twice(kt_)) - slope * dpast, NEG)
            lgn = jnp.where(okn, _dot(qst, nk) - slope * dnew, NEG)
            m = jnp.maximum(jnp.max(lgp, axis=-1, keepdims=True), jnp.max(lgn, axis=-1, keepdims=True))
            pp = jnp.exp(lgp - m)
            pn = jnp.exp(lgn - m)
            den = jnp.sum(pp, axis=-1, keepdims=True) + jnp.sum(pn, axis=-1, keepdims=True)
            o2 = _dot_nt(pp.astype(BF16), _twice(vt_)) + _dot_nt(pn.astype(BF16), nv)
            o_win.append(_unstack_group(o2 / den, dec_t))
        ge = _expand_gates(gates_ref[bi], gexp_ref)
        oa_ref[bi] = ge[0] * ocs_ref[bi, 0] + ge[1] * ocs_ref[bi, 1] + ge[2] * jnp.concatenate(o_win, axis=1)

        rolled = pltpu.roll(win_ref[bi], wbuf - dec_t, 1)
        newr = pltpu.roll(nw_ref[bi], LANES - dec_t, 1)
        wout_ref[bi, :, 0:wbuf - LANES] = rolled[:, 0:wbuf - LANES]
        wout_ref[bi, :, wbuf - LANES:wbuf] = jnp.where(lane >= LANES - dec_t, newr, rolled[:, wbuf - LANES:wbuf])


def _win_sample(win, q, nw, ocs, gates, gexp, past_len):
    nbatch, feat, wbuf = win.shape
    dec_t = q.shape[1]
    body = functools.partial(_win_sample_body, past_len=past_len, dec_t=dec_t)
    b3 = lambda i: (i, 0, 0)
    bps = WINDOW_BATCH_PER_STEP
    return pl.pallas_call(
        body, grid=(nbatch // bps,),
        in_specs=[pl.BlockSpec((bps, feat, wbuf), b3), pl.BlockSpec((bps, dec_t, ATTN_WIDTH), b3),
                  pl.BlockSpec((bps, feat, LANES), b3),
                  pl.BlockSpec((bps, 2, dec_t, ATTN_WIDTH), lambda i: (i, 0, 0, 0)),
                  pl.BlockSpec((bps, dec_t, LANES), b3), pl.BlockSpec(gexp.shape, lambda i: (0, 0, 0))],
        out_specs=(pl.BlockSpec((bps, dec_t, ATTN_WIDTH), b3), pl.BlockSpec((bps, feat, wbuf), b3)),
        out_shape=(jax.ShapeDtypeStruct((nbatch, dec_t, ATTN_WIDTH), F32),
                   jax.ShapeDtypeStruct((nbatch, feat, wbuf), F32)),
        compiler_params=pltpu.CompilerParams(dimension_semantics=("parallel",),
                                             vmem_limit_bytes=_vmem_limit(24 << 20)),
        name="win_sample",
    )(win, q, nw, ocs, gates, gexp)


def _hgrn_sample_body(q_ref, f_ref, v_ref, s_ref, o_ref, so_ref, *, dec_t):
    nb = s_ref.shape[3]
    o_ref[...] = jnp.zeros(o_ref.shape, F32)

    sub = 8

    def per_tile(i, carry):
        r0 = pl.multiple_of(i * sub, sub)
        f_t = [f_ref[pl.ds(r0, sub), pl.ds(t * nb, nb)] for t in range(dec_t)]
        q_t = [q_ref[pl.ds(r0, sub), pl.ds(t * nb, nb)] for t in range(dec_t)]
        for j in range(sub):
            s = s_ref[0, r0 + j]
            for t in range(dec_t):
                cols = pl.ds(t * nb, nb)
                fr = f_t[t][j:j + 1, :]
                s = fr * s + (1.0 - fr) * v_ref[:, cols]
                o_ref[:, cols] = o_ref[:, cols] + s * q_t[t][j:j + 1, :]
            so_ref[0, r0 + j] = s
        return carry

    lax.fori_loop(0, s_ref.shape[1] // sub, per_tile, 0)


def _hgrn_sample(ztb, state, dec_t):
    nh, dk, dv, nb = state.shape
    n = ztb.shape[1]
    body = functools.partial(_hgrn_sample_body, dec_t=dec_t)
    q0, f0, v0 = 512 // dk, 1024 // dk, 1536 // dk
    return pl.pallas_call(
        body, grid=(nh,),
        in_specs=[pl.BlockSpec((dk, n), lambda h: (q0 + h, 0)), pl.BlockSpec((dk, n), lambda h: (f0 + h, 0)),
                  pl.BlockSpec((dv, n), lambda h: (v0 + h, 0)),
                  pl.BlockSpec((1, dk, dv, nb), lambda h: (h, 0, 0, 0))],
        out_specs=(pl.BlockSpec((dv, n), lambda h: (h, 0)), pl.BlockSpec((1, dk, dv, nb), lambda h: (h, 0, 0, 0))),
        out_shape=(jax.ShapeDtypeStruct((nh * dv, n), F32), jax.ShapeDtypeStruct(state.shape, F32)),
        compiler_params=pltpu.CompilerParams(dimension_semantics=("parallel",),
                                             vmem_limit_bytes=_vmem_limit(24 << 20)),
        name="hgrn_sample",
    )(ztb, ztb, ztb, state)


def _gate_expander():
    m = np.zeros((N_BRANCH, LANES, ATTN_WIDTH), np.float32)
    for br in range(N_BRANCH):
        for h in range(N_ATTN_HEADS):
            m[br, h * N_BRANCH + br, h * HEAD_DIM:(h + 1) * HEAD_DIM] = 1.0
    return jnp.asarray(m, BF16)


def _block_expander(block_ids, n_keys):
    key_blk = np.arange(n_keys) // SEL_BLOCK
    return jnp.asarray((np.asarray(block_ids)[:, None] == key_blk[None, :]).astype(np.float32), BF16)


def _key_features(n_keys):
    s = np.arange(n_keys)
    m = np.zeros((LANES, n_keys), np.float32)
    m[0:HEAD_DIM] = (np.arange(HEAD_DIM)[:, None] == (s // SEL_BLOCK)[None, :])
    m[HEAD_DIM] = s // SEL_BLOCK
    m[HEAD_DIM + 1] = s % SEL_BLOCK
    return jnp.asarray(m, BF16)


def _query_slope_features(rows):
    m = np.zeros((N_KV_HEADS, GROUP * rows, LANES), np.float32)
    for g in range(N_KV_HEADS):
        for r in range(GROUP):
            m[g, r * rows:(r + 1) * rows, HEAD_DIM] = SLOPES[g][r] * SEL_BLOCK
            m[g, r * rows:(r + 1) * rows, HEAD_DIM + 1] = SLOPES[g][r]
    return jnp.asarray(m, F32)


def _chunk_lower_tri(n, c):
    i = np.arange(n)
    return jnp.asarray(((i[:, None] // c == i[None, :] // c) & (i[None, :] <= i[:, None])).astype(np.float32), BF16)


def _head_indicator():
    m = np.zeros((RNN_WIDTH, LANES), np.float32)
    m[np.arange(RNN_WIDTH), np.arange(RNN_WIDTH) // RNN_DK] = 1.0
    return jnp.asarray(m, BF16)


def _head_block_diag(n):
    i = np.arange(n)
    return jnp.asarray((i[:, None] // RNN_DV == i[None, :] // RNN_DK).astype(np.float32), F32)


def kernel(x_prompt, x_sample, cache_kv, cache_win, state_rnn, page_table, ln_mix, w_in, q_norm, k_norm, cmp_pe,
           cmp_w1, cmp_w2, attn_out_norm, rnn_lb_logits, rnn_out_norm, w_out, ln_mlp, w_up, w_down):
    assert w_in.shape[0] == 1, "single layer"
    b, t, dm = x_prompt.shape
    nbatch, dec_t, _ = x_sample.shape
    n_pool = cache_kv.shape[1]
    n_pages = page_table.shape[1]
    past_len = n_pages * PAGE_SIZE
    wbuf = cache_win.shape[2]
    assert t % PROJ_ROWS == 0 and t % RNN_ROWS == 0 and t % min(SEL_KEYS, t) == 0 and t >= WINDOW
    assert (b * t) % FINISH_ROWS == 0 and (nbatch * dec_t) % FINISH_ROWS == 0 and w_up.shape[2] % FF_CHUNK == 0
    assert (nbatch * dec_t) % PROJ_ROWS == 0 and nbatch == LANES and dec_t <= 8
    assert past_len % SEL_BLOCK == 0 and wbuf == WINDOW and wbuf >= LANES

    w = w_in[0]
    c_kv, c_gate, c_rq, c_rf, c_ri, c_rg = ATTN_WIDTH, ATTN_WIDTH + 6 * KV_WIDTH, 1304, 1816, 2328, 2840
    gate_cols = jnp.pad(w[:, c_gate:c_rq], ((0, 0), (0, LANES - N_ATTN_HEADS * N_BRANCH)))
    wtok = jnp.concatenate([w[:, 0:ATTN_WIDTH], w[:, c_rq:], gate_cols, w[:, c_kv:c_kv + 2 * KV_WIDTH]],
                           axis=1).astype(BF16)
    wft = w[:, c_kv:c_gate].T.astype(BF16)
    wtok_s = jnp.concatenate([w[:, 0:ATTN_WIDTH], w[:, c_rg:], gate_cols], axis=1).astype(BF16)
    wtb_s = jnp.concatenate([w[:, c_kv:c_kv + 4 * KV_WIDTH], w[:, c_rq:c_rg]], axis=1).T.astype(BF16)
    wbt_s = w[:, c_kv + 2 * KV_WIDTH:c_gate].T.astype(BF16)
    ln = ln_mix[0][None, :]
    qg = (jnp.tile(q_norm[0], N_ATTN_HEADS) * SCALE)[None, :]
    lbl = rnn_lb_logits.astype(F32)
    lblt = jnp.broadcast_to(lbl[:, :, None], lbl.shape + (PROJ_ROWS,))
    gsel = jnp.broadcast_to(k_norm[0, 1][:, None], (HEAD_DIM, PROJ_ROWS))
    gwin = jnp.broadcast_to(k_norm[0, 2][:, None], (HEAD_DIM, PROJ_ROWS))
    kg_dup = jnp.tile(k_norm[0, 0], 2)[None, :]
    pe = cmp_pe[0]
    pe_tok = jnp.tile(jnp.concatenate([jnp.tile(pe[0], (1, N_KV_HEADS)), jnp.tile(pe[1], (1, N_KV_HEADS))], axis=1),
                      (PROJ_ROWS // CMP_BLOCK, 1))
    w1 = cmp_w1[0].reshape(2, CMP_BLOCK, HEAD_DIM, CMP_HIDDEN)
    zeros = jnp.zeros_like(w1)
    w1bd = jnp.concatenate([jnp.concatenate([w1, zeros], axis=3), jnp.concatenate([zeros, w1], axis=3)],
                           axis=2).astype(BF16)
    w1r = jnp.tile(w1.transpose(0, 2, 1, 3), (1, 1, 2, 1)).astype(BF16)
    w1r = w1r.reshape(2, HEAD_DIM * LANES, CMP_HIDDEN)
    pe_t = jnp.tile(pe.transpose(0, 2, 1), (1, 1, 2))[:, :, None, :]
    w2dup = jnp.tile(cmp_w2[0], (1, 1, 2)).astype(BF16)
    ag = attn_out_norm[0][None, :]
    rg = rnn_out_norm[0][None, :]
    wout = w_out[0].astype(BF16)
    lnm = ln_mlp[0][None, :]
    wup = w_up[0].astype(BF16)
    wdn = w_down[0].astype(BF16)
    gexp = _gate_expander()

    (q_p, gates_p, rq_p, f_p, rv_p, rgs_p, kvc_p, kvt_p, wint_p, att_p) = _proj_prompt(
        x_prompt, ln, wtok, wft, qg, lbl, pe_tok, gsel, gwin)
    cmp_p = _compress_prompt(kvc_p, w1bd, w2dup, kg_dup)
    nb_p = t // CMP_BLOCK
    oa_p = _nsa_prompt(q_p, gates_p, cmp_p, att_p, _key_features(t), _query_slope_features(NSA_Q_ROWS), gexp)
    orn_p, st_p = _hgrn_prompt(rq_p, f_p, rv_p, _chunk_lower_tri(RNN_ROWS, RNN_CHUNK),
                               _head_block_diag(LANES), _head_indicator())
    y_p = _finish(x_prompt.reshape(b * t, dm), oa_p.reshape(b * t, ATTN_WIDTH), orn_p.reshape(b * t, RNN_WIDTH),
                  rgs_p.reshape(b * t, RNN_WIDTH), ag, rg, wout, lnm, wup, wdn).reshape(b, t, dm)
    kv_prompt = kvt_p.reshape(1, b, 4, N_KV_HEADS, HEAD_DIM, t).transpose(0, 1, 5, 2, 3, 4)
    wlen = min(WINDOW, t)
    win_prompt = wint_p[:, :, t - wlen:].reshape(1, b, 2, N_KV_HEADS, HEAD_DIM, wlen).transpose(0, 1, 5, 2, 3, 4)
    hh = LANES // RNN_DV
    st5 = st_p.reshape(b, RNN_WIDTH // LANES, hh, RNN_DV, hh, RNN_DK)
    rnn_prompt = jnp.stack([st5[:, :, i, :, i, :] for i in range(hh)], axis=2)
    rnn_prompt = rnn_prompt.reshape(b, N_RNN_HEADS, RNN_DV, RNN_DK).transpose(0, 1, 3, 2)[None]

    n_s = nbatch * dec_t
    xbt = x_sample.reshape(n_s, dm)
    xtb = x_sample.transpose(1, 0, 2).reshape(n_s, dm)
    q_s, rgs_s, gates_s, ztb, zbt = _proj_sample(xbt, xtb, ln, wtok_s, wtb_s, wbt_s, qg, lblt, gsel, gwin)
    kv_sample = ztb[0:4 * KV_WIDTH].reshape(4, N_KV_HEADS, HEAD_DIM, dec_t, nbatch).transpose(4, 3, 0, 1, 2)[None]
    new_rows = jnp.pad(zbt.reshape(4 * KV_WIDTH, nbatch, dec_t).transpose(1, 0, 2),
                       ((0, 0), (0, 0), (0, LANES - dec_t)))
    cache = cache_kv[0].transpose(0, 2, 3, 4, 1).reshape(n_pool, PAGE_ROWS, PAGE_SIZE)
    nblk_s = past_len // CMP_BLOCK
    rho = np.arange(nblk_s)
    e_perm = _block_expander(2 * (rho % n_pages) + rho // n_pages, past_len)
    ns_s = -(-(past_len + dec_t) // SEL_BLOCK)
    q_s3 = q_s.reshape(nbatch, dec_t, ATTN_WIDTH)
    ocs = _nsa_sample(page_table.reshape(-1), cache, q_s3, new_rows[:, 0:2 * KV_WIDTH], w1r, pe_t, w2dup, kg_dup,
                      e_perm, past_len, min(SEL_TOPK, ns_s))
    win = cache_win[0].transpose(0, 2, 3, 4, 1).reshape(nbatch, 2 * KV_WIDTH, wbuf)
    oa_s, win_new = _win_sample(win, q_s3, new_rows[:, 2 * KV_WIDTH:], ocs,
                                gates_s.reshape(nbatch, dec_t, LANES), gexp, past_len)
    win_sample = win_new.reshape(1, nbatch, 2, N_KV_HEADS, HEAD_DIM, wbuf).transpose(0, 1, 5, 2, 3, 4)
    state = state_rnn[0].transpose(1, 2, 3, 0)
    orn_t, state_new = _hgrn_sample(ztb, state, dec_t)
    rnn_sample = state_new.transpose(3, 0, 1, 2)[None]
    orn_s = orn_t.reshape(RNN_WIDTH, dec_t, nbatch).transpose(2, 1, 0).reshape(n_s, RNN_WIDTH)
    y_s = _finish(xbt, oa_s.reshape(n_s, ATTN_WIDTH), orn_s, rgs_s, ag, rg, wout, lnm, wup, wdn).reshape(nbatch, dec_t, dm)

    return (y_p, y_s, kv_prompt, kv_sample, win_prompt, win_sample, rnn_prompt, rnn_sample)
```

```python
import functools

import numpy as np
import jax
import jax.numpy as jnp
from jax import lax
from jax.experimental import pallas as pl
from jax.experimental.pallas import tpu as pltpu

F32 = jnp.float32
BF16 = jnp.bfloat16

HEAD_DIM = 64
N_ATTN_HEADS = 8
N_KV_HEADS = 2
GROUP = N_ATTN_HEADS // N_KV_HEADS
N_RNN_HEADS = 8
RNN_DK = 64
RNN_DV = 64
ATTN_WIDTH = N_ATTN_HEADS * HEAD_DIM
RNN_WIDTH = N_RNN_HEADS * RNN_DV
KV_WIDTH = N_KV_HEADS * HEAD_DIM
N_BRANCH = 3
CMP_BLOCK = 64
SEL_BLOCK = 64
SEL_TOPK = 16
WINDOW = 512
CMP_HIDDEN = 128
PAGE_SIZE = 128
SCALE = HEAD_DIM ** -0.5
EPS = 1e-6
NEG = -1e30
LOG2E = 1.4426950408889634
FORCED_SCORE = GROUP + 1.0
SLOPES = [[2.0 ** (-(g * GROUP + r + 1)) for r in range(GROUP)] for g in range(N_KV_HEADS)]

LANES = 128
VMEM_BYTES_V7X = 64 * 1024 * 1024

PROJ_ROWS = 256
NSA_Q_ROWS = 128
SEL_KEYS = 512
ROW_BLOCK = 16
RNN_ROWS = 256
RNN_CHUNK = 16
FF_CHUNK = 1024
FINISH_ROWS = 512

NT = (((1,), (1,)), ((), ()))
TN = (((0,), (0,)), ((), ()))


def _vmem_limit(nbytes):
    return int(min(VMEM_BYTES_V7X - (8 << 20), max(nbytes, 16 << 20)))


def _dot(a, b):
    return jnp.dot(a, b, preferred_element_type=F32)


def _dot_nt(a, b):
    return lax.dot_general(a, b, NT, preferred_element_type=F32)


def _dot_tn(a, b):
    return lax.dot_general(a, b, TN, preferred_element_type=F32)


def _low_half(shape):
    lane = lax.broadcasted_iota(jnp.int32, shape, len(shape) - 1)
    return (lane & HEAD_DIM) == 0


def _head_mean_sq(x):
    outs = []
    for j in range(x.shape[-1] // LANES):
        blk = x[:, j * LANES:(j + 1) * LANES]
        sq = blk * blk
        low = _low_half(blk.shape)
        s_lo = jnp.sum(jnp.where(low, sq, 0.0), axis=-1, keepdims=True)
        s_hi = jnp.sum(jnp.where(low, 0.0, sq), axis=-1, keepdims=True)
        outs.append(jnp.where(low, s_lo, s_hi))
    return jnp.concatenate(outs, axis=-1) * (1.0 / HEAD_DIM)


def _row_rms(x):
    return x * lax.rsqrt(jnp.mean(x * x, axis=-1, keepdims=True) + EPS)


def _col_head_norm(rows, gain):
    ms = jnp.mean(rows * rows, axis=0, keepdims=True)
    return rows * lax.rsqrt(ms + EPS) * gain


def _silu(z):
    return z * jax.nn.sigmoid(z)


def _lower_bound(logits, axis):
    m = jnp.max(logits, axis=axis, keepdims=True)
    e = jnp.exp(logits - m)
    lb = e / jnp.sum(e, axis=axis, keepdims=True)
    return lb[0:1] if axis == 0 else lb


def _stack_group_queries(q, g):
    rows = q.shape[0]
    low = _low_half((rows, LANES))
    zero = jnp.zeros((rows, LANES), q.dtype)
    pa = q[:, g * 2 * LANES:g * 2 * LANES + LANES]
    pb = q[:, g * 2 * LANES + LANES:(g + 1) * 2 * LANES]
    return jnp.concatenate([jnp.where(low, pa, zero), jnp.where(low, zero, pa),
                            jnp.where(low, pb, zero), jnp.where(low, zero, pb)], axis=0)


def _unstack_group(o2, rows):
    low = _low_half((rows, LANES))
    return jnp.concatenate([jnp.where(low, o2[0:rows], o2[rows:2 * rows]),
                            jnp.where(low, o2[2 * rows:3 * rows], o2[3 * rows:4 * rows])], axis=1)


def _log2(n):
    assert n > 0 and n & (n - 1) == 0, n
    return n.bit_length() - 1


def _twice(x):
    return jnp.concatenate([x, x], axis=0)


def _row_slopes(g, row_head):
    s = jnp.full(row_head.shape, SLOPES[g][GROUP - 1], F32)
    for r in range(GROUP - 2, -1, -1):
        s = jnp.where(row_head == r, SLOPES[g][r], s)
    return s


def _block_ranks(score, ids, cand):
    sub = 8
    nrow = score.shape[0]
    in_order = all(r == i for r, i in cand) and nrow % sub == 0
    rank = jnp.zeros(score.shape, F32)
    for row, id_i in cand:
        s_i = score[row:row + 1, :]
        if in_order:
            parts = []
            for v in range(nrow // sub):
                blk = score[v * sub:(v + 1) * sub, :]
                if (v + 1) * sub - 1 <= id_i:
                    parts.append((s_i > blk).astype(F32))
                elif v * sub > id_i:
                    parts.append((s_i >= blk).astype(F32))
                else:
                    parts.append(jnp.where(ids[v * sub:(v + 1) * sub, :] > id_i,
                                           (s_i >= blk).astype(F32), (s_i > blk).astype(F32)))
            rank = rank + jnp.concatenate(parts, axis=0)
        else:
            rank = rank + jnp.where(ids > id_i, (s_i >= score).astype(F32), (s_i > score).astype(F32))
    return rank


def _expand_gates(gates, gexp_ref):
    hi = gates.astype(BF16)
    lo = (gates - hi.astype(F32)).astype(BF16)
    return [_dot(hi, gexp_ref[br]) + _dot(lo, gexp_ref[br]) for br in range(N_BRANCH)]


TOK_Q, TOK_RQ, TOK_RF, TOK_RI, TOK_RG, TOK_GATE, TOK_KVC, TOK_END = 0, 512, 1024, 1536, 2048, 2560, 2688, 2944


def _proj_prompt_body(x_ref, ln_ref, wtok_ref, wft_ref, qg_ref, lbl_ref, pe_ref, gsel_ref, gwin_ref,
                      q_ref, gates_ref, rq_ref, f_ref, rv_ref, rgs_ref, kvc_ref, kvt_ref, wint_ref, att_ref):
    xb = (_row_rms(x_ref[0]) * ln_ref[...]).astype(BF16)

    def tok(lo, hi):
        return _dot(xb, wtok_ref[:, lo:hi])

    zq = tok(TOK_Q, TOK_RQ)
    q_ref[0] = (zq * lax.rsqrt(_head_mean_sq(zq) + EPS) * qg_ref[...]).astype(BF16)
    rq_ref[0] = _silu(tok(TOK_RQ, TOK_RF))
    lb = _lower_bound(lbl_ref[...], 0)
    f_ref[0] = lb + (1.0 - lb) * jax.nn.sigmoid(tok(TOK_RF, TOK_RI))
    rv_ref[0] = tok(TOK_RI, TOK_RG)
    rgs_ref[0] = _silu(tok(TOK_RG, TOK_GATE))
    gates_ref[0] = jax.nn.sigmoid(tok(TOK_GATE, TOK_KVC))
    zc = tok(TOK_KVC, TOK_END) + pe_ref[...]
    kvc_ref[0, 0] = zc[:, 0:LANES]
    kvc_ref[1, 0] = zc[:, LANES:2 * LANES]

    zf = _dot_nt(wft_ref[...], xb)
    d = HEAD_DIM
    ksel = jnp.concatenate([_col_head_norm(zf[256 + g * d:256 + (g + 1) * d], gsel_ref[...])
                            for g in range(N_KV_HEADS)], axis=0)
    kwin = jnp.concatenate([_col_head_norm(zf[512 + g * d:512 + (g + 1) * d], gwin_ref[...])
                            for g in range(N_KV_HEADS)], axis=0)
    vsel = zf[384:512]
    vwin = zf[640:768]
    kvt_ref[0, 0:256] = zf[0:256]
    kvt_ref[0, 256:384] = ksel
    kvt_ref[0, 384:512] = vsel
    wint_ref[0, 0:128] = kwin
    wint_ref[0, 128:256] = vwin
    att_ref[0, 0:128] = ksel.astype(BF16)
    att_ref[0, 128:256] = vsel.astype(BF16)
    att_ref[0, 256:384] = kwin.astype(BF16)
    att_ref[0, 384:512] = vwin.astype(BF16)


def _proj_prompt(x, ln, wtok, wft, qg, lbl, pe_tok, gsel, gwin):
    b, t, dm = x.shape
    tm = PROJ_ROWS
    grid = (b, t // tm)
    row = lambda i, j: (i, j, 0)
    col = lambda i, j: (i, 0, j)
    const2 = lambda i, j: (0, 0)
    out_shape = (
        jax.ShapeDtypeStruct((b, t, ATTN_WIDTH), BF16),
        jax.ShapeDtypeStruct((b, t, LANES), F32),
        jax.ShapeDtypeStruct((b, t, RNN_WIDTH), F32),
        jax.ShapeDtypeStruct((b, t, RNN_WIDTH), F32),
        jax.ShapeDtypeStruct((b, t, RNN_WIDTH), F32),
        jax.ShapeDtypeStruct((b, t, RNN_WIDTH), F32),
        jax.ShapeDtypeStruct((2, b, t, LANES), F32),
        jax.ShapeDtypeStruct((b, 4 * KV_WIDTH, t), F32),
        jax.ShapeDtypeStruct((b, 2 * KV_WIDTH, t), F32),
        jax.ShapeDtypeStruct((b, 4 * KV_WIDTH, t), BF16),
    )
    out_specs = (
        pl.BlockSpec((1, tm, ATTN_WIDTH), row), pl.BlockSpec((1, tm, LANES), row),
        pl.BlockSpec((1, tm, RNN_WIDTH), row), pl.BlockSpec((1, tm, RNN_WIDTH), row),
        pl.BlockSpec((1, tm, RNN_WIDTH), row), pl.BlockSpec((1, tm, RNN_WIDTH), row),
        pl.BlockSpec((2, 1, tm, LANES), lambda i, j: (0, i, j, 0)),
        pl.BlockSpec((1, 4 * KV_WIDTH, tm), col), pl.BlockSpec((1, 2 * KV_WIDTH, tm), col),
        pl.BlockSpec((1, 4 * KV_WIDTH, tm), col),
    )
    in_specs = [
        pl.BlockSpec((1, tm, dm), row), pl.BlockSpec(ln.shape, const2),
        pl.BlockSpec(wtok.shape, const2), pl.BlockSpec(wft.shape, const2),
        pl.BlockSpec(qg.shape, const2), pl.BlockSpec(lbl.shape, const2), pl.BlockSpec(pe_tok.shape, const2),
        pl.BlockSpec(gsel.shape, const2), pl.BlockSpec(gwin.shape, const2),
    ]
    return pl.pallas_call(
        _proj_prompt_body, grid=grid, in_specs=in_specs, out_specs=out_specs, out_shape=out_shape,
        compiler_params=pltpu.CompilerParams(dimension_semantics=("parallel", "parallel"),
                                             vmem_limit_bytes=_vmem_limit(48 << 20)),
        name="proj_prompt",
    )(x, ln, wtok, wft, qg, lbl, pe_tok, gsel, gwin)


def _compress_prompt_body(x_ref, w1_ref, w2_ref, kg_ref, out_ref):
    c = pl.program_id(0)
    nb = out_ref.shape[2]
    acc = jnp.zeros((nb, 2 * CMP_HIDDEN), F32)
    for pos in range(CMP_BLOCK):
        xp = x_ref[0, 0, pl.ds(pos, nb, stride=CMP_BLOCK), :]
        acc = acc + _dot(xp.astype(BF16), w1_ref[0, pos])
    hb = _silu(acc).astype(BF16)
    outs = []
    for g in range(N_KV_HEADS):
        y = _dot(hb[:, g * CMP_HIDDEN:(g + 1) * CMP_HIDDEN], w2_ref[0])
        yn = _row_rms(y) * kg_ref[...]
        outs.append(jnp.where(c == 0, yn, y))
    out_ref[0, 0] = jnp.concatenate(outs, axis=1)


def _compress_prompt(kvc, w1bd, w2dup, kg_dup):
    _, b, t, _ = kvc.shape
    nb = t // CMP_BLOCK
    return pl.pallas_call(
        _compress_prompt_body, grid=(2, b),
        in_specs=[pl.BlockSpec((1, 1, t, LANES), lambda c, i: (c, i, 0, 0)),
                  pl.BlockSpec((1,) + w1bd.shape[1:], lambda c, i: (c, 0, 0, 0)),
                  pl.BlockSpec((1,) + w2dup.shape[1:], lambda c, i: (c, 0, 0)),
                  pl.BlockSpec(kg_dup.shape, lambda c, i: (0, 0))],
        out_specs=pl.BlockSpec((1, 1, nb, 2 * LANES), lambda c, i: (c, i, 0, 0)),
        out_shape=jax.ShapeDtypeStruct((2, b, nb, 2 * LANES), F32),
        compiler_params=pltpu.CompilerParams(dimension_semantics=("arbitrary", "arbitrary"),
                                             vmem_limit_bytes=_vmem_limit(32 << 20)),
        name="compress_prompt",
    )(kvc, w1bd, w2dup, kg_dup)


def _row_max_update(sc_scr, nm_scr, m_scr, mn_scr, ncol, tq):
    rb = ROW_BLOCK
    for i in range(GROUP * tq // rb):
        rows = slice(i * rb, (i + 1) * rb)
        qrows = slice((i * rb) % tq, (i * rb) % tq + rb)
        mx = jnp.full((rb, LANES), NEG, F32)
        for j in range(ncol):
            cols = slice(j * LANES, (j + 1) * LANES)
            v = sc_scr[rows, cols]
            if nm_scr is not None:
                v = v + nm_scr[qrows, cols]
                sc_scr[rows, cols] = v
            mx = jnp.maximum(mx, v)
        mn_scr[rows, :] = jnp.maximum(m_scr[rows, :], jnp.max(mx, axis=-1, keepdims=True))


def _softmax_update(sc_scr, p_scr, m_scr, mn_scr, l_scr, a_scr, ncol, tq):
    rb = ROW_BLOCK
    for i in range(GROUP * tq // rb):
        rows = slice(i * rb, (i + 1) * rb)
        mn = mn_scr[rows, :]
        tot = jnp.zeros((rb, LANES), F32)
        for j in range(ncol):
            cols = slice(j * LANES, (j + 1) * LANES)
            p = jnp.exp(sc_scr[rows, cols] - mn)
            tot = tot + p
            p_scr[rows, cols] = p.astype(BF16)
        alpha = jnp.exp(m_scr[rows, :] - mn)
        l_scr[rows, :] = alpha * l_scr[rows, :] + jnp.sum(tot, axis=-1, keepdims=True)
        a_scr[rows, :] = alpha
        m_scr[rows, :] = mn


def _nsa_prompt_body(q_ref, gates_ref, cmp_ref, att_ref, kfeat_ref, qfeat_ref, gexp_ref, oa_ref,
                     sc_scr, sc2_scr, nm_scr, p_scr, m_scr, mn_scr, l_scr, a_scr, acc_scr, *, seq, topk):
    tq = NSA_Q_ROWS
    tk = min(SEL_KEYS, seq)
    nb = cmp_ref.shape[2]
    t0 = pl.program_id(1) * tq
    wk = min(WINDOW + tq, seq)
    ws = pl.multiple_of(jnp.clip(t0 - WINDOW, 0, seq - wk), LANES)
    q = q_ref[0]
    d = HEAD_DIM
    groups = range(N_KV_HEADS)
    def reset(stats):
        for g in groups:
            m_scr[g] = jnp.full(m_scr.shape[1:], NEG, F32)
            for ref in stats:
                ref[g] = jnp.zeros(ref.shape[1:], F32)

    sc_bufs = (sc_scr, sc2_scr)

    def logits(buf, g, qa, kt_, kf, width):
        sc_bufs[buf][g, :, 0:width] = _dot(qa, jnp.concatenate([kt_, kt_, kf], axis=0))

    def softmax_tiles(buf, ncol, masked):
        sc = sc_bufs[buf]
        for g in groups:
            _row_max_update(sc.at[g], nm_scr if masked else None, m_scr.at[g], mn_scr.at[g], ncol, tq)
        for g in groups:
            _softmax_update(sc.at[g], p_scr.at[g], m_scr.at[g], mn_scr.at[g], l_scr.at[g], a_scr.at[g], ncol, tq)

    qst = [_stack_group_queries(q, g) for g in groups]
    qwin = [jnp.concatenate([qst[g], qfeat_ref[g].astype(BF16)], axis=1) for g in groups]

    kcs = [cmp_ref[0, 0][:, g * LANES:(g + 1) * LANES].astype(BF16) for g in groups]
    vcs = [cmp_ref[1, 0][:, g * LANES:(g + 1) * LANES].astype(BF16) for g in groups]
    st = [_dot_nt(kcs[g], qst[g]) for g in groups]

    reset((l_scr,))
    kfw = kfeat_ref[:, pl.ds(ws, wk)]
    ddw = (t0 - ws) + lax.broadcasted_iota(jnp.int32, (tq, wk), 0) - lax.broadcasted_iota(jnp.int32, (tq, wk), 1)
    nm_scr[:, 0:wk] = jnp.where((ddw >= 0) & (ddw < WINDOW), 0.0, NEG)
    for g in groups:
        logits(0, g, qwin[g], att_ref[0, 2 * KV_WIDTH + g * d:2 * KV_WIDTH + (g + 1) * d, pl.ds(ws, wk)], kfw, wk)

    blk = lax.broadcasted_iota(jnp.int32, (nb, GROUP * tq), 0)
    colq = lax.broadcasted_iota(jnp.int32, (1, GROUP * tq), 1)
    dist = (t0 + (colq & (tq - 1)) - (blk * CMP_BLOCK + (CMP_BLOCK - 1))).astype(F32)
    valid = dist >= 0.0
    ps = []
    for g in groups:
        s = jnp.where(valid, st[g] - _row_slopes(g, colq >> _log2(tq)) * dist, NEG)
        e = jnp.exp(s - jnp.max(s, axis=0, keepdims=True))
        ps.append(jnp.where(valid, e / jnp.sum(e, axis=0, keepdims=True), 0.0))
    o_cmp = [_unstack_group(_dot_tn(ps[g].astype(BF16), vcs[g]), tq) for g in groups]

    softmax_tiles(0, wk // LANES, True)
    o_win = []
    for g in groups:
        vw = att_ref[0, 3 * KV_WIDTH + g * d:3 * KV_WIDTH + (g + 1) * d, pl.ds(ws, wk)]
        o_win.append(_unstack_group(_dot_nt(p_scr[g, :, 0:wk], _twice(vw)) / l_scr[g], tq))

    bj = lax.broadcasted_iota(jnp.int32, (nb, tq), 0)
    cur = (t0 + lax.broadcasted_iota(jnp.int32, (nb, tq), 1)) >> _log2(SEL_BLOCK)
    force = (bj == 0) | (bj == cur)
    qsel = []
    for g in groups:
        p = ps[g]
        imp = p[:, 0:tq] + p[:, tq:2 * tq] + p[:, 2 * tq:3 * tq] + p[:, 3 * tq:4 * tq]
        score = jnp.where(bj <= cur, jnp.where(force, FORCED_SCORE, imp), -1.0)
        rank = _block_ranks(score, bj, [(i, i) for i in range(nb)])
        mneg = jnp.where((rank < topk) & (score >= 0.0), 0.0, NEG)
        mtok = jnp.concatenate([mneg, jnp.zeros((LANES - nb, tq), F32)], axis=0).T
        qsel.append(jnp.concatenate(
            [qst[g], (qfeat_ref[g] + jnp.concatenate([mtok] * GROUP, axis=0)).astype(BF16)], axis=1))

    reset((l_scr, acc_scr))
    n_kt = (t0 + tq - 1) // tk + 1

    def sel_logits(kt, buf):
        s0 = pl.multiple_of(kt * tk, tk)
        kf = kfeat_ref[:, pl.ds(s0, tk)]
        for g in groups:
            logits(buf, g, qsel[g], att_ref[0, g * d:(g + 1) * d, pl.ds(s0, tk)], kf, tk)

    def sel_update(kt, buf, causal):
        s0 = pl.multiple_of(kt * tk, tk)
        if causal:
            dd = (t0 - s0) + lax.broadcasted_iota(jnp.int32, (tq, tk), 0) - lax.broadcasted_iota(jnp.int32, (tq, tk), 1)
            nm_scr[:, 0:tk] = jnp.where(dd >= 0, 0.0, NEG)
        softmax_tiles(buf, tk // LANES, causal)
        for g in groups:
            vt_ = att_ref[0, KV_WIDTH + g * d:KV_WIDTH + (g + 1) * d, pl.ds(s0, tk)]
            acc_scr[g] = a_scr[g] * acc_scr[g] + _dot_nt(p_scr[g, :, 0:tk], _twice(vt_))

    sel_logits(0, 0)

    def full_tile(kt, carry):
        for par in (0, 1):
            @pl.when((kt & 1) == par)
            def _(par=par):
                sel_logits(kt + 1, 1 - par)
                sel_update(kt, par, False)
        return carry

    lax.fori_loop(0, n_kt - 1, full_tile, 0)
    for par in (0, 1):
        @pl.when(((n_kt - 1) & 1) == par)
        def _(par=par):
            sel_update(n_kt - 1, par, True)
    o_sel = [_unstack_group(acc_scr[g] / l_scr[g], tq) for g in groups]

    ge = _expand_gates(gates_ref[0], gexp_ref)
    oa_ref[0] = (ge[0] * jnp.concatenate(o_cmp, axis=1) + ge[1] * jnp.concatenate(o_sel, axis=1)
                 + ge[2] * jnp.concatenate(o_win, axis=1))


def _nsa_prompt(q, gates, cmpkv, att, k_feat, q_feat, gexp):
    b, t, _ = q.shape
    tq = NSA_Q_ROWS
    nb = t // CMP_BLOCK
    assert nb <= HEAD_DIM, "the block mask uses 64 feature lanes"
    cw = max(min(SEL_KEYS, t), min(WINDOW + tq, t))
    body = functools.partial(_nsa_prompt_body, seq=t, topk=min(SEL_TOPK, nb))
    ng = N_KV_HEADS
    stat = pltpu.VMEM((ng, GROUP * tq, LANES), F32)
    return pl.pallas_call(
        body, grid=(b, t // tq),
        in_specs=[pl.BlockSpec((1, tq, ATTN_WIDTH), lambda i, j: (i, j, 0)),
                  pl.BlockSpec((1, tq, LANES), lambda i, j: (i, j, 0)),
                  pl.BlockSpec((2, 1, nb, 2 * LANES), lambda i, j: (0, i, 0, 0)),
                  pl.BlockSpec((1, 4 * KV_WIDTH, t), lambda i, j: (i, 0, 0)),
                  pl.BlockSpec(k_feat.shape, lambda i, j: (0, 0)),
                  pl.BlockSpec(q_feat.shape, lambda i, j: (0, 0, 0)),
                  pl.BlockSpec(gexp.shape, lambda i, j: (0, 0, 0))],
        out_specs=pl.BlockSpec((1, tq, ATTN_WIDTH), lambda i, j: (i, j, 0)),
        out_shape=jax.ShapeDtypeStruct((b, t, ATTN_WIDTH), F32),
        scratch_shapes=[pltpu.VMEM((ng, GROUP * tq, cw), F32), pltpu.VMEM((ng, GROUP * tq, cw), F32),
                        pltpu.VMEM((tq, cw), F32),
                        pltpu.VMEM((ng, GROUP * tq, cw), BF16), stat, stat, stat, stat, stat],
        compiler_params=pltpu.CompilerParams(dimension_semantics=("parallel", "parallel"),
                                             vmem_limit_bytes=_vmem_limit(48 << 20)),
        name="nsa_prompt",
    )(q, gates, cmpkv, att, k_feat, q_feat, gexp)


def _split3(x):
    hi = x.astype(BF16)
    r1 = x - hi.astype(F32)
    mid = r1.astype(BF16)
    lo = (r1 - mid.astype(F32)).astype(BF16)
    return hi, mid, lo


def _hgrn_prompt_body(rq_ref, f_ref, rv_ref, ltri_ref, bd_ref, ind_ref, indt_ref, o_ref, st_ref,
                      s_scr, cum_scr, k_scr, prod_scr):
    c16 = RNN_CHUNK
    nbat = rq_ref.shape[0]

    @pl.when(pl.program_id(0) == 0)
    def _():
        s_scr[...] = jnp.zeros(s_scr.shape, F32)

    ltri = ltri_ref[...]
    for bi in range(nbat):
        f = f_ref[bi]
        hi, mid, lo = _split3(jnp.log(f))
        cum = (_dot(ltri, hi) + _dot(ltri, mid) + _dot(ltri, lo)) * LOG2E
        cum_scr[bi] = cum
        k_scr[bi] = cum - jnp.log2(1.0 - f)

    si = lax.broadcasted_iota(jnp.int32, (c16, c16, RNN_WIDTH), 0)
    ti = lax.broadcasted_iota(jnp.int32, (c16, c16, RNN_WIDTH), 1)
    causal = si <= ti
    npair = RNN_WIDTH // LANES

    def pair_products(c, buf, bi):
        r0 = pl.multiple_of(c * c16, c16)
        cc = cum_scr[bi, pl.ds(r0, c16), :]
        qc = rq_ref[bi, pl.ds(r0, c16), :]
        lk = k_scr[bi, pl.ds(r0, c16), :]
        dec = jnp.exp2(jnp.where(causal, cc[None, :, :] - lk[:, None, :], NEG))
        prod_scr[buf, bi] = (qc[None, :, :] * dec).reshape(c16 * c16, RNN_WIDTH).astype(BF16)

    def recurrence(c, buf, nxt):
        r0 = pl.multiple_of(c * c16, c16)
        rows = range(nbat)
        a1 = [_dot(prod_scr[buf, bi], ind_ref[...]) for bi in rows]
        if nxt is not None:
            for bi in rows:
                pair_products(nxt, 1 - buf, bi)
        cc = [cum_scr[bi, pl.ds(r0, c16), :] for bi in rows]
        last = [x[c16 - 1:c16, :] for x in cc]
        o_int = []
        for bi in rows:
            qd = (rq_ref[bi, pl.ds(r0, c16), :] * jnp.exp2(cc[bi])).astype(BF16)
            o_int.append(jnp.concatenate([_dot_nt(qd[:, p * LANES:(p + 1) * LANES], s_scr[bi, p].astype(BF16))
                                          for p in range(npair)], axis=1))
        a2 = [_dot(a1[bi].astype(BF16), indt_ref[...]).reshape(c16, c16, RNN_WIDTH) for bi in rows]
        for bi in rows:
            kd = jnp.exp2(last[bi] - k_scr[bi, pl.ds(r0, c16), :]).astype(BF16)
            vb = rv_ref[bi, pl.ds(r0, c16), :].astype(BF16)
            dl = jnp.exp2(last[bi])
            for p in range(npair):
                u = _dot_tn(vb[:, p * LANES:(p + 1) * LANES], kd[:, p * LANES:(p + 1) * LANES])
                s_scr[bi, p] = s_scr[bi, p] * dl[:, p * LANES:(p + 1) * LANES] + u * bd_ref[...]
        for bi in rows:
            vc = rv_ref[bi, pl.ds(r0, c16), :]
            o_ref[bi, pl.ds(r0, c16), :] = o_int[bi] + jnp.sum(a2[bi] * vc[:, None, :], axis=0)

    nchunk = rq_ref.shape[1] // c16
    assert nchunk % 2 == 0 and nchunk >= 4
    for bi in range(nbat):
        pair_products(0, 0, bi)

    def two_chunks(j, carry):
        for sub in (0, 1):
            recurrence(2 * j + sub, sub, 2 * j + sub + 1)
        return carry

    lax.fori_loop(0, nchunk // 2 - 1, two_chunks, 0)
    recurrence(nchunk - 2, 0, nchunk - 1)
    recurrence(nchunk - 1, 1, None)
    st_ref[...] = s_scr[...]


def _hgrn_prompt(rq, f, rv, ltri, bd, ind):
    b, t, w = rq.shape
    tc = RNN_ROWS
    npair = w // LANES
    row = lambda j: (0, j, 0)
    const = lambda j: (0, 0)
    indt = ind.T
    return pl.pallas_call(
        _hgrn_prompt_body, grid=(t // tc,),
        in_specs=[pl.BlockSpec((b, tc, w), row), pl.BlockSpec((b, tc, w), row), pl.BlockSpec((b, tc, w), row),
                  pl.BlockSpec(ltri.shape, const), pl.BlockSpec(bd.shape, const),
                  pl.BlockSpec(ind.shape, const), pl.BlockSpec(indt.shape, const)],
        out_specs=(pl.BlockSpec((b, tc, w), row), pl.BlockSpec((b, npair, LANES, LANES), lambda j: (0, 0, 0, 0))),
        out_shape=(jax.ShapeDtypeStruct((b, t, w), F32), jax.ShapeDtypeStruct((b, npair, LANES, LANES), F32)),
        scratch_shapes=[pltpu.VMEM((b, npair, LANES, LANES), F32), pltpu.VMEM((b, tc, w), F32),
                        pltpu.VMEM((b, tc, w), F32), pltpu.VMEM((2, b, RNN_CHUNK * RNN_CHUNK, w), BF16)],
        compiler_params=pltpu.CompilerParams(dimension_semantics=("arbitrary",),
                                             vmem_limit_bytes=_vmem_limit(40 << 20)),
        name="hgrn_prompt",
    )(rq, f, rv, ltri, bd, ind, indt)


def _finish_body(x_ref, oa_ref, orn_ref, rgs_ref, ag_ref, rg_ref, wout_ref, lnm_ref, wup_ref, wdn_ref, y_ref, hn_scr):
    @pl.when(pl.program_id(1) == 0)
    def _():
        oa = oa_ref[...]
        orn = orn_ref[...]
        a_n = oa * lax.rsqrt(_head_mean_sq(oa) + EPS) * ag_ref[...]
        r_n = orn * lax.rsqrt(_head_mean_sq(orn) + EPS) * rg_ref[...] * rgs_ref[...]
        h = (x_ref[...] + _dot(a_n.astype(BF16), wout_ref[0:ATTN_WIDTH, :])
             + _dot(r_n.astype(BF16), wout_ref[ATTN_WIDTH:ATTN_WIDTH + RNN_WIDTH, :]))
        y_ref[...] = h
        hn_scr[...] = (_row_rms(h) * lnm_ref[...]).astype(BF16)

    u = jnp.maximum(_dot(hn_scr[...], wup_ref[...]), 0.0)
    y_ref[...] += _dot((u * u).astype(BF16), wdn_ref[...])


def _finish(x, oa, orn, rgs, ag, rg, wout, lnm, wup, wdn):
    n, dm = x.shape
    tm = FINISH_ROWS
    dff = wup.shape[1]
    row = lambda i, j: (i, 0)
    const = lambda i, j: (0, 0)
    return pl.pallas_call(
        _finish_body, grid=(n // tm, dff // FF_CHUNK),
        in_specs=[pl.BlockSpec((tm, dm), row), pl.BlockSpec((tm, ATTN_WIDTH), row),
                  pl.BlockSpec((tm, RNN_WIDTH), row), pl.BlockSpec((tm, RNN_WIDTH), row),
                  pl.BlockSpec(ag.shape, const), pl.BlockSpec(rg.shape, const),
                  pl.BlockSpec(wout.shape, const), pl.BlockSpec(lnm.shape, const),
                  pl.BlockSpec((dm, FF_CHUNK), lambda i, j: (0, j)),
                  pl.BlockSpec((FF_CHUNK, dm), lambda i, j: (j, 0))],
        out_specs=pl.BlockSpec((tm, dm), row),
        out_shape=jax.ShapeDtypeStruct((n, dm), F32),
        scratch_shapes=[pltpu.VMEM((tm, dm), BF16)],
        compiler_params=pltpu.CompilerParams(dimension_semantics=("parallel", "arbitrary"),
                                             vmem_limit_bytes=_vmem_limit(48 << 20)),
        name="finish",
    )(x, oa, orn, rgs, ag, rg, wout, lnm, wup, wdn)


def _proj_sample_body(xbt_ref, xtb_ref, ln_ref, wtok_ref, wtb_ref, wbt_ref, qg_ref, lbl_ref, gsel_ref, gwin_ref,
                      q_ref, rgs_ref, gates_ref, ztb_ref, zbt_ref):
    xb = (_row_rms(xbt_ref[...]) * ln_ref[...]).astype(BF16)
    xt = (_row_rms(xtb_ref[...]) * ln_ref[...]).astype(BF16)
    zq = _dot(xb, wtok_ref[:, 0:ATTN_WIDTH])
    q_ref[...] = (zq * lax.rsqrt(_head_mean_sq(zq) + EPS) * qg_ref[...]).astype(BF16)
    rgs_ref[...] = _silu(_dot(xb, wtok_ref[:, ATTN_WIDTH:ATTN_WIDTH + RNN_WIDTH]))
    gates_ref[...] = jax.nn.sigmoid(_dot(xb, wtok_ref[:, ATTN_WIDTH + RNN_WIDTH:ATTN_WIDTH + RNN_WIDTH + LANES]))

    d = HEAD_DIM
    zt = _dot_nt(wtb_ref[...], xt)
    ztb_ref[0:256] = zt[0:256]
    for g in range(N_KV_HEADS):
        ztb_ref[256 + g * d:256 + (g + 1) * d] = _col_head_norm(zt[256 + g * d:256 + (g + 1) * d], gsel_ref[...])
    ztb_ref[384:512] = zt[384:512]
    ztb_ref[512:1024] = _silu(zt[512:1024])
    lb = _lower_bound(lbl_ref[...], 0)[0]
    ztb_ref[1024:1536] = lb + (1.0 - lb) * jax.nn.sigmoid(zt[1024:1536])
    ztb_ref[1536:2048] = zt[1536:2048]

    zb = _dot_nt(wbt_ref[...], xb)
    for g in range(N_KV_HEADS):
        zbt_ref[g * d:(g + 1) * d] = _col_head_norm(zb[g * d:(g + 1) * d], gsel_ref[...])
        zbt_ref[256 + g * d:256 + (g + 1) * d] = _col_head_norm(zb[256 + g * d:256 + (g + 1) * d], gwin_ref[...])
    zbt_ref[128:256] = zb[128:256]
    zbt_ref[384:512] = zb[384:512]


def _proj_sample(xbt, xtb, ln, wtok, wtb, wbt, qg, lblt, gsel, gwin):
    n, dm = xbt.shape
    tm = PROJ_ROWS
    row = lambda i: (i, 0)
    col = lambda i: (0, i)
    const = lambda i: (0, 0)
    return pl.pallas_call(
        _proj_sample_body, grid=(n // tm,),
        in_specs=[pl.BlockSpec((tm, dm), row), pl.BlockSpec((tm, dm), row), pl.BlockSpec(ln.shape, const),
                  pl.BlockSpec(wtok.shape, const), pl.BlockSpec(wtb.shape, const), pl.BlockSpec(wbt.shape, const),
                  pl.BlockSpec(qg.shape, const), pl.BlockSpec(lblt.shape, lambda i: (0, 0, 0)),
                  pl.BlockSpec(gsel.shape, const), pl.BlockSpec(gwin.shape, const)],
        out_specs=(pl.BlockSpec((tm, ATTN_WIDTH), row), pl.BlockSpec((tm, RNN_WIDTH), row),
                   pl.BlockSpec((tm, LANES), row), pl.BlockSpec((wtb.shape[0], tm), col),
                   pl.BlockSpec((wbt.shape[0], tm), col)),
        out_shape=(jax.ShapeDtypeStruct((n, ATTN_WIDTH), BF16), jax.ShapeDtypeStruct((n, RNN_WIDTH), F32),
                   jax.ShapeDtypeStruct((n, LANES), F32), jax.ShapeDtypeStruct((wtb.shape[0], n), F32),
                   jax.ShapeDtypeStruct((wbt.shape[0], n), F32)),
        compiler_params=pltpu.CompilerParams(dimension_semantics=("parallel",),
                                             vmem_limit_bytes=_vmem_limit(40 << 20)),
        name="proj_sample",
    )(xbt, xtb, ln, wtok, wtb, wbt, qg, lblt, gsel, gwin)


PAGE_ROWS = 4 * KV_WIDTH
SAMPLE_BATCH_PER_STEP = 2
WINDOW_BATCH_PER_STEP = 4


def _nsa_sample_body(pt_ref, cache_ref, q_ref, nkv_ref, w1_ref, pe_ref, w2_ref, kg_ref, e_ref,
                     ocs_ref, cbuf, sbuf, lhs_scr, kk_scr, vv_scr, sem, *, n_pages, past_len, dec_t, topk):
    step_id = pl.program_id(0)
    nsteps = pl.num_programs(0)
    slot = step_id % 2
    d = HEAD_DIM
    nblk = 2 * n_pages
    half_rows = PAGE_ROWS // 2
    bps = q_ref.shape[0]

    def page_copies(st, sl, bi, j):
        pg = pt_ref[(st * bps + bi) * n_pages + j]
        return (pltpu.make_async_copy(cache_ref.at[pg, pl.ds(0, half_rows)], cbuf.at[sl, bi, :, j, :], sem.at[sl]),
                pltpu.make_async_copy(cache_ref.at[pg, pl.ds(half_rows, half_rows)],
                                      sbuf.at[sl, bi, pl.ds(j * half_rows, half_rows)], sem.at[sl]))

    def all_copies(st, sl):
        return [cp for bi in range(bps) for j in range(n_pages) for cp in page_copies(st, sl, bi, j)]

    @pl.when(step_id == 0)
    def _():
        for cp in all_copies(0, 0):
            cp.start()

    @pl.when(step_id + 1 < nsteps)
    def _():
        for cp in all_copies(step_id + 1, 1 - slot):
            cp.start()

    for cp in all_copies(step_id, slot):
        cp.wait()

    low = _low_half((n_pages, LANES))
    rows_b = 4 * n_pages

    def compress(c):
        for dd in range(d):
            rows = []
            for bi in range(bps):
                for g in range(N_KV_HEADS):
                    xg = cbuf[slot, bi, (c * N_KV_HEADS + g) * d + dd]
                    xg = xg + pe_ref[c, dd]
                    rows += [jnp.where(low, xg, 0.0), jnp.where(low, 0.0, xg)]
            lhs_scr[:, dd * LANES:(dd + 1) * LANES] = jnp.concatenate(rows, axis=0).astype(BF16)
        acc = _dot(lhs_scr[...], w1_ref[c])
        return _dot(_silu(acc).astype(BF16), w2_ref[c])

    kc_all = _row_rms(compress(0)) * kg_ref[...]
    vc_all = compress(1)

    nq = GROUP * dec_t
    rho = lax.broadcasted_iota(jnp.int32, (nblk, dec_t), 0)
    bid = 2 * (rho & (n_pages - 1)) + (rho >> _log2(n_pages))
    colq = lax.broadcasted_iota(jnp.int32, (1, nq), 1)
    rowq = lax.broadcasted_iota(jnp.int32, (nq, 1), 0)
    cur_blk = past_len // SEL_BLOCK
    chains = [(bi, g) for bi in range(bps) for g in range(N_KV_HEADS)]
    nch = len(chains)
    step = rowq & (dec_t - 1)
    qpos = past_len + (colq & (dec_t - 1))
    rho_q = lax.broadcasted_iota(jnp.int32, (nblk, 1), 0)
    end = (2 * (rho_q & (n_pages - 1)) + (rho_q >> _log2(n_pages))) * CMP_BLOCK + (CMP_BLOCK - 1)
    dist = (qpos - end).astype(F32)
    valid = dist >= 0.0
    kpos = lax.broadcasted_iota(jnp.int32, (1, past_len), 1)
    dpast = (past_len + step - kpos).astype(F32)
    lane = lax.broadcasted_iota(jnp.int32, (1, LANES), 1)
    dnew = (step - lane).astype(F32)

    qst = [_stack_group_queries(q_ref[bi], g) for bi, g in chains]
    kcs = [kc_all[bi * rows_b + g * nblk:bi * rows_b + (g + 1) * nblk].astype(BF16) for bi, g in chains]
    vcs = [vc_all[bi * rows_b + g * nblk:bi * rows_b + (g + 1) * nblk].astype(BF16) for bi, g in chains]
    st = [_dot_nt(kcs[ch], qst[ch]) for ch in range(nch)]
    for ch, (bi, g) in enumerate(chains):
        for pg in range(n_pages):
            kt_ = sbuf[slot, bi, pl.ds(pg * half_rows + g * d, d), :].astype(BF16)
            vt_ = sbuf[slot, bi, pl.ds(pg * half_rows + (N_KV_HEADS + g) * d, d), :].astype(BF16)
            kk_scr[ch, 0:d, pg * PAGE_SIZE:(pg + 1) * PAGE_SIZE] = kt_
            kk_scr[ch, d:2 * d, pg * PAGE_SIZE:(pg + 1) * PAGE_SIZE] = kt_
            vv_scr[ch, 0:d, pg * PAGE_SIZE:(pg + 1) * PAGE_SIZE] = vt_
            vv_scr[ch, d:2 * d, pg * PAGE_SIZE:(pg + 1) * PAGE_SIZE] = vt_
    sp = [_dot(qst[ch], kk_scr[ch]) for ch in range(nch)]
    nk = [_twice(nkv_ref[bi, g * d:(g + 1) * d, :].astype(BF16)) for bi, g in chains]
    nv = [_twice(nkv_ref[bi, KV_WIDTH + g * d:KV_WIDTH + (g + 1) * d, :].astype(BF16)) for bi, g in chains]
    sn = [_dot(qst[ch], nk[ch]) for ch in range(nch)]
    ps = []
    for ch, (bi, g) in enumerate(chains):
        s = jnp.where(valid, st[ch] - _row_slopes(g, colq >> _log2(dec_t)) * dist, NEG)
        e = jnp.exp(s - jnp.max(s, axis=0, keepdims=True))
        ps.append(jnp.where(valid, e / jnp.sum(e, axis=0, keepdims=True), 0.0))
    o_cmp = [_unstack_group(_dot_tn(ps[ch].astype(BF16), vcs[ch]), dec_t) for ch in range(nch)]
    msel = []
    for ch in range(nch):
        p = ps[ch]
        imp = p[:, 0:dec_t]
        for r in range(1, GROUP):
            imp = imp + p[:, r * dec_t:(r + 1) * dec_t]
        force = (bid == 0) | (bid == cur_blk)
        score = jnp.where(force, FORCED_SCORE, imp)
        rank = _block_ranks(score, bid, [(i, 2 * (i % n_pages) + i // n_pages) for i in range(nblk)])
        rank = rank + jnp.where(bid > cur_blk, (FORCED_SCORE >= score).astype(F32),
                                (FORCED_SCORE > score).astype(F32))
        msel.append(jnp.where((rank < topk) & (score >= 0.0), 1.0, 0.0).astype(BF16))
    mk = [_dot_tn(msel[ch], e_ref[...]) for ch in range(nch)]
    pp, pn, den = [], [], []
    for ch, (bi, g) in enumerate(chains):
        okp = jnp.concatenate([mk[ch]] * GROUP, axis=0) > 0.5
        slope = _row_slopes(g, rowq >> _log2(dec_t))
        lgp = jnp.where(okp & (dpast >= 0.0), sp[ch] - slope * dpast, NEG)
        lgn = jnp.where((lane < dec_t) & (dnew >= 0.0), sn[ch] - slope * dnew, NEG)
        m = jnp.maximum(jnp.max(lgp, axis=-1, keepdims=True), jnp.max(lgn, axis=-1, keepdims=True))
        pp.append(jnp.exp(lgp - m))
        pn.append(jnp.exp(lgn - m))
        den.append(jnp.sum(pp[ch], axis=-1, keepdims=True) + jnp.sum(pn[ch], axis=-1, keepdims=True))
    o_sel = [_unstack_group((_dot_nt(pp[ch].astype(BF16), vv_scr[ch]) + _dot_nt(pn[ch].astype(BF16), nv[ch]))
                            / den[ch], dec_t) for ch in range(nch)]

    for bi in range(bps):
        ocs_ref[bi, 0] = jnp.concatenate(o_cmp[bi * N_KV_HEADS:(bi + 1) * N_KV_HEADS], axis=1)
        ocs_ref[bi, 1] = jnp.concatenate(o_sel[bi * N_KV_HEADS:(bi + 1) * N_KV_HEADS], axis=1)


def _nsa_sample(page_flat, cache, q, nkv, w1r, pe_t, w2dup, kg_dup, e_perm, past_len, topk):
    nbatch, dec_t, _ = q.shape
    n_pages = past_len // PAGE_SIZE
    bps = SAMPLE_BATCH_PER_STEP
    nch = bps * N_KV_HEADS
    body = functools.partial(_nsa_sample_body, n_pages=n_pages, past_len=past_len, dec_t=dec_t, topk=topk)
    grid_spec = pltpu.PrefetchScalarGridSpec(
        num_scalar_prefetch=1, grid=(nbatch // bps,),
        in_specs=[pl.BlockSpec(memory_space=pl.ANY),
                  pl.BlockSpec((bps, dec_t, ATTN_WIDTH), lambda i, pt: (i, 0, 0)),
                  pl.BlockSpec((bps,) + nkv.shape[1:], lambda i, pt: (i, 0, 0)),
                  pl.BlockSpec(w1r.shape, lambda i, pt: (0, 0, 0)),
                  pl.BlockSpec(pe_t.shape, lambda i, pt: (0, 0, 0, 0)),
                  pl.BlockSpec(w2dup.shape, lambda i, pt: (0, 0, 0)),
                  pl.BlockSpec(kg_dup.shape, lambda i, pt: (0, 0)),
                  pl.BlockSpec(e_perm.shape, lambda i, pt: (0, 0))],
        out_specs=pl.BlockSpec((bps, 2, dec_t, ATTN_WIDTH), lambda i, pt: (i, 0, 0, 0)),
        scratch_shapes=[pltpu.VMEM((2, bps, PAGE_ROWS // 2, n_pages, PAGE_SIZE), F32),
                        pltpu.VMEM((2, bps, n_pages * PAGE_ROWS // 2, PAGE_SIZE), F32),
                        pltpu.VMEM((bps * 4 * n_pages, HEAD_DIM * LANES), BF16),
                        pltpu.VMEM((nch, 2 * HEAD_DIM, past_len), BF16),
                        pltpu.VMEM((nch, 2 * HEAD_DIM, past_len), BF16),
                        pltpu.SemaphoreType.DMA((2,))])
    return pl.pallas_call(
        body, grid_spec=grid_spec,
        out_shape=jax.ShapeDtypeStruct((nbatch, 2, dec_t, ATTN_WIDTH), F32),
        compiler_params=pltpu.CompilerParams(dimension_semantics=("arbitrary",),
                                             vmem_limit_bytes=_vmem_limit(40 << 20)),
        name="nsa_sample",
    )(page_flat, cache, q, nkv, w1r, pe_t, w2dup, kg_dup, e_perm)


def _win_sample_body(win_ref, q_ref, nw_ref, ocs_ref, gates_ref, gexp_ref, oa_ref, wout_ref, *, past_len, dec_t):
    d = HEAD_DIM
    wbuf = win_ref.shape[2]
    nq = GROUP * dec_t
    rowq = lax.broadcasted_iota(jnp.int32, (nq, 1), 0)
    step = rowq & (dec_t - 1)
    kpos = past_len - wbuf + lax.broadcasted_iota(jnp.int32, (1, wbuf), 1)
    dpast = (past_len + step - kpos).astype(F32)
    okp = (dpast >= 0.0) & (dpast < WINDOW)
    lane = lax.broadcasted_iota(jnp.int32, (1, LANES), 1)
    dnew = (step - lane).astype(F32)
    okn = (lane < dec_t) & (dnew >= 0.0) & (dnew < WINDOW)
    nbat = win_ref.shape[0]
    chains = [(bi, g) for bi in range(nbat) for g in range(N_KV_HEADS)]
    qst = [_stack_group_queries(q_ref[bi], g) for bi, g in chains]
    sp = [_dot(qst[ch], _twice(win_ref[bi, g * d:(g + 1) * d, :].astype(BF16))) for ch, (bi, g) in enumerate(chains)]
    sn = [_dot(qst[ch], _twice(nw_ref[bi, g * d:(g + 1) * d, :].astype(BF16))) for ch, (bi, g) in enumerate(chains)]
    pp, pn, den = [], [], []
    for ch, (bi, g) in enumerate(chains):
        slope = _row_slopes(g, rowq >> _log2(dec_t))
        lgp = jnp.where(okp, sp[ch] - slope * dpast, NEG)
        lgn = jnp.where(okn, sn[ch] - slope * dnew, NEG)
        m = jnp.maximum(jnp.max(lgp, axis=-1, keepdims=True), jnp.max(lgn, axis=-1, keepdims=True))
        pp.append(jnp.exp(lgp - m))
        pn.append(jnp.exp(lgn - m))
        den.append(jnp.sum(pp[ch], axis=-1, keepdims=True) + jnp.sum(pn[ch], axis=-1, keepdims=True))
    o_win = []
    for ch, (bi, g) in enumerate(chains):
        vt_ = win_ref[bi, KV_WIDTH + g * d:KV_WIDTH + (g + 1) * d, :].astype(BF16)
        nv = _twice(nw_ref[bi, KV_WIDTH + g * d:KV_WIDTH + (g + 1) * d, :].astype(BF16))
        o2 = _dot_nt(pp[ch].astype(BF16), _twice(vt_)) + _dot_nt(pn[ch].astype(BF16), nv)
        o_win.append(_unstack_group(o2 / den[ch], dec_t))
    for bi in range(nbat):
        ge = _expand_gates(gates_ref[bi], gexp_ref)
        oa_ref[bi] = (ge[0] * ocs_ref[bi, 0] + ge[1] * ocs_ref[bi, 1]
                      + ge[2] * jnp.concatenate(o_win[bi * N_KV_HEADS:(bi + 1) * N_KV_HEADS], axis=1))

        rolled = pltpu.roll(win_ref[bi], wbuf - dec_t, 1)
        newr = pltpu.roll(nw_ref[bi], LANES - dec_t, 1)
        wout_ref[bi, :, 0:wbuf - LANES] = rolled[:, 0:wbuf - LANES]
        wout_ref[bi, :, wbuf - LANES:wbuf] = jnp.where(lane >= LANES - dec_t, newr, rolled[:, wbuf - LANES:wbuf])


def _win_sample(win, q, nw, ocs, gates, gexp, past_len):
    nbatch, feat, wbuf = win.shape
    dec_t = q.shape[1]
    body = functools.partial(_win_sample_body, past_len=past_len, dec_t=dec_t)
    b3 = lambda i: (i, 0, 0)
    bps = WINDOW_BATCH_PER_STEP
    return pl.pallas_call(
        body, grid=(nbatch // bps,),
        in_specs=[pl.BlockSpec((bps, feat, wbuf), b3), pl.BlockSpec((bps, dec_t, ATTN_WIDTH), b3),
                  pl.BlockSpec((bps, feat, LANES), b3),
                  pl.BlockSpec((bps, 2, dec_t, ATTN_WIDTH), lambda i: (i, 0, 0, 0)),
                  pl.BlockSpec((bps, dec_t, LANES), b3), pl.BlockSpec(gexp.shape, lambda i: (0, 0, 0))],
        out_specs=(pl.BlockSpec((bps, dec_t, ATTN_WIDTH), b3), pl.BlockSpec((bps, feat, wbuf), b3)),
        out_shape=(jax.ShapeDtypeStruct((nbatch, dec_t, ATTN_WIDTH), F32),
                   jax.ShapeDtypeStruct((nbatch, feat, wbuf), F32)),
        compiler_params=pltpu.CompilerParams(dimension_semantics=("parallel",),
                                             vmem_limit_bytes=_vmem_limit(24 << 20)),
        name="win_sample",
    )(win, q, nw, ocs, gates, gexp)


def _hgrn_sample_body(q_ref, f_ref, v_ref, s_ref, o_ref, so_ref, *, dec_t):
    nb = s_ref.shape[3]
    o_ref[...] = jnp.zeros(o_ref.shape, F32)

    sub = 8

    def per_tile(i, carry):
        r0 = pl.multiple_of(i * sub, sub)
        f_t = [f_ref[pl.ds(r0, sub), pl.ds(t * nb, nb)] for t in range(dec_t)]
        q_t = [q_ref[pl.ds(r0, sub), pl.ds(t * nb, nb)] for t in range(dec_t)]
        for j in range(sub):
            s = s_ref[0, r0 + j]
            for t in range(dec_t):
                cols = pl.ds(t * nb, nb)
                fr = f_t[t][j:j + 1, :]
                s = fr * s + (1.0 - fr) * v_ref[:, cols]
                o_ref[:, cols] = o_ref[:, cols] + s * q_t[t][j:j + 1, :]
            so_ref[0, r0 + j] = s
        return carry

    lax.fori_loop(0, s_ref.shape[1] // sub, per_tile, 0)


def _hgrn_sample(ztb, state, dec_t):
    nh, dk, dv, nb = state.shape
    n = ztb.shape[1]
    body = functools.partial(_hgrn_sample_body, dec_t=dec_t)
    q0, f0, v0 = 512 // dk, 1024 // dk, 1536 // dk
    return pl.pallas_call(
        body, grid=(nh,),
        in_specs=[pl.BlockSpec((dk, n), lambda h: (q0 + h, 0)), pl.BlockSpec((dk, n), lambda h: (f0 + h, 0)),
                  pl.BlockSpec((dv, n), lambda h: (v0 + h, 0)),
                  pl.BlockSpec((1, dk, dv, nb), lambda h: (h, 0, 0, 0))],
        out_specs=(pl.BlockSpec((dv, n), lambda h: (h, 0)), pl.BlockSpec((1, dk, dv, nb), lambda h: (h, 0, 0, 0))),
        out_shape=(jax.ShapeDtypeStruct((nh * dv, n), F32), jax.ShapeDtypeStruct(state.shape, F32)),
        compiler_params=pltpu.CompilerParams(dimension_semantics=("parallel",),
                                             vmem_limit_bytes=_vmem_limit(24 << 20)),
        name="hgrn_sample",
    )(ztb, ztb, ztb, state)


def _gate_expander():
    m = np.zeros((N_BRANCH, LANES, ATTN_WIDTH), np.float32)
    for br in range(N_BRANCH):
        for h in range(N_ATTN_HEADS):
            m[br, h * N_BRANCH + br, h * HEAD_DIM:(h + 1) * HEAD_DIM] = 1.0
    return jnp.asarray(m, BF16)


def _block_expander(block_ids, n_keys):
    key_blk = np.arange(n_keys) // SEL_BLOCK
    return jnp.asarray((np.asarray(block_ids)[:, None] == key_blk[None, :]).astype(np.float32), BF16)


def _key_features(n_keys):
    s = np.arange(n_keys)
    m = np.zeros((LANES, n_keys), np.float32)
    m[0:HEAD_DIM] = (np.arange(HEAD_DIM)[:, None] == (s // SEL_BLOCK)[None, :])
    m[HEAD_DIM] = s // SEL_BLOCK
    m[HEAD_DIM + 1] = s % SEL_BLOCK
    return jnp.asarray(m, BF16)


def _query_slope_features(rows):
    m = np.zeros((N_KV_HEADS, GROUP * rows, LANES), np.float32)
    for g in range(N_KV_HEADS):
        for r in range(GROUP):
            m[g, r * rows:(r + 1) * rows, HEAD_DIM] = SLOPES[g][r] * SEL_BLOCK
            m[g, r * rows:(r + 1) * rows, HEAD_DIM + 1] = SLOPES[g][r]
    return jnp.asarray(m, F32)


def _chunk_lower_tri(n, c):
    i = np.arange(n)
    return jnp.asarray(((i[:, None] // c == i[None, :] // c) & (i[None, :] <= i[:, None])).astype(np.float32), BF16)


def _head_indicator():
    m = np.zeros((RNN_WIDTH, LANES), np.float32)
    m[np.arange(RNN_WIDTH), np.arange(RNN_WIDTH) // RNN_DK] = 1.0
    return jnp.asarray(m, BF16)


def _head_block_diag(n):
    i = np.arange(n)
    return jnp.asarray((i[:, None] // RNN_DV == i[None, :] // RNN_DK).astype(np.float32), F32)


def kernel(x_prompt, x_sample, cache_kv, cache_win, state_rnn, page_table, ln_mix, w_in, q_norm, k_norm, cmp_pe,
           cmp_w1, cmp_w2, attn_out_norm, rnn_lb_logits, rnn_out_norm, w_out, ln_mlp, w_up, w_down):
    assert w_in.shape[0] == 1, "single layer"
    b, t, dm = x_prompt.shape
    nbatch, dec_t, _ = x_sample.shape
    n_pool = cache_kv.shape[1]
    n_pages = page_table.shape[1]
    past_len = n_pages * PAGE_SIZE
    wbuf = cache_win.shape[2]
    assert t % PROJ_ROWS == 0 and t % RNN_ROWS == 0 and t % min(SEL_KEYS, t) == 0 and t >= WINDOW
    assert (b * t) % FINISH_ROWS == 0 and (nbatch * dec_t) % FINISH_ROWS == 0 and w_up.shape[2] % FF_CHUNK == 0
    assert (nbatch * dec_t) % PROJ_ROWS == 0 and nbatch == LANES and dec_t <= 8
    assert past_len % SEL_BLOCK == 0 and wbuf == WINDOW and wbuf >= LANES

    w = w_in[0]
    c_kv, c_gate, c_rq, c_rf, c_ri, c_rg = ATTN_WIDTH, ATTN_WIDTH + 6 * KV_WIDTH, 1304, 1816, 2328, 2840
    gate_cols = jnp.pad(w[:, c_gate:c_rq], ((0, 0), (0, LANES - N_ATTN_HEADS * N_BRANCH)))
    wtok = jnp.concatenate([w[:, 0:ATTN_WIDTH], w[:, c_rq:], gate_cols, w[:, c_kv:c_kv + 2 * KV_WIDTH]],
                           axis=1).astype(BF16)
    wft = w[:, c_kv:c_gate].T.astype(BF16)
    wtok_s = jnp.concatenate([w[:, 0:ATTN_WIDTH], w[:, c_rg:], gate_cols], axis=1).astype(BF16)
    wtb_s = jnp.concatenate([w[:, c_kv:c_kv + 4 * KV_WIDTH], w[:, c_rq:c_rg]], axis=1).T.astype(BF16)
    wbt_s = w[:, c_kv + 2 * KV_WIDTH:c_gate].T.astype(BF16)
    ln = ln_mix[0][None, :]
    qg = (jnp.tile(q_norm[0], N_ATTN_HEADS) * SCALE)[None, :]
    lbl = rnn_lb_logits.astype(F32)
    lblt = jnp.broadcast_to(lbl[:, :, None], lbl.shape + (PROJ_ROWS,))
    gsel = jnp.broadcast_to(k_norm[0, 1][:, None], (HEAD_DIM, PROJ_ROWS))
    gwin = jnp.broadcast_to(k_norm[0, 2][:, None], (HEAD_DIM, PROJ_ROWS))
    kg_dup = jnp.tile(k_norm[0, 0], 2)[None, :]
    pe = cmp_pe[0]
    pe_tok = jnp.tile(jnp.concatenate([jnp.tile(pe[0], (1, N_KV_HEADS)), jnp.tile(pe[1], (1, N_KV_HEADS))], axis=1),
                      (PROJ_ROWS // CMP_BLOCK, 1))
    w1 = cmp_w1[0].reshape(2, CMP_BLOCK, HEAD_DIM, CMP_HIDDEN)
    zeros = jnp.zeros_like(w1)
    w1bd = jnp.concatenate([jnp.concatenate([w1, zeros], axis=3), jnp.concatenate([zeros, w1], axis=3)],
                           axis=2).astype(BF16)
    w1r = jnp.tile(w1.transpose(0, 2, 1, 3), (1, 1, 2, 1)).astype(BF16)
    w1r = w1r.reshape(2, HEAD_DIM * LANES, CMP_HIDDEN)
    pe_t = jnp.tile(pe.transpose(0, 2, 1), (1, 1, 2))[:, :, None, :]
    w2dup = jnp.tile(cmp_w2[0], (1, 1, 2)).astype(BF16)
    ag = attn_out_norm[0][None, :]
    rg = rnn_out_norm[0][None, :]
    wout = w_out[0].astype(BF16)
    lnm = ln_mlp[0][None, :]
    wup = w_up[0].astype(BF16)
    wdn = w_down[0].astype(BF16)
    gexp = _gate_expander()

    (q_p, gates_p, rq_p, f_p, rv_p, rgs_p, kvc_p, kvt_p, wint_p, att_p) = _proj_prompt(
        x_prompt, ln, wtok, wft, qg, lbl, pe_tok, gsel, gwin)
    cmp_p = _compress_prompt(kvc_p, w1bd, w2dup, kg_dup)
    nb_p = t // CMP_BLOCK
    oa_p = _nsa_prompt(q_p, gates_p, cmp_p, att_p, _key_features(t), _query_slope_features(NSA_Q_ROWS), gexp)
    orn_p, st_p = _hgrn_prompt(rq_p, f_p, rv_p, _chunk_lower_tri(RNN_ROWS, RNN_CHUNK),
                               _head_block_diag(LANES), _head_indicator())
    y_p = _finish(x_prompt.reshape(b * t, dm), oa_p.reshape(b * t, ATTN_WIDTH), orn_p.reshape(b * t, RNN_WIDTH),
                  rgs_p.reshape(b * t, RNN_WIDTH), ag, rg, wout, lnm, wup, wdn).reshape(b, t, dm)
    kv_prompt = kvt_p.reshape(1, b, 4, N_KV_HEADS, HEAD_DIM, t).transpose(0, 1, 5, 2, 3, 4)
    wlen = min(WINDOW, t)
    win_prompt = wint_p[:, :, t - wlen:].reshape(1, b, 2, N_KV_HEADS, HEAD_DIM, wlen).transpose(0, 1, 5, 2, 3, 4)
    hh = LANES // RNN_DV
    st5 = st_p.reshape(b, RNN_WIDTH // LANES, hh, RNN_DV, hh, RNN_DK)
    rnn_prompt = jnp.stack([st5[:, :, i, :, i, :] for i in range(hh)], axis=2)
    rnn_prompt = rnn_prompt.reshape(b, N_RNN_HEADS, RNN_DV, RNN_DK).transpose(0, 1, 3, 2)[None]

    n_s = nbatch * dec_t
    xbt = x_sample.reshape(n_s, dm)
    xtb = x_sample.transpose(1, 0, 2).reshape(n_s, dm)
    q_s, rgs_s, gates_s, ztb, zbt = _proj_sample(xbt, xtb, ln, wtok_s, wtb_s, wbt_s, qg, lblt, gsel, gwin)
    kv_sample = ztb[0:4 * KV_WIDTH].reshape(4, N_KV_HEADS, HEAD_DIM, dec_t, nbatch).transpose(4, 3, 0, 1, 2)[None]
    new_rows = jnp.pad(zbt.reshape(4 * KV_WIDTH, nbatch, dec_t).transpose(1, 0, 2),
                       ((0, 0), (0, 0), (0, LANES - dec_t)))
    cache = cache_kv[0].transpose(0, 2, 3, 4, 1).reshape(n_pool, PAGE_ROWS, PAGE_SIZE)
    nblk_s = past_len // CMP_BLOCK
    rho = np.arange(nblk_s)
    e_perm = _block_expander(2 * (rho % n_pages) + rho // n_pages, past_len)
    ns_s = -(-(past_len + dec_t) // SEL_BLOCK)
    q_s3 = q_s.reshape(nbatch, dec_t, ATTN_WIDTH)
    ocs = _nsa_sample(page_table.reshape(-1), cache, q_s3, new_rows[:, 0:2 * KV_WIDTH], w1r, pe_t, w2dup, kg_dup,
                      e_perm, past_len, min(SEL_TOPK, ns_s))
    win = cache_win[0].transpose(0, 2, 3, 4, 1).reshape(nbatch, 2 * KV_WIDTH, wbuf)
    oa_s, win_new = _win_sample(win, q_s3, new_rows[:, 2 * KV_WIDTH:], ocs,
                                gates_s.reshape(nbatch, dec_t, LANES), gexp, past_len)
    win_sample = win_new.reshape(1, nbatch, 2, N_KV_HEADS, HEAD_DIM, wbuf).transpose(0, 1, 5, 2, 3, 4)
    state = state_rnn[0].transpose(1, 2, 3, 0)
    orn_t, state_new = _hgrn_sample(ztb, state, dec_t)
    rnn_sample = state_new.transpose(3, 0, 1, 2)[None]
    orn_s = orn_t.reshape(RNN_WIDTH, dec_t, nbatch).transpose(2, 1, 0).reshape(n_s, RNN_WIDTH)
    y_s = _finish(xbt, oa_s.reshape(n_s, ATTN_WIDTH), orn_s, rgs_s, ag, rg, wout, lnm, wup, wdn).reshape(nbatch, dec_t, dm)

    return (y_p, y_s, kv_prompt, kv_sample, win_prompt, win_sample, rnn_prompt, rnn_sample)
```

```python
import functools

import numpy as np
import jax
import jax.numpy as jnp
from jax import lax
from jax.experimental import pallas as pl
from jax.experimental.pallas import tpu as pltpu

F32 = jnp.float32
BF16 = jnp.bfloat16

HEAD_DIM = 64
N_ATTN_HEADS = 8
N_KV_HEADS = 2
GROUP = N_ATTN_HEADS // N_KV_HEADS
N_RNN_HEADS = 8
RNN_DK = 64
RNN_DV = 64
ATTN_WIDTH = N_ATTN_HEADS * HEAD_DIM
RNN_WIDTH = N_RNN_HEADS * RNN_DV
KV_WIDTH = N_KV_HEADS * HEAD_DIM
N_BRANCH = 3
CMP_BLOCK = 64
SEL_BLOCK = 64
SEL_TOPK = 16
WINDOW = 512
CMP_HIDDEN = 128
PAGE_SIZE = 128
SCALE = HEAD_DIM ** -0.5
EPS = 1e-6
NEG = -1e30
LOG2E = 1.4426950408889634
FORCED_SCORE = GROUP + 1.0
SLOPES = [[2.0 ** (-(g * GROUP + r + 1)) for r in range(GROUP)] for g in range(N_KV_HEADS)]

LANES = 128
VMEM_BYTES_V7X = 64 * 1024 * 1024

PROJ_ROWS = 256
NSA_Q_ROWS = 128
SEL_KEYS = 512
ROW_BLOCK = 16
RNN_ROWS = 256
RNN_CHUNK = 16
FF_CHUNK = 1024
FINISH_ROWS = 512

NT = (((1,), (1,)), ((), ()))
TN = (((0,), (0,)), ((), ()))


def _vmem_limit(nbytes):
    return int(min(VMEM_BYTES_V7X - (8 << 20), max(nbytes, 16 << 20)))


def _dot(a, b):
    return jnp.dot(a, b, preferred_element_type=F32)


def _dot_nt(a, b):
    return lax.dot_general(a, b, NT, preferred_element_type=F32)


def _dot_tn(a, b):
    return lax.dot_general(a, b, TN, preferred_element_type=F32)


def _low_half(shape):
    lane = lax.broadcasted_iota(jnp.int32, shape, len(shape) - 1)
    return (lane & HEAD_DIM) == 0


def _head_mean_sq(x):
    outs = []
    for j in range(x.shape[-1] // LANES):
        blk = x[:, j * LANES:(j + 1) * LANES]
        sq = blk * blk
        low = _low_half(blk.shape)
        s_lo = jnp.sum(jnp.where(low, sq, 0.0), axis=-1, keepdims=True)
        s_hi = jnp.sum(jnp.where(low, 0.0, sq), axis=-1, keepdims=True)
        outs.append(jnp.where(low, s_lo, s_hi))
    return jnp.concatenate(outs, axis=-1) * (1.0 / HEAD_DIM)


def _row_rms(x):
    return x * lax.rsqrt(jnp.mean(x * x, axis=-1, keepdims=True) + EPS)


def _col_head_norm(rows, gain):
    ms = jnp.mean(rows * rows, axis=0, keepdims=True)
    return rows * lax.rsqrt(ms + EPS) * gain


def _silu(z):
    return z * jax.nn.sigmoid(z)


def _lower_bound(logits, axis):
    m = jnp.max(logits, axis=axis, keepdims=True)
    e = jnp.exp(logits - m)
    lb = e / jnp.sum(e, axis=axis, keepdims=True)
    return lb[0:1] if axis == 0 else lb


def _stack_group_queries(q, g):
    rows = q.shape[0]
    low = _low_half((rows, LANES))
    zero = jnp.zeros((rows, LANES), q.dtype)
    pa = q[:, g * 2 * LANES:g * 2 * LANES + LANES]
    pb = q[:, g * 2 * LANES + LANES:(g + 1) * 2 * LANES]
    return jnp.concatenate([jnp.where(low, pa, zero), jnp.where(low, zero, pa),
                            jnp.where(low, pb, zero), jnp.where(low, zero, pb)], axis=0)


def _unstack_group(o2, rows):
    low = _low_half((rows, LANES))
    return jnp.concatenate([jnp.where(low, o2[0:rows], o2[rows:2 * rows]),
                            jnp.where(low, o2[2 * rows:3 * rows], o2[3 * rows:4 * rows])], axis=1)


def _log2(n):
    assert n > 0 and n & (n - 1) == 0, n
    return n.bit_length() - 1


def _twice(x):
    return jnp.concatenate([x, x], axis=0)


def _row_slopes(g, row_head):
    s = jnp.full(row_head.shape, SLOPES[g][GROUP - 1], F32)
    for r in range(GROUP - 2, -1, -1):
        s = jnp.where(row_head == r, SLOPES[g][r], s)
    return s


def _block_ranks(score, ids, cand):
    sub = 8
    nrow = score.shape[0]
    in_order = all(r == i for r, i in cand) and nrow % sub == 0
    rank = jnp.zeros(score.shape, F32)
    for row, id_i in cand:
        s_i = score[row:row + 1, :]
        if in_order:
            parts = []
            for v in range(nrow // sub):
                blk = score[v * sub:(v + 1) * sub, :]
                if (v + 1) * sub - 1 <= id_i:
                    parts.append((s_i > blk).astype(F32))
                elif v * sub > id_i:
                    parts.append((s_i >= blk).astype(F32))
                else:
                    parts.append(jnp.where(ids[v * sub:(v + 1) * sub, :] > id_i,
                                           (s_i >= blk).astype(F32), (s_i > blk).astype(F32)))
            rank = rank + jnp.concatenate(parts, axis=0)
        else:
            rank = rank + jnp.where(ids > id_i, (s_i >= score).astype(F32), (s_i > score).astype(F32))
    return rank


def _expand_gates(gates, gexp_ref):
    hi = gates.astype(BF16)
    lo = (gates - hi.astype(F32)).astype(BF16)
    return [_dot(hi, gexp_ref[br]) + _dot(lo, gexp_ref[br]) for br in range(N_BRANCH)]


TOK_Q, TOK_RQ, TOK_RF, TOK_RI, TOK_RG, TOK_GATE, TOK_KVC, TOK_END = 0, 512, 1024, 1536, 2048, 2560, 2688, 2944


def _proj_prompt_body(x_ref, ln_ref, wtok_ref, wft_ref, qg_ref, lbl_ref, pe_ref, gsel_ref, gwin_ref,
                      q_ref, gates_ref, rq_ref, f_ref, rv_ref, rgs_ref, kvc_ref, kvt_ref, wint_ref, att_ref):
    xb = (_row_rms(x_ref[0]) * ln_ref[...]).astype(BF16)

    def tok(lo, hi):
        return _dot(xb, wtok_ref[:, lo:hi])

    zq = tok(TOK_Q, TOK_RQ)
    q_ref[0] = (zq * lax.rsqrt(_head_mean_sq(zq) + EPS) * qg_ref[...]).astype(BF16)
    rq_ref[0] = _silu(tok(TOK_RQ, TOK_RF))
    lb = _lower_bound(lbl_ref[...], 0)
    f_ref[0] = lb + (1.0 - lb) * jax.nn.sigmoid(tok(TOK_RF, TOK_RI))
    rv_ref[0] = tok(TOK_RI, TOK_RG)
    rgs_ref[0] = _silu(tok(TOK_RG, TOK_GATE))
    gates_ref[0] = jax.nn.sigmoid(tok(TOK_GATE, TOK_KVC))
    zc = tok(TOK_KVC, TOK_END) + pe_ref[...]
    kvc_ref[0, 0] = zc[:, 0:LANES]
    kvc_ref[1, 0] = zc[:, LANES:2 * LANES]

    zf = _dot_nt(wft_ref[...], xb)
    d = HEAD_DIM
    ksel = jnp.concatenate([_col_head_norm(zf[256 + g * d:256 + (g + 1) * d], gsel_ref[...])
                            for g in range(N_KV_HEADS)], axis=0)
    kwin = jnp.concatenate([_col_head_norm(zf[512 + g * d:512 + (g + 1) * d], gwin_ref[...])
                            for g in range(N_KV_HEADS)], axis=0)
    vsel = zf[384:512]
    vwin = zf[640:768]
    kvt_ref[0, 0:256] = zf[0:256]
    kvt_ref[0, 256:384] = ksel
    kvt_ref[0, 384:512] = vsel
    wint_ref[0, 0:128] = kwin
    wint_ref[0, 128:256] = vwin
    att_ref[0, 0:128] = ksel.astype(BF16)
    att_ref[0, 128:256] = vsel.astype(BF16)
    att_ref[0, 256:384] = kwin.astype(BF16)
    att_ref[0, 384:512] = vwin.astype(BF16)


def _proj_prompt(x, ln, wtok, wft, qg, lbl, pe_tok, gsel, gwin):
    b, t, dm = x.shape
    tm = PROJ_ROWS
    grid = (b, t // tm)
    row = lambda i, j: (i, j, 0)
    col = lambda i, j: (i, 0, j)
    const2 = lambda i, j: (0, 0)
    out_shape = (
        jax.ShapeDtypeStruct((b, t, ATTN_WIDTH), BF16),
        jax.ShapeDtypeStruct((b, t, LANES), F32),
        jax.ShapeDtypeStruct((b, t, RNN_WIDTH), F32),
        jax.ShapeDtypeStruct((b, t, RNN_WIDTH), F32),
        jax.ShapeDtypeStruct((b, t, RNN_WIDTH), F32),
        jax.ShapeDtypeStruct((b, t, RNN_WIDTH), F32),
        jax.ShapeDtypeStruct((2, b, t, LANES), F32),
        jax.ShapeDtypeStruct((b, 4 * KV_WIDTH, t), F32),
        jax.ShapeDtypeStruct((b, 2 * KV_WIDTH, t), F32),
        jax.ShapeDtypeStruct((b, 4 * KV_WIDTH, t), BF16),
    )
    out_specs = (
        pl.BlockSpec((1, tm, ATTN_WIDTH), row), pl.BlockSpec((1, tm, LANES), row),
        pl.BlockSpec((1, tm, RNN_WIDTH), row), pl.BlockSpec((1, tm, RNN_WIDTH), row),
        pl.BlockSpec((1, tm, RNN_WIDTH), row), pl.BlockSpec((1, tm, RNN_WIDTH), row),
        pl.BlockSpec((2, 1, tm, LANES), lambda i, j: (0, i, j, 0)),
        pl.BlockSpec((1, 4 * KV_WIDTH, tm), col), pl.BlockSpec((1, 2 * KV_WIDTH, tm), col),
        pl.BlockSpec((1, 4 * KV_WIDTH, tm), col),
    )
    in_specs = [
        pl.BlockSpec((1, tm, dm), row), pl.BlockSpec(ln.shape, const2),
        pl.BlockSpec(wtok.shape, const2), pl.BlockSpec(wft.shape, const2),
        pl.BlockSpec(qg.shape, const2), pl.BlockSpec(lbl.shape, const2), pl.BlockSpec(pe_tok.shape, const2),
        pl.BlockSpec(gsel.shape, const2), pl.BlockSpec(gwin.shape, const2),
    ]
    return pl.pallas_call(
        _proj_prompt_body, grid=grid, in_specs=in_specs, out_specs=out_specs, out_shape=out_shape,
        compiler_params=pltpu.CompilerParams(dimension_semantics=("parallel", "parallel"),
                                             vmem_limit_bytes=_vmem_limit(48 << 20)),
        name="proj_prompt",
    )(x, ln, wtok, wft, qg, lbl, pe_tok, gsel, gwin)


def _compress_prompt_body(x_ref, w1_ref, w2_ref, kg_ref, out_ref):
    c = pl.program_id(0)
    nb = out_ref.shape[2]
    acc = jnp.zeros((nb, 2 * CMP_HIDDEN), F32)
    for pos in range(CMP_BLOCK):
        xp = x_ref[0, 0, pl.ds(pos, nb, stride=CMP_BLOCK), :]
        acc = acc + _dot(xp.astype(BF16), w1_ref[0, pos])
    hb = _silu(acc).astype(BF16)
    outs = []
    for g in range(N_KV_HEADS):
        y = _dot(hb[:, g * CMP_HIDDEN:(g + 1) * CMP_HIDDEN], w2_ref[0])
        yn = _row_rms(y) * kg_ref[...]
        outs.append(jnp.where(c == 0, yn, y))
    out_ref[0, 0] = jnp.concatenate(outs, axis=1)


def _compress_prompt(kvc, w1bd, w2dup, kg_dup):
    _, b, t, _ = kvc.shape
    nb = t // CMP_BLOCK
    return pl.pallas_call(
        _compress_prompt_body, grid=(2, b),
        in_specs=[pl.BlockSpec((1, 1, t, LANES), lambda c, i: (c, i, 0, 0)),
                  pl.BlockSpec((1,) + w1bd.shape[1:], lambda c, i: (c, 0, 0, 0)),
                  pl.BlockSpec((1,) + w2dup.shape[1:], lambda c, i: (c, 0, 0)),
                  pl.BlockSpec(kg_dup.shape, lambda c, i: (0, 0))],
        out_specs=pl.BlockSpec((1, 1, nb, 2 * LANES), lambda c, i: (c, i, 0, 0)),
        out_shape=jax.ShapeDtypeStruct((2, b, nb, 2 * LANES), F32),
        compiler_params=pltpu.CompilerParams(dimension_semantics=("arbitrary", "arbitrary"),
                                             vmem_limit_bytes=_vmem_limit(32 << 20)),
        name="compress_prompt",
    )(kvc, w1bd, w2dup, kg_dup)


def _row_max_update(sc_scr, nm_scr, m_scr, mn_scr, ncol, tq):
    rb = ROW_BLOCK
    for i in range(GROUP * tq // rb):
        rows = slice(i * rb, (i + 1) * rb)
        qrows = slice((i * rb) % tq, (i * rb) % tq + rb)
        mx = jnp.full((rb, LANES), NEG, F32)
        for j in range(ncol):
            cols = slice(j * LANES, (j + 1) * LANES)
            v = sc_scr[rows, cols]
            if nm_scr is not None:
                v = v + nm_scr[qrows, cols]
                sc_scr[rows, cols] = v
            mx = jnp.maximum(mx, v)
        mn_scr[rows, :] = jnp.maximum(m_scr[rows, :], jnp.max(mx, axis=-1, keepdims=True))


def _softmax_update(sc_scr, p_scr, m_scr, mn_scr, l_scr, a_scr, ncol, tq):
    rb = ROW_BLOCK
    for i in range(GROUP * tq // rb):
        rows = slice(i * rb, (i + 1) * rb)
        mn = mn_scr[rows, :]
        tot = jnp.zeros((rb, LANES), F32)
        for j in range(ncol):
            cols = slice(j * LANES, (j + 1) * LANES)
            p = jnp.exp(sc_scr[rows, cols] - mn)
            tot = tot + p
            p_scr[rows, cols] = p.astype(BF16)
        alpha = jnp.exp(m_scr[rows, :] - mn)
        l_scr[rows, :] = alpha * l_scr[rows, :] + jnp.sum(tot, axis=-1, keepdims=True)
        a_scr[rows, :] = alpha
        m_scr[rows, :] = mn


def _nsa_prompt_body(q_ref, gates_ref, cmp_ref, att_ref, kfeat_ref, qfeat_ref, gexp_ref, oa_ref,
                     sc_scr, sc2_scr, nm_scr, p_scr, m_scr, mn_scr, l_scr, a_scr, acc_scr, *, seq, topk):
    tq = NSA_Q_ROWS
    tk = min(SEL_KEYS, seq)
    nb = cmp_ref.shape[2]
    t0 = pl.program_id(1) * tq
    wk = min(WINDOW + tq, seq)
    ws = pl.multiple_of(jnp.clip(t0 - WINDOW, 0, seq - wk), LANES)
    q = q_ref[0]
    d = HEAD_DIM
    groups = range(N_KV_HEADS)
    def reset(stats):
        for g in groups:
            m_scr[g] = jnp.full(m_scr.shape[1:], NEG, F32)
            for ref in stats:
                ref[g] = jnp.zeros(ref.shape[1:], F32)

    sc_bufs = (sc_scr, sc2_scr)

    def logits(buf, g, qa, kt_, kf, width):
        sc_bufs[buf][g, :, 0:width] = _dot(qa, jnp.concatenate([kt_, kt_, kf], axis=0))

    def softmax_tiles(buf, ncol, masked):
        sc = sc_bufs[buf]
        for g in groups:
            _row_max_update(sc.at[g], nm_scr if masked else None, m_scr.at[g], mn_scr.at[g], ncol, tq)
        for g in groups:
            _softmax_update(sc.at[g], p_scr.at[g], m_scr.at[g], mn_scr.at[g], l_scr.at[g], a_scr.at[g], ncol, tq)

    qst = [_stack_group_queries(q, g) for g in groups]
    qwin = [jnp.concatenate([qst[g], qfeat_ref[g].astype(BF16)], axis=1) for g in groups]

    kcs = [cmp_ref[0, 0][:, g * LANES:(g + 1) * LANES].astype(BF16) for g in groups]
    vcs = [cmp_ref[1, 0][:, g * LANES:(g + 1) * LANES].astype(BF16) for g in groups]
    st = [_dot_nt(kcs[g], qst[g]) for g in groups]

    reset((l_scr,))
    kfw = kfeat_ref[:, pl.ds(ws, wk)]
    ddw = (t0 - ws) + lax.broadcasted_iota(jnp.int32, (tq, wk), 0) - lax.broadcasted_iota(jnp.int32, (tq, wk), 1)
    nm_scr[:, 0:wk] = jnp.where((ddw >= 0) & (ddw < WINDOW), 0.0, NEG)
    for g in groups:
        logits(0, g, qwin[g], att_ref[0, 2 * KV_WIDTH + g * d:2 * KV_WIDTH + (g + 1) * d, pl.ds(ws, wk)], kfw, wk)

    blk = lax.broadcasted_iota(jnp.int32, (nb, GROUP * tq), 0)
    colq = lax.broadcasted_iota(jnp.int32, (1, GROUP * tq), 1)
    dist = (t0 + (colq & (tq - 1)) - (blk * CMP_BLOCK + (CMP_BLOCK - 1))).astype(F32)
    valid = dist >= 0.0
    ps = []
    for g in groups:
        s = jnp.where(valid, st[g] - _row_slopes(g, colq >> _log2(tq)) * dist, NEG)
        e = jnp.exp(s - jnp.max(s, axis=0, keepdims=True))
        ps.append(jnp.where(valid, e / jnp.sum(e, axis=0, keepdims=True), 0.0))
    o_cmp = [_unstack_group(_dot_tn(ps[g].astype(BF16), vcs[g]), tq) for g in groups]

    softmax_tiles(0, wk // LANES, True)
    o_win = []
    for g in groups:
        vw = att_ref[0, 3 * KV_WIDTH + g * d:3 * KV_WIDTH + (g + 1) * d, pl.ds(ws, wk)]
        o_win.append(_unstack_group(_dot_nt(p_scr[g, :, 0:wk], _twice(vw)) / l_scr[g], tq))

    bj = lax.broadcasted_iota(jnp.int32, (nb, tq), 0)
    cur = (t0 + lax.broadcasted_iota(jnp.int32, (nb, tq), 1)) >> _log2(SEL_BLOCK)
    force = (bj == 0) | (bj == cur)
    qsel = []
    for g in groups:
        p = ps[g]
        imp = p[:, 0:tq] + p[:, tq:2 * tq] + p[:, 2 * tq:3 * tq] + p[:, 3 * tq:4 * tq]
        score = jnp.where(bj <= cur, jnp.where(force, FORCED_SCORE, imp), -1.0)
        rank = _block_ranks(score, bj, [(i, i) for i in range(nb)])
        mneg = jnp.where((rank < topk) & (score >= 0.0), 0.0, NEG)
        mtok = jnp.concatenate([mneg, jnp.zeros((LANES - nb, tq), F32)], axis=0).T
        qsel.append(jnp.concatenate(
            [qst[g], (qfeat_ref[g] + jnp.concatenate([mtok] * GROUP, axis=0)).astype(BF16)], axis=1))

    reset((l_scr, acc_scr))
    n_kt = (t0 + tq - 1) // tk + 1

    def sel_logits(kt, buf):
        s0 = pl.multiple_of(kt * tk, tk)
        kf = kfeat_ref[:, pl.ds(s0, tk)]
        for g in groups:
            logits(buf, g, qsel[g], att_ref[0, g * d:(g + 1) * d, pl.ds(s0, tk)], kf, tk)

    def sel_update(kt, buf, causal):
        s0 = pl.multiple_of(kt * tk, tk)
        if causal:
            dd = (t0 - s0) + lax.broadcasted_iota(jnp.int32, (tq, tk), 0) - lax.broadcasted_iota(jnp.int32, (tq, tk), 1)
            nm_scr[:, 0:tk] = jnp.where(dd >= 0, 0.0, NEG)
        softmax_tiles(buf, tk // LANES, causal)
        for g in groups:
            vt_ = att_ref[0, KV_WIDTH + g * d:KV_WIDTH + (g + 1) * d, pl.ds(s0, tk)]
            acc_scr[g] = a_scr[g] * acc_scr[g] + _dot_nt(p_scr[g, :, 0:tk], _twice(vt_))

    sel_logits(0, 0)

    def full_tile(kt, carry):
        for par in (0, 1):
            @pl.when((kt & 1) == par)
            def _(par=par):
                sel_logits(kt + 1, 1 - par)
                sel_update(kt, par, False)
        return carry

    lax.fori_loop(0, n_kt - 1, full_tile, 0)
    for par in (0, 1):
        @pl.when(((n_kt - 1) & 1) == par)
        def _(par=par):
            sel_update(n_kt - 1, par, True)
    o_sel = [_unstack_group(acc_scr[g] / l_scr[g], tq) for g in groups]

    ge = _expand_gates(gates_ref[0], gexp_ref)
    oa_ref[0] = (ge[0] * jnp.concatenate(o_cmp, axis=1) + ge[1] * jnp.concatenate(o_sel, axis=1)
                 + ge[2] * jnp.concatenate(o_win, axis=1))


def _nsa_prompt(q, gates, cmpkv, att, k_feat, q_feat, gexp):
    b, t, _ = q.shape
    tq = NSA_Q_ROWS
    nb = t // CMP_BLOCK
    assert nb <= HEAD_DIM, "the block mask uses 64 feature lanes"
    cw = max(min(SEL_KEYS, t), min(WINDOW + tq, t))
    body = functools.partial(_nsa_prompt_body, seq=t, topk=min(SEL_TOPK, nb))
    ng = N_KV_HEADS
    stat = pltpu.VMEM((ng, GROUP * tq, LANES), F32)
    return pl.pallas_call(
        body, grid=(b, t // tq),
        in_specs=[pl.BlockSpec((1, tq, ATTN_WIDTH), lambda i, j: (i, j, 0)),
                  pl.BlockSpec((1, tq, LANES), lambda i, j: (i, j, 0)),
                  pl.BlockSpec((2, 1, nb, 2 * LANES), lambda i, j: (0, i, 0, 0)),
                  pl.BlockSpec((1, 4 * KV_WIDTH, t), lambda i, j: (i, 0, 0)),
                  pl.BlockSpec(k_feat.shape, lambda i, j: (0, 0)),
                  pl.BlockSpec(q_feat.shape, lambda i, j: (0, 0, 0)),
                  pl.BlockSpec(gexp.shape, lambda i, j: (0, 0, 0))],
        out_specs=pl.BlockSpec((1, tq, ATTN_WIDTH), lambda i, j: (i, j, 0)),
        out_shape=jax.ShapeDtypeStruct((b, t, ATTN_WIDTH), F32),
        scratch_shapes=[pltpu.VMEM((ng, GROUP * tq, cw), F32), pltpu.VMEM((ng, GROUP * tq, cw), F32),
                        pltpu.VMEM((tq, cw), F32),
                        pltpu.VMEM((ng, GROUP * tq, cw), BF16), stat, stat, stat, stat, stat],
        compiler_params=pltpu.CompilerParams(dimension_semantics=("parallel", "parallel"),
                                             vmem_limit_bytes=_vmem_limit(48 << 20)),
        name="nsa_prompt",
    )(q, gates, cmpkv, att, k_feat, q_feat, gexp)


def _split3(x):
    hi = x.astype(BF16)
    r1 = x - hi.astype(F32)
    mid = r1.astype(BF16)
    lo = (r1 - mid.astype(F32)).astype(BF16)
    return hi, mid, lo


def _hgrn_prompt_body(rq_ref, f_ref, rv_ref, ltri_ref, bd_ref, ind_ref, indt_ref, o_ref, st_ref,
                      s_scr, cum_scr, k_scr, prod_scr):
    c16 = RNN_CHUNK
    nbat = rq_ref.shape[0]

    @pl.when(pl.program_id(0) == 0)
    def _():
        s_scr[...] = jnp.zeros(s_scr.shape, F32)

    ltri = ltri_ref[...]
    for bi in range(nbat):
        f = f_ref[bi]
        hi, mid, lo = _split3(jnp.log(f))
        cum = (_dot(ltri, hi) + _dot(ltri, mid) + _dot(ltri, lo)) * LOG2E
        cum_scr[bi] = cum
        k_scr[bi] = cum - jnp.log2(1.0 - f)

    si = lax.broadcasted_iota(jnp.int32, (c16, c16, RNN_WIDTH), 0)
    ti = lax.broadcasted_iota(jnp.int32, (c16, c16, RNN_WIDTH), 1)
    causal = si <= ti
    npair = RNN_WIDTH // LANES

    def pair_products(c, buf, bi):
        r0 = pl.multiple_of(c * c16, c16)
        cc = cum_scr[bi, pl.ds(r0, c16), :]
        qc = rq_ref[bi, pl.ds(r0, c16), :]
        lk = k_scr[bi, pl.ds(r0, c16), :]
        dec = jnp.exp2(jnp.where(causal, cc[None, :, :] - lk[:, None, :], NEG))
        prod_scr[buf, bi] = (qc[None, :, :] * dec).reshape(c16 * c16, RNN_WIDTH).astype(BF16)

    def recurrence(c, buf, nxt):
        r0 = pl.multiple_of(c * c16, c16)
        rows = range(nbat)
        a1 = [_dot(prod_scr[buf, bi], ind_ref[...]) for bi in rows]
        if nxt is not None:
            for bi in rows:
                pair_products(nxt, 1 - buf, bi)
        cc = [cum_scr[bi, pl.ds(r0, c16), :] for bi in rows]
        last = [x[c16 - 1:c16, :] for x in cc]
        o_int = []
        for bi in rows:
            qd = (rq_ref[bi, pl.ds(r0, c16), :] * jnp.exp2(cc[bi])).astype(BF16)
            o_int.append(jnp.concatenate([_dot_nt(qd[:, p * LANES:(p + 1) * LANES], s_scr[bi, p].astype(BF16))
                                          for p in range(npair)], axis=1))
        a2 = [_dot(a1[bi].astype(BF16), indt_ref[...]).reshape(c16, c16, RNN_WIDTH) for bi in rows]
        for bi in rows:
            kd = jnp.exp2(last[bi] - k_scr[bi, pl.ds(r0, c16), :]).astype(BF16)
            vb = rv_ref[bi, pl.ds(r0, c16), :].astype(BF16)
            dl = jnp.exp2(last[bi])
            for p in range(npair):
                u = _dot_tn(vb[:, p * LANES:(p + 1) * LANES], kd[:, p * LANES:(p + 1) * LANES])
                s_scr[bi, p] = s_scr[bi, p] * dl[:, p * LANES:(p + 1) * LANES] + u * bd_ref[...]
        for bi in rows:
            vc = rv_ref[bi, pl.ds(r0, c16), :]
            o_ref[bi, pl.ds(r0, c16), :] = o_int[bi] + jnp.sum(a2[bi] * vc[:, None, :], axis=0)

    nchunk = rq_ref.shape[1] // c16
    assert nchunk % 2 == 0 and nchunk >= 4
    for bi in range(nbat):
        pair_products(0, 0, bi)

    def two_chunks(j, carry):
        for sub in (0, 1):
            recurrence(2 * j + sub, sub, 2 * j + sub + 1)
        return carry

    lax.fori_loop(0, nchunk // 2 - 1, two_chunks, 0)
    recurrence(nchunk - 2, 0, nchunk - 1)
    recurrence(nchunk - 1, 1, None)
    st_ref[...] = s_scr[...]


def _hgrn_prompt(rq, f, rv, ltri, bd, ind):
    b, t, w = rq.shape
    tc = RNN_ROWS
    npair = w // LANES
    row = lambda j: (0, j, 0)
    const = lambda j: (0, 0)
    indt = ind.T
    return pl.pallas_call(
        _hgrn_prompt_body, grid=(t // tc,),
        in_specs=[pl.BlockSpec((b, tc, w), row), pl.BlockSpec((b, tc, w), row), pl.BlockSpec((b, tc, w), row),
                  pl.BlockSpec(ltri.shape, const), pl.BlockSpec(bd.shape, const),
                  pl.BlockSpec(ind.shape, const), pl.BlockSpec(indt.shape, const)],
        out_specs=(pl.BlockSpec((b, tc, w), row), pl.BlockSpec((b, npair, LANES, LANES), lambda j: (0, 0, 0, 0))),
        out_shape=(jax.ShapeDtypeStruct((b, t, w), F32), jax.ShapeDtypeStruct((b, npair, LANES, LANES), F32)),
        scratch_shapes=[pltpu.VMEM((b, npair, LANES, LANES), F32), pltpu.VMEM((b, tc, w), F32),
                        pltpu.VMEM((b, tc, w), F32), pltpu.VMEM((2, b, RNN_CHUNK * RNN_CHUNK, w), BF16)],
        compiler_params=pltpu.CompilerParams(dimension_semantics=("arbitrary",),
                                             vmem_limit_bytes=_vmem_limit(40 << 20)),
        name="hgrn_prompt",
    )(rq, f, rv, ltri, bd, ind, indt)


def _finish_body(x_ref, oa_ref, orn_ref, rgs_ref, ag_ref, rg_ref, wout_ref, lnm_ref, wup_ref, wdn_ref, y_ref, hn_scr):
    @pl.when(pl.program_id(1) == 0)
    def _():
        oa = oa_ref[...]
        orn = orn_ref[...]
        a_n = oa * lax.rsqrt(_head_mean_sq(oa) + EPS) * ag_ref[...]
        r_n = orn * lax.rsqrt(_head_mean_sq(orn) + EPS) * rg_ref[...] * rgs_ref[...]
        h = (x_ref[...] + _dot(a_n.astype(BF16), wout_ref[0:ATTN_WIDTH, :])
             + _dot(r_n.astype(BF16), wout_ref[ATTN_WIDTH:ATTN_WIDTH + RNN_WIDTH, :]))
        y_ref[...] = h
        hn_scr[...] = (_row_rms(h) * lnm_ref[...]).astype(BF16)

    u = jnp.maximum(_dot(hn_scr[...], wup_ref[...]), 0.0)
    y_ref[...] += _dot((u * u).astype(BF16), wdn_ref[...])


def _finish(x, oa, orn, rgs, ag, rg, wout, lnm, wup, wdn):
    n, dm = x.shape
    tm = FINISH_ROWS
    dff = wup.shape[1]
    row = lambda i, j: (i, 0)
    const = lambda i, j: (0, 0)
    return pl.pallas_call(
        _finish_body, grid=(n // tm, dff // FF_CHUNK),
        in_specs=[pl.BlockSpec((tm, dm), row), pl.BlockSpec((tm, ATTN_WIDTH), row),
                  pl.BlockSpec((tm, RNN_WIDTH), row), pl.BlockSpec((tm, RNN_WIDTH), row),
                  pl.BlockSpec(ag.shape, const), pl.BlockSpec(rg.shape, const),
                  pl.BlockSpec(wout.shape, const), pl.BlockSpec(lnm.shape, const),
                  pl.BlockSpec((dm, FF_CHUNK), lambda i, j: (0, j)),
                  pl.BlockSpec((FF_CHUNK, dm), lambda i, j: (j, 0))],
        out_specs=pl.BlockSpec((tm, dm), row),
        out_shape=jax.ShapeDtypeStruct((n, dm), F32),
        scratch_shapes=[pltpu.VMEM((tm, dm), BF16)],
        compiler_params=pltpu.CompilerParams(dimension_semantics=("parallel", "arbitrary"),
                                             vmem_limit_bytes=_vmem_limit(48 << 20)),
        name="finish",
    )(x, oa, orn, rgs, ag, rg, wout, lnm, wup, wdn)


def _proj_sample_body(xbt_ref, xtb_ref, ln_ref, wtok_ref, wtb_ref, wbt_ref, qg_ref, lbl_ref, gsel_ref, gwin_ref,
                      q_ref, rgs_ref, gates_ref, ztb_ref, zbt_ref):
    xb = (_row_rms(xbt_ref[...]) * ln_ref[...]).astype(BF16)
    xt = (_row_rms(xtb_ref[...]) * ln_ref[...]).astype(BF16)
    zq = _dot(xb, wtok_ref[:, 0:ATTN_WIDTH])
    q_ref[...] = (zq * lax.rsqrt(_head_mean_sq(zq) + EPS) * qg_ref[...]).astype(BF16)
    rgs_ref[...] = _silu(_dot(xb, wtok_ref[:, ATTN_WIDTH:ATTN_WIDTH + RNN_WIDTH]))
    gates_ref[...] = jax.nn.sigmoid(_dot(xb, wtok_ref[:, ATTN_WIDTH + RNN_WIDTH:ATTN_WIDTH + RNN_WIDTH + LANES]))

    d = HEAD_DIM
    zt = _dot_nt(wtb_ref[...], xt)
    ztb_ref[0:256] = zt[0:256]
    for g in range(N_KV_HEADS):
        ztb_ref[256 + g * d:256 + (g + 1) * d] = _col_head_norm(zt[256 + g * d:256 + (g + 1) * d], gsel_ref[...])
    ztb_ref[384:512] = zt[384:512]
    ztb_ref[512:1024] = _silu(zt[512:1024])
    lb = _lower_bound(lbl_ref[...], 0)[0]
    ztb_ref[1024:1536] = lb + (1.0 - lb) * jax.nn.sigmoid(zt[1024:1536])
    ztb_ref[1536:2048] = zt[1536:2048]

    zb = _dot_nt(wbt_ref[...], xb)
    for g in range(N_KV_HEADS):
        zbt_ref[g * d:(g + 1) * d] = _col_head_norm(zb[g * d:(g + 1) * d], gsel_ref[...])
        zbt_ref[256 + g * d:256 + (g + 1) * d] = _col_head_norm(zb[256 + g * d:256 + (g + 1) * d], gwin_ref[...])
    zbt_ref[128:256] = zb[128:256]
    zbt_ref[384:512] = zb[384:512]


def _proj_sample(xbt, xtb, ln, wtok, wtb, wbt, qg, lblt, gsel, gwin):
    n, dm = xbt.shape
    tm = PROJ_ROWS
    row = lambda i: (i, 0)
    col = lambda i: (0, i)
    const = lambda i: (0, 0)
    return pl.pallas_call(
        _proj_sample_body, grid=(n // tm,),
        in_specs=[pl.BlockSpec((tm, dm), row), pl.BlockSpec((tm, dm), row), pl.BlockSpec(ln.shape, const),
                  pl.BlockSpec(wtok.shape, const), pl.BlockSpec(wtb.shape, const), pl.BlockSpec(wbt.shape, const),
                  pl.BlockSpec(qg.shape, const), pl.BlockSpec(lblt.shape, lambda i: (0, 0, 0)),
                  pl.BlockSpec(gsel.shape, const), pl.BlockSpec(gwin.shape, const)],
        out_specs=(pl.BlockSpec((tm, ATTN_WIDTH), row), pl.BlockSpec((tm, RNN_WIDTH), row),
                   pl.BlockSpec((tm, LANES), row), pl.BlockSpec((wtb.shape[0], tm), col),
                   pl.BlockSpec((wbt.shape[0], tm), col)),
        out_shape=(jax.ShapeDtypeStruct((n, ATTN_WIDTH), BF16), jax.ShapeDtypeStruct((n, RNN_WIDTH), F32),
                   jax.ShapeDtypeStruct((n, LANES), F32), jax.ShapeDtypeStruct((wtb.shape[0], n), F32),
                   jax.ShapeDtypeStruct((wbt.shape[0], n), F32)),
        compiler_params=pltpu.CompilerParams(dimension_semantics=("parallel",),
                                             vmem_limit_bytes=_vmem_limit(40 << 20)),
        name="proj_sample",
    )(xbt, xtb, ln, wtok, wtb, wbt, qg, lblt, gsel, gwin)


PAGE_ROWS = 4 * KV_WIDTH
SAMPLE_BATCH_PER_STEP = 2
WINDOW_BATCH_PER_STEP = 4


def _nsa_sample_body(pt_ref, cache_ref, q_ref, nkv_ref, w1_ref, pe_ref, w2_ref, kg_ref, e_ref,
                     ocs_ref, cbuf, sbuf, lhs_scr, kk_scr, vv_scr, sem, *, n_pages, past_len, dec_t, topk):
    step_id = pl.program_id(0)
    nsteps = pl.num_programs(0)
    slot = step_id % 2
    d = HEAD_DIM
    nblk = 2 * n_pages
    half_rows = PAGE_ROWS // 2
    bps = q_ref.shape[0]

    def page_copies(st, sl, bi, j):
        pg = pt_ref[(st * bps + bi) * n_pages + j]
        return (pltpu.make_async_copy(cache_ref.at[pg, pl.ds(0, half_rows)], cbuf.at[sl, bi, :, j, :], sem.at[sl]),
                pltpu.make_async_copy(cache_ref.at[pg, pl.ds(half_rows, half_rows)],
                                      sbuf.at[sl, bi, pl.ds(j * half_rows, half_rows)], sem.at[sl]))

    def all_copies(st, sl):
        return [cp for bi in range(bps) for j in range(n_pages) for cp in page_copies(st, sl, bi, j)]

    @pl.when(step_id == 0)
    def _():
        for cp in all_copies(0, 0):
            cp.start()

    @pl.when(step_id + 1 < nsteps)
    def _():
        for cp in all_copies(step_id + 1, 1 - slot):
            cp.start()

    for cp in all_copies(step_id, slot):
        cp.wait()

    low = _low_half((n_pages, LANES))
    rows_b = 4 * n_pages

    def compress(c):
        for dd in range(d):
            rows = []
            for bi in range(bps):
                for g in range(N_KV_HEADS):
                    xg = cbuf[slot, bi, (c * N_KV_HEADS + g) * d + dd]
                    xg = xg + pe_ref[c, dd]
                    rows += [jnp.where(low, xg, 0.0), jnp.where(low, 0.0, xg)]
            lhs_scr[:, dd * LANES:(dd + 1) * LANES] = jnp.concatenate(rows, axis=0).astype(BF16)
        acc = _dot(lhs_scr[...], w1_ref[c])
        return _dot(_silu(acc).astype(BF16), w2_ref[c])

    kc_all = _row_rms(compress(0)) * kg_ref[...]
    vc_all = compress(1)

    nq = GROUP * dec_t
    rho = lax.broadcasted_iota(jnp.int32, (nblk, dec_t), 0)
    bid = 2 * (rho & (n_pages - 1)) + (rho >> _log2(n_pages))
    colq = lax.broadcasted_iota(jnp.int32, (1, nq), 1)
    rowq = lax.broadcasted_iota(jnp.int32, (nq, 1), 0)
    cur_blk = past_len // SEL_BLOCK
    chains = [(bi, g) for bi in range(bps) for g in range(N_KV_HEADS)]
    nch = len(chains)
    step = rowq & (dec_t - 1)
    qpos = past_len + (colq & (dec_t - 1))
    rho_q = lax.broadcasted_iota(jnp.int32, (nblk, 1), 0)
    end = (2 * (rho_q & (n_pages - 1)) + (rho_q >> _log2(n_pages))) * CMP_BLOCK + (CMP_BLOCK - 1)
    dist = (qpos - end).astype(F32)
    valid = dist >= 0.0
    kpos = lax.broadcasted_iota(jnp.int32, (1, past_len), 1)
    dpast = (past_len + step - kpos).astype(F32)
    lane = lax.broadcasted_iota(jnp.int32, (1, LANES), 1)
    per_tile = LANES // dec_t
    offs = [((step_id * bps + bi) & (per_tile - 1)) * dec_t for bi in range(bps)]
    dnews = [(step - (lane - off)).astype(F32) for off in offs]
    oknew = [(lane >= off) & (lane < off + dec_t) for off in offs]

    qst = [_stack_group_queries(q_ref[bi], g) for bi, g in chains]
    kcs = [kc_all[bi * rows_b + g * nblk:bi * rows_b + (g + 1) * nblk].astype(BF16) for bi, g in chains]
    vcs = [vc_all[bi * rows_b + g * nblk:bi * rows_b + (g + 1) * nblk].astype(BF16) for bi, g in chains]
    st = [_dot_nt(kcs[ch], qst[ch]) for ch in range(nch)]
    for ch, (bi, g) in enumerate(chains):
        for pg in range(n_pages):
            kt_ = sbuf[slot, bi, pl.ds(pg * half_rows + g * d, d), :].astype(BF16)
            vt_ = sbuf[slot, bi, pl.ds(pg * half_rows + (N_KV_HEADS + g) * d, d), :].astype(BF16)
            kk_scr[ch, 0:d, pg * PAGE_SIZE:(pg + 1) * PAGE_SIZE] = kt_
            kk_scr[ch, d:2 * d, pg * PAGE_SIZE:(pg + 1) * PAGE_SIZE] = kt_
            vv_scr[ch, 0:d, pg * PAGE_SIZE:(pg + 1) * PAGE_SIZE] = vt_
            vv_scr[ch, d:2 * d, pg * PAGE_SIZE:(pg + 1) * PAGE_SIZE] = vt_
    sp = [_dot(qst[ch], kk_scr[ch]) for ch in range(nch)]
    nk = [_twice(nkv_ref[g * d:(g + 1) * d, :].astype(BF16)) for g in range(N_KV_HEADS)]
    nv = [_twice(nkv_ref[KV_WIDTH + g * d:KV_WIDTH + (g + 1) * d, :].astype(BF16)) for g in range(N_KV_HEADS)]
    sn = [_dot(qst[ch], nk[g]) for ch, (bi, g) in enumerate(chains)]
    ps = []
    for ch, (bi, g) in enumerate(chains):
        s = jnp.where(valid, st[ch] - _row_slopes(g, colq >> _log2(dec_t)) * dist, NEG)
        e = jnp.exp(s - jnp.max(s, axis=0, keepdims=True))
        ps.append(jnp.where(valid, e / jnp.sum(e, axis=0, keepdims=True), 0.0))
    o_cmp = [_unstack_group(_dot_tn(ps[ch].astype(BF16), vcs[ch]), dec_t) for ch in range(nch)]
    msel = []
    for ch in range(nch):
        p = ps[ch]
        imp = p[:, 0:dec_t]
        for r in range(1, GROUP):
            imp = imp + p[:, r * dec_t:(r + 1) * dec_t]
        force = (bid == 0) | (bid == cur_blk)
        score = jnp.where(force, FORCED_SCORE, imp)
        rank = _block_ranks(score, bid, [(i, 2 * (i % n_pages) + i // n_pages) for i in range(nblk)])
        rank = rank + jnp.where(bid > cur_blk, (FORCED_SCORE >= score).astype(F32),
                                (FORCED_SCORE > score).astype(F32))
        msel.append(jnp.where((rank < topk) & (score >= 0.0), 1.0, 0.0).astype(BF16))
    mk = [_dot_tn(msel[ch], e_ref[...]) for ch in range(nch)]
    pp, pn, den = [], [], []
    for ch, (bi, g) in enumerate(chains):
        okp = jnp.concatenate([mk[ch]] * GROUP, axis=0) > 0.5
        slope = _row_slopes(g, rowq >> _log2(dec_t))
        lgp = jnp.where(okp & (dpast >= 0.0), sp[ch] - slope * dpast, NEG)
        lgn = jnp.where(oknew[bi] & (dnews[bi] >= 0.0), sn[ch] - slope * dnews[bi], NEG)
        m = jnp.maximum(jnp.max(lgp, axis=-1, keepdims=True), jnp.max(lgn, axis=-1, keepdims=True))
        pp.append(jnp.exp(lgp - m))
        pn.append(jnp.exp(lgn - m))
        den.append(jnp.sum(pp[ch], axis=-1, keepdims=True) + jnp.sum(pn[ch], axis=-1, keepdims=True))
    o_sel = [_unstack_group((_dot_nt(pp[ch].astype(BF16), vv_scr[ch]) + _dot_nt(pn[ch].astype(BF16), nv[g]))
                            / den[ch], dec_t) for ch, (bi, g) in enumerate(chains)]

    for bi in range(bps):
        ocs_ref[bi, 0] = jnp.concatenate(o_cmp[bi * N_KV_HEADS:(bi + 1) * N_KV_HEADS], axis=1)
        ocs_ref[bi, 1] = jnp.concatenate(o_sel[bi * N_KV_HEADS:(bi + 1) * N_KV_HEADS], axis=1)


def _nsa_sample(page_flat, cache, q, nkv, w1r, pe_t, w2dup, kg_dup, e_perm, past_len, topk):
    nbatch, dec_t, _ = q.shape
    n_pages = past_len // PAGE_SIZE
    bps = SAMPLE_BATCH_PER_STEP
    nch = bps * N_KV_HEADS
    body = functools.partial(_nsa_sample_body, n_pages=n_pages, past_len=past_len, dec_t=dec_t, topk=topk)
    grid_spec = pltpu.PrefetchScalarGridSpec(
        num_scalar_prefetch=1, grid=(nbatch // bps,),
        in_specs=[pl.BlockSpec(memory_space=pl.ANY),
                  pl.BlockSpec((bps, dec_t, ATTN_WIDTH), lambda i, pt: (i, 0, 0)),
                  pl.BlockSpec((2 * KV_WIDTH, LANES), lambda i, pt: (0, i * bps * dec_t // LANES)),
                  pl.BlockSpec(w1r.shape, lambda i, pt: (0, 0, 0)),
                  pl.BlockSpec(pe_t.shape, lambda i, pt: (0, 0, 0, 0)),
                  pl.BlockSpec(w2dup.shape, lambda i, pt: (0, 0, 0)),
                  pl.BlockSpec(kg_dup.shape, lambda i, pt: (0, 0)),
                  pl.BlockSpec(e_perm.shape, lambda i, pt: (0, 0))],
        out_specs=pl.BlockSpec((bps, 2, dec_t, ATTN_WIDTH), lambda i, pt: (i, 0, 0, 0)),
        scratch_shapes=[pltpu.VMEM((2, bps, PAGE_ROWS // 2, n_pages, PAGE_SIZE), F32),
                        pltpu.VMEM((2, bps, n_pages * PAGE_ROWS // 2, PAGE_SIZE), F32),
                        pltpu.VMEM((bps * 4 * n_pages, HEAD_DIM * LANES), BF16),
                        pltpu.VMEM((nch, 2 * HEAD_DIM, past_len), BF16),
                        pltpu.VMEM((nch, 2 * HEAD_DIM, past_len), BF16),
                        pltpu.SemaphoreType.DMA((2,))])
    return pl.pallas_call(
        body, grid_spec=grid_spec,
        out_shape=jax.ShapeDtypeStruct((nbatch, 2, dec_t, ATTN_WIDTH), F32),
        compiler_params=pltpu.CompilerParams(dimension_semantics=("arbitrary",),
                                             vmem_limit_bytes=_vmem_limit(40 << 20)),
        name="nsa_sample",
    )(page_flat, cache, q, nkv, w1r, pe_t, w2dup, kg_dup, e_perm)


def _win_sample_body(win_ref, q_ref, nw_ref, ocs_ref, gates_ref, gexp_ref, oa_ref, wout_ref, *, past_len, dec_t):
    d = HEAD_DIM
    wbuf = win_ref.shape[2]
    nq = GROUP * dec_t
    rowq = lax.broadcasted_iota(jnp.int32, (nq, 1), 0)
    step = rowq & (dec_t - 1)
    kpos = past_len - wbuf + lax.broadcasted_iota(jnp.int32, (1, wbuf), 1)
    dpast = (past_len + step - kpos).astype(F32)
    okp = (dpast >= 0.0) & (dpast < WINDOW)
    nbat = win_ref.shape[0]
    lane = lax.broadcasted_iota(jnp.int32, (1, LANES), 1)
    per_tile = LANES // dec_t
    offs = [((pl.program_id(0) * nbat + bi) & (per_tile - 1)) * dec_t for bi in range(nbat)]
    dnews = [(step - (lane - off)).astype(F32) for off in offs]
    okns = [(lane >= off) & (lane < off + dec_t) & (dn >= 0.0) & (dn < WINDOW) for off, dn in zip(offs, dnews)]
    chains = [(bi, g) for bi in range(nbat) for g in range(N_KV_HEADS)]
    qst = [_stack_group_queries(q_ref[bi], g) for bi, g in chains]
    sp = [_dot(qst[ch], _twice(win_ref[bi, g * d:(g + 1) * d, :].astype(BF16))) for ch, (bi, g) in enumerate(chains)]
    nk = [_twice(nw_ref[g * d:(g + 1) * d, :].astype(BF16)) for g in range(N_KV_HEADS)]
    nvs = [_twice(nw_ref[KV_WIDTH + g * d:KV_WIDTH + (g + 1) * d, :].astype(BF16)) for g in range(N_KV_HEADS)]
    sn = [_dot(qst[ch], nk[g]) for ch, (bi, g) in enumerate(chains)]
    pp, pn, den = [], [], []
    for ch, (bi, g) in enumerate(chains):
        slope = _row_slopes(g, rowq >> _log2(dec_t))
        lgp = jnp.where(okp, sp[ch] - slope * dpast, NEG)
        lgn = jnp.where(okns[bi], sn[ch] - slope * dnews[bi], NEG)
        m = jnp.maximum(jnp.max(lgp, axis=-1, keepdims=True), jnp.max(lgn, axis=-1, keepdims=True))
        pp.append(jnp.exp(lgp - m))
        pn.append(jnp.exp(lgn - m))
        den.append(jnp.sum(pp[ch], axis=-1, keepdims=True) + jnp.sum(pn[ch], axis=-1, keepdims=True))
    o_win = []
    for ch, (bi, g) in enumerate(chains):
        vt_ = win_ref[bi, KV_WIDTH + g * d:KV_WIDTH + (g + 1) * d, :].astype(BF16)
        o2 = _dot_nt(pp[ch].astype(BF16), _twice(vt_)) + _dot_nt(pn[ch].astype(BF16), nvs[g])
        o_win.append(_unstack_group(o2 / den[ch], dec_t))
    for bi in range(nbat):
        ge = _expand_gates(gates_ref[bi], gexp_ref)
        oa_ref[bi] = (ge[0] * ocs_ref[bi, 0] + ge[1] * ocs_ref[bi, 1]
                      + ge[2] * jnp.concatenate(o_win[bi * N_KV_HEADS:(bi + 1) * N_KV_HEADS], axis=1))

        rolled = pltpu.roll(win_ref[bi], wbuf - dec_t, 1)
        newr = pltpu.roll(nw_ref[...], (LANES - dec_t - offs[bi]) & (LANES - 1), 1)
        wout_ref[bi, :, 0:wbuf - LANES] = rolled[:, 0:wbuf - LANES]
        wout_ref[bi, :, wbuf - LANES:wbuf] = jnp.where(lane >= LANES - dec_t, newr, rolled[:, wbuf - LANES:wbuf])


def _win_sample(win, q, nw, ocs, gates, gexp, past_len):
    nbatch, feat, wbuf = win.shape
    dec_t = q.shape[1]
    body = functools.partial(_win_sample_body, past_len=past_len, dec_t=dec_t)
    b3 = lambda i: (i, 0, 0)
    bps = WINDOW_BATCH_PER_STEP
    return pl.pallas_call(
        body, grid=(nbatch // bps,),
        in_specs=[pl.BlockSpec((bps, feat, wbuf), b3), pl.BlockSpec((bps, dec_t, ATTN_WIDTH), b3),
                  pl.BlockSpec((feat, LANES), lambda i: (1, i * bps * dec_t // LANES)),
                  pl.BlockSpec((bps, 2, dec_t, ATTN_WIDTH), lambda i: (i, 0, 0, 0)),
                  pl.BlockSpec((bps, dec_t, LANES), b3), pl.BlockSpec(gexp.shape, lambda i: (0, 0, 0))],
        out_specs=(pl.BlockSpec((bps, dec_t, ATTN_WIDTH), b3), pl.BlockSpec((bps, feat, wbuf), b3)),
        out_shape=(jax.ShapeDtypeStruct((nbatch, dec_t, ATTN_WIDTH), F32),
                   jax.ShapeDtypeStruct((nbatch, feat, wbuf), F32)),
        compiler_params=pltpu.CompilerParams(dimension_semantics=("parallel",),
                                             vmem_limit_bytes=_vmem_limit(24 << 20)),
        name="win_sample",
    )(win, q, nw, ocs, gates, gexp)


def _hgrn_sample_body(q_ref, f_ref, v_ref, s_ref, o_ref, so_ref, *, dec_t):
    nb = s_ref.shape[3]
    o_ref[...] = jnp.zeros(o_ref.shape, F32)

    sub = 8

    def per_tile(i, carry):
        r0 = pl.multiple_of(i * sub, sub)
        f_t = [f_ref[pl.ds(r0, sub), pl.ds(t * nb, nb)] for t in range(dec_t)]
        q_t = [q_ref[pl.ds(r0, sub), pl.ds(t * nb, nb)] for t in range(dec_t)]
        for j in range(sub):
            s = s_ref[0, r0 + j]
            for t in range(dec_t):
                cols = pl.ds(t * nb, nb)
                fr = f_t[t][j:j + 1, :]
                s = fr * s + (1.0 - fr) * v_ref[:, cols]
                o_ref[:, cols] = o_ref[:, cols] + s * q_t[t][j:j + 1, :]
            so_ref[0, r0 + j] = s
        return carry

    lax.fori_loop(0, s_ref.shape[1] // sub, per_tile, 0)


def _hgrn_sample(ztb, state, dec_t):
    nh, dk, dv, nb = state.shape
    n = ztb.shape[1]
    body = functools.partial(_hgrn_sample_body, dec_t=dec_t)
    q0, f0, v0 = 512 // dk, 1024 // dk, 1536 // dk
    return pl.pallas_call(
        body, grid=(nh,),
        in_specs=[pl.BlockSpec((dk, n), lambda h: (q0 + h, 0)), pl.BlockSpec((dk, n), lambda h: (f0 + h, 0)),
                  pl.BlockSpec((dv, n), lambda h: (v0 + h, 0)),
                  pl.BlockSpec((1, dk, dv, nb), lambda h: (h, 0, 0, 0))],
        out_specs=(pl.BlockSpec((dv, n), lambda h: (h, 0)), pl.BlockSpec((1, dk, dv, nb), lambda h: (h, 0, 0, 0))),
        out_shape=(jax.ShapeDtypeStruct((nh * dv, n), F32), jax.ShapeDtypeStruct(state.shape, F32)),
        compiler_params=pltpu.CompilerParams(dimension_semantics=("parallel",),
                                             vmem_limit_bytes=_vmem_limit(24 << 20)),
        name="hgrn_sample",
    )(ztb, ztb, ztb, state)


def _gate_expander():
    m = np.zeros((N_BRANCH, LANES, ATTN_WIDTH), np.float32)
    for br in range(N_BRANCH):
        for h in range(N_ATTN_HEADS):
            m[br, h * N_BRANCH + br, h * HEAD_DIM:(h + 1) * HEAD_DIM] = 1.0
    return jnp.asarray(m, BF16)


def _block_expander(block_ids, n_keys):
    key_blk = np.arange(n_keys) // SEL_BLOCK
    return jnp.asarray((np.asarray(block_ids)[:, None] == key_blk[None, :]).astype(np.float32), BF16)


def _key_features(n_keys):
    s = np.arange(n_keys)
    m = np.zeros((LANES, n_keys), np.float32)
    m[0:HEAD_DIM] = (np.arange(HEAD_DIM)[:, None] == (s // SEL_BLOCK)[None, :])
    m[HEAD_DIM] = s // SEL_BLOCK
    m[HEAD_DIM + 1] = s % SEL_BLOCK
    return jnp.asarray(m, BF16)


def _query_slope_features(rows):
    m = np.zeros((N_KV_HEADS, GROUP * rows, LANES), np.float32)
    for g in range(N_KV_HEADS):
        for r in range(GROUP):
            m[g, r * rows:(r + 1) * rows, HEAD_DIM] = SLOPES[g][r] * SEL_BLOCK
            m[g, r * rows:(r + 1) * rows, HEAD_DIM + 1] = SLOPES[g][r]
    return jnp.asarray(m, F32)


def _chunk_lower_tri(n, c):
    i = np.arange(n)
    return jnp.asarray(((i[:, None] // c == i[None, :] // c) & (i[None, :] <= i[:, None])).astype(np.float32), BF16)


def _head_indicator():
    m = np.zeros((RNN_WIDTH, LANES), np.float32)
    m[np.arange(RNN_WIDTH), np.arange(RNN_WIDTH) // RNN_DK] = 1.0
    return jnp.asarray(m, BF16)


def _head_block_diag(n):
    i = np.arange(n)
    return jnp.asarray((i[:, None] // RNN_DV == i[None, :] // RNN_DK).astype(np.float32), F32)


def kernel(x_prompt, x_sample, cache_kv, cache_win, state_rnn, page_table, ln_mix, w_in, q_norm, k_norm, cmp_pe,
           cmp_w1, cmp_w2, attn_out_norm, rnn_lb_logits, rnn_out_norm, w_out, ln_mlp, w_up, w_down):
    assert w_in.shape[0] == 1, "single layer"
    b, t, dm = x_prompt.shape
    nbatch, dec_t, _ = x_sample.shape
    n_pool = cache_kv.shape[1]
    n_pages = page_table.shape[1]
    past_len = n_pages * PAGE_SIZE
    wbuf = cache_win.shape[2]
    assert t % PROJ_ROWS == 0 and t % RNN_ROWS == 0 and t % min(SEL_KEYS, t) == 0 and t >= WINDOW
    assert (b * t) % FINISH_ROWS == 0 and (nbatch * dec_t) % FINISH_ROWS == 0 and w_up.shape[2] % FF_CHUNK == 0
    assert (nbatch * dec_t) % PROJ_ROWS == 0 and nbatch == LANES and dec_t <= 8
    assert past_len % SEL_BLOCK == 0 and wbuf == WINDOW and wbuf >= LANES
    assert LANES % dec_t == 0 and (LANES // dec_t) % max(SAMPLE_BATCH_PER_STEP, WINDOW_BATCH_PER_STEP) == 0

    w = w_in[0]
    c_kv, c_gate, c_rq, c_rf, c_ri, c_rg = ATTN_WIDTH, ATTN_WIDTH + 6 * KV_WIDTH, 1304, 1816, 2328, 2840
    gate_cols = jnp.pad(w[:, c_gate:c_rq], ((0, 0), (0, LANES - N_ATTN_HEADS * N_BRANCH)))
    wtok = jnp.concatenate([w[:, 0:ATTN_WIDTH], w[:, c_rq:], gate_cols, w[:, c_kv:c_kv + 2 * KV_WIDTH]],
                           axis=1).astype(BF16)
    wft = w[:, c_kv:c_gate].T.astype(BF16)
    wtok_s = jnp.concatenate([w[:, 0:ATTN_WIDTH], w[:, c_rg:], gate_cols], axis=1).astype(BF16)
    wtb_s = jnp.concatenate([w[:, c_kv:c_kv + 4 * KV_WIDTH], w[:, c_rq:c_rg]], axis=1).T.astype(BF16)
    wbt_s = w[:, c_kv + 2 * KV_WIDTH:c_gate].T.astype(BF16)
    ln = ln_mix[0][None, :]
    qg = (jnp.tile(q_norm[0], N_ATTN_HEADS) * SCALE)[None, :]
    lbl = rnn_lb_logits.astype(F32)
    lblt = jnp.broadcast_to(lbl[:, :, None], lbl.shape + (PROJ_ROWS,))
    gsel = jnp.broadcast_to(k_norm[0, 1][:, None], (HEAD_DIM, PROJ_ROWS))
    gwin = jnp.broadcast_to(k_norm[0, 2][:, None], (HEAD_DIM, PROJ_ROWS))
    kg_dup = jnp.tile(k_norm[0, 0], 2)[None, :]
    pe = cmp_pe[0]
    pe_tok = jnp.tile(jnp.concatenate([jnp.tile(pe[0], (1, N_KV_HEADS)), jnp.tile(pe[1], (1, N_KV_HEADS))], axis=1),
                      (PROJ_ROWS // CMP_BLOCK, 1))
    w1 = cmp_w1[0].reshape(2, CMP_BLOCK, HEAD_DIM, CMP_HIDDEN)
    zeros = jnp.zeros_like(w1)
    w1bd = jnp.concatenate([jnp.concatenate([w1, zeros], axis=3), jnp.concatenate([zeros, w1], axis=3)],
                           axis=2).astype(BF16)
    w1r = jnp.tile(w1.transpose(0, 2, 1, 3), (1, 1, 2, 1)).astype(BF16)
    w1r = w1r.reshape(2, HEAD_DIM * LANES, CMP_HIDDEN)
    pe_t = jnp.tile(pe.transpose(0, 2, 1), (1, 1, 2))[:, :, None, :]
    w2dup = jnp.tile(cmp_w2[0], (1, 1, 2)).astype(BF16)
    ag = attn_out_norm[0][None, :]
    rg = rnn_out_norm[0][None, :]
    wout = w_out[0].astype(BF16)
    lnm = ln_mlp[0][None, :]
    wup = w_up[0].astype(BF16)
    wdn = w_down[0].astype(BF16)
    gexp = _gate_expander()

    (q_p, gates_p, rq_p, f_p, rv_p, rgs_p, kvc_p, kvt_p, wint_p, att_p) = _proj_prompt(
        x_prompt, ln, wtok, wft, qg, lbl, pe_tok, gsel, gwin)
    cmp_p = _compress_prompt(kvc_p, w1bd, w2dup, kg_dup)
    nb_p = t // CMP_BLOCK
    oa_p = _nsa_prompt(q_p, gates_p, cmp_p, att_p, _key_features(t), _query_slope_features(NSA_Q_ROWS), gexp)
    orn_p, st_p = _hgrn_prompt(rq_p, f_p, rv_p, _chunk_lower_tri(RNN_ROWS, RNN_CHUNK),
                               _head_block_diag(LANES), _head_indicator())
    y_p = _finish(x_prompt.reshape(b * t, dm), oa_p.reshape(b * t, ATTN_WIDTH), orn_p.reshape(b * t, RNN_WIDTH),
                  rgs_p.reshape(b * t, RNN_WIDTH), ag, rg, wout, lnm, wup, wdn).reshape(b, t, dm)
    kv_prompt = kvt_p.reshape(1, b, 4, N_KV_HEADS, HEAD_DIM, t).transpose(0, 1, 5, 2, 3, 4)
    wlen = min(WINDOW, t)
    win_prompt = wint_p[:, :, t - wlen:].reshape(1, b, 2, N_KV_HEADS, HEAD_DIM, wlen).transpose(0, 1, 5, 2, 3, 4)
    hh = LANES // RNN_DV
    st5 = st_p.reshape(b, RNN_WIDTH // LANES, hh, RNN_DV, hh, RNN_DK)
    rnn_prompt = jnp.stack([st5[:, :, i, :, i, :] for i in range(hh)], axis=2)
    rnn_prompt = rnn_prompt.reshape(b, N_RNN_HEADS, RNN_DV, RNN_DK).transpose(0, 1, 3, 2)[None]

    n_s = nbatch * dec_t
    xbt = x_sample.reshape(n_s, dm)
    xtb = x_sample.transpose(1, 0, 2).reshape(n_s, dm)
    q_s, rgs_s, gates_s, ztb, zbt = _proj_sample(xbt, xtb, ln, wtok_s, wtb_s, wbt_s, qg, lblt, gsel, gwin)
    kv_sample = ztb[0:4 * KV_WIDTH].reshape(4, N_KV_HEADS, HEAD_DIM, dec_t, nbatch).transpose(4, 3, 0, 1, 2)[None]
    cache = cache_kv[0].transpose(0, 2, 3, 4, 1).reshape(n_pool, PAGE_ROWS, PAGE_SIZE)
    nblk_s = past_len // CMP_BLOCK
    rho = np.arange(nblk_s)
    e_perm = _block_expander(2 * (rho % n_pages) + rho // n_pages, past_len)
    ns_s = -(-(past_len + dec_t) // SEL_BLOCK)
    q_s3 = q_s.reshape(nbatch, dec_t, ATTN_WIDTH)
    ocs = _nsa_sample(page_table.reshape(-1), cache, q_s3, zbt, w1r, pe_t, w2dup, kg_dup,
                      e_perm, past_len, min(SEL_TOPK, ns_s))
    win = cache_win[0].transpose(0, 2, 3, 4, 1).reshape(nbatch, 2 * KV_WIDTH, wbuf)
    oa_s, win_new = _win_sample(win, q_s3, zbt, ocs, gates_s.reshape(nbatch, dec_t, LANES), gexp, past_len)
    win_sample = win_new.reshape(1, nbatch, 2, N_KV_HEADS, HEAD_DIM, wbuf).transpose(0, 1, 5, 2, 3, 4)
    state = state_rnn[0].transpose(1, 2, 3, 0)
    orn_t, state_new = _hgrn_sample(ztb, state, dec_t)
    rnn_sample = state_new.transpose(3, 0, 1, 2)[None]
    orn_s = orn_t.reshape(RNN_WIDTH, dec_t, nbatch).transpose(2, 1, 0).reshape(n_s, RNN_WIDTH)
    y_s = _finish(xbt, oa_s.reshape(n_s, ATTN_WIDTH), orn_s, rgs_s, ag, rg, wout, lnm, wup, wdn).reshape(nbatch, dec_t, dm)

    return (y_p, y_s, kv_prompt, kv_sample, win_prompt, win_sample, rnn_prompt, rnn_sample)
```

```python
import functools

import numpy as np
import jax
import jax.numpy as jnp
from jax import lax
from jax.experimental import pallas as pl
from jax.experimental.pallas import tpu as pltpu

F32 = jnp.float32
BF16 = jnp.bfloat16

HEAD_DIM = 64
N_ATTN_HEADS = 8
N_KV_HEADS = 2
GROUP = N_ATTN_HEADS // N_KV_HEADS
N_RNN_HEADS = 8
RNN_DK = 64
RNN_DV = 64
ATTN_WIDTH = N_ATTN_HEADS * HEAD_DIM
RNN_WIDTH = N_RNN_HEADS * RNN_DV
KV_WIDTH = N_KV_HEADS * HEAD_DIM
N_BRANCH = 3
CMP_BLOCK = 64
SEL_BLOCK = 64
SEL_TOPK = 16
WINDOW = 512
CMP_HIDDEN = 128
PAGE_SIZE = 128
SCALE = HEAD_DIM ** -0.5
EPS = 1e-6
NEG = -1e30
LOG2E = 1.4426950408889634
FORCED_SCORE = GROUP + 1.0
SLOPES = [[2.0 ** (-(g * GROUP + r + 1)) for r in range(GROUP)] for g in range(N_KV_HEADS)]

LANES = 128
VMEM_BYTES_V7X = 64 * 1024 * 1024

PROJ_ROWS = 256
NSA_Q_ROWS = 128
SEL_KEYS = 512
ROW_BLOCK = 16
RNN_ROWS = 256
RNN_CHUNK = 16
FF_CHUNK = 512
FINISH_ROWS = 1024

NT = (((1,), (1,)), ((), ()))
TN = (((0,), (0,)), ((), ()))


def _vmem_limit(nbytes):
    return int(min(VMEM_BYTES_V7X - (8 << 20), max(nbytes, 16 << 20)))


def _dot(a, b):
    return jnp.dot(a, b, preferred_element_type=F32)


def _dot_nt(a, b):
    return lax.dot_general(a, b, NT, preferred_element_type=F32)


def _dot_tn(a, b):
    return lax.dot_general(a, b, TN, preferred_element_type=F32)


def _low_half(shape):
    lane = lax.broadcasted_iota(jnp.int32, shape, len(shape) - 1)
    return (lane & HEAD_DIM) == 0


def _head_mean_sq(x):
    outs = []
    for j in range(x.shape[-1] // LANES):
        blk = x[:, j * LANES:(j + 1) * LANES]
        sq = blk * blk
        low = _low_half(blk.shape)
        s_lo = jnp.sum(jnp.where(low, sq, 0.0), axis=-1, keepdims=True)
        s_hi = jnp.sum(jnp.where(low, 0.0, sq), axis=-1, keepdims=True)
        outs.append(jnp.where(low, s_lo, s_hi))
    return jnp.concatenate(outs, axis=-1) * (1.0 / HEAD_DIM)


def _row_rms(x):
    return x * lax.rsqrt(jnp.mean(x * x, axis=-1, keepdims=True) + EPS)


def _col_head_norm(rows, gain):
    ms = jnp.mean(rows * rows, axis=0, keepdims=True)
    return rows * lax.rsqrt(ms + EPS) * gain


def _silu(z):
    return z * jax.nn.sigmoid(z)


def _lower_bound(logits, axis):
    m = jnp.max(logits, axis=axis, keepdims=True)
    e = jnp.exp(logits - m)
    lb = e / jnp.sum(e, axis=axis, keepdims=True)
    return lb[0:1] if axis == 0 else lb


def _stack_group_queries(q, g):
    rows = q.shape[0]
    low = _low_half((rows, LANES))
    zero = jnp.zeros((rows, LANES), q.dtype)
    pa = q[:, g * 2 * LANES:g * 2 * LANES + LANES]
    pb = q[:, g * 2 * LANES + LANES:(g + 1) * 2 * LANES]
    return jnp.concatenate([jnp.where(low, pa, zero), jnp.where(low, zero, pa),
                            jnp.where(low, pb, zero), jnp.where(low, zero, pb)], axis=0)


def _unstack_group(o2, rows):
    low = _low_half((rows, LANES))
    return jnp.concatenate([jnp.where(low, o2[0:rows], o2[rows:2 * rows]),
                            jnp.where(low, o2[2 * rows:3 * rows], o2[3 * rows:4 * rows])], axis=1)


def _log2(n):
    assert n > 0 and n & (n - 1) == 0, n
    return n.bit_length() - 1


def _twice(x):
    return jnp.concatenate([x, x], axis=0)


def _row_slopes(g, row_head):
    s = jnp.full(row_head.shape, SLOPES[g][GROUP - 1], F32)
    for r in range(GROUP - 2, -1, -1):
        s = jnp.where(row_head == r, SLOPES[g][r], s)
    return s


def _block_ranks(score, ids, cand):
    sub = 8
    nrow = score.shape[0]
    in_order = all(r == i for r, i in cand) and nrow % sub == 0
    rank = jnp.zeros(score.shape, F32)
    for row, id_i in cand:
        s_i = score[row:row + 1, :]
        if in_order:
            parts = []
            for v in range(nrow // sub):
                blk = score[v * sub:(v + 1) * sub, :]
                if (v + 1) * sub - 1 <= id_i:
                    parts.append((s_i > blk).astype(F32))
                elif v * sub > id_i:
                    parts.append((s_i >= blk).astype(F32))
                else:
                    parts.append(jnp.where(ids[v * sub:(v + 1) * sub, :] > id_i,
                                           (s_i >= blk).astype(F32), (s_i > blk).astype(F32)))
            rank = rank + jnp.concatenate(parts, axis=0)
        else:
            rank = rank + jnp.where(ids > id_i, (s_i >= score).astype(F32), (s_i > score).astype(F32))
    return rank


def _expand_gates(gates, gexp_ref):
    hi = gates.astype(BF16)
    lo = (gates - hi.astype(F32)).astype(BF16)
    return [_dot(hi, gexp_ref[br]) + _dot(lo, gexp_ref[br]) for br in range(N_BRANCH)]


TOK_Q, TOK_RQ, TOK_RF, TOK_RI, TOK_RG, TOK_GATE, TOK_KVC, TOK_END = 0, 512, 1024, 1536, 2048, 2560, 2688, 2944


def _proj_prompt_body(x_ref, ln_ref, wtok_ref, wft_ref, qg_ref, lbl_ref, pe_ref, gsel_ref, gwin_ref,
                      q_ref, gates_ref, rq_ref, f_ref, rv_ref, rgs_ref, kvc_ref, kvt_ref, wint_ref, att_ref):
    xb = (_row_rms(x_ref[0]) * ln_ref[...]).astype(BF16)

    def tok(lo, hi):
        return _dot(xb, wtok_ref[:, lo:hi])

    zq = tok(TOK_Q, TOK_RQ)
    q_ref[0] = (zq * lax.rsqrt(_head_mean_sq(zq) + EPS) * qg_ref[...]).astype(BF16)
    rq_ref[0] = _silu(tok(TOK_RQ, TOK_RF))
    lb = _lower_bound(lbl_ref[...], 0)
    f_ref[0] = lb + (1.0 - lb) * jax.nn.sigmoid(tok(TOK_RF, TOK_RI))
    rv_ref[0] = tok(TOK_RI, TOK_RG)
    rgs_ref[0] = _silu(tok(TOK_RG, TOK_GATE))
    gates_ref[0] = jax.nn.sigmoid(tok(TOK_GATE, TOK_KVC))
    zc = tok(TOK_KVC, TOK_END) + pe_ref[...]
    kvc_ref[0, 0] = zc[:, 0:LANES]
    kvc_ref[1, 0] = zc[:, LANES:2 * LANES]

    zf = _dot_nt(wft_ref[...], xb)
    d = HEAD_DIM
    ksel = jnp.concatenate([_col_head_norm(zf[256 + g * d:256 + (g + 1) * d], gsel_ref[...])
                            for g in range(N_KV_HEADS)], axis=0)
    kwin = jnp.concatenate([_col_head_norm(zf[512 + g * d:512 + (g + 1) * d], gwin_ref[...])
                            for g in range(N_KV_HEADS)], axis=0)
    vsel = zf[384:512]
    vwin = zf[640:768]
    kvt_ref[0, 0:256] = zf[0:256]
    kvt_ref[0, 256:384] = ksel
    kvt_ref[0, 384:512] = vsel
    wint_ref[0, 0:128] = kwin
    wint_ref[0, 128:256] = vwin
    att_ref[0, 0:128] = ksel.astype(BF16)
    att_ref[0, 128:256] = vsel.astype(BF16)
    att_ref[0, 256:384] = kwin.astype(BF16)
    att_ref[0, 384:512] = vwin.astype(BF16)


def _proj_prompt(x, ln, wtok, wft, qg, lbl, pe_tok, gsel, gwin):
    b, t, dm = x.shape
    tm = PROJ_ROWS
    grid = (b, t // tm)
    row = lambda i, j: (i, j, 0)
    col = lambda i, j: (i, 0, j)
    const2 = lambda i, j: (0, 0)
    out_shape = (
        jax.ShapeDtypeStruct((b, t, ATTN_WIDTH), BF16),
        jax.ShapeDtypeStruct((b, t, LANES), F32),
        jax.ShapeDtypeStruct((b, t, RNN_WIDTH), F32),
        jax.ShapeDtypeStruct((b, t, RNN_WIDTH), F32),
        jax.ShapeDtypeStruct((b, t, RNN_WIDTH), F32),
        jax.ShapeDtypeStruct((b, t, RNN_WIDTH), F32),
        jax.ShapeDtypeStruct((2, b, t, LANES), F32),
        jax.ShapeDtypeStruct((b, 4 * KV_WIDTH, t), F32),
        jax.ShapeDtypeStruct((b, 2 * KV_WIDTH, t), F32),
        jax.ShapeDtypeStruct((b, 4 * KV_WIDTH, t), BF16),
    )
    out_specs = (
        pl.BlockSpec((1, tm, ATTN_WIDTH), row), pl.BlockSpec((1, tm, LANES), row),
        pl.BlockSpec((1, tm, RNN_WIDTH), row), pl.BlockSpec((1, tm, RNN_WIDTH), row),
        pl.BlockSpec((1, tm, RNN_WIDTH), row), pl.BlockSpec((1, tm, RNN_WIDTH), row),
        pl.BlockSpec((2, 1, tm, LANES), lambda i, j: (0, i, j, 0)),
        pl.BlockSpec((1, 4 * KV_WIDTH, tm), col), pl.BlockSpec((1, 2 * KV_WIDTH, tm), col),
        pl.BlockSpec((1, 4 * KV_WIDTH, tm), col),
    )
    in_specs = [
        pl.BlockSpec((1, tm, dm), row), pl.BlockSpec(ln.shape, const2),
        pl.BlockSpec(wtok.shape, const2), pl.BlockSpec(wft.shape, const2),
        pl.BlockSpec(qg.shape, const2), pl.BlockSpec(lbl.shape, const2), pl.BlockSpec(pe_tok.shape, const2),
        pl.BlockSpec(gsel.shape, const2), pl.BlockSpec(gwin.shape, const2),
    ]
    return pl.pallas_call(
        _proj_prompt_body, grid=grid, in_specs=in_specs, out_specs=out_specs, out_shape=out_shape,
        compiler_params=pltpu.CompilerParams(dimension_semantics=("parallel", "parallel"),
                                             vmem_limit_bytes=_vmem_limit(48 << 20)),
        name="proj_prompt",
    )(x, ln, wtok, wft, qg, lbl, pe_tok, gsel, gwin)


def _compress_prompt_body(x_ref, w1_ref, w2_ref, kg_ref, out_ref):
    c = pl.program_id(0)
    nb = out_ref.shape[2]
    acc = jnp.zeros((nb, 2 * CMP_HIDDEN), F32)
    for pos in range(CMP_BLOCK):
        xp = x_ref[0, 0, pl.ds(pos, nb, stride=CMP_BLOCK), :]
        acc = acc + _dot(xp.astype(BF16), w1_ref[0, pos])
    hb = _silu(acc).astype(BF16)
    outs = []
    for g in range(N_KV_HEADS):
        y = _dot(hb[:, g * CMP_HIDDEN:(g + 1) * CMP_HIDDEN], w2_ref[0])
        yn = _row_rms(y) * kg_ref[...]
        outs.append(jnp.where(c == 0, yn, y))
    out_ref[0, 0] = jnp.concatenate(outs, axis=1)


def _compress_prompt(kvc, w1bd, w2dup, kg_dup):
    _, b, t, _ = kvc.shape
    nb = t // CMP_BLOCK
    return pl.pallas_call(
        _compress_prompt_body, grid=(2, b),
        in_specs=[pl.BlockSpec((1, 1, t, LANES), lambda c, i: (c, i, 0, 0)),
                  pl.BlockSpec((1,) + w1bd.shape[1:], lambda c, i: (c, 0, 0, 0)),
                  pl.BlockSpec((1,) + w2dup.shape[1:], lambda c, i: (c, 0, 0)),
                  pl.BlockSpec(kg_dup.shape, lambda c, i: (0, 0))],
        out_specs=pl.BlockSpec((1, 1, nb, 2 * LANES), lambda c, i: (c, i, 0, 0)),
        out_shape=jax.ShapeDtypeStruct((2, b, nb, 2 * LANES), F32),
        compiler_params=pltpu.CompilerParams(dimension_semantics=("arbitrary", "arbitrary"),
                                             vmem_limit_bytes=_vmem_limit(32 << 20)),
        name="compress_prompt",
    )(kvc, w1bd, w2dup, kg_dup)


def _row_max_update(sc_scr, nm_scr, m_scr, mn_scr, ncol, tq):
    rb = ROW_BLOCK
    for i in range(GROUP * tq // rb):
        rows = slice(i * rb, (i + 1) * rb)
        qrows = slice((i * rb) % tq, (i * rb) % tq + rb)
        mx = jnp.full((rb, LANES), NEG, F32)
        for j in range(ncol):
            cols = slice(j * LANES, (j + 1) * LANES)
            v = sc_scr[rows, cols]
            if nm_scr is not None:
                v = v + nm_scr[qrows, cols]
                sc_scr[rows, cols] = v
            mx = jnp.maximum(mx, v)
        mn_scr[rows, :] = jnp.maximum(m_scr[rows, :], jnp.max(mx, axis=-1, keepdims=True))


def _softmax_update(sc_scr, p_scr, m_scr, mn_scr, l_scr, a_scr, ncol, tq):
    rb = ROW_BLOCK
    for i in range(GROUP * tq // rb):
        rows = slice(i * rb, (i + 1) * rb)
        mn = mn_scr[rows, :]
        tot = jnp.zeros((rb, LANES), F32)
        for j in range(ncol):
            cols = slice(j * LANES, (j + 1) * LANES)
            p = jnp.exp(sc_scr[rows, cols] - mn)
            tot = tot + p
            p_scr[rows, cols] = p.astype(BF16)
        alpha = jnp.exp(m_scr[rows, :] - mn)
        l_scr[rows, :] = alpha * l_scr[rows, :] + jnp.sum(tot, axis=-1, keepdims=True)
        a_scr[rows, :] = alpha
        m_scr[rows, :] = mn


def _nsa_prompt_body(q_ref, gates_ref, cmp_ref, att_ref, kfeat_ref, qfeat_ref, gexp_ref, oa_ref,
                     sc_scr, sc2_scr, sc3_scr, nm_scr, p_scr, m_scr, mn_scr, l_scr, a_scr, acc_scr, *, seq, topk):
    tq = NSA_Q_ROWS
    tk = min(SEL_KEYS, seq)
    nb = cmp_ref.shape[2]
    t0 = pl.program_id(1) * tq
    wk = min(WINDOW + tq, seq)
    ws = pl.multiple_of(jnp.clip(t0 - WINDOW, 0, seq - wk), LANES)
    q = q_ref[0]
    d = HEAD_DIM
    groups = range(N_KV_HEADS)
    def reset(stats):
        for g in groups:
            m_scr[g] = jnp.full(m_scr.shape[1:], NEG, F32)
            for ref in stats:
                ref[g] = jnp.zeros(ref.shape[1:], F32)

    sc_bufs = (sc_scr, sc2_scr, sc3_scr)

    def logits(buf, g, qa, kt_, kf, width):
        sc_bufs[buf][g, :, 0:width] = _dot(qa, jnp.concatenate([kt_, kt_, kf], axis=0))

    def softmax_tiles(buf, ncol, masked):
        sc = sc_bufs[buf]
        for g in groups:
            _row_max_update(sc.at[g], nm_scr if masked else None, m_scr.at[g], mn_scr.at[g], ncol, tq)
        for g in groups:
            _softmax_update(sc.at[g], p_scr.at[g], m_scr.at[g], mn_scr.at[g], l_scr.at[g], a_scr.at[g], ncol, tq)

    qst = [_stack_group_queries(q, g) for g in groups]
    qwin = [jnp.concatenate([qst[g], qfeat_ref[g].astype(BF16)], axis=1) for g in groups]

    kcs = [cmp_ref[0, 0][:, g * LANES:(g + 1) * LANES].astype(BF16) for g in groups]
    vcs = [cmp_ref[1, 0][:, g * LANES:(g + 1) * LANES].astype(BF16) for g in groups]
    st = [_dot_nt(kcs[g], qst[g]) for g in groups]

    reset((l_scr,))
    kfw = kfeat_ref[:, pl.ds(ws, wk)]
    ddw = (t0 - ws) + lax.broadcasted_iota(jnp.int32, (tq, wk), 0) - lax.broadcasted_iota(jnp.int32, (tq, wk), 1)
    nm_scr[:, 0:wk] = jnp.where((ddw >= 0) & (ddw < WINDOW), 0.0, NEG)
    for g in groups:
        logits(0, g, qwin[g], att_ref[0, 2 * KV_WIDTH + g * d:2 * KV_WIDTH + (g + 1) * d, pl.ds(ws, wk)], kfw, wk)

    blk = lax.broadcasted_iota(jnp.int32, (nb, GROUP * tq), 0)
    colq = lax.broadcasted_iota(jnp.int32, (1, GROUP * tq), 1)
    dist = (t0 + (colq & (tq - 1)) - (blk * CMP_BLOCK + (CMP_BLOCK - 1))).astype(F32)
    valid = dist >= 0.0
    ps = []
    for g in groups:
        s = jnp.where(valid, st[g] - _row_slopes(g, colq >> _log2(tq)) * dist, NEG)
        e = jnp.exp(s - jnp.max(s, axis=0, keepdims=True))
        ps.append(jnp.where(valid, e / jnp.sum(e, axis=0, keepdims=True), 0.0))
    o_cmp = [_unstack_group(_dot_tn(ps[g].astype(BF16), vcs[g]), tq) for g in groups]

    softmax_tiles(0, wk // LANES, True)
    o_win = []
    for g in groups:
        vw = att_ref[0, 3 * KV_WIDTH + g * d:3 * KV_WIDTH + (g + 1) * d, pl.ds(ws, wk)]
        o_win.append(_unstack_group(_dot_nt(p_scr[g, :, 0:wk], _twice(vw)) / l_scr[g], tq))

    bj = lax.broadcasted_iota(jnp.int32, (nb, tq), 0)
    cur = (t0 + lax.broadcasted_iota(jnp.int32, (nb, tq), 1)) >> _log2(SEL_BLOCK)
    force = (bj == 0) | (bj == cur)
    qsel = []
    for g in groups:
        p = ps[g]
        imp = p[:, 0:tq] + p[:, tq:2 * tq] + p[:, 2 * tq:3 * tq] + p[:, 3 * tq:4 * tq]
        score = jnp.where(bj <= cur, jnp.where(force, FORCED_SCORE, imp), -1.0)
        rank = _block_ranks(score, bj, [(i, i) for i in range(nb)])
        mneg = jnp.where((rank < topk) & (score >= 0.0), 0.0, NEG)
        mtok = jnp.concatenate([mneg, jnp.zeros((LANES - nb, tq), F32)], axis=0).T
        qsel.append(jnp.concatenate(
            [qst[g], (qfeat_ref[g] + jnp.concatenate([mtok] * GROUP, axis=0)).astype(BF16)], axis=1))

    reset((l_scr, acc_scr))
    n_kt = (t0 + tq - 1) // tk + 1

    def sel_logits(kt, buf):
        s0 = pl.multiple_of(kt * tk, tk)
        kf = kfeat_ref[:, pl.ds(s0, tk)]
        for g in groups:
            logits(buf, g, qsel[g], att_ref[0, g * d:(g + 1) * d, pl.ds(s0, tk)], kf, tk)

    def sel_update(s0, buf, width, masked):
        softmax_tiles(buf, width // LANES, masked)
        for g in groups:
            vt_ = att_ref[0, KV_WIDTH + g * d:KV_WIDTH + (g + 1) * d, pl.ds(s0, width)]
            acc_scr[g] = a_scr[g] * acc_scr[g] + _dot_nt(p_scr[g, :, 0:width], _twice(vt_))

    last = n_kt - 1
    sel_logits(last, 2)
    sel_logits(0, 0)
    s_last = pl.multiple_of(last * tk, tk)
    for v in range(tk // tq):
        @pl.when(t0 - s_last == v * tq)
        def _(v=v):
            w = (v + 1) * tq
            dd = v * tq + lax.broadcasted_iota(jnp.int32, (tq, w), 0) - lax.broadcasted_iota(jnp.int32, (tq, w), 1)
            nm_scr[:, 0:w] = jnp.where(dd >= 0, 0.0, NEG)
            sel_update(s_last, 2, w, True)

    def past_tile(kt, prefetch):
        for par in (0, 1):
            @pl.when((kt & 1) == par)
            def _(par=par):
                if prefetch:
                    sel_logits(kt + 1, 1 - par)
                sel_update(pl.multiple_of(kt * tk, tk), par, tk, False)

    def full_tile(kt, carry):
        past_tile(kt, True)
        return carry

    lax.fori_loop(0, last - 1, full_tile, 0)

    @pl.when(last >= 1)
    def _():
        past_tile(last - 1, False)

    o_sel = [_unstack_group(acc_scr[g] / l_scr[g], tq) for g in groups]

    ge = _expand_gates(gates_ref[0], gexp_ref)
    oa_ref[0] = (ge[0] * jnp.concatenate(o_cmp, axis=1) + ge[1] * jnp.concatenate(o_sel, axis=1)
                 + ge[2] * jnp.concatenate(o_win, axis=1))


def _nsa_prompt(q, gates, cmpkv, att, k_feat, q_feat, gexp):
    b, t, _ = q.shape
    tq = NSA_Q_ROWS
    nb = t // CMP_BLOCK
    assert nb <= HEAD_DIM, "the block mask uses 64 feature lanes"
    cw = max(min(SEL_KEYS, t), min(WINDOW + tq, t))
    body = functools.partial(_nsa_prompt_body, seq=t, topk=min(SEL_TOPK, nb))
    ng = N_KV_HEADS
    stat = pltpu.VMEM((ng, GROUP * tq, LANES), F32)
    return pl.pallas_call(
        body, grid=(b, t // tq),
        in_specs=[pl.BlockSpec((1, tq, ATTN_WIDTH), lambda i, j: (i, j, 0)),
                  pl.BlockSpec((1, tq, LANES), lambda i, j: (i, j, 0)),
                  pl.BlockSpec((2, 1, nb, 2 * LANES), lambda i, j: (0, i, 0, 0)),
                  pl.BlockSpec((1, 4 * KV_WIDTH, t), lambda i, j: (i, 0, 0)),
                  pl.BlockSpec(k_feat.shape, lambda i, j: (0, 0)),
                  pl.BlockSpec(q_feat.shape, lambda i, j: (0, 0, 0)),
                  pl.BlockSpec(gexp.shape, lambda i, j: (0, 0, 0))],
        out_specs=pl.BlockSpec((1, tq, ATTN_WIDTH), lambda i, j: (i, j, 0)),
        out_shape=jax.ShapeDtypeStruct((b, t, ATTN_WIDTH), F32),
        scratch_shapes=[pltpu.VMEM((ng, GROUP * tq, cw), F32), pltpu.VMEM((ng, GROUP * tq, cw), F32),
                        pltpu.VMEM((ng, GROUP * tq, cw), F32), pltpu.VMEM((tq, cw), F32),
                        pltpu.VMEM((ng, GROUP * tq, cw), BF16), stat, stat, stat, stat, stat],
        compiler_params=pltpu.CompilerParams(dimension_semantics=("parallel", "parallel"),
                                             vmem_limit_bytes=_vmem_limit(48 << 20)),
        name="nsa_prompt",
    )(q, gates, cmpkv, att, k_feat, q_feat, gexp)


def _split3(x):
    hi = x.astype(BF16)
    r1 = x - hi.astype(F32)
    mid = r1.astype(BF16)
    lo = (r1 - mid.astype(F32)).astype(BF16)
    return hi, mid, lo


def _hgrn_prompt_body(rq_ref, f_ref, rv_ref, ltri_ref, bd_ref, ind_ref, indt_ref, o_ref, st_ref,
                      s_scr, cum_scr, k_scr, prod_scr):
    c16 = RNN_CHUNK
    nbat = rq_ref.shape[0]

    @pl.when(pl.program_id(0) == 0)
    def _():
        s_scr[...] = jnp.zeros(s_scr.shape, F32)

    ltri = ltri_ref[...]
    for bi in range(nbat):
        f = f_ref[bi]
        hi, mid, lo = _split3(jnp.log(f))
        cum = (_dot(ltri, hi) + _dot(ltri, mid) + _dot(ltri, lo)) * LOG2E
        cum_scr[bi] = cum
        k_scr[bi] = cum - jnp.log2(1.0 - f)

    si = lax.broadcasted_iota(jnp.int32, (c16, c16, RNN_WIDTH), 0)
    ti = lax.broadcasted_iota(jnp.int32, (c16, c16, RNN_WIDTH), 1)
    causal = si <= ti
    npair = RNN_WIDTH // LANES

    def pair_products(c, buf, bi):
        r0 = pl.multiple_of(c * c16, c16)
        cc = cum_scr[bi, pl.ds(r0, c16), :]
        qc = rq_ref[bi, pl.ds(r0, c16), :]
        lk = k_scr[bi, pl.ds(r0, c16), :]
        dec = jnp.exp2(jnp.where(causal, cc[None, :, :] - lk[:, None, :], NEG))
        prod_scr[buf, bi] = (qc[None, :, :] * dec).reshape(c16 * c16, RNN_WIDTH).astype(BF16)

    def recurrence(c, buf, nxt):
        r0 = pl.multiple_of(c * c16, c16)
        rows = range(nbat)
        a1 = [_dot(prod_scr[buf, bi], ind_ref[...]) for bi in rows]
        if nxt is not None:
            for bi in rows:
                pair_products(nxt, 1 - buf, bi)
        cc = [cum_scr[bi, pl.ds(r0, c16), :] for bi in rows]
        last = [x[c16 - 1:c16, :] for x in cc]
        o_int = []
        for bi in rows:
            qd = (rq_ref[bi, pl.ds(r0, c16), :] * jnp.exp2(cc[bi])).astype(BF16)
            o_int.append(jnp.concatenate([_dot_nt(qd[:, p * LANES:(p + 1) * LANES], s_scr[bi, p].astype(BF16))
                                          for p in range(npair)], axis=1))
        a2 = [_dot(a1[bi].astype(BF16), indt_ref[...]).reshape(c16, c16, RNN_WIDTH) for bi in rows]
        for bi in rows:
            kd = jnp.exp2(last[bi] - k_scr[bi, pl.ds(r0, c16), :]).astype(BF16)
            vb = rv_ref[bi, pl.ds(r0, c16), :].astype(BF16)
            dl = jnp.exp2(last[bi])
            for p in range(npair):
                u = _dot_tn(vb[:, p * LANES:(p + 1) * LANES], kd[:, p * LANES:(p + 1) * LANES])
                s_scr[bi, p] = s_scr[bi, p] * dl[:, p * LANES:(p + 1) * LANES] + u * bd_ref[...]
        for bi in rows:
            vc = rv_ref[bi, pl.ds(r0, c16), :]
            o_ref[bi, pl.ds(r0, c16), :] = o_int[bi] + jnp.sum(a2[bi] * vc[:, None, :], axis=0)

    nchunk = rq_ref.shape[1] // c16
    assert nchunk % 2 == 0 and nchunk >= 4
    for bi in range(nbat):
        pair_products(0, 0, bi)

    def two_chunks(j, carry):
        for sub in (0, 1):
            recurrence(2 * j + sub, sub, 2 * j + sub + 1)
        return carry

    lax.fori_loop(0, nchunk // 2 - 1, two_chunks, 0)
    recurrence(nchunk - 2, 0, nchunk - 1)
    recurrence(nchunk - 1, 1, None)
    st_ref[...] = s_scr[...]


def _hgrn_prompt(rq, f, rv, ltri, bd, ind):
    b, t, w = rq.shape
    tc = RNN_ROWS
    npair = w // LANES
    row = lambda j: (0, j, 0)
    const = lambda j: (0, 0)
    indt = ind.T
    return pl.pallas_call(
        _hgrn_prompt_body, grid=(t // tc,),
        in_specs=[pl.BlockSpec((b, tc, w), row), pl.BlockSpec((b, tc, w), row), pl.BlockSpec((b, tc, w), row),
                  pl.BlockSpec(ltri.shape, const), pl.BlockSpec(bd.shape, const),
                  pl.BlockSpec(ind.shape, const), pl.BlockSpec(indt.shape, const)],
        out_specs=(pl.BlockSpec((b, tc, w), row), pl.BlockSpec((b, npair, LANES, LANES), lambda j: (0, 0, 0, 0))),
        out_shape=(jax.ShapeDtypeStruct((b, t, w), F32), jax.ShapeDtypeStruct((b, npair, LANES, LANES), F32)),
        scratch_shapes=[pltpu.VMEM((b, npair, LANES, LANES), F32), pltpu.VMEM((b, tc, w), F32),
                        pltpu.VMEM((b, tc, w), F32), pltpu.VMEM((2, b, RNN_CHUNK * RNN_CHUNK, w), BF16)],
        compiler_params=pltpu.CompilerParams(dimension_semantics=("arbitrary",),
                                             vmem_limit_bytes=_vmem_limit(40 << 20)),
        name="hgrn_prompt",
    )(rq, f, rv, ltri, bd, ind, indt)


def _finish_body(x_ref, oa_ref, orn_ref, rgs_ref, ag_ref, rg_ref, wout_ref, lnm_ref, wup_ref, wdn_ref, y_ref, hn_scr):
    @pl.when(pl.program_id(1) == 0)
    def _():
        oa = oa_ref[...]
        orn = orn_ref[...]
        a_n = oa * lax.rsqrt(_head_mean_sq(oa) + EPS) * ag_ref[...]
        r_n = orn * lax.rsqrt(_head_mean_sq(orn) + EPS) * rg_ref[...] * rgs_ref[...]
        h = (x_ref[...] + _dot(a_n.astype(BF16), wout_ref[0:ATTN_WIDTH, :])
             + _dot(r_n.astype(BF16), wout_ref[ATTN_WIDTH:ATTN_WIDTH + RNN_WIDTH, :]))
        y_ref[...] = h
        hn_scr[...] = (_row_rms(h) * lnm_ref[...]).astype(BF16)

    u = jnp.maximum(_dot(hn_scr[...], wup_ref[...]), 0.0)
    y_ref[...] += _dot((u * u).astype(BF16), wdn_ref[...])


def _finish(x, oa, orn, rgs, ag, rg, wout, lnm, wup, wdn):
    n, dm = x.shape
    tm = FINISH_ROWS
    dff = wup.shape[1]
    row = lambda i, j: (i, 0)
    const = lambda i, j: (0, 0)
    return pl.pallas_call(
        _finish_body, grid=(n // tm, dff // FF_CHUNK),
        in_specs=[pl.BlockSpec((tm, dm), row), pl.BlockSpec((tm, ATTN_WIDTH), row),
                  pl.BlockSpec((tm, RNN_WIDTH), row), pl.BlockSpec((tm, RNN_WIDTH), row),
                  pl.BlockSpec(ag.shape, const), pl.BlockSpec(rg.shape, const),
                  pl.BlockSpec(wout.shape, const), pl.BlockSpec(lnm.shape, const),
                  pl.BlockSpec((dm, FF_CHUNK), lambda i, j: (0, j)),
                  pl.BlockSpec((FF_CHUNK, dm), lambda i, j: (j, 0))],
        out_specs=pl.BlockSpec((tm, dm), row),
        out_shape=jax.ShapeDtypeStruct((n, dm), F32),
        scratch_shapes=[pltpu.VMEM((tm, dm), BF16)],
        compiler_params=pltpu.CompilerParams(dimension_semantics=("parallel", "arbitrary"),
                                             vmem_limit_bytes=_vmem_limit(48 << 20)),
        name="finish",
    )(x, oa, orn, rgs, ag, rg, wout, lnm, wup, wdn)


def _proj_sample_body(xbt_ref, xtb_ref, ln_ref, wtok_ref, wtb_ref, wbt_ref, qg_ref, lbl_ref, gsel_ref, gwin_ref,
                      q_ref, rgs_ref, gates_ref, ztb_ref, zbt_ref):
    xb = (_row_rms(xbt_ref[...]) * ln_ref[...]).astype(BF16)
    xt = (_row_rms(xtb_ref[...]) * ln_ref[...]).astype(BF16)
    zq = _dot(xb, wtok_ref[:, 0:ATTN_WIDTH])
    q_ref[...] = (zq * lax.rsqrt(_head_mean_sq(zq) + EPS) * qg_ref[...]).astype(BF16)
    rgs_ref[...] = _silu(_dot(xb, wtok_ref[:, ATTN_WIDTH:ATTN_WIDTH + RNN_WIDTH]))
    gates_ref[...] = jax.nn.sigmoid(_dot(xb, wtok_ref[:, ATTN_WIDTH + RNN_WIDTH:ATTN_WIDTH + RNN_WIDTH + LANES]))

    d = HEAD_DIM
    zt = _dot_nt(wtb_ref[...], xt)
    ztb_ref[0:256] = zt[0:256]
    for g in range(N_KV_HEADS):
        ztb_ref[256 + g * d:256 + (g + 1) * d] = _col_head_norm(zt[256 + g * d:256 + (g + 1) * d], gsel_ref[...])
    ztb_ref[384:512] = zt[384:512]
    ztb_ref[512:1024] = _silu(zt[512:1024])
    lb = _lower_bound(lbl_ref[...], 0)[0]
    ztb_ref[1024:1536] = lb + (1.0 - lb) * jax.nn.sigmoid(zt[1024:1536])
    ztb_ref[1536:2048] = zt[1536:2048]

    zb = _dot_nt(wbt_ref[...], xb)
    for g in range(N_KV_HEADS):
        zbt_ref[g * d:(g + 1) * d] = _col_head_norm(zb[g * d:(g + 1) * d], gsel_ref[...])
        zbt_ref[256 + g * d:256 + (g + 1) * d] = _col_head_norm(zb[256 + g * d:256 + (g + 1) * d], gwin_ref[...])
    zbt_ref[128:256] = zb[128:256]
    zbt_ref[384:512] = zb[384:512]


def _proj_sample(xbt, xtb, ln, wtok, wtb, wbt, qg, lblt, gsel, gwin):
    n, dm = xbt.shape
    tm = PROJ_ROWS
    row = lambda i: (i, 0)
    col = lambda i: (0, i)
    const = lambda i: (0, 0)
    return pl.pallas_call(
        _proj_sample_body, grid=(n // tm,),
        in_specs=[pl.BlockSpec((tm, dm), row), pl.BlockSpec((tm, dm), row), pl.BlockSpec(ln.shape, const),
                  pl.BlockSpec(wtok.shape, const), pl.BlockSpec(wtb.shape, const), pl.BlockSpec(wbt.shape, const),
                  pl.BlockSpec(qg.shape, const), pl.BlockSpec(lblt.shape, lambda i: (0, 0, 0)),
                  pl.BlockSpec(gsel.shape, const), pl.BlockSpec(gwin.shape, const)],
        out_specs=(pl.BlockSpec((tm, ATTN_WIDTH), row), pl.BlockSpec((tm, RNN_WIDTH), row),
                   pl.BlockSpec((tm, LANES), row), pl.BlockSpec((wtb.shape[0], tm), col),
                   pl.BlockSpec((wbt.shape[0], tm), col)),
        out_shape=(jax.ShapeDtypeStruct((n, ATTN_WIDTH), BF16), jax.ShapeDtypeStruct((n, RNN_WIDTH), F32),
                   jax.ShapeDtypeStruct((n, LANES), F32), jax.ShapeDtypeStruct((wtb.shape[0], n), F32),
                   jax.ShapeDtypeStruct((wbt.shape[0], n), F32)),
        compiler_params=pltpu.CompilerParams(dimension_semantics=("parallel",),
                                             vmem_limit_bytes=_vmem_limit(40 << 20)),
        name="proj_sample",
    )(xbt, xtb, ln, wtok, wtb, wbt, qg, lblt, gsel, gwin)


PAGE_ROWS = 4 * KV_WIDTH
SAMPLE_BATCH_PER_STEP = 2
WINDOW_BATCH_PER_STEP = 4


def _nsa_sample_body(pt_ref, cache_ref, q_ref, nkv_ref, w1_ref, pe_ref, w2_ref, kg_ref, e_ref,
                     ocs_ref, cbuf, sbuf, lhs_scr, kk_scr, vv_scr, sem, *, n_pages, past_len, dec_t, topk):
    step_id = pl.program_id(0)
    nsteps = pl.num_programs(0)
    slot = step_id % 2
    d = HEAD_DIM
    nblk = 2 * n_pages
    half_rows = PAGE_ROWS // 2
    bps = q_ref.shape[0]

    def page_copies(st, sl, bi, j):
        pg = pt_ref[(st * bps + bi) * n_pages + j]
        return (pltpu.make_async_copy(cache_ref.at[pg, pl.ds(0, half_rows)], cbuf.at[sl, bi, :, j, :], sem.at[sl]),
                pltpu.make_async_copy(cache_ref.at[pg, pl.ds(half_rows, half_rows)],
                                      sbuf.at[sl, bi, pl.ds(j * half_rows, half_rows)], sem.at[sl]))

    def all_copies(st, sl):
        return [cp for bi in range(bps) for j in range(n_pages) for cp in page_copies(st, sl, bi, j)]

    @pl.when(step_id == 0)
    def _():
        for cp in all_copies(0, 0):
            cp.start()

    @pl.when(step_id + 1 < nsteps)
    def _():
        for cp in all_copies(step_id + 1, 1 - slot):
            cp.start()

    for cp in all_copies(step_id, slot):
        cp.wait()

    low = _low_half((n_pages, LANES))
    rows_b = 4 * n_pages

    def compress(c):
        for dd in range(d):
            rows = []
            for bi in range(bps):
                for g in range(N_KV_HEADS):
                    xg = cbuf[slot, bi, (c * N_KV_HEADS + g) * d + dd]
                    xg = xg + pe_ref[c, dd]
                    rows += [jnp.where(low, xg, 0.0), jnp.where(low, 0.0, xg)]
            lhs_scr[:, dd * LANES:(dd + 1) * LANES] = jnp.concatenate(rows, axis=0).astype(BF16)
        acc = _dot(lhs_scr[...], w1_ref[c])
        return _dot(_silu(acc).astype(BF16), w2_ref[c])

    kc_all = _row_rms(compress(0)) * kg_ref[...]
    vc_all = compress(1)

    nq = GROUP * dec_t
    rho = lax.broadcasted_iota(jnp.int32, (nblk, dec_t), 0)
    bid = 2 * (rho & (n_pages - 1)) + (rho >> _log2(n_pages))
    colq = lax.broadcasted_iota(jnp.int32, (1, nq), 1)
    rowq = lax.broadcasted_iota(jnp.int32, (nq, 1), 0)
    cur_blk = past_len // SEL_BLOCK
    chains = [(bi, g) for bi in range(bps) for g in range(N_KV_HEADS)]
    nch = len(chains)
    step = rowq & (dec_t - 1)
    qpos = past_len + (colq & (dec_t - 1))
    rho_q = lax.broadcasted_iota(jnp.int32, (nblk, 1), 0)
    end = (2 * (rho_q & (n_pages - 1)) + (rho_q >> _log2(n_pages))) * CMP_BLOCK + (CMP_BLOCK - 1)
    dist = (qpos - end).astype(F32)
    valid = dist >= 0.0
    kpos = lax.broadcasted_iota(jnp.int32, (1, past_len), 1)
    dpast = (past_len + step - kpos).astype(F32)
    lane = lax.broadcasted_iota(jnp.int32, (1, LANES), 1)
    per_tile = LANES // dec_t
    offs = [((step_id * bps + bi) & (per_tile - 1)) * dec_t for bi in range(bps)]
    dnews = [(step - (lane - off)).astype(F32) for off in offs]
    oknew = [(lane >= off) & (lane < off + dec_t) for off in offs]

    qst = [_stack_group_queries(q_ref[bi], g) for bi, g in chains]
    kcs = [kc_all[bi * rows_b + g * nblk:bi * rows_b + (g + 1) * nblk].astype(BF16) for bi, g in chains]
    vcs = [vc_all[bi * rows_b + g * nblk:bi * rows_b + (g + 1) * nblk].astype(BF16) for bi, g in chains]
    st = [_dot_nt(kcs[ch], qst[ch]) for ch in range(nch)]
    for ch, (bi, g) in enumerate(chains):
        for pg in range(n_pages):
            kt_ = sbuf[slot, bi, pl.ds(pg * half_rows + g * d, d), :].astype(BF16)
            vt_ = sbuf[slot, bi, pl.ds(pg * half_rows + (N_KV_HEADS + g) * d, d), :].astype(BF16)
            kk_scr[ch, 0:d, pg * PAGE_SIZE:(pg + 1) * PAGE_SIZE] = kt_
            kk_scr[ch, d:2 * d, pg * PAGE_SIZE:(pg + 1) * PAGE_SIZE] = kt_
            vv_scr[ch, 0:d, pg * PAGE_SIZE:(pg + 1) * PAGE_SIZE] = vt_
            vv_scr[ch, d:2 * d, pg * PAGE_SIZE:(pg + 1) * PAGE_SIZE] = vt_
    sp = [_dot(qst[ch], kk_scr[ch]) for ch in range(nch)]
    nk = [_twice(nkv_ref[g * d:(g + 1) * d, :].astype(BF16)) for g in range(N_KV_HEADS)]
    nv = [_twice(nkv_ref[KV_WIDTH + g * d:KV_WIDTH + (g + 1) * d, :].astype(BF16)) for g in range(N_KV_HEADS)]
    sn = [_dot(qst[ch], nk[g]) for ch, (bi, g) in enumerate(chains)]
    ps = []
    for ch, (bi, g) in enumerate(chains):
        s = jnp.where(valid, st[ch] - _row_slopes(g, colq >> _log2(dec_t)) * dist, NEG)
        e = jnp.exp(s - jnp.max(s, axis=0, keepdims=True))
        ps.append(jnp.where(valid, e / jnp.sum(e, axis=0, keepdims=True), 0.0))
    o_cmp = [_unstack_group(_dot_tn(ps[ch].astype(BF16), vcs[ch]), dec_t) for ch in range(nch)]
    msel = []
    for ch in range(nch):
        p = ps[ch]
        imp = p[:, 0:dec_t]
        for r in range(1, GROUP):
            imp = imp + p[:, r * dec_t:(r + 1) * dec_t]
        force = (bid == 0) | (bid == cur_blk)
        score = jnp.where(force, FORCED_SCORE, imp)
        rank = _block_ranks(score, bid, [(i, 2 * (i % n_pages) + i // n_pages) for i in range(nblk)])
        rank = rank + jnp.where(bid > cur_blk, (FORCED_SCORE >= score).astype(F32),
                                (FORCED_SCORE > score).astype(F32))
        msel.append(jnp.where((rank < topk) & (score >= 0.0), 1.0, 0.0).astype(BF16))
    mk = [_dot_tn(msel[ch], e_ref[...]) for ch in range(nch)]
    pp, pn, den = [], [], []
    for ch, (bi, g) in enumerate(chains):
        okp = jnp.concatenate([mk[ch]] * GROUP, axis=0) > 0.5
        slope = _row_slopes(g, rowq >> _log2(dec_t))
        lgp = jnp.where(okp & (dpast >= 0.0), sp[ch] - slope * dpast, NEG)
        lgn = jnp.where(oknew[bi] & (dnews[bi] >= 0.0), sn[ch] - slope * dnews[bi], NEG)
        m = jnp.maximum(jnp.max(lgp, axis=-1, keepdims=True), jnp.max(lgn, axis=-1, keepdims=True))
        pp.append(jnp.exp(lgp - m))
        pn.append(jnp.exp(lgn - m))
        den.append(jnp.sum(pp[ch], axis=-1, keepdims=True) + jnp.sum(pn[ch], axis=-1, keepdims=True))
    o_sel = [_unstack_group((_dot_nt(pp[ch].astype(BF16), vv_scr[ch]) + _dot_nt(pn[ch].astype(BF16), nv[g]))
                            / den[ch], dec_t) for ch, (bi, g) in enumerate(chains)]

    for bi in range(bps):
        ocs_ref[bi, 0] = jnp.concatenate(o_cmp[bi * N_KV_HEADS:(bi + 1) * N_KV_HEADS], axis=1)
        ocs_ref[bi, 1] = jnp.concatenate(o_sel[bi * N_KV_HEADS:(bi + 1) * N_KV_HEADS], axis=1)


def _nsa_sample(page_flat, cache, q, nkv, w1r, pe_t, w2dup, kg_dup, e_perm, past_len, topk):
    nbatch, dec_t, _ = q.shape
    n_pages = past_len // PAGE_SIZE
    bps = SAMPLE_BATCH_PER_STEP
    nch = bps * N_KV_HEADS
    body = functools.partial(_nsa_sample_body, n_pages=n_pages, past_len=past_len, dec_t=dec_t, topk=topk)
    grid_spec = pltpu.PrefetchScalarGridSpec(
        num_scalar_prefetch=1, grid=(nbatch // bps,),
        in_specs=[pl.BlockSpec(memory_space=pl.ANY),
                  pl.BlockSpec((bps, dec_t, ATTN_WIDTH), lambda i, pt: (i, 0, 0)),
                  pl.BlockSpec((2 * KV_WIDTH, LANES), lambda i, pt: (0, i * bps * dec_t // LANES)),
                  pl.BlockSpec(w1r.shape, lambda i, pt: (0, 0, 0)),
                  pl.BlockSpec(pe_t.shape, lambda i, pt: (0, 0, 0, 0)),
                  pl.BlockSpec(w2dup.shape, lambda i, pt: (0, 0, 0)),
                  pl.BlockSpec(kg_dup.shape, lambda i, pt: (0, 0)),
                  pl.BlockSpec(e_perm.shape, lambda i, pt: (0, 0))],
        out_specs=pl.BlockSpec((bps, 2, dec_t, ATTN_WIDTH), lambda i, pt: (i, 0, 0, 0)),
        scratch_shapes=[pltpu.VMEM((2, bps, PAGE_ROWS // 2, n_pages, PAGE_SIZE), F32),
                        pltpu.VMEM((2, bps, n_pages * PAGE_ROWS // 2, PAGE_SIZE), F32),
                        pltpu.VMEM((bps * 4 * n_pages, HEAD_DIM * LANES), BF16),
                        pltpu.VMEM((nch, 2 * HEAD_DIM, past_len), BF16),
                        pltpu.VMEM((nch, 2 * HEAD_DIM, past_len), BF16),
                        pltpu.SemaphoreType.DMA((2,))])
    return pl.pallas_call(
        body, grid_spec=grid_spec,
        out_shape=jax.ShapeDtypeStruct((nbatch, 2, dec_t, ATTN_WIDTH), F32),
        compiler_params=pltpu.CompilerParams(dimension_semantics=("arbitrary",),
                                             vmem_limit_bytes=_vmem_limit(40 << 20)),
        name="nsa_sample",
    )(page_flat, cache, q, nkv, w1r, pe_t, w2dup, kg_dup, e_perm)


def _win_sample_body(win_ref, q_ref, nw_ref, ocs_ref, gates_ref, gexp_ref, oa_ref, wout_ref, *, past_len, dec_t):
    d = HEAD_DIM
    wbuf = win_ref.shape[2]
    nq = GROUP * dec_t
    rowq = lax.broadcasted_iota(jnp.int32, (nq, 1), 0)
    step = rowq & (dec_t - 1)
    kpos = past_len - wbuf + lax.broadcasted_iota(jnp.int32, (1, wbuf), 1)
    dpast = (past_len + step - kpos).astype(F32)
    okp = (dpast >= 0.0) & (dpast < WINDOW)
    nbat = win_ref.shape[0]
    lane = lax.broadcasted_iota(jnp.int32, (1, LANES), 1)
    per_tile = LANES // dec_t
    offs = [((pl.program_id(0) * nbat + bi) & (per_tile - 1)) * dec_t for bi in range(nbat)]
    dnews = [(step - (lane - off)).astype(F32) for off in offs]
    okns = [(lane >= off) & (lane < off + dec_t) & (dn >= 0.0) & (dn < WINDOW) for off, dn in zip(offs, dnews)]
    chains = [(bi, g) for bi in range(nbat) for g in range(N_KV_HEADS)]
    qst = [_stack_group_queries(q_ref[bi], g) for bi, g in chains]
    sp = [_dot(qst[ch], _twice(win_ref[bi, g * d:(g + 1) * d, :].astype(BF16))) for ch, (bi, g) in enumerate(chains)]
    nk = [_twice(nw_ref[g * d:(g + 1) * d, :].astype(BF16)) for g in range(N_KV_HEADS)]
    nvs = [_twice(nw_ref[KV_WIDTH + g * d:KV_WIDTH + (g + 1) * d, :].astype(BF16)) for g in range(N_KV_HEADS)]
    sn = [_dot(qst[ch], nk[g]) for ch, (bi, g) in enumerate(chains)]
    pp, pn, den = [], [], []
    for ch, (bi, g) in enumerate(chains):
        slope = _row_slopes(g, rowq >> _log2(dec_t))
        lgp = jnp.where(okp, sp[ch] - slope * dpast, NEG)
        lgn = jnp.where(okns[bi], sn[ch] - slope * dnews[bi], NEG)
        m = jnp.maximum(jnp.max(lgp, axis=-1, keepdims=True), jnp.max(lgn, axis=-1, keepdims=True))
        pp.append(jnp.exp(lgp - m))
        pn.append(jnp.exp(lgn - m))
        den.append(jnp.sum(pp[ch], axis=-1, keepdims=True) + jnp.sum(pn[ch], axis=-1, keepdims=True))
    o_win = []
    for ch, (bi, g) in enumerate(chains):
        vt_ = win_ref[bi, KV_WIDTH + g * d:KV_WIDTH + (g + 1) * d, :].astype(BF16)
        o2 = _dot_nt(pp[ch].astype(BF16), _twice(vt_)) + _dot_nt(pn[ch].astype(BF16), nvs[g])
        o_win.append(_unstack_group(o2 / den[ch], dec_t))
    for bi in range(nbat):
        ge = _expand_gates(gates_ref[bi], gexp_ref)
        oa_ref[bi] = (ge[0] * ocs_ref[bi, 0] + ge[1] * ocs_ref[bi, 1]
                      + ge[2] * jnp.concatenate(o_win[bi * N_KV_HEADS:(bi + 1) * N_KV_HEADS], axis=1))

        rolled = pltpu.roll(win_ref[bi], wbuf - dec_t, 1)
        newr = pltpu.roll(nw_ref[...], (LANES - dec_t - offs[bi]) & (LANES - 1), 1)
        wout_ref[bi, :, 0:wbuf - LANES] = rolled[:, 0:wbuf - LANES]
        wout_ref[bi, :, wbuf - LANES:wbuf] = jnp.where(lane >= LANES - dec_t, newr, rolled[:, wbuf - LANES:wbuf])


def _win_sample(win, q, nw, ocs, gates, gexp, past_len):
    nbatch, feat, wbuf = win.shape
    dec_t = q.shape[1]
    body = functools.partial(_win_sample_body, past_len=past_len, dec_t=dec_t)
    b3 = lambda i: (i, 0, 0)
    bps = WINDOW_BATCH_PER_STEP
    return pl.pallas_call(
        body, grid=(nbatch // bps,),
        in_specs=[pl.BlockSpec((bps, feat, wbuf), b3), pl.BlockSpec((bps, dec_t, ATTN_WIDTH), b3),
                  pl.BlockSpec((feat, LANES), lambda i: (1, i * bps * dec_t // LANES)),
                  pl.BlockSpec((bps, 2, dec_t, ATTN_WIDTH), lambda i: (i, 0, 0, 0)),
                  pl.BlockSpec((bps, dec_t, LANES), b3), pl.BlockSpec(gexp.shape, lambda i: (0, 0, 0))],
        out_specs=(pl.BlockSpec((bps, dec_t, ATTN_WIDTH), b3), pl.BlockSpec((bps, feat, wbuf), b3)),
        out_shape=(jax.ShapeDtypeStruct((nbatch, dec_t, ATTN_WIDTH), F32),
                   jax.ShapeDtypeStruct((nbatch, feat, wbuf), F32)),
        compiler_params=pltpu.CompilerParams(dimension_semantics=("parallel",),
                                             vmem_limit_bytes=_vmem_limit(24 << 20)),
        name="win_sample",
    )(win, q, nw, ocs, gates, gexp)


def _hgrn_sample_body(q_ref, f_ref, v_ref, s_ref, o_ref, so_ref, *, dec_t):
    nb = s_ref.shape[3]
    o_ref[...] = jnp.zeros(o_ref.shape, F32)

    sub = 8

    def per_tile(i, carry):
        r0 = pl.multiple_of(i * sub, sub)
        f_t = [f_ref[pl.ds(r0, sub), pl.ds(t * nb, nb)] for t in range(dec_t)]
        q_t = [q_ref[pl.ds(r0, sub), pl.ds(t * nb, nb)] for t in range(dec_t)]
        for j in range(sub):
            s = s_ref[0, r0 + j]
            for t in range(dec_t):
                cols = pl.ds(t * nb, nb)
                fr = f_t[t][j:j + 1, :]
                s = fr * s + (1.0 - fr) * v_ref[:, cols]
                o_ref[:, cols] = o_ref[:, cols] + s * q_t[t][j:j + 1, :]
            so_ref[0, r0 + j] = s
        return carry

    lax.fori_loop(0, s_ref.shape[1] // sub, per_tile, 0)


def _hgrn_sample(ztb, state, dec_t):
    nh, dk, dv, nb = state.shape
    n = ztb.shape[1]
    body = functools.partial(_hgrn_sample_body, dec_t=dec_t)
    q0, f0, v0 = 512 // dk, 1024 // dk, 1536 // dk
    return pl.pallas_call(
        body, grid=(nh,),
        in_specs=[pl.BlockSpec((dk, n), lambda h: (q0 + h, 0)), pl.BlockSpec((dk, n), lambda h: (f0 + h, 0)),
                  pl.BlockSpec((dv, n), lambda h: (v0 + h, 0)),
                  pl.BlockSpec((1, dk, dv, nb), lambda h: (h, 0, 0, 0))],
        out_specs=(pl.BlockSpec((dv, n), lambda h: (h, 0)), pl.BlockSpec((1, dk, dv, nb), lambda h: (h, 0, 0, 0))),
        out_shape=(jax.ShapeDtypeStruct((nh * dv, n), F32), jax.ShapeDtypeStruct(state.shape, F32)),
        compiler_params=pltpu.CompilerParams(dimension_semantics=("parallel",),
                                             vmem_limit_bytes=_vmem_limit(24 << 20)),
        name="hgrn_sample",
    )(ztb, ztb, ztb, state)


def _gate_expander():
    m = np.zeros((N_BRANCH, LANES, ATTN_WIDTH), np.float32)
    for br in range(N_BRANCH):
        for h in range(N_ATTN_HEADS):
            m[br, h * N_BRANCH + br, h * HEAD_DIM:(h + 1) * HEAD_DIM] = 1.0
    return jnp.asarray(m, BF16)


def _block_expander(block_ids, n_keys):
    key_blk = np.arange(n_keys) // SEL_BLOCK
    return jnp.asarray((np.asarray(block_ids)[:, None] == key_blk[None, :]).astype(np.float32), BF16)


def _key_features(n_keys):
    s = np.arange(n_keys)
    m = np.zeros((LANES, n_keys), np.float32)
    m[0:HEAD_DIM] = (np.arange(HEAD_DIM)[:, None] == (s // SEL_BLOCK)[None, :])
    m[HEAD_DIM] = s // SEL_BLOCK
    m[HEAD_DIM + 1] = s % SEL_BLOCK
    return jnp.asarray(m, BF16)


def _query_slope_features(rows):
    m = np.zeros((N_KV_HEADS, GROUP * rows, LANES), np.float32)
    for g in range(N_KV_HEADS):
        for r in range(GROUP):
            m[g, r * rows:(r + 1) * rows, HEAD_DIM] = SLOPES[g][r] * SEL_BLOCK
            m[g, r * rows:(r + 1) * rows, HEAD_DIM + 1] = SLOPES[g][r]
    return jnp.asarray(m, F32)


def _chunk_lower_tri(n, c):
    i = np.arange(n)
    return jnp.asarray(((i[:, None] // c == i[None, :] // c) & (i[None, :] <= i[:, None])).astype(np.float32), BF16)


def _head_indicator():
    m = np.zeros((RNN_WIDTH, LANES), np.float32)
    m[np.arange(RNN_WIDTH), np.arange(RNN_WIDTH) // RNN_DK] = 1.0
    return jnp.asarray(m, BF16)


def _head_block_diag(n):
    i = np.arange(n)
    return jnp.asarray((i[:, None] // RNN_DV == i[None, :] // RNN_DK).astype(np.float32), F32)


def kernel(x_prompt, x_sample, cache_kv, cache_win, state_rnn, page_table, ln_mix, w_in, q_norm, k_norm, cmp_pe,
           cmp_w1, cmp_w2, attn_out_norm, rnn_lb_logits, rnn_out_norm, w_out, ln_mlp, w_up, w_down):
    assert w_in.shape[0] == 1, "single layer"
    b, t, dm = x_prompt.shape
    nbatch, dec_t, _ = x_sample.shape
    n_pool = cache_kv.shape[1]
    n_pages = page_table.shape[1]
    past_len = n_pages * PAGE_SIZE
    wbuf = cache_win.shape[2]
    assert t % PROJ_ROWS == 0 and t % RNN_ROWS == 0 and t % min(SEL_KEYS, t) == 0 and t >= WINDOW
    assert (b * t) % FINISH_ROWS == 0 and (nbatch * dec_t) % FINISH_ROWS == 0 and w_up.shape[2] % FF_CHUNK == 0
    assert (nbatch * dec_t) % PROJ_ROWS == 0 and nbatch == LANES and dec_t <= 8
    assert past_len % SEL_BLOCK == 0 and wbuf == WINDOW and wbuf >= LANES
    assert LANES % dec_t == 0 and (LANES // dec_t) % max(SAMPLE_BATCH_PER_STEP, WINDOW_BATCH_PER_STEP) == 0

    w = w_in[0]
    c_kv, c_gate, c_rq, c_rf, c_ri, c_rg = ATTN_WIDTH, ATTN_WIDTH + 6 * KV_WIDTH, 1304, 1816, 2328, 2840
    gate_cols = jnp.pad(w[:, c_gate:c_rq], ((0, 0), (0, LANES - N_ATTN_HEADS * N_BRANCH)))
    wtok = jnp.concatenate([w[:, 0:ATTN_WIDTH], w[:, c_rq:], gate_cols, w[:, c_kv:c_kv + 2 * KV_WIDTH]],
                           axis=1).astype(BF16)
    wft = w[:, c_kv:c_gate].T.astype(BF16)
    wtok_s = jnp.concatenate([w[:, 0:ATTN_WIDTH], w[:, c_rg:], gate_cols], axis=1).astype(BF16)
    wtb_s = jnp.concatenate([w[:, c_kv:c_kv + 4 * KV_WIDTH], w[:, c_rq:c_rg]], axis=1).T.astype(BF16)
    wbt_s = w[:, c_kv + 2 * KV_WIDTH:c_gate].T.astype(BF16)
    ln = ln_mix[0][None, :]
    qg = (jnp.tile(q_norm[0], N_ATTN_HEADS) * SCALE)[None, :]
    lbl = rnn_lb_logits.astype(F32)
    lblt = jnp.broadcast_to(lbl[:, :, None], lbl.shape + (PROJ_ROWS,))
    gsel = jnp.broadcast_to(k_norm[0, 1][:, None], (HEAD_DIM, PROJ_ROWS))
    gwin = jnp.broadcast_to(k_norm[0, 2][:, None], (HEAD_DIM, PROJ_ROWS))
    kg_dup = jnp.tile(k_norm[0, 0], 2)[None, :]
    pe = cmp_pe[0]
    pe_tok = jnp.tile(jnp.concatenate([jnp.tile(pe[0], (1, N_KV_HEADS)), jnp.tile(pe[1], (1, N_KV_HEADS))], axis=1),
                      (PROJ_ROWS // CMP_BLOCK, 1))
    w1 = cmp_w1[0].reshape(2, CMP_BLOCK, HEAD_DIM, CMP_HIDDEN)
    zeros = jnp.zeros_like(w1)
    w1bd = jnp.concatenate([jnp.concatenate([w1, zeros], axis=3), jnp.concatenate([zeros, w1], axis=3)],
                           axis=2).astype(BF16)
    w1r = jnp.tile(w1.transpose(0, 2, 1, 3), (1, 1, 2, 1)).astype(BF16)
    w1r = w1r.reshape(2, HEAD_DIM * LANES, CMP_HIDDEN)
    pe_t = jnp.tile(pe.transpose(0, 2, 1), (1, 1, 2))[:, :, None, :]
    w2dup = jnp.tile(cmp_w2[0], (1, 1, 2)).astype(BF16)
    ag = attn_out_norm[0][None, :]
    rg = rnn_out_norm[0][None, :]
    wout = w_out[0].astype(BF16)
    lnm = ln_mlp[0][None, :]
    wup = w_up[0].astype(BF16)
    wdn = w_down[0].astype(BF16)
    gexp = _gate_expander()

    (q_p, gates_p, rq_p, f_p, rv_p, rgs_p, kvc_p, kvt_p, wint_p, att_p) = _proj_prompt(
        x_prompt, ln, wtok, wft, qg, lbl, pe_tok, gsel, gwin)
    cmp_p = _compress_prompt(kvc_p, w1bd, w2dup, kg_dup)
    nb_p = t // CMP_BLOCK
    oa_p = _nsa_prompt(q_p, gates_p, cmp_p, att_p, _key_features(t), _query_slope_features(NSA_Q_ROWS), gexp)
    orn_p, st_p = _hgrn_prompt(rq_p, f_p, rv_p, _chunk_lower_tri(RNN_ROWS, RNN_CHUNK),
                               _head_block_diag(LANES), _head_indicator())
    y_p = _finish(x_prompt.reshape(b * t, dm), oa_p.reshape(b * t, ATTN_WIDTH), orn_p.reshape(b * t, RNN_WIDTH),
                  rgs_p.reshape(b * t, RNN_WIDTH), ag, rg, wout, lnm, wup, wdn).reshape(b, t, dm)
    kv_prompt = kvt_p.reshape(1, b, 4, N_KV_HEADS, HEAD_DIM, t).transpose(0, 1, 5, 2, 3, 4)
    wlen = min(WINDOW, t)
    win_prompt = wint_p[:, :, t - wlen:].reshape(1, b, 2, N_KV_HEADS, HEAD_DIM, wlen).transpose(0, 1, 5, 2, 3, 4)
    hh = LANES // RNN_DV
    st5 = st_p.reshape(b, RNN_WIDTH // LANES, hh, RNN_DV, hh, RNN_DK)
    rnn_prompt = jnp.stack([st5[:, :, i, :, i, :] for i in range(hh)], axis=2)
    rnn_prompt = rnn_prompt.reshape(b, N_RNN_HEADS, RNN_DV, RNN_DK).transpose(0, 1, 3, 2)[None]

    n_s = nbatch * dec_t
    xbt = x_sample.reshape(n_s, dm)
    xtb = x_sample.transpose(1, 0, 2).reshape(n_s, dm)
    q_s, rgs_s, gates_s, ztb, zbt = _proj_sample(xbt, xtb, ln, wtok_s, wtb_s, wbt_s, qg, lblt, gsel, gwin)
    kv_sample = ztb[0:4 * KV_WIDTH].reshape(4, N_KV_HEADS, HEAD_DIM, dec_t, nbatch).transpose(4, 3, 0, 1, 2)[None]
    cache = cache_kv[0].transpose(0, 2, 3, 4, 1).reshape(n_pool, PAGE_ROWS, PAGE_SIZE)
    nblk_s = past_len // CMP_BLOCK
    rho = np.arange(nblk_s)
    e_perm = _block_expander(2 * (rho % n_pages) + rho // n_pages, past_len)
    ns_s = -(-(past_len + dec_t) // SEL_BLOCK)
    q_s3 = q_s.reshape(nbatch, dec_t, ATTN_WIDTH)
    ocs = _nsa_sample(page_table.reshape(-1), cache, q_s3, zbt, w1r, pe_t, w2dup, kg_dup,
                      e_perm, past_len, min(SEL_TOPK, ns_s))
    win = cache_win[0].transpose(0, 2, 3, 4, 1).reshape(nbatch, 2 * KV_WIDTH, wbuf)
    oa_s, win_new = _win_sample(win, q_s3, zbt, ocs, gates_s.reshape(nbatch, dec_t, LANES), gexp, past_len)
    win_sample = win_new.reshape(1, nbatch, 2, N_KV_HEADS, HEAD_DIM, wbuf).transpose(0, 1, 5, 2, 3, 4)
    state = state_rnn[0].transpose(1, 2, 3, 0)
    orn_t, state_new = _hgrn_sample(ztb, state, dec_t)
    rnn_sample = state_new.transpose(3, 0, 1, 2)[None]
    orn_s = orn_t.reshape(RNN_WIDTH, dec_t, nbatch).transpose(2, 1, 0).reshape(n_s, RNN_WIDTH)
    y_s = _finish(xbt, oa_s.reshape(n_s, ATTN_WIDTH), orn_s, rgs_s, ag, rg, wout, lnm, wup, wdn).reshape(nbatch, dec_t, dm)

    return (y_p, y_s, kv_prompt, kv_sample, win_prompt, win_sample, rnn_prompt, rnn_sample)
```

```python
import functools

import numpy as np
import jax
import jax.numpy as jnp
from jax import lax
from jax.experimental import pallas as pl
from jax.experimental.pallas import tpu as pltpu

F32 = jnp.float32
BF16 = jnp.bfloat16

HEAD_DIM = 64
N_ATTN_HEADS = 8
N_KV_HEADS = 2
GROUP = N_ATTN_HEADS // N_KV_HEADS
N_RNN_HEADS = 8
RNN_DK = 64
RNN_DV = 64
ATTN_WIDTH = N_ATTN_HEADS * HEAD_DIM
RNN_WIDTH = N_RNN_HEADS * RNN_DV
KV_WIDTH = N_KV_HEADS * HEAD_DIM
N_BRANCH = 3
CMP_BLOCK = 64
SEL_BLOCK = 64
SEL_TOPK = 16
WINDOW = 512
CMP_HIDDEN = 128
PAGE_SIZE = 128
SCALE = HEAD_DIM ** -0.5
EPS = 1e-6
NEG = -1e30
LOG2E = 1.4426950408889634
FORCED_SCORE = GROUP + 1.0
SLOPES = [[2.0 ** (-(g * GROUP + r + 1)) for r in range(GROUP)] for g in range(N_KV_HEADS)]

LANES = 128
VMEM_BYTES_V7X = 64 * 1024 * 1024

PROJ_ROWS = 256
NSA_Q_ROWS = 128
SEL_KEYS = 512
ROW_BLOCK = 16
RNN_ROWS = 256
RNN_CHUNK = 16
FF_CHUNK = 1024
FINISH_ROWS = 512

NT = (((1,), (1,)), ((), ()))
TN = (((0,), (0,)), ((), ()))


def _vmem_limit(nbytes):
    return int(min(VMEM_BYTES_V7X - (8 << 20), max(nbytes, 16 << 20)))


def _dot(a, b):
    return jnp.dot(a, b, preferred_element_type=F32)


def _dot_nt(a, b):
    return lax.dot_general(a, b, NT, preferred_element_type=F32)


def _dot_tn(a, b):
    return lax.dot_general(a, b, TN, preferred_element_type=F32)


def _low_half(shape):
    lane = lax.broadcasted_iota(jnp.int32, shape, len(shape) - 1)
    return (lane & HEAD_DIM) == 0


def _head_mean_sq(x):
    outs = []
    for j in range(x.shape[-1] // LANES):
        blk = x[:, j * LANES:(j + 1) * LANES]
        sq = blk * blk
        low = _low_half(blk.shape)
        s_lo = jnp.sum(jnp.where(low, sq, 0.0), axis=-1, keepdims=True)
        s_hi = jnp.sum(jnp.where(low, 0.0, sq), axis=-1, keepdims=True)
        outs.append(jnp.where(low, s_lo, s_hi))
    return jnp.concatenate(outs, axis=-1) * (1.0 / HEAD_DIM)


def _row_rms(x):
    return x * lax.rsqrt(jnp.mean(x * x, axis=-1, keepdims=True) + EPS)


def _col_head_norm(rows, gain):
    ms = jnp.mean(rows * rows, axis=0, keepdims=True)
    return rows * lax.rsqrt(ms + EPS) * gain


def _silu(z):
    return z * jax.nn.sigmoid(z)


def _lower_bound(logits, axis):
    m = jnp.max(logits, axis=axis, keepdims=True)
    e = jnp.exp(logits - m)
    lb = e / jnp.sum(e, axis=axis, keepdims=True)
    return lb[0:1] if axis == 0 else lb


def _stack_group_queries(q, g):
    rows = q.shape[0]
    low = _low_half((rows, LANES))
    zero = jnp.zeros((rows, LANES), q.dtype)
    pa = q[:, g * 2 * LANES:g * 2 * LANES + LANES]
    pb = q[:, g * 2 * LANES + LANES:(g + 1) * 2 * LANES]
    return jnp.concatenate([jnp.where(low, pa, zero), jnp.where(low, zero, pa),
                            jnp.where(low, pb, zero), jnp.where(low, zero, pb)], axis=0)


def _unstack_group(o2, rows):
    low = _low_half((rows, LANES))
    return jnp.concatenate([jnp.where(low, o2[0:rows], o2[rows:2 * rows]),
                            jnp.where(low, o2[2 * rows:3 * rows], o2[3 * rows:4 * rows])], axis=1)


def _log2(n):
    assert n > 0 and n & (n - 1) == 0, n
    return n.bit_length() - 1


def _twice(x):
    return jnp.concatenate([x, x], axis=0)


def _row_slopes(g, row_head):
    s = jnp.full(row_head.shape, SLOPES[g][GROUP - 1], F32)
    for r in range(GROUP - 2, -1, -1):
        s = jnp.where(row_head == r, SLOPES[g][r], s)
    return s


def _block_ranks(score, ids, cand):
    sub = 8
    nrow = score.shape[0]
    in_order = all(r == i for r, i in cand) and nrow % sub == 0
    rank = jnp.zeros(score.shape, F32)
    for row, id_i in cand:
        s_i = score[row:row + 1, :]
        if in_order:
            parts = []
            for v in range(nrow // sub):
                blk = score[v * sub:(v + 1) * sub, :]
                if (v + 1) * sub - 1 <= id_i:
                    parts.append((s_i > blk).astype(F32))
                elif v * sub > id_i:
                    parts.append((s_i >= blk).astype(F32))
                else:
                    parts.append(jnp.where(ids[v * sub:(v + 1) * sub, :] > id_i,
                                           (s_i >= blk).astype(F32), (s_i > blk).astype(F32)))
            rank = rank + jnp.concatenate(parts, axis=0)
        else:
            rank = rank + jnp.where(ids > id_i, (s_i >= score).astype(F32), (s_i > score).astype(F32))
    return rank


def _expand_gates(gates, gexp_ref):
    hi = gates.astype(BF16)
    lo = (gates - hi.astype(F32)).astype(BF16)
    return [_dot(hi, gexp_ref[br]) + _dot(lo, gexp_ref[br]) for br in range(N_BRANCH)]


TOK_Q, TOK_RQ, TOK_RF, TOK_RI, TOK_RG, TOK_GATE, TOK_KVC, TOK_END = 0, 512, 1024, 1536, 2048, 2560, 2688, 2944


def _proj_prompt_body(x_ref, ln_ref, wtok_ref, wft_ref, qg_ref, lbl_ref, pe_ref, gsel_ref, gwin_ref,
                      q_ref, gates_ref, rq_ref, f_ref, rv_ref, rgs_ref, kvc_ref, kvt_ref, wint_ref, att_ref):
    xb = (_row_rms(x_ref[0]) * ln_ref[...]).astype(BF16)

    def tok(lo, hi):
        return _dot(xb, wtok_ref[:, lo:hi])

    zq = tok(TOK_Q, TOK_RQ)
    q_ref[0] = (zq * lax.rsqrt(_head_mean_sq(zq) + EPS) * qg_ref[...]).astype(BF16)
    rq_ref[0] = _silu(tok(TOK_RQ, TOK_RF))
    lb = _lower_bound(lbl_ref[...], 0)
    f_ref[0] = lb + (1.0 - lb) * jax.nn.sigmoid(tok(TOK_RF, TOK_RI))
    rv_ref[0] = tok(TOK_RI, TOK_RG)
    rgs_ref[0] = _silu(tok(TOK_RG, TOK_GATE))
    gates_ref[0] = jax.nn.sigmoid(tok(TOK_GATE, TOK_KVC))
    zc = tok(TOK_KVC, TOK_END) + pe_ref[...]
    kvc_ref[0, 0] = zc[:, 0:LANES]
    kvc_ref[1, 0] = zc[:, LANES:2 * LANES]

    zf = _dot_nt(wft_ref[...], xb)
    d = HEAD_DIM
    ksel = jnp.concatenate([_col_head_norm(zf[256 + g * d:256 + (g + 1) * d], gsel_ref[...])
                            for g in range(N_KV_HEADS)], axis=0)
    kwin = jnp.concatenate([_col_head_norm(zf[512 + g * d:512 + (g + 1) * d], gwin_ref[...])
                            for g in range(N_KV_HEADS)], axis=0)
    vsel = zf[384:512]
    vwin = zf[640:768]
    kvt_ref[0, 0:256] = zf[0:256]
    kvt_ref[0, 256:384] = ksel
    kvt_ref[0, 384:512] = vsel
    wint_ref[0, 0:128] = kwin
    wint_ref[0, 128:256] = vwin
    att_ref[0, 0:128] = ksel.astype(BF16)
    att_ref[0, 128:256] = vsel.astype(BF16)
    att_ref[0, 256:384] = kwin.astype(BF16)
    att_ref[0, 384:512] = vwin.astype(BF16)


def _proj_prompt(x, ln, wtok, wft, qg, lbl, pe_tok, gsel, gwin):
    b, t, dm = x.shape
    tm = PROJ_ROWS
    grid = (b, t // tm)
    row = lambda i, j: (i, j, 0)
    col = lambda i, j: (i, 0, j)
    const2 = lambda i, j: (0, 0)
    out_shape = (
        jax.ShapeDtypeStruct((b, t, ATTN_WIDTH), BF16),
        jax.ShapeDtypeStruct((b, t, LANES), F32),
        jax.ShapeDtypeStruct((b, t, RNN_WIDTH), F32),
        jax.ShapeDtypeStruct((b, t, RNN_WIDTH), F32),
        jax.ShapeDtypeStruct((b, t, RNN_WIDTH), F32),
        jax.ShapeDtypeStruct((b, t, RNN_WIDTH), F32),
        jax.ShapeDtypeStruct((2, b, t, LANES), F32),
        jax.ShapeDtypeStruct((b, 4 * KV_WIDTH, t), F32),
        jax.ShapeDtypeStruct((b, 2 * KV_WIDTH, t), F32),
        jax.ShapeDtypeStruct((b, 4 * KV_WIDTH, t), BF16),
    )
    out_specs = (
        pl.BlockSpec((1, tm, ATTN_WIDTH), row), pl.BlockSpec((1, tm, LANES), row),
        pl.BlockSpec((1, tm, RNN_WIDTH), row), pl.BlockSpec((1, tm, RNN_WIDTH), row),
        pl.BlockSpec((1, tm, RNN_WIDTH), row), pl.BlockSpec((1, tm, RNN_WIDTH), row),
        pl.BlockSpec((2, 1, tm, LANES), lambda i, j: (0, i, j, 0)),
        pl.BlockSpec((1, 4 * KV_WIDTH, tm), col), pl.BlockSpec((1, 2 * KV_WIDTH, tm), col),
        pl.BlockSpec((1, 4 * KV_WIDTH, tm), col),
    )
    in_specs = [
        pl.BlockSpec((1, tm, dm), row), pl.BlockSpec(ln.shape, const2),
        pl.BlockSpec(wtok.shape, const2), pl.BlockSpec(wft.shape, const2),
        pl.BlockSpec(qg.shape, const2), pl.BlockSpec(lbl.shape, const2), pl.BlockSpec(pe_tok.shape, const2),
        pl.BlockSpec(gsel.shape, const2), pl.BlockSpec(gwin.shape, const2),
    ]
    return pl.pallas_call(
        _proj_prompt_body, grid=grid, in_specs=in_specs, out_specs=out_specs, out_shape=out_shape,
        compiler_params=pltpu.CompilerParams(dimension_semantics=("parallel", "parallel"),
                                             vmem_limit_bytes=_vmem_limit(48 << 20)),
        name="proj_prompt",
    )(x, ln, wtok, wft, qg, lbl, pe_tok, gsel, gwin)


def _compress_prompt_body(x_ref, w1_ref, w2_ref, kg_ref, out_ref):
    c = pl.program_id(0)
    nb = out_ref.shape[2]
    acc = jnp.zeros((nb, 2 * CMP_HIDDEN), F32)
    for pos in range(CMP_BLOCK):
        xp = x_ref[0, 0, pl.ds(pos, nb, stride=CMP_BLOCK), :]
        acc = acc + _dot(xp.astype(BF16), w1_ref[0, pos])
    hb = _silu(acc).astype(BF16)
    outs = []
    for g in range(N_KV_HEADS):
        y = _dot(hb[:, g * CMP_HIDDEN:(g + 1) * CMP_HIDDEN], w2_ref[0])
        yn = _row_rms(y) * kg_ref[...]
        outs.append(jnp.where(c == 0, yn, y))
    out_ref[0, 0] = jnp.concatenate(outs, axis=1)


def _compress_prompt(kvc, w1bd, w2dup, kg_dup):
    _, b, t, _ = kvc.shape
    nb = t // CMP_BLOCK
    return pl.pallas_call(
        _compress_prompt_body, grid=(2, b),
        in_specs=[pl.BlockSpec((1, 1, t, LANES), lambda c, i: (c, i, 0, 0)),
                  pl.BlockSpec((1,) + w1bd.shape[1:], lambda c, i: (c, 0, 0, 0)),
                  pl.BlockSpec((1,) + w2dup.shape[1:], lambda c, i: (c, 0, 0)),
                  pl.BlockSpec(kg_dup.shape, lambda c, i: (0, 0))],
        out_specs=pl.BlockSpec((1, 1, nb, 2 * LANES), lambda c, i: (c, i, 0, 0)),
        out_shape=jax.ShapeDtypeStruct((2, b, nb, 2 * LANES), F32),
        compiler_params=pltpu.CompilerParams(dimension_semantics=("arbitrary", "arbitrary"),
                                             vmem_limit_bytes=_vmem_limit(32 << 20)),
        name="compress_prompt",
    )(kvc, w1bd, w2dup, kg_dup)


def _row_max_update(sc_scr, nm_scr, m_scr, mn_scr, ncol, tq):
    rb = ROW_BLOCK
    for i in range(GROUP * tq // rb):
        rows = slice(i * rb, (i + 1) * rb)
        qrows = slice((i * rb) % tq, (i * rb) % tq + rb)
        mx = jnp.full((rb, LANES), NEG, F32)
        for j in range(ncol):
            cols = slice(j * LANES, (j + 1) * LANES)
            v = sc_scr[rows, cols]
            if nm_scr is not None:
                v = v + nm_scr[qrows, cols]
                sc_scr[rows, cols] = v
            mx = jnp.maximum(mx, v)
        mn_scr[rows, :] = jnp.maximum(m_scr[rows, :], jnp.max(mx, axis=-1, keepdims=True))


def _softmax_update(sc_scr, p_scr, m_scr, mn_scr, l_scr, a_scr, ncol, tq):
    rb = ROW_BLOCK
    for i in range(GROUP * tq // rb):
        rows = slice(i * rb, (i + 1) * rb)
        mn = mn_scr[rows, :]
        tot = jnp.zeros((rb, LANES), F32)
        for j in range(ncol):
            cols = slice(j * LANES, (j + 1) * LANES)
            p = jnp.exp(sc_scr[rows, cols] - mn)
            tot = tot + p
            p_scr[rows, cols] = p.astype(BF16)
        alpha = jnp.exp(m_scr[rows, :] - mn)
        l_scr[rows, :] = alpha * l_scr[rows, :] + jnp.sum(tot, axis=-1, keepdims=True)
        a_scr[rows, :] = alpha
        m_scr[rows, :] = mn


def _nsa_prompt_body(q_ref, gates_ref, cmp_ref, att_ref, kfeat_ref, qfeat_ref, gexp_ref, oa_ref,
                     sc_scr, sc2_scr, nm_scr, p_scr, m_scr, mn_scr, l_scr, a_scr, acc_scr, *, seq, topk):
    tq = NSA_Q_ROWS
    tk = min(SEL_KEYS, seq)
    nb = cmp_ref.shape[2]
    t0 = pl.program_id(1) * tq
    wk = min(WINDOW + tq, seq)
    ws = pl.multiple_of(jnp.clip(t0 - WINDOW, 0, seq - wk), LANES)
    q = q_ref[0]
    d = HEAD_DIM
    groups = range(N_KV_HEADS)
    def reset(stats):
        for g in groups:
            m_scr[g] = jnp.full(m_scr.shape[1:], NEG, F32)
            for ref in stats:
                ref[g] = jnp.zeros(ref.shape[1:], F32)

    sc_bufs = (sc_scr, sc2_scr)

    def logits(buf, g, qa, kt_, kf, width):
        sc_bufs[buf][g, :, 0:width] = _dot(qa, jnp.concatenate([kt_, kt_, kf], axis=0))

    def softmax_tiles(buf, ncol, masked):
        sc = sc_bufs[buf]
        for g in groups:
            _row_max_update(sc.at[g], nm_scr if masked else None, m_scr.at[g], mn_scr.at[g], ncol, tq)
        for g in groups:
            _softmax_update(sc.at[g], p_scr.at[g], m_scr.at[g], mn_scr.at[g], l_scr.at[g], a_scr.at[g], ncol, tq)

    qst = [_stack_group_queries(q, g) for g in groups]
    qwin = [jnp.concatenate([qst[g], qfeat_ref[g].astype(BF16)], axis=1) for g in groups]

    kcs = [cmp_ref[0, 0][:, g * LANES:(g + 1) * LANES].astype(BF16) for g in groups]
    vcs = [cmp_ref[1, 0][:, g * LANES:(g + 1) * LANES].astype(BF16) for g in groups]
    st = [_dot_nt(kcs[g], qst[g]) for g in groups]

    reset((l_scr,))
    kfw = kfeat_ref[:, pl.ds(ws, wk)]
    ddw = (t0 - ws) + lax.broadcasted_iota(jnp.int32, (tq, wk), 0) - lax.broadcasted_iota(jnp.int32, (tq, wk), 1)
    nm_scr[:, 0:wk] = jnp.where((ddw >= 0) & (ddw < WINDOW), 0.0, NEG)
    for g in groups:
        logits(0, g, qwin[g], att_ref[0, 2 * KV_WIDTH + g * d:2 * KV_WIDTH + (g + 1) * d, pl.ds(ws, wk)], kfw, wk)

    blk = lax.broadcasted_iota(jnp.int32, (nb, GROUP * tq), 0)
    colq = lax.broadcasted_iota(jnp.int32, (1, GROUP * tq), 1)
    dist = (t0 + (colq & (tq - 1)) - (blk * CMP_BLOCK + (CMP_BLOCK - 1))).astype(F32)
    valid = dist >= 0.0
    ps = []
    for g in groups:
        s = jnp.where(valid, st[g] - _row_slopes(g, colq >> _log2(tq)) * dist, NEG)
        e = jnp.exp(s - jnp.max(s, axis=0, keepdims=True))
        ps.append(jnp.where(valid, e / jnp.sum(e, axis=0, keepdims=True), 0.0))
    o_cmp = [_unstack_group(_dot_tn(ps[g].astype(BF16), vcs[g]), tq) for g in groups]

    softmax_tiles(0, wk // LANES, True)
    o_win = []
    for g in groups:
        vw = att_ref[0, 3 * KV_WIDTH + g * d:3 * KV_WIDTH + (g + 1) * d, pl.ds(ws, wk)]
        o_win.append(_unstack_group(_dot_nt(p_scr[g, :, 0:wk], _twice(vw)) / l_scr[g], tq))

    bj = lax.broadcasted_iota(jnp.int32, (nb, tq), 0)
    cur = (t0 + lax.broadcasted_iota(jnp.int32, (nb, tq), 1)) >> _log2(SEL_BLOCK)
    force = (bj == 0) | (bj == cur)
    qsel = []
    for g in groups:
        p = ps[g]
        imp = p[:, 0:tq] + p[:, tq:2 * tq] + p[:, 2 * tq:3 * tq] + p[:, 3 * tq:4 * tq]
        score = jnp.where(bj <= cur, jnp.where(force, FORCED_SCORE, imp), -1.0)
        rank = _block_ranks(score, bj, [(i, i) for i in range(nb)])
        mneg = jnp.where((rank < topk) & (score >= 0.0), 0.0, NEG)
        mtok = jnp.concatenate([mneg, jnp.zeros((LANES - nb, tq), F32)], axis=0).T
        qsel.append(jnp.concatenate(
            [qst[g], (qfeat_ref[g] + jnp.concatenate([mtok] * GROUP, axis=0)).astype(BF16)], axis=1))

    reset((l_scr, acc_scr))
    n_kt = (t0 + tq - 1) // tk + 1

    def sel_logits(kt, buf):
        s0 = pl.multiple_of(kt * tk, tk)
        kf = kfeat_ref[:, pl.ds(s0, tk)]
        for g in groups:
            logits(buf, g, qsel[g], att_ref[0, g * d:(g + 1) * d, pl.ds(s0, tk)], kf, tk)

    def sel_update(kt, buf, causal):
        s0 = pl.multiple_of(kt * tk, tk)
        if causal:
            dd = (t0 - s0) + lax.broadcasted_iota(jnp.int32, (tq, tk), 0) - lax.broadcasted_iota(jnp.int32, (tq, tk), 1)
            nm_scr[:, 0:tk] = jnp.where(dd >= 0, 0.0, NEG)
        softmax_tiles(buf, tk // LANES, causal)
        for g in groups:
            vt_ = att_ref[0, KV_WIDTH + g * d:KV_WIDTH + (g + 1) * d, pl.ds(s0, tk)]
            acc_scr[g] = a_scr[g] * acc_scr[g] + _dot_nt(p_scr[g, :, 0:tk], _twice(vt_))

    sel_logits(0, 0)

    def full_tile(kt, carry):
        for par in (0, 1):
            @pl.when((kt & 1) == par)
            def _(par=par):
                sel_logits(kt + 1, 1 - par)
                sel_update(kt, par, False)
        return carry

    lax.fori_loop(0, n_kt - 1, full_tile, 0)
    for par in (0, 1):
        @pl.when(((n_kt - 1) & 1) == par)
        def _(par=par):
            sel_update(n_kt - 1, par, True)
    o_sel = [_unstack_group(acc_scr[g] / l_scr[g], tq) for g in groups]

    ge = _expand_gates(gates_ref[0], gexp_ref)
    oa_ref[0] = (ge[0] * jnp.concatenate(o_cmp, axis=1) + ge[1] * jnp.concatenate(o_sel, axis=1)
                 + ge[2] * jnp.concatenate(o_win, axis=1))


def _nsa_prompt(q, gates, cmpkv, att, k_feat, q_feat, gexp):
    b, t, _ = q.shape
    tq = NSA_Q_ROWS
    nb = t // CMP_BLOCK
    assert nb <= HEAD_DIM, "the block mask uses 64 feature lanes"
    cw = max(min(SEL_KEYS, t), min(WINDOW + tq, t))
    body = functools.partial(_nsa_prompt_body, seq=t, topk=min(SEL_TOPK, nb))
    ng = N_KV_HEADS
    stat = pltpu.VMEM((ng, GROUP * tq, LANES), F32)
    return pl.pallas_call(
        body, grid=(b, t // tq),
        in_specs=[pl.BlockSpec((1, tq, ATTN_WIDTH), lambda i, j: (i, j, 0)),
                  pl.BlockSpec((1, tq, LANES), lambda i, j: (i, j, 0)),
                  pl.BlockSpec((2, 1, nb, 2 * LANES), lambda i, j: (0, i, 0, 0)),
                  pl.BlockSpec((1, 4 * KV_WIDTH, t), lambda i, j: (i, 0, 0)),
                  pl.BlockSpec(k_feat.shape, lambda i, j: (0, 0)),
                  pl.BlockSpec(q_feat.shape, lambda i, j: (0, 0, 0)),
                  pl.BlockSpec(gexp.shape, lambda i, j: (0, 0, 0))],
        out_specs=pl.BlockSpec((1, tq, ATTN_WIDTH), lambda i, j: (i, j, 0)),
        out_shape=jax.ShapeDtypeStruct((b, t, ATTN_WIDTH), F32),
        scratch_shapes=[pltpu.VMEM((ng, GROUP * tq, cw), F32), pltpu.VMEM((ng, GROUP * tq, cw), F32),
                        pltpu.VMEM((tq, cw), F32),
                        pltpu.VMEM((ng, GROUP * tq, cw), BF16), stat, stat, stat, stat, stat],
        compiler_params=pltpu.CompilerParams(dimension_semantics=("parallel", "parallel"),
                                             vmem_limit_bytes=_vmem_limit(48 << 20)),
        name="nsa_prompt",
    )(q, gates, cmpkv, att, k_feat, q_feat, gexp)


def _split3(x):
    hi = x.astype(BF16)
    r1 = x - hi.astype(F32)
    mid = r1.astype(BF16)
    lo = (r1 - mid.astype(F32)).astype(BF16)
    return hi, mid, lo


def _hgrn_prompt_body(rq_ref, f_ref, rv_ref, ltri_ref, bd_ref, ind_ref, indt_ref, o_ref, st_ref,
                      s_scr, cum_scr, k_scr, prod_scr):
    c16 = RNN_CHUNK
    nbat = rq_ref.shape[0]

    @pl.when(pl.program_id(0) == 0)
    def _():
        s_scr[...] = jnp.zeros(s_scr.shape, F32)

    ltri = ltri_ref[...]
    for bi in range(nbat):
        f = f_ref[bi]
        hi, mid, lo = _split3(jnp.log(f))
        cum = (_dot(ltri, hi) + _dot(ltri, mid) + _dot(ltri, lo)) * LOG2E
        cum_scr[bi] = cum
        k_scr[bi] = cum - jnp.log2(1.0 - f)

    si = lax.broadcasted_iota(jnp.int32, (c16, c16, RNN_WIDTH), 0)
    ti = lax.broadcasted_iota(jnp.int32, (c16, c16, RNN_WIDTH), 1)
    causal = si <= ti
    npair = RNN_WIDTH // LANES

    def pair_products(c, buf, bi):
        r0 = pl.multiple_of(c * c16, c16)
        cc = cum_scr[bi, pl.ds(r0, c16), :]
        qc = rq_ref[bi, pl.ds(r0, c16), :]
        lk = k_scr[bi, pl.ds(r0, c16), :]
        dec = jnp.exp2(jnp.where(causal, cc[None, :, :] - lk[:, None, :], NEG))
        prod_scr[buf, bi] = (qc[None, :, :] * dec).reshape(c16 * c16, RNN_WIDTH).astype(BF16)

    def recurrence(c, buf, nxt):
        r0 = pl.multiple_of(c * c16, c16)
        rows = range(nbat)
        a1 = [_dot(prod_scr[buf, bi], ind_ref[...]) for bi in rows]
        if nxt is not None:
            for bi in rows:
                pair_products(nxt, 1 - buf, bi)
        cc = [cum_scr[bi, pl.ds(r0, c16), :] for bi in rows]
        last = [x[c16 - 1:c16, :] for x in cc]
        o_int = []
        for bi in rows:
            qd = (rq_ref[bi, pl.ds(r0, c16), :] * jnp.exp2(cc[bi])).astype(BF16)
            o_int.append(jnp.concatenate([_dot_nt(qd[:, p * LANES:(p + 1) * LANES], s_scr[bi, p].astype(BF16))
                                          for p in range(npair)], axis=1))
        a2 = [_dot(a1[bi].astype(BF16), indt_ref[...]).reshape(c16, c16, RNN_WIDTH) for bi in rows]
        for bi in rows:
            kd = jnp.exp2(last[bi] - k_scr[bi, pl.ds(r0, c16), :]).astype(BF16)
            vb = rv_ref[bi, pl.ds(r0, c16), :].astype(BF16)
            dl = jnp.exp2(last[bi])
            for p in range(npair):
                u = _dot_tn(vb[:, p * LANES:(p + 1) * LANES], kd[:, p * LANES:(p + 1) * LANES])
                s_scr[bi, p] = s_scr[bi, p] * dl[:, p * LANES:(p + 1) * LANES] + u * bd_ref[...]
        for bi in rows:
            vc = rv_ref[bi, pl.ds(r0, c16), :]
            o_ref[bi, pl.ds(r0, c16), :] = o_int[bi] + jnp.sum(a2[bi] * vc[:, None, :], axis=0)

    nchunk = rq_ref.shape[1] // c16
    assert nchunk % 2 == 0 and nchunk >= 4
    for bi in range(nbat):
        pair_products(0, 0, bi)

    def two_chunks(j, carry):
        for sub in (0, 1):
            recurrence(2 * j + sub, sub, 2 * j + sub + 1)
        return carry

    lax.fori_loop(0, nchunk // 2 - 1, two_chunks, 0)
    recurrence(nchunk - 2, 0, nchunk - 1)
    recurrence(nchunk - 1, 1, None)
    st_ref[...] = s_scr[...]


def _hgrn_prompt(rq, f, rv, ltri, bd, ind):
    b, t, w = rq.shape
    tc = RNN_ROWS
    npair = w // LANES
    row = lambda j: (0, j, 0)
    const = lambda j: (0, 0)
    indt = ind.T
    return pl.pallas_call(
        _hgrn_prompt_body, grid=(t // tc,),
        in_specs=[pl.BlockSpec((b, tc, w), row), pl.BlockSpec((b, tc, w), row), pl.BlockSpec((b, tc, w), row),
                  pl.BlockSpec(ltri.shape, const), pl.BlockSpec(bd.shape, const),
                  pl.BlockSpec(ind.shape, const), pl.BlockSpec(indt.shape, const)],
        out_specs=(pl.BlockSpec((b, tc, w), row), pl.BlockSpec((b, npair, LANES, LANES), lambda j: (0, 0, 0, 0))),
        out_shape=(jax.ShapeDtypeStruct((b, t, w), F32), jax.ShapeDtypeStruct((b, npair, LANES, LANES), F32)),
        scratch_shapes=[pltpu.VMEM((b, npair, LANES, LANES), F32), pltpu.VMEM((b, tc, w), F32),
                        pltpu.VMEM((b, tc, w), F32), pltpu.VMEM((2, b, RNN_CHUNK * RNN_CHUNK, w), BF16)],
        compiler_params=pltpu.CompilerParams(dimension_semantics=("arbitrary",),
                                             vmem_limit_bytes=_vmem_limit(40 << 20)),
        name="hgrn_prompt",
    )(rq, f, rv, ltri, bd, ind, indt)


def _finish_body(x_ref, oa_ref, orn_ref, rgs_ref, ag_ref, rg_ref, wout_ref, lnm_ref, wup_ref, wdn_ref, y_ref, hn_scr):
    @pl.when(pl.program_id(1) == 0)
    def _():
        oa = oa_ref[...]
        orn = orn_ref[...]
        a_n = oa * lax.rsqrt(_head_mean_sq(oa) + EPS) * ag_ref[...]
        r_n = orn * lax.rsqrt(_head_mean_sq(orn) + EPS) * rg_ref[...] * rgs_ref[...]
        h = (x_ref[...] + _dot(a_n.astype(BF16), wout_ref[0:ATTN_WIDTH, :])
             + _dot(r_n.astype(BF16), wout_ref[ATTN_WIDTH:ATTN_WIDTH + RNN_WIDTH, :]))
        y_ref[...] = h
        hn_scr[...] = (_row_rms(h) * lnm_ref[...]).astype(BF16)

    u = jnp.maximum(_dot(hn_scr[...], wup_ref[...]), 0.0)
    y_ref[...] += _dot((u * u).astype(BF16), wdn_ref[...])


def _finish(x, oa, orn, rgs, ag, rg, wout, lnm, wup, wdn):
    n, dm = x.shape
    tm = FINISH_ROWS
    dff = wup.shape[1]
    row = lambda i, j: (i, 0)
    const = lambda i, j: (0, 0)
    return pl.pallas_call(
        _finish_body, grid=(n // tm, dff // FF_CHUNK),
        in_specs=[pl.BlockSpec((tm, dm), row), pl.BlockSpec((tm, ATTN_WIDTH), row),
                  pl.BlockSpec((tm, RNN_WIDTH), row), pl.BlockSpec((tm, RNN_WIDTH), row),
                  pl.BlockSpec(ag.shape, const), pl.BlockSpec(rg.shape, const),
                  pl.BlockSpec(wout.shape, const), pl.BlockSpec(lnm.shape, const),
                  pl.BlockSpec((dm, FF_CHUNK), lambda i, j: (0, j)),
                  pl.BlockSpec((FF_CHUNK, dm), lambda i, j: (j, 0))],
        out_specs=pl.BlockSpec((tm, dm), row),
        out_shape=jax.ShapeDtypeStruct((n, dm), F32),
        scratch_shapes=[pltpu.VMEM((tm, dm), BF16)],
        compiler_params=pltpu.CompilerParams(dimension_semantics=("parallel", "arbitrary"),
                                             vmem_limit_bytes=_vmem_limit(48 << 20)),
        name="finish",
    )(x, oa, orn, rgs, ag, rg, wout, lnm, wup, wdn)


def _proj_sample_body(xbt_ref, xtb_ref, ln_ref, wtok_ref, wtb_ref, wbt_ref, qg_ref, lbl_ref, gsel_ref, gwin_ref,
                      q_ref, rgs_ref, gates_ref, ztb_ref, zbt_ref):
    xb = (_row_rms(xbt_ref[...]) * ln_ref[...]).astype(BF16)
    xt = (_row_rms(xtb_ref[...]) * ln_ref[...]).astype(BF16)
    zq = _dot(xb, wtok_ref[:, 0:ATTN_WIDTH])
    q_ref[...] = (zq * lax.rsqrt(_head_mean_sq(zq) + EPS) * qg_ref[...]).astype(BF16)
    rgs_ref[...] = _silu(_dot(xb, wtok_ref[:, ATTN_WIDTH:ATTN_WIDTH + RNN_WIDTH]))
    gates_ref[...] = jax.nn.sigmoid(_dot(xb, wtok_ref[:, ATTN_WIDTH + RNN_WIDTH:ATTN_WIDTH + RNN_WIDTH + LANES]))

    d = HEAD_DIM
    zt = _dot_nt(wtb_ref[...], xt)
    ztb_ref[0:256] = zt[0:256]
    for g in range(N_KV_HEADS):
        ztb_ref[256 + g * d:256 + (g + 1) * d] = _col_head_norm(zt[256 + g * d:256 + (g + 1) * d], gsel_ref[...])
    ztb_ref[384:512] = zt[384:512]
    ztb_ref[512:1024] = _silu(zt[512:1024])
    lb = _lower_bound(lbl_ref[...], 0)[0]
    ztb_ref[1024:1536] = lb + (1.0 - lb) * jax.nn.sigmoid(zt[1024:1536])
    ztb_ref[1536:2048] = zt[1536:2048]

    zb = _dot_nt(wbt_ref[...], xb)
    for g in range(N_KV_HEADS):
        zbt_ref[g * d:(g + 1) * d] = _col_head_norm(zb[g * d:(g + 1) * d], gsel_ref[...])
        zbt_ref[256 + g * d:256 + (g + 1) * d] = _col_head_norm(zb[256 + g * d:256 + (g + 1) * d], gwin_ref[...])
    zbt_ref[128:256] = zb[128:256]
    zbt_ref[384:512] = zb[384:512]


def _proj_sample(xbt, xtb, ln, wtok, wtb, wbt, qg, lblt, gsel, gwin):
    n, dm = xbt.shape
    tm = PROJ_ROWS
    row = lambda i: (i, 0)
    col = lambda i: (0, i)
    const = lambda i: (0, 0)
    return pl.pallas_call(
        _proj_sample_body, grid=(n // tm,),
        in_specs=[pl.BlockSpec((tm, dm), row), pl.BlockSpec((tm, dm), row), pl.BlockSpec(ln.shape, const),
                  pl.BlockSpec(wtok.shape, const), pl.BlockSpec(wtb.shape, const), pl.BlockSpec(wbt.shape, const),
                  pl.BlockSpec(qg.shape, const), pl.BlockSpec(lblt.shape, lambda i: (0, 0, 0)),
                  pl.BlockSpec(gsel.shape, const), pl.BlockSpec(gwin.shape, const)],
        out_specs=(pl.BlockSpec((tm, ATTN_WIDTH), row), pl.BlockSpec((tm, RNN_WIDTH), row),
                   pl.BlockSpec((tm, LANES), row), pl.BlockSpec((wtb.shape[0], tm), col),
                   pl.BlockSpec((wbt.shape[0], tm), col)),
        out_shape=(jax.ShapeDtypeStruct((n, ATTN_WIDTH), BF16), jax.ShapeDtypeStruct((n, RNN_WIDTH), F32),
                   jax.ShapeDtypeStruct((n, LANES), F32), jax.ShapeDtypeStruct((wtb.shape[0], n), F32),
                   jax.ShapeDtypeStruct((wbt.shape[0], n), F32)),
        compiler_params=pltpu.CompilerParams(dimension_semantics=("parallel",),
                                             vmem_limit_bytes=_vmem_limit(40 << 20)),
        name="proj_sample",
    )(xbt, xtb, ln, wtok, wtb, wbt, qg, lblt, gsel, gwin)


PAGE_ROWS = 4 * KV_WIDTH
SAMPLE_BATCH_PER_STEP = 4
WINDOW_BATCH_PER_STEP = 4


def _nsa_sample_body(pt_ref, cache_ref, q_ref, nkv_ref, w1_ref, pe_ref, w2_ref, kg_ref, e_ref,
                     ocs_ref, cbuf, sbuf, lhs_scr, kk_scr, vv_scr, sem, *, n_pages, past_len, dec_t, topk):
    step_id = pl.program_id(0)
    nsteps = pl.num_programs(0)
    slot = step_id % 2
    d = HEAD_DIM
    nblk = 2 * n_pages
    half_rows = PAGE_ROWS // 2
    bps = q_ref.shape[0]

    def page_copies(st, sl, bi, j):
        pg = pt_ref[(st * bps + bi) * n_pages + j]
        return (pltpu.make_async_copy(cache_ref.at[pg, pl.ds(0, half_rows)], cbuf.at[sl, bi, :, j, :], sem.at[sl]),
                pltpu.make_async_copy(cache_ref.at[pg, pl.ds(half_rows, half_rows)],
                                      sbuf.at[sl, bi, pl.ds(j * half_rows, half_rows)], sem.at[sl]))

    def all_copies(st, sl):
        return [cp for bi in range(bps) for j in range(n_pages) for cp in page_copies(st, sl, bi, j)]

    @pl.when(step_id == 0)
    def _():
        for cp in all_copies(0, 0):
            cp.start()

    @pl.when(step_id + 1 < nsteps)
    def _():
        for cp in all_copies(step_id + 1, 1 - slot):
            cp.start()

    for cp in all_copies(step_id, slot):
        cp.wait()

    low = _low_half((n_pages, LANES))
    rows_b = 4 * n_pages

    def compress(c):
        for dd in range(d):
            rows = []
            for bi in range(bps):
                for g in range(N_KV_HEADS):
                    xg = cbuf[slot, bi, (c * N_KV_HEADS + g) * d + dd]
                    xg = xg + pe_ref[c, dd]
                    rows += [jnp.where(low, xg, 0.0), jnp.where(low, 0.0, xg)]
            lhs_scr[:, dd * LANES:(dd + 1) * LANES] = jnp.concatenate(rows, axis=0).astype(BF16)
        acc = _dot(lhs_scr[...], w1_ref[c])
        return _dot(_silu(acc).astype(BF16), w2_ref[c])

    kc_all = _row_rms(compress(0)) * kg_ref[...]
    vc_all = compress(1)

    nq = GROUP * dec_t
    colq = lax.broadcasted_iota(jnp.int32, (1, nq), 1)
    rowq = lax.broadcasted_iota(jnp.int32, (nq, 1), 0)
    cur_blk = past_len // SEL_BLOCK
    chains = [(bi, g) for bi in range(bps) for g in range(N_KV_HEADS)]
    nch = len(chains)
    step = rowq & (dec_t - 1)
    qpos = past_len + (colq & (dec_t - 1))
    rho_q = lax.broadcasted_iota(jnp.int32, (nblk, 1), 0)
    end = (2 * (rho_q & (n_pages - 1)) + (rho_q >> _log2(n_pages))) * CMP_BLOCK + (CMP_BLOCK - 1)
    dist = (qpos - end).astype(F32)
    valid = dist >= 0.0
    kpos = lax.broadcasted_iota(jnp.int32, (1, past_len), 1)
    dpast = (past_len + step - kpos).astype(F32)
    lane = lax.broadcasted_iota(jnp.int32, (1, LANES), 1)
    per_tile = LANES // dec_t
    offs = [((step_id * bps + bi) & (per_tile - 1)) * dec_t for bi in range(bps)]
    dnews = [(step - (lane - off)).astype(F32) for off in offs]
    oknew = [(lane >= off) & (lane < off + dec_t) for off in offs]

    qst = [_stack_group_queries(q_ref[bi], g) for bi, g in chains]
    kcs = [kc_all[bi * rows_b + g * nblk:bi * rows_b + (g + 1) * nblk].astype(BF16) for bi, g in chains]
    vcs = [vc_all[bi * rows_b + g * nblk:bi * rows_b + (g + 1) * nblk].astype(BF16) for bi, g in chains]
    st = [_dot_nt(kcs[ch], qst[ch]) for ch in range(nch)]
    for ch, (bi, g) in enumerate(chains):
        for pg in range(n_pages):
            kt_ = sbuf[slot, bi, pl.ds(pg * half_rows + g * d, d), :].astype(BF16)
            vt_ = sbuf[slot, bi, pl.ds(pg * half_rows + (N_KV_HEADS + g) * d, d), :].astype(BF16)
            kk_scr[ch, 0:d, pg * PAGE_SIZE:(pg + 1) * PAGE_SIZE] = kt_
            kk_scr[ch, d:2 * d, pg * PAGE_SIZE:(pg + 1) * PAGE_SIZE] = kt_
            vv_scr[ch, 0:d, pg * PAGE_SIZE:(pg + 1) * PAGE_SIZE] = vt_
            vv_scr[ch, d:2 * d, pg * PAGE_SIZE:(pg + 1) * PAGE_SIZE] = vt_
    sp = [_dot(qst[ch], kk_scr[ch]) for ch in range(nch)]
    nk = [_twice(nkv_ref[g * d:(g + 1) * d, :].astype(BF16)) for g in range(N_KV_HEADS)]
    nv = [_twice(nkv_ref[KV_WIDTH + g * d:KV_WIDTH + (g + 1) * d, :].astype(BF16)) for g in range(N_KV_HEADS)]
    sn = [_dot(qst[ch], nk[g]) for ch, (bi, g) in enumerate(chains)]
    ps = []
    for ch, (bi, g) in enumerate(chains):
        s = jnp.where(valid, st[ch] - _row_slopes(g, colq >> _log2(dec_t)) * dist, NEG)
        e = jnp.exp(s - jnp.max(s, axis=0, keepdims=True))
        ps.append(jnp.where(valid, e / jnp.sum(e, axis=0, keepdims=True), 0.0))
    o_cmp = [_unstack_group(_dot_tn(ps[ch].astype(BF16), vcs[ch]), dec_t) for ch in range(nch)]
    imps = []
    for ch in range(nch):
        p = ps[ch]
        imp = p[:, 0:dec_t]
        for r in range(1, GROUP):
            imp = imp + p[:, r * dec_t:(r + 1) * dec_t]
        imps.append(imp)
    imp_all = jnp.concatenate(imps, axis=1)
    rho_a = lax.broadcasted_iota(jnp.int32, imp_all.shape, 0)
    bid_a = 2 * (rho_a & (n_pages - 1)) + (rho_a >> _log2(n_pages))
    score = jnp.where((bid_a == 0) | (bid_a == cur_blk), FORCED_SCORE, imp_all)
    rank = _block_ranks(score, bid_a, [(i, 2 * (i % n_pages) + i // n_pages) for i in range(nblk)])
    rank = rank + jnp.where(bid_a > cur_blk, (FORCED_SCORE >= score).astype(F32), (FORCED_SCORE > score).astype(F32))
    msel = jnp.where((rank < topk) & (score >= 0.0), 1.0, 0.0).astype(BF16)
    mk_all = _dot_tn(msel, e_ref[...])
    pp, pn, den = [], [], []
    for ch, (bi, g) in enumerate(chains):
        okp = jnp.concatenate([mk_all[ch * dec_t:(ch + 1) * dec_t]] * GROUP, axis=0) > 0.5
        slope = _row_slopes(g, rowq >> _log2(dec_t))
        lgp = jnp.where(okp & (dpast >= 0.0), sp[ch] - slope * dpast, NEG)
        lgn = jnp.where(oknew[bi] & (dnews[bi] >= 0.0), sn[ch] - slope * dnews[bi], NEG)
        m = jnp.maximum(jnp.max(lgp, axis=-1, keepdims=True), jnp.max(lgn, axis=-1, keepdims=True))
        pp.append(jnp.exp(lgp - m))
        pn.append(jnp.exp(lgn - m))
        den.append(jnp.sum(pp[ch], axis=-1, keepdims=True) + jnp.sum(pn[ch], axis=-1, keepdims=True))
    o_sel = [_unstack_group((_dot_nt(pp[ch].astype(BF16), vv_scr[ch]) + _dot_nt(pn[ch].astype(BF16), nv[g]))
                            / den[ch], dec_t) for ch, (bi, g) in enumerate(chains)]

    for bi in range(bps):
        ocs_ref[bi, 0] = jnp.concatenate(o_cmp[bi * N_KV_HEADS:(bi + 1) * N_KV_HEADS], axis=1)
        ocs_ref[bi, 1] = jnp.concatenate(o_sel[bi * N_KV_HEADS:(bi + 1) * N_KV_HEADS], axis=1)


def _nsa_sample(page_flat, cache, q, nkv, w1r, pe_t, w2dup, kg_dup, e_perm, past_len, topk):
    nbatch, dec_t, _ = q.shape
    n_pages = past_len // PAGE_SIZE
    bps = SAMPLE_BATCH_PER_STEP
    nch = bps * N_KV_HEADS
    body = functools.partial(_nsa_sample_body, n_pages=n_pages, past_len=past_len, dec_t=dec_t, topk=topk)
    grid_spec = pltpu.PrefetchScalarGridSpec(
        num_scalar_prefetch=1, grid=(nbatch // bps,),
        in_specs=[pl.BlockSpec(memory_space=pl.ANY),
                  pl.BlockSpec((bps, dec_t, ATTN_WIDTH), lambda i, pt: (i, 0, 0)),
                  pl.BlockSpec((2 * KV_WIDTH, LANES), lambda i, pt: (0, i * bps * dec_t // LANES)),
                  pl.BlockSpec(w1r.shape, lambda i, pt: (0, 0, 0)),
                  pl.BlockSpec(pe_t.shape, lambda i, pt: (0, 0, 0, 0)),
                  pl.BlockSpec(w2dup.shape, lambda i, pt: (0, 0, 0)),
                  pl.BlockSpec(kg_dup.shape, lambda i, pt: (0, 0)),
                  pl.BlockSpec(e_perm.shape, lambda i, pt: (0, 0))],
        out_specs=pl.BlockSpec((bps, 2, dec_t, ATTN_WIDTH), lambda i, pt: (i, 0, 0, 0)),
        scratch_shapes=[pltpu.VMEM((2, bps, PAGE_ROWS // 2, n_pages, PAGE_SIZE), F32),
                        pltpu.VMEM((2, bps, n_pages * PAGE_ROWS // 2, PAGE_SIZE), F32),
                        pltpu.VMEM((bps * 4 * n_pages, HEAD_DIM * LANES), BF16),
                        pltpu.VMEM((nch, 2 * HEAD_DIM, past_len), BF16),
                        pltpu.VMEM((nch, 2 * HEAD_DIM, past_len), BF16),
                        pltpu.SemaphoreType.DMA((2,))])
    return pl.pallas_call(
        body, grid_spec=grid_spec,
        out_shape=jax.ShapeDtypeStruct((nbatch, 2, dec_t, ATTN_WIDTH), F32),
        compiler_params=pltpu.CompilerParams(dimension_semantics=("arbitrary",),
                                             vmem_limit_bytes=_vmem_limit(56 << 20)),
        name="nsa_sample",
    )(page_flat, cache, q, nkv, w1r, pe_t, w2dup, kg_dup, e_perm)


def _win_sample_body(win_ref, q_ref, nw_ref, ocs_ref, gates_ref, gexp_ref, oa_ref, wout_ref, *, past_len, dec_t):
    d = HEAD_DIM
    wbuf = win_ref.shape[2]
    nq = GROUP * dec_t
    rowq = lax.broadcasted_iota(jnp.int32, (nq, 1), 0)
    step = rowq & (dec_t - 1)
    kpos = past_len - wbuf + lax.broadcasted_iota(jnp.int32, (1, wbuf), 1)
    dpast = (past_len + step - kpos).astype(F32)
    okp = (dpast >= 0.0) & (dpast < WINDOW)
    nbat = win_ref.shape[0]
    lane = lax.broadcasted_iota(jnp.int32, (1, LANES), 1)
    per_tile = LANES // dec_t
    offs = [((pl.program_id(0) * nbat + bi) & (per_tile - 1)) * dec_t for bi in range(nbat)]
    dnews = [(step - (lane - off)).astype(F32) for off in offs]
    okns = [(lane >= off) & (lane < off + dec_t) & (dn >= 0.0) & (dn < WINDOW) for off, dn in zip(offs, dnews)]
    chains = [(bi, g) for bi in range(nbat) for g in range(N_KV_HEADS)]
    qst = [_stack_group_queries(q_ref[bi], g) for bi, g in chains]
    sp = [_dot(qst[ch], _twice(win_ref[bi, g * d:(g + 1) * d, :].astype(BF16))) for ch, (bi, g) in enumerate(chains)]
    nk = [_twice(nw_ref[g * d:(g + 1) * d, :].astype(BF16)) for g in range(N_KV_HEADS)]
    nvs = [_twice(nw_ref[KV_WIDTH + g * d:KV_WIDTH + (g + 1) * d, :].astype(BF16)) for g in range(N_KV_HEADS)]
    sn = [_dot(qst[ch], nk[g]) for ch, (bi, g) in enumerate(chains)]
    pp, pn, den = [], [], []
    for ch, (bi, g) in enumerate(chains):
        slope = _row_slopes(g, rowq >> _log2(dec_t))
        lgp = jnp.where(okp, sp[ch] - slope * dpast, NEG)
        lgn = jnp.where(okns[bi], sn[ch] - slope * dnews[bi], NEG)
        m = jnp.maximum(jnp.max(lgp, axis=-1, keepdims=True), jnp.max(lgn, axis=-1, keepdims=True))
        pp.append(jnp.exp(lgp - m))
        pn.append(jnp.exp(lgn - m))
        den.append(jnp.sum(pp[ch], axis=-1, keepdims=True) + jnp.sum(pn[ch], axis=-1, keepdims=True))
    o_win = []
    for ch, (bi, g) in enumerate(chains):
        vt_ = win_ref[bi, KV_WIDTH + g * d:KV_WIDTH + (g + 1) * d, :].astype(BF16)
        o2 = _dot_nt(pp[ch].astype(BF16), _twice(vt_)) + _dot_nt(pn[ch].astype(BF16), nvs[g])
        o_win.append(_unstack_group(o2 / den[ch], dec_t))
    for bi in range(nbat):
        ge = _expand_gates(gates_ref[bi], gexp_ref)
        oa_ref[bi] = (ge[0] * ocs_ref[bi, 0] + ge[1] * ocs_ref[bi, 1]
                      + ge[2] * jnp.concatenate(o_win[bi * N_KV_HEADS:(bi + 1) * N_KV_HEADS], axis=1))

        rolled = pltpu.roll(win_ref[bi], wbuf - dec_t, 1)
        newr = pltpu.roll(nw_ref[...], (LANES - dec_t - offs[bi]) & (LANES - 1), 1)
        wout_ref[bi, :, 0:wbuf - LANES] = rolled[:, 0:wbuf - LANES]
        wout_ref[bi, :, wbuf - LANES:wbuf] = jnp.where(lane >= LANES - dec_t, newr, rolled[:, wbuf - LANES:wbuf])


def _win_sample(win, q, nw, ocs, gates, gexp, past_len):
    nbatch, feat, wbuf = win.shape
    dec_t = q.shape[1]
    body = functools.partial(_win_sample_body, past_len=past_len, dec_t=dec_t)
    b3 = lambda i: (i, 0, 0)
    bps = WINDOW_BATCH_PER_STEP
    return pl.pallas_call(
        body, grid=(nbatch // bps,),
        in_specs=[pl.BlockSpec((bps, feat, wbuf), b3), pl.BlockSpec((bps, dec_t, ATTN_WIDTH), b3),
                  pl.BlockSpec((feat, LANES), lambda i: (1, i * bps * dec_t // LANES)),
                  pl.BlockSpec((bps, 2, dec_t, ATTN_WIDTH), lambda i: (i, 0, 0, 0)),
                  pl.BlockSpec((bps, dec_t, LANES), b3), pl.BlockSpec(gexp.shape, lambda i: (0, 0, 0))],
        out_specs=(pl.BlockSpec((bps, dec_t, ATTN_WIDTH), b3), pl.BlockSpec((bps, feat, wbuf), b3)),
        out_shape=(jax.ShapeDtypeStruct((nbatch, dec_t, ATTN_WIDTH), F32),
                   jax.ShapeDtypeStruct((nbatch, feat, wbuf), F32)),
        compiler_params=pltpu.CompilerParams(dimension_semantics=("parallel",),
                                             vmem_limit_bytes=_vmem_limit(24 << 20)),
        name="win_sample",
    )(win, q, nw, ocs, gates, gexp)


def _hgrn_sample_body(q_ref, f_ref, v_ref, s_ref, o_ref, so_ref, *, dec_t):
    nb = s_ref.shape[3]
    o_ref[...] = jnp.zeros(o_ref.shape, F32)

    sub = 8

    def per_tile(i, carry):
        r0 = pl.multiple_of(i * sub, sub)
        f_t = [f_ref[pl.ds(r0, sub), pl.ds(t * nb, nb)] for t in range(dec_t)]
        q_t = [q_ref[pl.ds(r0, sub), pl.ds(t * nb, nb)] for t in range(dec_t)]
        for j in range(sub):
            s = s_ref[0, r0 + j]
            for t in range(dec_t):
                cols = pl.ds(t * nb, nb)
                fr = f_t[t][j:j + 1, :]
                s = fr * s + (1.0 - fr) * v_ref[:, cols]
                o_ref[:, cols] = o_ref[:, cols] + s * q_t[t][j:j + 1, :]
            so_ref[0, r0 + j] = s
        return carry

    lax.fori_loop(0, s_ref.shape[1] // sub, per_tile, 0)


def _hgrn_sample(ztb, state, dec_t):
    nh, dk, dv, nb = state.shape
    n = ztb.shape[1]
    body = functools.partial(_hgrn_sample_body, dec_t=dec_t)
    q0, f0, v0 = 512 // dk, 1024 // dk, 1536 // dk
    return pl.pallas_call(
        body, grid=(nh,),
        in_specs=[pl.BlockSpec((dk, n), lambda h: (q0 + h, 0)), pl.BlockSpec((dk, n), lambda h: (f0 + h, 0)),
                  pl.BlockSpec((dv, n), lambda h: (v0 + h, 0)),
                  pl.BlockSpec((1, dk, dv, nb), lambda h: (h, 0, 0, 0))],
        out_specs=(pl.BlockSpec((dv, n), lambda h: (h, 0)), pl.BlockSpec((1, dk, dv, nb), lambda h: (h, 0, 0, 0))),
        out_shape=(jax.ShapeDtypeStruct((nh * dv, n), F32), jax.ShapeDtypeStruct(state.shape, F32)),
        compiler_params=pltpu.CompilerParams(dimension_semantics=("parallel",),
                                             vmem_limit_bytes=_vmem_limit(24 << 20)),
        name="hgrn_sample",
    )(ztb, ztb, ztb, state)


def _gate_expander():
    m = np.zeros((N_BRANCH, LANES, ATTN_WIDTH), np.float32)
    for br in range(N_BRANCH):
        for h in range(N_ATTN_HEADS):
            m[br, h * N_BRANCH + br, h * HEAD_DIM:(h + 1) * HEAD_DIM] = 1.0
    return jnp.asarray(m, BF16)


def _block_expander(block_ids, n_keys):
    key_blk = np.arange(n_keys) // SEL_BLOCK
    return jnp.asarray((np.asarray(block_ids)[:, None] == key_blk[None, :]).astype(np.float32), BF16)


def _key_features(n_keys):
    s = np.arange(n_keys)
    m = np.zeros((LANES, n_keys), np.float32)
    m[0:HEAD_DIM] = (np.arange(HEAD_DIM)[:, None] == (s // SEL_BLOCK)[None, :])
    m[HEAD_DIM] = s // SEL_BLOCK
    m[HEAD_DIM + 1] = s % SEL_BLOCK
    return jnp.asarray(m, BF16)


def _query_slope_features(rows):
    m = np.zeros((N_KV_HEADS, GROUP * rows, LANES), np.float32)
    for g in range(N_KV_HEADS):
        for r in range(GROUP):
            m[g, r * rows:(r + 1) * rows, HEAD_DIM] = SLOPES[g][r] * SEL_BLOCK
            m[g, r * rows:(r + 1) * rows, HEAD_DIM + 1] = SLOPES[g][r]
    return jnp.asarray(m, F32)


def _chunk_lower_tri(n, c):
    i = np.arange(n)
    return jnp.asarray(((i[:, None] // c == i[None, :] // c) & (i[None, :] <= i[:, None])).astype(np.float32), BF16)


def _head_indicator():
    m = np.zeros((RNN_WIDTH, LANES), np.float32)
    m[np.arange(RNN_WIDTH), np.arange(RNN_WIDTH) // RNN_DK] = 1.0
    return jnp.asarray(m, BF16)


def _head_block_diag(n):
    i = np.arange(n)
    return jnp.asarray((i[:, None] // RNN_DV == i[None, :] // RNN_DK).astype(np.float32), F32)


def kernel(x_prompt, x_sample, cache_kv, cache_win, state_rnn, page_table, ln_mix, w_in, q_norm, k_norm, cmp_pe,
           cmp_w1, cmp_w2, attn_out_norm, rnn_lb_logits, rnn_out_norm, w_out, ln_mlp, w_up, w_down):
    assert w_in.shape[0] == 1, "single layer"
    b, t, dm = x_prompt.shape
    nbatch, dec_t, _ = x_sample.shape
    n_pool = cache_kv.shape[1]
    n_pages = page_table.shape[1]
    past_len = n_pages * PAGE_SIZE
    wbuf = cache_win.shape[2]
    assert t % PROJ_ROWS == 0 and t % RNN_ROWS == 0 and t % min(SEL_KEYS, t) == 0 and t >= WINDOW
    assert (b * t) % FINISH_ROWS == 0 and (nbatch * dec_t) % FINISH_ROWS == 0 and w_up.shape[2] % FF_CHUNK == 0
    assert (nbatch * dec_t) % PROJ_ROWS == 0 and nbatch == LANES and dec_t <= 8
    assert past_len % SEL_BLOCK == 0 and wbuf == WINDOW and wbuf >= LANES
    assert LANES % dec_t == 0 and (LANES // dec_t) % max(SAMPLE_BATCH_PER_STEP, WINDOW_BATCH_PER_STEP) == 0

    w = w_in[0]
    c_kv, c_gate, c_rq, c_rf, c_ri, c_rg = ATTN_WIDTH, ATTN_WIDTH + 6 * KV_WIDTH, 1304, 1816, 2328, 2840
    gate_cols = jnp.pad(w[:, c_gate:c_rq], ((0, 0), (0, LANES - N_ATTN_HEADS * N_BRANCH)))
    wtok = jnp.concatenate([w[:, 0:ATTN_WIDTH], w[:, c_rq:], gate_cols, w[:, c_kv:c_kv + 2 * KV_WIDTH]],
                           axis=1).astype(BF16)
    wft = w[:, c_kv:c_gate].T.astype(BF16)
    wtok_s = jnp.concatenate([w[:, 0:ATTN_WIDTH], w[:, c_rg:], gate_cols], axis=1).astype(BF16)
    wtb_s = jnp.concatenate([w[:, c_kv:c_kv + 4 * KV_WIDTH], w[:, c_rq:c_rg]], axis=1).T.astype(BF16)
    wbt_s = w[:, c_kv + 2 * KV_WIDTH:c_gate].T.astype(BF16)
    ln = ln_mix[0][None, :]
    qg = (jnp.tile(q_norm[0], N_ATTN_HEADS) * SCALE)[None, :]
    lbl = rnn_lb_logits.astype(F32)
    lblt = jnp.broadcast_to(lbl[:, :, None], lbl.shape + (PROJ_ROWS,))
    gsel = jnp.broadcast_to(k_norm[0, 1][:, None], (HEAD_DIM, PROJ_ROWS))
    gwin = jnp.broadcast_to(k_norm[0, 2][:, None], (HEAD_DIM, PROJ_ROWS))
    kg_dup = jnp.tile(k_norm[0, 0], 2)[None, :]
    pe = cmp_pe[0]
    pe_tok = jnp.tile(jnp.concatenate([jnp.tile(pe[0], (1, N_KV_HEADS)), jnp.tile(pe[1], (1, N_KV_HEADS))], axis=1),
                      (PROJ_ROWS // CMP_BLOCK, 1))
    w1 = cmp_w1[0].reshape(2, CMP_BLOCK, HEAD_DIM, CMP_HIDDEN)
    zeros = jnp.zeros_like(w1)
    w1bd = jnp.concatenate([jnp.concatenate([w1, zeros], axis=3), jnp.concatenate([zeros, w1], axis=3)],
                           axis=2).astype(BF16)
    w1r = jnp.tile(w1.transpose(0, 2, 1, 3), (1, 1, 2, 1)).astype(BF16)
    w1r = w1r.reshape(2, HEAD_DIM * LANES, CMP_HIDDEN)
    pe_t = jnp.tile(pe.transpose(0, 2, 1), (1, 1, 2))[:, :, None, :]
    w2dup = jnp.tile(cmp_w2[0], (1, 1, 2)).astype(BF16)
    ag = attn_out_norm[0][None, :]
    rg = rnn_out_norm[0][None, :]
    wout = w_out[0].astype(BF16)
    lnm = ln_mlp[0][None, :]
    wup = w_up[0].astype(BF16)
    wdn = w_down[0].astype(BF16)
    gexp = _gate_expander()

    (q_p, gates_p, rq_p, f_p, rv_p, rgs_p, kvc_p, kvt_p, wint_p, att_p) = _proj_prompt(
        x_prompt, ln, wtok, wft, qg, lbl, pe_tok, gsel, gwin)
    cmp_p = _compress_prompt(kvc_p, w1bd, w2dup, kg_dup)
    nb_p = t // CMP_BLOCK
    oa_p = _nsa_prompt(q_p, gates_p, cmp_p, att_p, _key_features(t), _query_slope_features(NSA_Q_ROWS), gexp)
    orn_p, st_p = _hgrn_prompt(rq_p, f_p, rv_p, _chunk_lower_tri(RNN_ROWS, RNN_CHUNK),
                               _head_block_diag(LANES), _head_indicator())
    y_p = _finish(x_prompt.reshape(b * t, dm), oa_p.reshape(b * t, ATTN_WIDTH), orn_p.reshape(b * t, RNN_WIDTH),
                  rgs_p.reshape(b * t, RNN_WIDTH), ag, rg, wout, lnm, wup, wdn).reshape(b, t, dm)
    kv_prompt = kvt_p.reshape(1, b, 4, N_KV_HEADS, HEAD_DIM, t).transpose(0, 1, 5, 2, 3, 4)
    wlen = min(WINDOW, t)
    win_prompt = wint_p[:, :, t - wlen:].reshape(1, b, 2, N_KV_HEADS, HEAD_DIM, wlen).transpose(0, 1, 5, 2, 3, 4)
    hh = LANES // RNN_DV
    st5 = st_p.reshape(b, RNN_WIDTH // LANES, hh, RNN_DV, hh, RNN_DK)
    rnn_prompt = jnp.stack([st5[:, :, i, :, i, :] for i in range(hh)], axis=2)
    rnn_prompt = rnn_prompt.reshape(b, N_RNN_HEADS, RNN_DV, RNN_DK).transpose(0, 1, 3, 2)[None]

    n_s = nbatch * dec_t
    xbt = x_sample.reshape(n_s, dm)
    xtb = x_sample.transpose(1, 0, 2).reshape(n_s, dm)
    q_s, rgs_s, gates_s, ztb, zbt = _proj_sample(xbt, xtb, ln, wtok_s, wtb_s, wbt_s, qg, lblt, gsel, gwin)
    kv_sample = ztb[0:4 * KV_WIDTH].reshape(4, N_KV_HEADS, HEAD_DIM, dec_t, nbatch).transpose(4, 3, 0, 1, 2)[None]
    cache = cache_kv[0].transpose(0, 2, 3, 4, 1).reshape(n_pool, PAGE_ROWS, PAGE_SIZE)
    nblk_s = past_len // CMP_BLOCK
    rho = np.arange(nblk_s)
    e_perm = _block_expander(2 * (rho % n_pages) + rho // n_pages, past_len)
    ns_s = -(-(past_len + dec_t) // SEL_BLOCK)
    q_s3 = q_s.reshape(nbatch, dec_t, ATTN_WIDTH)
    ocs = _nsa_sample(page_table.reshape(-1), cache, q_s3, zbt, w1r, pe_t, w2dup, kg_dup,
                      e_perm, past_len, min(SEL_TOPK, ns_s))
    win = cache_win[0].transpose(0, 2, 3, 4, 1).reshape(nbatch, 2 * KV_WIDTH, wbuf)
    oa_s, win_new = _win_sample(win, q_s3, zbt, ocs, gates_s.reshape(nbatch, dec_t, LANES), gexp, past_len)
    win_sample = win_new.reshape(1, nbatch, 2, N_KV_HEADS, HEAD_DIM, wbuf).transpose(0, 1, 5, 2, 3, 4)
    state = state_rnn[0].transpose(1, 2, 3, 0)
    orn_t, state_new = _hgrn_sample(ztb, state, dec_t)
    rnn_sample = state_new.transpose(3, 0, 1, 2)[None]
    orn_s = orn_t.reshape(RNN_WIDTH, dec_t, nbatch).transpose(2, 1, 0).reshape(n_s, RNN_WIDTH)
    y_s = _finish(xbt, oa_s.reshape(n_s, ATTN_WIDTH), orn_s, rgs_s, ag, rg, wout, lnm, wup, wdn).reshape(nbatch, dec_t, dm)

    return (y_p, y_s, kv_prompt, kv_sample, win_prompt, win_sample, rnn_prompt, rnn_sample)
```

```python
import functools

import numpy as np
import jax
import jax.numpy as jnp
from jax import lax
from jax.experimental import pallas as pl
from jax.experimental.pallas import tpu as pltpu

F32 = jnp.float32
BF16 = jnp.bfloat16

HEAD_DIM = 64
N_ATTN_HEADS = 8
N_KV_HEADS = 2
GROUP = N_ATTN_HEADS // N_KV_HEADS
N_RNN_HEADS = 8
RNN_DK = 64
RNN_DV = 64
ATTN_WIDTH = N_ATTN_HEADS * HEAD_DIM
RNN_WIDTH = N_RNN_HEADS * RNN_DV
KV_WIDTH = N_KV_HEADS * HEAD_DIM
N_BRANCH = 3
CMP_BLOCK = 64
SEL_BLOCK = 64
SEL_TOPK = 16
WINDOW = 512
CMP_HIDDEN = 128
PAGE_SIZE = 128
SCALE = HEAD_DIM ** -0.5
EPS = 1e-6
NEG = -1e30
LOG2E = 1.4426950408889634
FORCED_SCORE = GROUP + 1.0
SLOPES = [[2.0 ** (-(g * GROUP + r + 1)) for r in range(GROUP)] for g in range(N_KV_HEADS)]

LANES = 128
VMEM_BYTES_V7X = 64 * 1024 * 1024

PROJ_ROWS = 256
NSA_Q_ROWS = 128
SEL_KEYS = 512
ROW_BLOCK = 16
RNN_ROWS = 256
RNN_CHUNK = 16
FF_CHUNK = 1024
FINISH_ROWS = 512

NT = (((1,), (1,)), ((), ()))
TN = (((0,), (0,)), ((), ()))


def _vmem_limit(nbytes):
    return int(min(VMEM_BYTES_V7X - (8 << 20), max(nbytes, 16 << 20)))


def _dot(a, b):
    return jnp.dot(a, b, preferred_element_type=F32)


def _dot_nt(a, b):
    return lax.dot_general(a, b, NT, preferred_element_type=F32)


def _dot_tn(a, b):
    return lax.dot_general(a, b, TN, preferred_element_type=F32)


def _low_half(shape):
    lane = lax.broadcasted_iota(jnp.int32, shape, len(shape) - 1)
    return (lane & HEAD_DIM) == 0


def _head_mean_sq(x):
    outs = []
    for j in range(x.shape[-1] // LANES):
        blk = x[:, j * LANES:(j + 1) * LANES]
        sq = blk * blk
        low = _low_half(blk.shape)
        s_lo = jnp.sum(jnp.where(low, sq, 0.0), axis=-1, keepdims=True)
        s_hi = jnp.sum(jnp.where(low, 0.0, sq), axis=-1, keepdims=True)
        outs.append(jnp.where(low, s_lo, s_hi))
    return jnp.concatenate(outs, axis=-1) * (1.0 / HEAD_DIM)


def _row_rms(x):
    return x * lax.rsqrt(jnp.mean(x * x, axis=-1, keepdims=True) + EPS)


def _col_head_norm(rows, gain):
    ms = jnp.mean(rows * rows, axis=0, keepdims=True)
    return rows * lax.rsqrt(ms + EPS) * gain


def _silu(z):
    return z * jax.nn.sigmoid(z)


def _lower_bound(logits, axis):
    m = jnp.max(logits, axis=axis, keepdims=True)
    e = jnp.exp(logits - m)
    lb = e / jnp.sum(e, axis=axis, keepdims=True)
    return lb[0:1] if axis == 0 else lb


def _stack_group_queries(q, g):
    rows = q.shape[0]
    low = _low_half((rows, LANES))
    zero = jnp.zeros((rows, LANES), q.dtype)
    pa = q[:, g * 2 * LANES:g * 2 * LANES + LANES]
    pb = q[:, g * 2 * LANES + LANES:(g + 1) * 2 * LANES]
    return jnp.concatenate([jnp.where(low, pa, zero), jnp.where(low, zero, pa),
                            jnp.where(low, pb, zero), jnp.where(low, zero, pb)], axis=0)


def _unstack_group(o2, rows):
    low = _low_half((rows, LANES))
    return jnp.concatenate([jnp.where(low, o2[0:rows], o2[rows:2 * rows]),
                            jnp.where(low, o2[2 * rows:3 * rows], o2[3 * rows:4 * rows])], axis=1)


def _log2(n):
    assert n > 0 and n & (n - 1) == 0, n
    return n.bit_length() - 1


def _twice(x):
    return jnp.concatenate([x, x], axis=0)


def _row_slopes(g, row_head):
    s = jnp.full(row_head.shape, SLOPES[g][GROUP - 1], F32)
    for r in range(GROUP - 2, -1, -1):
        s = jnp.where(row_head == r, SLOPES[g][r], s)
    return s


def _block_ranks(score, ids, cand):
    sub = 8
    nrow = score.shape[0]
    in_order = all(r == i for r, i in cand) and nrow % sub == 0
    rank = jnp.zeros(score.shape, F32)
    for row, id_i in cand:
        s_i = score[row:row + 1, :]
        if in_order:
            parts = []
            for v in range(nrow // sub):
                blk = score[v * sub:(v + 1) * sub, :]
                if (v + 1) * sub - 1 <= id_i:
                    parts.append((s_i > blk).astype(F32))
                elif v * sub > id_i:
                    parts.append((s_i >= blk).astype(F32))
                else:
                    parts.append(jnp.where(ids[v * sub:(v + 1) * sub, :] > id_i,
                                           (s_i >= blk).astype(F32), (s_i > blk).astype(F32)))
            rank = rank + jnp.concatenate(parts, axis=0)
        else:
            rank = rank + jnp.where(ids > id_i, (s_i >= score).astype(F32), (s_i > score).astype(F32))
    return rank


def _expand_gates(gates, gexp_ref):
    hi = gates.astype(BF16)
    lo = (gates - hi.astype(F32)).astype(BF16)
    return [_dot(hi, gexp_ref[br]) + _dot(lo, gexp_ref[br]) for br in range(N_BRANCH)]


TOK_Q, TOK_RQ, TOK_RF, TOK_RI, TOK_RG, TOK_GATE, TOK_KVC, TOK_END = 0, 512, 1024, 1536, 2048, 2560, 2688, 2944


def _proj_prompt_body(x_ref, ln_ref, wtok_ref, wft_ref, qg_ref, lbl_ref, pe_ref, gsel_ref, gwin_ref,
                      q_ref, gates_ref, rq_ref, f_ref, rv_ref, rgs_ref, kvc_ref, kvt_ref, wint_ref, att_ref):
    xb = (_row_rms(x_ref[0]) * ln_ref[...]).astype(BF16)

    def tok(lo, hi):
        return _dot_nt(xb, wtok_ref[lo:hi, :])

    zq = tok(TOK_Q, TOK_RQ)
    q_ref[0] = (zq * lax.rsqrt(_head_mean_sq(zq) + EPS) * qg_ref[...]).astype(BF16)
    rq_ref[0] = _silu(tok(TOK_RQ, TOK_RF))
    lb = _lower_bound(lbl_ref[...], 0)
    f_ref[0] = lb + (1.0 - lb) * jax.nn.sigmoid(tok(TOK_RF, TOK_RI))
    rv_ref[0] = tok(TOK_RI, TOK_RG)
    rgs_ref[0] = _silu(tok(TOK_RG, TOK_GATE))
    gates_ref[0] = jax.nn.sigmoid(tok(TOK_GATE, TOK_KVC))
    zc = tok(TOK_KVC, TOK_END) + pe_ref[...]
    kvc_ref[0, 0] = zc[:, 0:LANES]
    kvc_ref[1, 0] = zc[:, LANES:2 * LANES]

    zf = _dot_nt(wft_ref[...], xb)
    d = HEAD_DIM
    ksel = jnp.concatenate([_col_head_norm(zf[256 + g * d:256 + (g + 1) * d], gsel_ref[...])
                            for g in range(N_KV_HEADS)], axis=0)
    kwin = jnp.concatenate([_col_head_norm(zf[512 + g * d:512 + (g + 1) * d], gwin_ref[...])
                            for g in range(N_KV_HEADS)], axis=0)
    vsel = zf[384:512]
    vwin = zf[640:768]
    kvt_ref[0, 0:256] = zf[0:256]
    kvt_ref[0, 256:384] = ksel
    kvt_ref[0, 384:512] = vsel
    wint_ref[0, 0:128] = kwin
    wint_ref[0, 128:256] = vwin
    att_ref[0, 0:128] = ksel.astype(BF16)
    att_ref[0, 128:256] = vsel.astype(BF16)
    att_ref[0, 256:384] = kwin.astype(BF16)
    att_ref[0, 384:512] = vwin.astype(BF16)


def _proj_prompt(x, ln, wtok, wft, qg, lbl, pe_tok, gsel, gwin):
    b, t, dm = x.shape
    tm = PROJ_ROWS
    grid = (b, t // tm)
    row = lambda i, j: (i, j, 0)
    col = lambda i, j: (i, 0, j)
    const2 = lambda i, j: (0, 0)
    out_shape = (
        jax.ShapeDtypeStruct((b, t, ATTN_WIDTH), BF16),
        jax.ShapeDtypeStruct((b, t, LANES), F32),
        jax.ShapeDtypeStruct((b, t, RNN_WIDTH), F32),
        jax.ShapeDtypeStruct((b, t, RNN_WIDTH), F32),
        jax.ShapeDtypeStruct((b, t, RNN_WIDTH), F32),
        jax.ShapeDtypeStruct((b, t, RNN_WIDTH), F32),
        jax.ShapeDtypeStruct((2, b, t, LANES), F32),
        jax.ShapeDtypeStruct((b, 4 * KV_WIDTH, t), F32),
        jax.ShapeDtypeStruct((b, 2 * KV_WIDTH, t), F32),
        jax.ShapeDtypeStruct((b, 4 * KV_WIDTH, t), BF16),
    )
    out_specs = (
        pl.BlockSpec((1, tm, ATTN_WIDTH), row), pl.BlockSpec((1, tm, LANES), row),
        pl.BlockSpec((1, tm, RNN_WIDTH), row), pl.BlockSpec((1, tm, RNN_WIDTH), row),
        pl.BlockSpec((1, tm, RNN_WIDTH), row), pl.BlockSpec((1, tm, RNN_WIDTH), row),
        pl.BlockSpec((2, 1, tm, LANES), lambda i, j: (0, i, j, 0)),
        pl.BlockSpec((1, 4 * KV_WIDTH, tm), col), pl.BlockSpec((1, 2 * KV_WIDTH, tm), col),
        pl.BlockSpec((1, 4 * KV_WIDTH, tm), col),
    )
    in_specs = [
        pl.BlockSpec((1, tm, dm), row), pl.BlockSpec(ln.shape, const2),
        pl.BlockSpec(wtok.shape, const2), pl.BlockSpec(wft.shape, const2),
        pl.BlockSpec(qg.shape, const2), pl.BlockSpec(lbl.shape, const2), pl.BlockSpec(pe_tok.shape, const2),
        pl.BlockSpec(gsel.shape, const2), pl.BlockSpec(gwin.shape, const2),
    ]
    return pl.pallas_call(
        _proj_prompt_body, grid=grid, in_specs=in_specs, out_specs=out_specs, out_shape=out_shape,
        compiler_params=pltpu.CompilerParams(dimension_semantics=("parallel", "parallel"),
                                             vmem_limit_bytes=_vmem_limit(48 << 20)),
        name="proj_prompt",
    )(x, ln, wtok, wft, qg, lbl, pe_tok, gsel, gwin)


def _compress_prompt_body(x_ref, w1_ref, w2_ref, kg_ref, out_ref):
    c = pl.program_id(0)
    nb = out_ref.shape[2]
    acc = jnp.zeros((nb, 2 * CMP_HIDDEN), F32)
    for pos in range(CMP_BLOCK):
        xp = x_ref[0, 0, pl.ds(pos, nb, stride=CMP_BLOCK), :]
        acc = acc + _dot(xp.astype(BF16), w1_ref[0, pos])
    hb = _silu(acc).astype(BF16)
    outs = []
    for g in range(N_KV_HEADS):
        y = _dot(hb[:, g * CMP_HIDDEN:(g + 1) * CMP_HIDDEN], w2_ref[0])
        yn = _row_rms(y) * kg_ref[...]
        outs.append(jnp.where(c == 0, yn, y))
    out_ref[0, 0] = jnp.concatenate(outs, axis=1)


def _compress_prompt(kvc, w1bd, w2dup, kg_dup):
    _, b, t, _ = kvc.shape
    nb = t // CMP_BLOCK
    return pl.pallas_call(
        _compress_prompt_body, grid=(2, b),
        in_specs=[pl.BlockSpec((1, 1, t, LANES), lambda c, i: (c, i, 0, 0)),
                  pl.BlockSpec((1,) + w1bd.shape[1:], lambda c, i: (c, 0, 0, 0)),
                  pl.BlockSpec((1,) + w2dup.shape[1:], lambda c, i: (c, 0, 0)),
                  pl.BlockSpec(kg_dup.shape, lambda c, i: (0, 0))],
        out_specs=pl.BlockSpec((1, 1, nb, 2 * LANES), lambda c, i: (c, i, 0, 0)),
        out_shape=jax.ShapeDtypeStruct((2, b, nb, 2 * LANES), F32),
        compiler_params=pltpu.CompilerParams(dimension_semantics=("arbitrary", "arbitrary"),
                                             vmem_limit_bytes=_vmem_limit(32 << 20)),
        name="compress_prompt",
    )(kvc, w1bd, w2dup, kg_dup)


def _row_max_update(sc_scr, nm_scr, m_scr, mn_scr, ncol, tq):
    rb = ROW_BLOCK
    for i in range(GROUP * tq // rb):
        rows = slice(i * rb, (i + 1) * rb)
        qrows = slice((i * rb) % tq, (i * rb) % tq + rb)
        mx = jnp.full((rb, LANES), NEG, F32)
        for j in range(ncol):
            cols = slice(j * LANES, (j + 1) * LANES)
            v = sc_scr[rows, cols]
            if nm_scr is not None:
                v = v + nm_scr[qrows, cols]
                sc_scr[rows, cols] = v
            mx = jnp.maximum(mx, v)
        mn_scr[rows, :] = jnp.maximum(m_scr[rows, :], jnp.max(mx, axis=-1, keepdims=True))


def _softmax_update(sc_scr, p_scr, m_scr, mn_scr, l_scr, a_scr, ncol, tq):
    rb = ROW_BLOCK
    for i in range(GROUP * tq // rb):
        rows = slice(i * rb, (i + 1) * rb)
        mn = mn_scr[rows, :]
        tot = jnp.zeros((rb, LANES), F32)
        for j in range(ncol):
            cols = slice(j * LANES, (j + 1) * LANES)
            p = jnp.exp(sc_scr[rows, cols] - mn)
            tot = tot + p
            p_scr[rows, cols] = p.astype(BF16)
        alpha = jnp.exp(m_scr[rows, :] - mn)
        l_scr[rows, :] = alpha * l_scr[rows, :] + jnp.sum(tot, axis=-1, keepdims=True)
        a_scr[rows, :] = alpha
        m_scr[rows, :] = mn


def _nsa_prompt_body(q_ref, gates_ref, cmp_ref, att_ref, kfeat_ref, qfeat_ref, gexp_ref, oa_ref,
                     sc_scr, sc2_scr, nm_scr, p_scr, m_scr, mn_scr, l_scr, a_scr, acc_scr, *, seq, topk):
    tq = NSA_Q_ROWS
    tk = min(SEL_KEYS, seq)
    nb = cmp_ref.shape[2]
    t0 = pl.program_id(1) * tq
    wk = min(WINDOW + tq, seq)
    ws = pl.multiple_of(jnp.clip(t0 - WINDOW, 0, seq - wk), LANES)
    q = q_ref[0]
    d = HEAD_DIM
    groups = range(N_KV_HEADS)
    def reset(stats):
        for g in groups:
            m_scr[g] = jnp.full(m_scr.shape[1:], NEG, F32)
            for ref in stats:
                ref[g] = jnp.zeros(ref.shape[1:], F32)

    sc_bufs = (sc_scr, sc2_scr)

    def logits(buf, g, qa, kt_, kf, width):
        sc_bufs[buf][g, :, 0:width] = _dot(qa, jnp.concatenate([kt_, kt_, kf], axis=0))

    def softmax_tiles(buf, ncol, masked):
        sc = sc_bufs[buf]
        for g in groups:
            _row_max_update(sc.at[g], nm_scr if masked else None, m_scr.at[g], mn_scr.at[g], ncol, tq)
        for g in groups:
            _softmax_update(sc.at[g], p_scr.at[g], m_scr.at[g], mn_scr.at[g], l_scr.at[g], a_scr.at[g], ncol, tq)

    qst = [_stack_group_queries(q, g) for g in groups]
    qwin = [jnp.concatenate([qst[g], qfeat_ref[g].astype(BF16)], axis=1) for g in groups]

    kcs = [cmp_ref[0, 0][:, g * LANES:(g + 1) * LANES].astype(BF16) for g in groups]
    vcs = [cmp_ref[1, 0][:, g * LANES:(g + 1) * LANES].astype(BF16) for g in groups]
    st = [_dot_nt(kcs[g], qst[g]) for g in groups]

    reset((l_scr,))
    kfw = kfeat_ref[:, pl.ds(ws, wk)]
    ddw = (t0 - ws) + lax.broadcasted_iota(jnp.int32, (tq, wk), 0) - lax.broadcasted_iota(jnp.int32, (tq, wk), 1)
    nm_scr[:, 0:wk] = jnp.where((ddw >= 0) & (ddw < WINDOW), 0.0, NEG)
    for g in groups:
        logits(0, g, qwin[g], att_ref[0, 2 * KV_WIDTH + g * d:2 * KV_WIDTH + (g + 1) * d, pl.ds(ws, wk)], kfw, wk)

    blk = lax.broadcasted_iota(jnp.int32, (nb, GROUP * tq), 0)
    colq = lax.broadcasted_iota(jnp.int32, (1, GROUP * tq), 1)
    dist = (t0 + (colq & (tq - 1)) - (blk * CMP_BLOCK + (CMP_BLOCK - 1))).astype(F32)
    valid = dist >= 0.0
    ps = []
    for g in groups:
        s = jnp.where(valid, st[g] - _row_slopes(g, colq >> _log2(tq)) * dist, NEG)
        e = jnp.exp(s - jnp.max(s, axis=0, keepdims=True))
        ps.append(jnp.where(valid, e / jnp.sum(e, axis=0, keepdims=True), 0.0))
    o_cmp = [_unstack_group(_dot_tn(ps[g].astype(BF16), vcs[g]), tq) for g in groups]

    softmax_tiles(0, wk // LANES, True)
    o_win = []
    for g in groups:
        vw = att_ref[0, 3 * KV_WIDTH + g * d:3 * KV_WIDTH + (g + 1) * d, pl.ds(ws, wk)]
        o_win.append(_unstack_group(_dot_nt(p_scr[g, :, 0:wk], _twice(vw)) / l_scr[g], tq))

    bj = lax.broadcasted_iota(jnp.int32, (nb, tq), 0)
    cur = (t0 + lax.broadcasted_iota(jnp.int32, (nb, tq), 1)) >> _log2(SEL_BLOCK)
    force = (bj == 0) | (bj == cur)
    qsel = []
    for g in groups:
        p = ps[g]
        imp = p[:, 0:tq] + p[:, tq:2 * tq] + p[:, 2 * tq:3 * tq] + p[:, 3 * tq:4 * tq]
        score = jnp.where(bj <= cur, jnp.where(force, FORCED_SCORE, imp), -1.0)
        rank = _block_ranks(score, bj, [(i, i) for i in range(nb)])
        mneg = jnp.where((rank < topk) & (score >= 0.0), 0.0, NEG)
        mtok = jnp.concatenate([mneg, jnp.zeros((LANES - nb, tq), F32)], axis=0).T
        qsel.append(jnp.concatenate(
            [qst[g], (qfeat_ref[g] + jnp.concatenate([mtok] * GROUP, axis=0)).astype(BF16)], axis=1))

    reset((l_scr, acc_scr))
    n_kt = (t0 + tq - 1) // tk + 1

    def sel_logits(kt, buf):
        s0 = pl.multiple_of(kt * tk, tk)
        kf = kfeat_ref[:, pl.ds(s0, tk)]
        for g in groups:
            logits(buf, g, qsel[g], att_ref[0, g * d:(g + 1) * d, pl.ds(s0, tk)], kf, tk)

    def sel_update(kt, buf, causal):
        s0 = pl.multiple_of(kt * tk, tk)
        if causal:
            dd = (t0 - s0) + lax.broadcasted_iota(jnp.int32, (tq, tk), 0) - lax.broadcasted_iota(jnp.int32, (tq, tk), 1)
            nm_scr[:, 0:tk] = jnp.where(dd >= 0, 0.0, NEG)
        softmax_tiles(buf, tk // LANES, causal)
        for g in groups:
            vt_ = att_ref[0, KV_WIDTH + g * d:KV_WIDTH + (g + 1) * d, pl.ds(s0, tk)]
            acc_scr[g] = a_scr[g] * acc_scr[g] + _dot_nt(p_scr[g, :, 0:tk], _twice(vt_))

    sel_logits(0, 0)

    def full_tile(kt, carry):
        for par in (0, 1):
            @pl.when((kt & 1) == par)
            def _(par=par):
                sel_logits(kt + 1, 1 - par)
                sel_update(kt, par, False)
        return carry

    lax.fori_loop(0, n_kt - 1, full_tile, 0)
    for par in (0, 1):
        @pl.when(((n_kt - 1) & 1) == par)
        def _(par=par):
            sel_update(n_kt - 1, par, True)
    o_sel = [_unstack_group(acc_scr[g] / l_scr[g], tq) for g in groups]

    ge = _expand_gates(gates_ref[0], gexp_ref)
    oa_ref[0] = (ge[0] * jnp.concatenate(o_cmp, axis=1) + ge[1] * jnp.concatenate(o_sel, axis=1)
                 + ge[2] * jnp.concatenate(o_win, axis=1))


def _nsa_prompt(q, gates, cmpkv, att, k_feat, q_feat, gexp):
    b, t, _ = q.shape
    tq = NSA_Q_ROWS
    nb = t // CMP_BLOCK
    assert nb <= HEAD_DIM, "the block mask uses 64 feature lanes"
    cw = max(min(SEL_KEYS, t), min(WINDOW + tq, t))
    body = functools.partial(_nsa_prompt_body, seq=t, topk=min(SEL_TOPK, nb))
    ng = N_KV_HEADS
    stat = pltpu.VMEM((ng, GROUP * tq, LANES), F32)
    return pl.pallas_call(
        body, grid=(b, t // tq),
        in_specs=[pl.BlockSpec((1, tq, ATTN_WIDTH), lambda i, j: (i, j, 0)),
                  pl.BlockSpec((1, tq, LANES), lambda i, j: (i, j, 0)),
                  pl.BlockSpec((2, 1, nb, 2 * LANES), lambda i, j: (0, i, 0, 0)),
                  pl.BlockSpec((1, 4 * KV_WIDTH, t), lambda i, j: (i, 0, 0)),
                  pl.BlockSpec(k_feat.shape, lambda i, j: (0, 0)),
                  pl.BlockSpec(q_feat.shape, lambda i, j: (0, 0, 0)),
                  pl.BlockSpec(gexp.shape, lambda i, j: (0, 0, 0))],
        out_specs=pl.BlockSpec((1, tq, ATTN_WIDTH), lambda i, j: (i, j, 0)),
        out_shape=jax.ShapeDtypeStruct((b, t, ATTN_WIDTH), F32),
        scratch_shapes=[pltpu.VMEM((ng, GROUP * tq, cw), F32), pltpu.VMEM((ng, GROUP * tq, cw), F32),
                        pltpu.VMEM((tq, cw), F32),
                        pltpu.VMEM((ng, GROUP * tq, cw), BF16), stat, stat, stat, stat, stat],
        compiler_params=pltpu.CompilerParams(dimension_semantics=("parallel", "parallel"),
                                             vmem_limit_bytes=_vmem_limit(48 << 20)),
        name="nsa_prompt",
    )(q, gates, cmpkv, att, k_feat, q_feat, gexp)


def _split3(x):
    hi = x.astype(BF16)
    r1 = x - hi.astype(F32)
    mid = r1.astype(BF16)
    lo = (r1 - mid.astype(F32)).astype(BF16)
    return hi, mid, lo


def _hgrn_prompt_body(rq_ref, f_ref, rv_ref, ltri_ref, bd_ref, ind_ref, indt_ref, o_ref, st_ref,
                      s_scr, cum_scr, k_scr, prod_scr):
    c16 = RNN_CHUNK
    nbat = rq_ref.shape[0]

    @pl.when(pl.program_id(0) == 0)
    def _():
        s_scr[...] = jnp.zeros(s_scr.shape, F32)

    ltri = ltri_ref[...]
    for bi in range(nbat):
        f = f_ref[bi]
        hi, mid, lo = _split3(jnp.log(f))
        cum = (_dot(ltri, hi) + _dot(ltri, mid) + _dot(ltri, lo)) * LOG2E
        cum_scr[bi] = cum
        k_scr[bi] = cum - jnp.log2(1.0 - f)

    si = lax.broadcasted_iota(jnp.int32, (c16, c16, RNN_WIDTH), 0)
    ti = lax.broadcasted_iota(jnp.int32, (c16, c16, RNN_WIDTH), 1)
    causal = si <= ti
    npair = RNN_WIDTH // LANES

    def pair_products(c, buf, bi):
        r0 = pl.multiple_of(c * c16, c16)
        cc = cum_scr[bi, pl.ds(r0, c16), :]
        qc = rq_ref[bi, pl.ds(r0, c16), :]
        lk = k_scr[bi, pl.ds(r0, c16), :]
        dec = jnp.exp2(jnp.where(causal, cc[None, :, :] - lk[:, None, :], NEG))
        prod_scr[buf, bi] = (qc[None, :, :] * dec).reshape(c16 * c16, RNN_WIDTH).astype(BF16)

    def recurrence(c, buf, nxt):
        r0 = pl.multiple_of(c * c16, c16)
        rows = range(nbat)
        a1 = [_dot(prod_scr[buf, bi], ind_ref[...]) for bi in rows]
        if nxt is not None:
            for bi in rows:
                pair_products(nxt, 1 - buf, bi)
        cc = [cum_scr[bi, pl.ds(r0, c16), :] for bi in rows]
        last = [x[c16 - 1:c16, :] for x in cc]
        o_int = []
        for bi in rows:
            qd = (rq_ref[bi, pl.ds(r0, c16), :] * jnp.exp2(cc[bi])).astype(BF16)
            o_int.append(jnp.concatenate([_dot_nt(qd[:, p * LANES:(p + 1) * LANES], s_scr[bi, p].astype(BF16))
                                          for p in range(npair)], axis=1))
        a2 = [_dot(a1[bi].astype(BF16), indt_ref[...]).reshape(c16, c16, RNN_WIDTH) for bi in rows]
        for bi in rows:
            kd = jnp.exp2(last[bi] - k_scr[bi, pl.ds(r0, c16), :]).astype(BF16)
            vb = rv_ref[bi, pl.ds(r0, c16), :].astype(BF16)
            dl = jnp.exp2(last[bi])
            for p in range(npair):
                u = _dot_tn(vb[:, p * LANES:(p + 1) * LANES], kd[:, p * LANES:(p + 1) * LANES])
                s_scr[bi, p] = s_scr[bi, p] * dl[:, p * LANES:(p + 1) * LANES] + u * bd_ref[...]
        for bi in rows:
            vc = rv_ref[bi, pl.ds(r0, c16), :]
            o_ref[bi, pl.ds(r0, c16), :] = o_int[bi] + jnp.sum(a2[bi] * vc[:, None, :], axis=0)

    nchunk = rq_ref.shape[1] // c16
    assert nchunk % 2 == 0 and nchunk >= 4
    for bi in range(nbat):
        pair_products(0, 0, bi)

    def two_chunks(j, carry):
        for sub in (0, 1):
            recurrence(2 * j + sub, sub, 2 * j + sub + 1)
        return carry

    lax.fori_loop(0, nchunk // 2 - 1, two_chunks, 0)
    recurrence(nchunk - 2, 0, nchunk - 1)
    recurrence(nchunk - 1, 1, None)
    st_ref[...] = s_scr[...]


def _hgrn_prompt(rq, f, rv, ltri, bd, ind):
    b, t, w = rq.shape
    tc = RNN_ROWS
    npair = w // LANES
    row = lambda j: (0, j, 0)
    const = lambda j: (0, 0)
    indt = ind.T
    return pl.pallas_call(
        _hgrn_prompt_body, grid=(t // tc,),
        in_specs=[pl.BlockSpec((b, tc, w), row), pl.BlockSpec((b, tc, w), row), pl.BlockSpec((b, tc, w), row),
                  pl.BlockSpec(ltri.shape, const), pl.BlockSpec(bd.shape, const),
                  pl.BlockSpec(ind.shape, const), pl.BlockSpec(indt.shape, const)],
        out_specs=(pl.BlockSpec((b, tc, w), row), pl.BlockSpec((b, npair, LANES, LANES), lambda j: (0, 0, 0, 0))),
        out_shape=(jax.ShapeDtypeStruct((b, t, w), F32), jax.ShapeDtypeStruct((b, npair, LANES, LANES), F32)),
        scratch_shapes=[pltpu.VMEM((b, npair, LANES, LANES), F32), pltpu.VMEM((b, tc, w), F32),
                        pltpu.VMEM((b, tc, w), F32), pltpu.VMEM((2, b, RNN_CHUNK * RNN_CHUNK, w), BF16)],
        compiler_params=pltpu.CompilerParams(dimension_semantics=("arbitrary",),
                                             vmem_limit_bytes=_vmem_limit(40 << 20)),
        name="hgrn_prompt",
    )(rq, f, rv, ltri, bd, ind, indt)


def _finish_body(x_ref, oa_ref, orn_ref, rgs_ref, ag_ref, rg_ref, wout_ref, lnm_ref, wup_ref, wdn_ref, y_ref, hn_scr):
    @pl.when(pl.program_id(1) == 0)
    def _():
        oa = oa_ref[...]
        orn = orn_ref[...]
        a_n = oa * lax.rsqrt(_head_mean_sq(oa) + EPS) * ag_ref[...]
        r_n = orn * lax.rsqrt(_head_mean_sq(orn) + EPS) * rg_ref[...] * rgs_ref[...]
        h = (x_ref[...] + _dot(a_n.astype(BF16), wout_ref[0:ATTN_WIDTH, :])
             + _dot(r_n.astype(BF16), wout_ref[ATTN_WIDTH:ATTN_WIDTH + RNN_WIDTH, :]))
        y_ref[...] = h
        hn_scr[...] = (_row_rms(h) * lnm_ref[...]).astype(BF16)

    u = jnp.maximum(_dot(hn_scr[...], wup_ref[...]), 0.0)
    y_ref[...] += _dot((u * u).astype(BF16), wdn_ref[...])


def _finish(x, oa, orn, rgs, ag, rg, wout, lnm, wup, wdn):
    n, dm = x.shape
    tm = FINISH_ROWS
    dff = wup.shape[1]
    row = lambda i, j: (i, 0)
    const = lambda i, j: (0, 0)
    return pl.pallas_call(
        _finish_body, grid=(n // tm, dff // FF_CHUNK),
        in_specs=[pl.BlockSpec((tm, dm), row), pl.BlockSpec((tm, ATTN_WIDTH), row),
                  pl.BlockSpec((tm, RNN_WIDTH), row), pl.BlockSpec((tm, RNN_WIDTH), row),
                  pl.BlockSpec(ag.shape, const), pl.BlockSpec(rg.shape, const),
                  pl.BlockSpec(wout.shape, const), pl.BlockSpec(lnm.shape, const),
                  pl.BlockSpec((dm, FF_CHUNK), lambda i, j: (0, j)),
                  pl.BlockSpec((FF_CHUNK, dm), lambda i, j: (j, 0))],
        out_specs=pl.BlockSpec((tm, dm), row),
        out_shape=jax.ShapeDtypeStruct((n, dm), F32),
        scratch_shapes=[pltpu.VMEM((tm, dm), BF16)],
        compiler_params=pltpu.CompilerParams(dimension_semantics=("parallel", "arbitrary"),
                                             vmem_limit_bytes=_vmem_limit(48 << 20)),
        name="finish",
    )(x, oa, orn, rgs, ag, rg, wout, lnm, wup, wdn)


def _proj_sample_body(xbt_ref, xtb_ref, ln_ref, wtok_ref, wtb_ref, wbt_ref, qg_ref, lbl_ref, gsel_ref, gwin_ref,
                      q_ref, rgs_ref, gates_ref, ztb_ref, zbt_ref):
    xb = (_row_rms(xbt_ref[...]) * ln_ref[...]).astype(BF16)
    xt = (_row_rms(xtb_ref[...]) * ln_ref[...]).astype(BF16)
    zq = _dot_nt(xb, wtok_ref[0:ATTN_WIDTH, :])
    q_ref[...] = (zq * lax.rsqrt(_head_mean_sq(zq) + EPS) * qg_ref[...]).astype(BF16)
    rgs_ref[...] = _silu(_dot_nt(xb, wtok_ref[ATTN_WIDTH:ATTN_WIDTH + RNN_WIDTH, :]))
    gates_ref[...] = jax.nn.sigmoid(_dot_nt(xb, wtok_ref[ATTN_WIDTH + RNN_WIDTH:ATTN_WIDTH + RNN_WIDTH + LANES, :]))

    d = HEAD_DIM
    zt = _dot_nt(wtb_ref[...], xt)
    ztb_ref[0:256] = zt[0:256]
    for g in range(N_KV_HEADS):
        ztb_ref[256 + g * d:256 + (g + 1) * d] = _col_head_norm(zt[256 + g * d:256 + (g + 1) * d], gsel_ref[...])
    ztb_ref[384:512] = zt[384:512]
    ztb_ref[512:1024] = _silu(zt[512:1024])
    lb = _lower_bound(lbl_ref[...], 0)[0]
    ztb_ref[1024:1536] = lb + (1.0 - lb) * jax.nn.sigmoid(zt[1024:1536])
    ztb_ref[1536:2048] = zt[1536:2048]

    zb = _dot_nt(wbt_ref[...], xb)
    for g in range(N_KV_HEADS):
        zbt_ref[g * d:(g + 1) * d] = _col_head_norm(zb[g * d:(g + 1) * d], gsel_ref[...])
        zbt_ref[256 + g * d:256 + (g + 1) * d] = _col_head_norm(zb[256 + g * d:256 + (g + 1) * d], gwin_ref[...])
    zbt_ref[128:256] = zb[128:256]
    zbt_ref[384:512] = zb[384:512]


def _proj_sample(xbt, xtb, ln, wtok, wtb, wbt, qg, lblt, gsel, gwin):
    n, dm = xbt.shape
    tm = PROJ_ROWS
    row = lambda i: (i, 0)
    col = lambda i: (0, i)
    const = lambda i: (0, 0)
    return pl.pallas_call(
        _proj_sample_body, grid=(n // tm,),
        in_specs=[pl.BlockSpec((tm, dm), row), pl.BlockSpec((tm, dm), row), pl.BlockSpec(ln.shape, const),
                  pl.BlockSpec(wtok.shape, const), pl.BlockSpec(wtb.shape, const), pl.BlockSpec(wbt.shape, const),
                  pl.BlockSpec(qg.shape, const), pl.BlockSpec(lblt.shape, lambda i: (0, 0, 0)),
                  pl.BlockSpec(gsel.shape, const), pl.BlockSpec(gwin.shape, const)],
        out_specs=(pl.BlockSpec((tm, ATTN_WIDTH), row), pl.BlockSpec((tm, RNN_WIDTH), row),
                   pl.BlockSpec((tm, LANES), row), pl.BlockSpec((wtb.shape[0], tm), col),
                   pl.BlockSpec((wbt.shape[0], tm), col)),
        out_shape=(jax.ShapeDtypeStruct((n, ATTN_WIDTH), BF16), jax.ShapeDtypeStruct((n, RNN_WIDTH), F32),
                   jax.ShapeDtypeStruct((n, LANES), F32), jax.ShapeDtypeStruct((wtb.shape[0], n), F32),
                   jax.ShapeDtypeStruct((wbt.shape[0], n), F32)),
        compiler_params=pltpu.CompilerParams(dimension_semantics=("parallel",),
                                             vmem_limit_bytes=_vmem_limit(40 << 20)),
        name="proj_sample",
    )(xbt, xtb, ln, wtok, wtb, wbt, qg, lblt, gsel, gwin)


PAGE_ROWS = 4 * KV_WIDTH
SAMPLE_BATCH_PER_STEP = 4
WINDOW_BATCH_PER_STEP = 4


def _nsa_sample_body(pt_ref, cache_ref, q_ref, nkv_ref, w1_ref, pe_ref, w2_ref, kg_ref, e_ref,
                     ocs_ref, cbuf, sbuf, lhs_scr, kk_scr, vv_scr, sem, *, n_pages, past_len, dec_t, topk):
    step_id = pl.program_id(0)
    nsteps = pl.num_programs(0)
    slot = step_id % 2
    d = HEAD_DIM
    nblk = 2 * n_pages
    half_rows = PAGE_ROWS // 2
    bps = q_ref.shape[0]

    def page_copies(st, sl, bi, j):
        pg = pt_ref[(st * bps + bi) * n_pages + j]
        return (pltpu.make_async_copy(cache_ref.at[pg, pl.ds(0, half_rows)], cbuf.at[sl, bi, :, j, :], sem.at[sl]),
                pltpu.make_async_copy(cache_ref.at[pg, pl.ds(half_rows, half_rows)],
                                      sbuf.at[sl, bi, pl.ds(j * half_rows, half_rows)], sem.at[sl]))

    def all_copies(st, sl):
        return [cp for bi in range(bps) for j in range(n_pages) for cp in page_copies(st, sl, bi, j)]

    @pl.when(step_id == 0)
    def _():
        for cp in all_copies(0, 0):
            cp.start()

    @pl.when(step_id + 1 < nsteps)
    def _():
        for cp in all_copies(step_id + 1, 1 - slot):
            cp.start()

    for cp in all_copies(step_id, slot):
        cp.wait()

    low = _low_half((n_pages, LANES))
    rows_b = 4 * n_pages

    def compress(c):
        for dd in range(d):
            rows = []
            for bi in range(bps):
                for g in range(N_KV_HEADS):
                    xg = cbuf[slot, bi, (c * N_KV_HEADS + g) * d + dd]
                    xg = xg + pe_ref[c, dd]
                    rows += [jnp.where(low, xg, 0.0), jnp.where(low, 0.0, xg)]
            lhs_scr[:, dd * LANES:(dd + 1) * LANES] = jnp.concatenate(rows, axis=0).astype(BF16)
        acc = _dot(lhs_scr[...], w1_ref[c])
        return _dot(_silu(acc).astype(BF16), w2_ref[c])

    kc_all = _row_rms(compress(0)) * kg_ref[...]
    vc_all = compress(1)

    nq = GROUP * dec_t
    colq = lax.broadcasted_iota(jnp.int32, (1, nq), 1)
    rowq = lax.broadcasted_iota(jnp.int32, (nq, 1), 0)
    cur_blk = past_len // SEL_BLOCK
    chains = [(bi, g) for bi in range(bps) for g in range(N_KV_HEADS)]
    nch = len(chains)
    step = rowq & (dec_t - 1)
    qpos = past_len + (colq & (dec_t - 1))
    rho_q = lax.broadcasted_iota(jnp.int32, (nblk, 1), 0)
    end = (2 * (rho_q & (n_pages - 1)) + (rho_q >> _log2(n_pages))) * CMP_BLOCK + (CMP_BLOCK - 1)
    dist = (qpos - end).astype(F32)
    valid = dist >= 0.0
    kpos = lax.broadcasted_iota(jnp.int32, (1, past_len), 1)
    dpast = (past_len + step - kpos).astype(F32)
    lane = lax.broadcasted_iota(jnp.int32, (1, LANES), 1)
    per_tile = LANES // dec_t
    offs = [((step_id * bps + bi) & (per_tile - 1)) * dec_t for bi in range(bps)]
    dnews = [(step - (lane - off)).astype(F32) for off in offs]
    oknew = [(lane >= off) & (lane < off + dec_t) for off in offs]

    qst = [_stack_group_queries(q_ref[bi], g) for bi, g in chains]
    kcs = [kc_all[bi * rows_b + g * nblk:bi * rows_b + (g + 1) * nblk].astype(BF16) for bi, g in chains]
    vcs = [vc_all[bi * rows_b + g * nblk:bi * rows_b + (g + 1) * nblk].astype(BF16) for bi, g in chains]
    st = [_dot_nt(kcs[ch], qst[ch]) for ch in range(nch)]
    for ch, (bi, g) in enumerate(chains):
        for pg in range(n_pages):
            kt_ = sbuf[slot, bi, pl.ds(pg * half_rows + g * d, d), :].astype(BF16)
            vt_ = sbuf[slot, bi, pl.ds(pg * half_rows + (N_KV_HEADS + g) * d, d), :].astype(BF16)
            kk_scr[ch, 0:d, pg * PAGE_SIZE:(pg + 1) * PAGE_SIZE] = kt_
            kk_scr[ch, d:2 * d, pg * PAGE_SIZE:(pg + 1) * PAGE_SIZE] = kt_
            vv_scr[ch, 0:d, pg * PAGE_SIZE:(pg + 1) * PAGE_SIZE] = vt_
            vv_scr[ch, d:2 * d, pg * PAGE_SIZE:(pg + 1) * PAGE_SIZE] = vt_
    sp = [_dot(qst[ch], kk_scr[ch]) for ch in range(nch)]
    nk = [_twice(nkv_ref[g * d:(g + 1) * d, :].astype(BF16)) for g in range(N_KV_HEADS)]
    nv = [_twice(nkv_ref[KV_WIDTH + g * d:KV_WIDTH + (g + 1) * d, :].astype(BF16)) for g in range(N_KV_HEADS)]
    sn = [_dot(qst[ch], nk[g]) for ch, (bi, g) in enumerate(chains)]
    ps = []
    for ch, (bi, g) in enumerate(chains):
        s = jnp.where(valid, st[ch] - _row_slopes(g, colq >> _log2(dec_t)) * dist, NEG)
        e = jnp.exp(s - jnp.max(s, axis=0, keepdims=True))
        ps.append(jnp.where(valid, e / jnp.sum(e, axis=0, keepdims=True), 0.0))
    o_cmp = [_unstack_group(_dot_tn(ps[ch].astype(BF16), vcs[ch]), dec_t) for ch in range(nch)]
    imps = []
    for ch in range(nch):
        p = ps[ch]
        imp = p[:, 0:dec_t]
        for r in range(1, GROUP):
            imp = imp + p[:, r * dec_t:(r + 1) * dec_t]
        imps.append(imp)
    imp_all = jnp.concatenate(imps, axis=1)
    rho_a = lax.broadcasted_iota(jnp.int32, imp_all.shape, 0)
    bid_a = 2 * (rho_a & (n_pages - 1)) + (rho_a >> _log2(n_pages))
    score = jnp.where((bid_a == 0) | (bid_a == cur_blk), FORCED_SCORE, imp_all)
    rank = _block_ranks(score, bid_a, [(i, 2 * (i % n_pages) + i // n_pages) for i in range(nblk)])
    rank = rank + jnp.where(bid_a > cur_blk, (FORCED_SCORE >= score).astype(F32), (FORCED_SCORE > score).astype(F32))
    msel = jnp.where((rank < topk) & (score >= 0.0), 1.0, 0.0).astype(BF16)
    mk_all = _dot_tn(msel, e_ref[...])
    pp, pn, den = [], [], []
    for ch, (bi, g) in enumerate(chains):
        okp = jnp.concatenate([mk_all[ch * dec_t:(ch + 1) * dec_t]] * GROUP, axis=0) > 0.5
        slope = _row_slopes(g, rowq >> _log2(dec_t))
        lgp = jnp.where(okp & (dpast >= 0.0), sp[ch] - slope * dpast, NEG)
        lgn = jnp.where(oknew[bi] & (dnews[bi] >= 0.0), sn[ch] - slope * dnews[bi], NEG)
        m = jnp.maximum(jnp.max(lgp, axis=-1, keepdims=True), jnp.max(lgn, axis=-1, keepdims=True))
        pp.append(jnp.exp(lgp - m))
        pn.append(jnp.exp(lgn - m))
        den.append(jnp.sum(pp[ch], axis=-1, keepdims=True) + jnp.sum(pn[ch], axis=-1, keepdims=True))
    o_sel = [_unstack_group((_dot_nt(pp[ch].astype(BF16), vv_scr[ch]) + _dot_nt(pn[ch].astype(BF16), nv[g]))
                            / den[ch], dec_t) for ch, (bi, g) in enumerate(chains)]

    for bi in range(bps):
        ocs_ref[bi, 0] = jnp.concatenate(o_cmp[bi * N_KV_HEADS:(bi + 1) * N_KV_HEADS], axis=1)
        ocs_ref[bi, 1] = jnp.concatenate(o_sel[bi * N_KV_HEADS:(bi + 1) * N_KV_HEADS], axis=1)


def _nsa_sample(page_flat, cache, q, nkv, w1r, pe_t, w2dup, kg_dup, e_perm, past_len, topk):
    nbatch, dec_t, _ = q.shape
    n_pages = past_len // PAGE_SIZE
    bps = SAMPLE_BATCH_PER_STEP
    nch = bps * N_KV_HEADS
    body = functools.partial(_nsa_sample_body, n_pages=n_pages, past_len=past_len, dec_t=dec_t, topk=topk)
    grid_spec = pltpu.PrefetchScalarGridSpec(
        num_scalar_prefetch=1, grid=(nbatch // bps,),
        in_specs=[pl.BlockSpec(memory_space=pl.ANY),
                  pl.BlockSpec((bps, dec_t, ATTN_WIDTH), lambda i, pt: (i, 0, 0)),
                  pl.BlockSpec((2 * KV_WIDTH, LANES), lambda i, pt: (0, i * bps * dec_t // LANES)),
                  pl.BlockSpec(w1r.shape, lambda i, pt: (0, 0, 0)),
                  pl.BlockSpec(pe_t.shape, lambda i, pt: (0, 0, 0, 0)),
                  pl.BlockSpec(w2dup.shape, lambda i, pt: (0, 0, 0)),
                  pl.BlockSpec(kg_dup.shape, lambda i, pt: (0, 0)),
                  pl.BlockSpec(e_perm.shape, lambda i, pt: (0, 0))],
        out_specs=pl.BlockSpec((bps, 2, dec_t, ATTN_WIDTH), lambda i, pt: (i, 0, 0, 0)),
        scratch_shapes=[pltpu.VMEM((2, bps, PAGE_ROWS // 2, n_pages, PAGE_SIZE), F32),
                        pltpu.VMEM((2, bps, n_pages * PAGE_ROWS // 2, PAGE_SIZE), F32),
                        pltpu.VMEM((bps * 4 * n_pages, HEAD_DIM * LANES), BF16),
                        pltpu.VMEM((nch, 2 * HEAD_DIM, past_len), BF16),
                        pltpu.VMEM((nch, 2 * HEAD_DIM, past_len), BF16),
                        pltpu.SemaphoreType.DMA((2,))])
    return pl.pallas_call(
        body, grid_spec=grid_spec,
        out_shape=jax.ShapeDtypeStruct((nbatch, 2, dec_t, ATTN_WIDTH), F32),
        compiler_params=pltpu.CompilerParams(dimension_semantics=("arbitrary",),
                                             vmem_limit_bytes=_vmem_limit(56 << 20)),
        name="nsa_sample",
    )(page_flat, cache, q, nkv, w1r, pe_t, w2dup, kg_dup, e_perm)


def _win_sample_body(win_ref, q_ref, nw_ref, ocs_ref, gates_ref, gexp_ref, oa_ref, wout_ref, *, past_len, dec_t):
    d = HEAD_DIM
    wbuf = win_ref.shape[2]
    nq = GROUP * dec_t
    rowq = lax.broadcasted_iota(jnp.int32, (nq, 1), 0)
    step = rowq & (dec_t - 1)
    kpos = past_len - wbuf + lax.broadcasted_iota(jnp.int32, (1, wbuf), 1)
    dpast = (past_len + step - kpos).astype(F32)
    okp = (dpast >= 0.0) & (dpast < WINDOW)
    nbat = win_ref.shape[0]
    lane = lax.broadcasted_iota(jnp.int32, (1, LANES), 1)
    per_tile = LANES // dec_t
    offs = [((pl.program_id(0) * nbat + bi) & (per_tile - 1)) * dec_t for bi in range(nbat)]
    dnews = [(step - (lane - off)).astype(F32) for off in offs]
    okns = [(lane >= off) & (lane < off + dec_t) & (dn >= 0.0) & (dn < WINDOW) for off, dn in zip(offs, dnews)]
    chains = [(bi, g) for bi in range(nbat) for g in range(N_KV_HEADS)]
    qst = [_stack_group_queries(q_ref[bi], g) for bi, g in chains]
    sp = [_dot(qst[ch], _twice(win_ref[bi, g * d:(g + 1) * d, :].astype(BF16))) for ch, (bi, g) in enumerate(chains)]
    nk = [_twice(nw_ref[g * d:(g + 1) * d, :].astype(BF16)) for g in range(N_KV_HEADS)]
    nvs = [_twice(nw_ref[KV_WIDTH + g * d:KV_WIDTH + (g + 1) * d, :].astype(BF16)) for g in range(N_KV_HEADS)]
    sn = [_dot(qst[ch], nk[g]) for ch, (bi, g) in enumerate(chains)]
    pp, pn, den = [], [], []
    for ch, (bi, g) in enumerate(chains):
        slope = _row_slopes(g, rowq >> _log2(dec_t))
        lgp = jnp.where(okp, sp[ch] - slope * dpast, NEG)
        lgn = jnp.where(okns[bi], sn[ch] - slope * dnews[bi], NEG)
        m = jnp.maximum(jnp.max(lgp, axis=-1, keepdims=True), jnp.max(lgn, axis=-1, keepdims=True))
        pp.append(jnp.exp(lgp - m))
        pn.append(jnp.exp(lgn - m))
        den.append(jnp.sum(pp[ch], axis=-1, keepdims=True) + jnp.sum(pn[ch], axis=-1, keepdims=True))
    o_win = []
    for ch, (bi, g) in enumerate(chains):
        vt_ = win_ref[bi, KV_WIDTH + g * d:KV_WIDTH + (g + 1) * d, :].astype(BF16)
        o2 = _dot_nt(pp[ch].astype(BF16), _twice(vt_)) + _dot_nt(pn[ch].astype(BF16), nvs[g])
        o_win.append(_unstack_group(o2 / den[ch], dec_t))
    for bi in range(nbat):
        ge = _expand_gates(gates_ref[bi], gexp_ref)
        oa_ref[bi] = (ge[0] * ocs_ref[bi, 0] + ge[1] * ocs_ref[bi, 1]
                      + ge[2] * jnp.concatenate(o_win[bi * N_KV_HEADS:(bi + 1) * N_KV_HEADS], axis=1))

        rolled = pltpu.roll(win_ref[bi], wbuf - dec_t, 1)
        newr = pltpu.roll(nw_ref[...], (LANES - dec_t - offs[bi]) & (LANES - 1), 1)
        wout_ref[bi, :, 0:wbuf - LANES] = rolled[:, 0:wbuf - LANES]
        wout_ref[bi, :, wbuf - LANES:wbuf] = jnp.where(lane >= LANES - dec_t, newr, rolled[:, wbuf - LANES:wbuf])


def _win_sample(win, q, nw, ocs, gates, gexp, past_len):
    nbatch, feat, wbuf = win.shape
    dec_t = q.shape[1]
    body = functools.partial(_win_sample_body, past_len=past_len, dec_t=dec_t)
    b3 = lambda i: (i, 0, 0)
    bps = WINDOW_BATCH_PER_STEP
    return pl.pallas_call(
        body, grid=(nbatch // bps,),
        in_specs=[pl.BlockSpec((bps, feat, wbuf), b3), pl.BlockSpec((bps, dec_t, ATTN_WIDTH), b3),
                  pl.BlockSpec((feat, LANES), lambda i: (1, i * bps * dec_t // LANES)),
                  pl.BlockSpec((bps, 2, dec_t, ATTN_WIDTH), lambda i: (i, 0, 0, 0)),
                  pl.BlockSpec((bps, dec_t, LANES), b3), pl.BlockSpec(gexp.shape, lambda i: (0, 0, 0))],
        out_specs=(pl.BlockSpec((bps, dec_t, ATTN_WIDTH), b3), pl.BlockSpec((bps, feat, wbuf), b3)),
        out_shape=(jax.ShapeDtypeStruct((nbatch, dec_t, ATTN_WIDTH), F32),
                   jax.ShapeDtypeStruct((nbatch, feat, wbuf), F32)),
        compiler_params=pltpu.CompilerParams(dimension_semantics=("parallel",),
                                             vmem_limit_bytes=_vmem_limit(24 << 20)),
        name="win_sample",
    )(win, q, nw, ocs, gates, gexp)


def _hgrn_sample_body(q_ref, f_ref, v_ref, s_ref, o_ref, so_ref, *, dec_t):
    nb = s_ref.shape[3]
    o_ref[...] = jnp.zeros(o_ref.shape, F32)

    sub = 8

    def per_tile(i, carry):
        r0 = pl.multiple_of(i * sub, sub)
        f_t = [f_ref[pl.ds(r0, sub), pl.ds(t * nb, nb)] for t in range(dec_t)]
        q_t = [q_ref[pl.ds(r0, sub), pl.ds(t * nb, nb)] for t in range(dec_t)]
        for j in range(sub):
            s = s_ref[0, r0 + j]
            for t in range(dec_t):
                cols = pl.ds(t * nb, nb)
                fr = f_t[t][j:j + 1, :]
                s = fr * s + (1.0 - fr) * v_ref[:, cols]
                o_ref[:, cols] = o_ref[:, cols] + s * q_t[t][j:j + 1, :]
            so_ref[0, r0 + j] = s
        return carry

    lax.fori_loop(0, s_ref.shape[1] // sub, per_tile, 0)


def _hgrn_sample(ztb, state, dec_t):
    nh, dk, dv, nb = state.shape
    n = ztb.shape[1]
    body = functools.partial(_hgrn_sample_body, dec_t=dec_t)
    q0, f0, v0 = 512 // dk, 1024 // dk, 1536 // dk
    return pl.pallas_call(
        body, grid=(nh,),
        in_specs=[pl.BlockSpec((dk, n), lambda h: (q0 + h, 0)), pl.BlockSpec((dk, n), lambda h: (f0 + h, 0)),
                  pl.BlockSpec((dv, n), lambda h: (v0 + h, 0)),
                  pl.BlockSpec((1, dk, dv, nb), lambda h: (h, 0, 0, 0))],
        out_specs=(pl.BlockSpec((dv, n), lambda h: (h, 0)), pl.BlockSpec((1, dk, dv, nb), lambda h: (h, 0, 0, 0))),
        out_shape=(jax.ShapeDtypeStruct((nh * dv, n), F32), jax.ShapeDtypeStruct(state.shape, F32)),
        compiler_params=pltpu.CompilerParams(dimension_semantics=("parallel",),
                                             vmem_limit_bytes=_vmem_limit(24 << 20)),
        name="hgrn_sample",
    )(ztb, ztb, ztb, state)


def _gate_expander():
    m = np.zeros((N_BRANCH, LANES, ATTN_WIDTH), np.float32)
    for br in range(N_BRANCH):
        for h in range(N_ATTN_HEADS):
            m[br, h * N_BRANCH + br, h * HEAD_DIM:(h + 1) * HEAD_DIM] = 1.0
    return jnp.asarray(m, BF16)


def _block_expander(block_ids, n_keys):
    key_blk = np.arange(n_keys) // SEL_BLOCK
    return jnp.asarray((np.asarray(block_ids)[:, None] == key_blk[None, :]).astype(np.float32), BF16)


def _key_features(n_keys):
    s = np.arange(n_keys)
    m = np.zeros((LANES, n_keys), np.float32)
    m[0:HEAD_DIM] = (np.arange(HEAD_DIM)[:, None] == (s // SEL_BLOCK)[None, :])
    m[HEAD_DIM] = s // SEL_BLOCK
    m[HEAD_DIM + 1] = s % SEL_BLOCK
    return jnp.asarray(m, BF16)


def _query_slope_features(rows):
    m = np.zeros((N_KV_HEADS, GROUP * rows, LANES), np.float32)
    for g in range(N_KV_HEADS):
        for r in range(GROUP):
            m[g, r * rows:(r + 1) * rows, HEAD_DIM] = SLOPES[g][r] * SEL_BLOCK
            m[g, r * rows:(r + 1) * rows, HEAD_DIM + 1] = SLOPES[g][r]
    return jnp.asarray(m, F32)


def _chunk_lower_tri(n, c):
    i = np.arange(n)
    return jnp.asarray(((i[:, None] // c == i[None, :] // c) & (i[None, :] <= i[:, None])).astype(np.float32), BF16)


def _head_indicator():
    m = np.zeros((RNN_WIDTH, LANES), np.float32)
    m[np.arange(RNN_WIDTH), np.arange(RNN_WIDTH) // RNN_DK] = 1.0
    return jnp.asarray(m, BF16)


def _head_block_diag(n):
    i = np.arange(n)
    return jnp.asarray((i[:, None] // RNN_DV == i[None, :] // RNN_DK).astype(np.float32), F32)


def kernel(x_prompt, x_sample, cache_kv, cache_win, state_rnn, page_table, ln_mix, w_in, q_norm, k_norm, cmp_pe,
           cmp_w1, cmp_w2, attn_out_norm, rnn_lb_logits, rnn_out_norm, w_out, ln_mlp, w_up, w_down):
    assert w_in.shape[0] == 1, "single layer"
    b, t, dm = x_prompt.shape
    nbatch, dec_t, _ = x_sample.shape
    n_pool = cache_kv.shape[1]
    n_pages = page_table.shape[1]
    past_len = n_pages * PAGE_SIZE
    wbuf = cache_win.shape[2]
    assert t % PROJ_ROWS == 0 and t % RNN_ROWS == 0 and t % min(SEL_KEYS, t) == 0 and t >= WINDOW
    assert (b * t) % FINISH_ROWS == 0 and (nbatch * dec_t) % FINISH_ROWS == 0 and w_up.shape[2] % FF_CHUNK == 0
    assert (nbatch * dec_t) % PROJ_ROWS == 0 and nbatch == LANES and dec_t <= 8
    assert past_len % SEL_BLOCK == 0 and wbuf == WINDOW and wbuf >= LANES
    assert LANES % dec_t == 0 and (LANES // dec_t) % max(SAMPLE_BATCH_PER_STEP, WINDOW_BATCH_PER_STEP) == 0

    w = w_in[0]
    c_kv, c_gate, c_rq, c_rf, c_ri, c_rg = ATTN_WIDTH, ATTN_WIDTH + 6 * KV_WIDTH, 1304, 1816, 2328, 2840
    gate_cols = jnp.pad(w[:, c_gate:c_rq], ((0, 0), (0, LANES - N_ATTN_HEADS * N_BRANCH)))
    wtok = jnp.concatenate([w[:, 0:ATTN_WIDTH], w[:, c_rq:], gate_cols, w[:, c_kv:c_kv + 2 * KV_WIDTH]],
                           axis=1).T.astype(BF16)
    wft = w[:, c_kv:c_gate].T.astype(BF16)
    wtok_s = jnp.concatenate([w[:, 0:ATTN_WIDTH], w[:, c_rg:], gate_cols], axis=1).T.astype(BF16)
    wtb_s = jnp.concatenate([w[:, c_kv:c_kv + 4 * KV_WIDTH], w[:, c_rq:c_rg]], axis=1).T.astype(BF16)
    wbt_s = w[:, c_kv + 2 * KV_WIDTH:c_gate].T.astype(BF16)
    ln = ln_mix[0][None, :]
    qg = (jnp.tile(q_norm[0], N_ATTN_HEADS) * SCALE)[None, :]
    lbl = rnn_lb_logits.astype(F32)
    lblt = jnp.broadcast_to(lbl[:, :, None], lbl.shape + (PROJ_ROWS,))
    gsel = jnp.broadcast_to(k_norm[0, 1][:, None], (HEAD_DIM, PROJ_ROWS))
    gwin = jnp.broadcast_to(k_norm[0, 2][:, None], (HEAD_DIM, PROJ_ROWS))
    kg_dup = jnp.tile(k_norm[0, 0], 2)[None, :]
    pe = cmp_pe[0]
    pe_tok = jnp.tile(jnp.concatenate([jnp.tile(pe[0], (1, N_KV_HEADS)), jnp.tile(pe[1], (1, N_KV_HEADS))], axis=1),
                      (PROJ_ROWS // CMP_BLOCK, 1))
    w1 = cmp_w1[0].reshape(2, CMP_BLOCK, HEAD_DIM, CMP_HIDDEN)
    zeros = jnp.zeros_like(w1)
    w1bd = jnp.concatenate([jnp.concatenate([w1, zeros], axis=3), jnp.concatenate([zeros, w1], axis=3)],
                           axis=2).astype(BF16)
    w1r = jnp.tile(w1.transpose(0, 2, 1, 3), (1, 1, 2, 1)).astype(BF16)
    w1r = w1r.reshape(2, HEAD_DIM * LANES, CMP_HIDDEN)
    pe_t = jnp.tile(pe.transpose(0, 2, 1), (1, 1, 2))[:, :, None, :]
    w2dup = jnp.tile(cmp_w2[0], (1, 1, 2)).astype(BF16)
    ag = attn_out_norm[0][None, :]
    rg = rnn_out_norm[0][None, :]
    wout = w_out[0].astype(BF16)
    lnm = ln_mlp[0][None, :]
    wup = w_up[0].astype(BF16)
    wdn = w_down[0].astype(BF16)
    gexp = _gate_expander()

    (q_p, gates_p, rq_p, f_p, rv_p, rgs_p, kvc_p, kvt_p, wint_p, att_p) = _proj_prompt(
        x_prompt, ln, wtok, wft, qg, lbl, pe_tok, gsel, gwin)
    cmp_p = _compress_prompt(kvc_p, w1bd, w2dup, kg_dup)
    nb_p = t // CMP_BLOCK
    oa_p = _nsa_prompt(q_p, gates_p, cmp_p, att_p, _key_features(t), _query_slope_features(NSA_Q_ROWS), gexp)
    orn_p, st_p = _hgrn_prompt(rq_p, f_p, rv_p, _chunk_lower_tri(RNN_ROWS, RNN_CHUNK),
                               _head_block_diag(LANES), _head_indicator())
    y_p = _finish(x_prompt.reshape(b * t, dm), oa_p.reshape(b * t, ATTN_WIDTH), orn_p.reshape(b * t, RNN_WIDTH),
                  rgs_p.reshape(b * t, RNN_WIDTH), ag, rg, wout, lnm, wup, wdn).reshape(b, t, dm)
    kv_prompt = kvt_p.reshape(1, b, 4, N_KV_HEADS, HEAD_DIM, t).transpose(0, 1, 5, 2, 3, 4)
    wlen = min(WINDOW, t)
    win_prompt = wint_p[:, :, t - wlen:].reshape(1, b, 2, N_KV_HEADS, HEAD_DIM, wlen).transpose(0, 1, 5, 2, 3, 4)
    hh = LANES // RNN_DV
    st5 = st_p.reshape(b, RNN_WIDTH // LANES, hh, RNN_DV, hh, RNN_DK)
    rnn_prompt = jnp.stack([st5[:, :, i, :, i, :] for i in range(hh)], axis=2)
    rnn_prompt = rnn_prompt.reshape(b, N_RNN_HEADS, RNN_DV, RNN_DK).transpose(0, 1, 3, 2)[None]

    n_s = nbatch * dec_t
    xbt = x_sample.reshape(n_s, dm)
    xtb = x_sample.transpose(1, 0, 2).reshape(n_s, dm)
    q_s, rgs_s, gates_s, ztb, zbt = _proj_sample(xbt, xtb, ln, wtok_s, wtb_s, wbt_s, qg, lblt, gsel, gwin)
    kv_sample = ztb[0:4 * KV_WIDTH].reshape(4, N_KV_HEADS, HEAD_DIM, dec_t, nbatch).transpose(4, 3, 0, 1, 2)[None]
    cache = cache_kv[0].transpose(0, 2, 3, 4, 1).reshape(n_pool, PAGE_ROWS, PAGE_SIZE)
    nblk_s = past_len // CMP_BLOCK
    rho = np.arange(nblk_s)
    e_perm = _block_expander(2 * (rho % n_pages) + rho // n_pages, past_len)
    ns_s = -(-(past_len + dec_t) // SEL_BLOCK)
    q_s3 = q_s.reshape(nbatch, dec_t, ATTN_WIDTH)
    ocs = _nsa_sample(page_table.reshape(-1), cache, q_s3, zbt, w1r, pe_t, w2dup, kg_dup,
                      e_perm, past_len, min(SEL_TOPK, ns_s))
    win = cache_win[0].transpose(0, 2, 3, 4, 1).reshape(nbatch, 2 * KV_WIDTH, wbuf)
    oa_s, win_new = _win_sample(win, q_s3, zbt, ocs, gates_s.reshape(nbatch, dec_t, LANES), gexp, past_len)
    win_sample = win_new.reshape(1, nbatch, 2, N_KV_HEADS, HEAD_DIM, wbuf).transpose(0, 1, 5, 2, 3, 4)
    state = state_rnn[0].transpose(1, 2, 3, 0)
    orn_t, state_new = _hgrn_sample(ztb, state, dec_t)
    rnn_sample = state_new.transpose(3, 0, 1, 2)[None]
    orn_s = orn_t.reshape(RNN_WIDTH, dec_t, nbatch).transpose(2, 1, 0).reshape(n_s, RNN_WIDTH)
    y_s = _finish(xbt, oa_s.reshape(n_s, ATTN_WIDTH), orn_s, rgs_s, ag, rg, wout, lnm, wup, wdn).reshape(nbatch, dec_t, dm)

    return (y_p, y_s, kv_prompt, kv_sample, win_prompt, win_sample, rnn_prompt, rnn_sample)
```

```python
import functools

import numpy as np
import jax
import jax.numpy as jnp
from jax import lax
from jax.experimental import pallas as pl
from jax.experimental.pallas import tpu as pltpu

F32 = jnp.float32
BF16 = jnp.bfloat16

HEAD_DIM = 64
N_ATTN_HEADS = 8
N_KV_HEADS = 2
GROUP = N_ATTN_HEADS // N_KV_HEADS
N_RNN_HEADS = 8
RNN_DK = 64
RNN_DV = 64
ATTN_WIDTH = N_ATTN_HEADS * HEAD_DIM
RNN_WIDTH = N_RNN_HEADS * RNN_DV
KV_WIDTH = N_KV_HEADS * HEAD_DIM
N_BRANCH = 3
CMP_BLOCK = 64
SEL_BLOCK = 64
SEL_TOPK = 16
WINDOW = 512
CMP_HIDDEN = 128
PAGE_SIZE = 128
SCALE = HEAD_DIM ** -0.5
EPS = 1e-6
NEG = -1e30
LOG2E = 1.4426950408889634
FORCED_SCORE = GROUP + 1.0
SLOPES = [[2.0 ** (-(g * GROUP + r + 1)) for r in range(GROUP)] for g in range(N_KV_HEADS)]

LANES = 128
VMEM_BYTES_V7X = 64 * 1024 * 1024

PROJ_ROWS = 256
NSA_Q_ROWS = 128
SEL_KEYS = 512
ROW_BLOCK = 16
RNN_ROWS = 256
RNN_CHUNK = 16
FF_CHUNK = 2048
FINISH_ROWS = 512

NT = (((1,), (1,)), ((), ()))
TN = (((0,), (0,)), ((), ()))


def _vmem_limit(nbytes):
    return int(min(VMEM_BYTES_V7X - (8 << 20), max(nbytes, 16 << 20)))


def _dot(a, b):
    return jnp.dot(a, b, preferred_element_type=F32)


def _dot_nt(a, b):
    return lax.dot_general(a, b, NT, preferred_element_type=F32)


def _dot_tn(a, b):
    return lax.dot_general(a, b, TN, preferred_element_type=F32)


def _low_half(shape):
    lane = lax.broadcasted_iota(jnp.int32, shape, len(shape) - 1)
    return (lane & HEAD_DIM) == 0


def _head_mean_sq(x):
    outs = []
    for j in range(x.shape[-1] // LANES):
        blk = x[:, j * LANES:(j + 1) * LANES]
        sq = blk * blk
        low = _low_half(blk.shape)
        s_lo = jnp.sum(jnp.where(low, sq, 0.0), axis=-1, keepdims=True)
        s_hi = jnp.sum(jnp.where(low, 0.0, sq), axis=-1, keepdims=True)
        outs.append(jnp.where(low, s_lo, s_hi))
    return jnp.concatenate(outs, axis=-1) * (1.0 / HEAD_DIM)


def _row_rms(x):
    return x * lax.rsqrt(jnp.mean(x * x, axis=-1, keepdims=True) + EPS)


def _col_head_norm(rows, gain):
    ms = jnp.mean(rows * rows, axis=0, keepdims=True)
    return rows * lax.rsqrt(ms + EPS) * gain


def _silu(z):
    return z * jax.nn.sigmoid(z)


def _lower_bound(logits, axis):
    m = jnp.max(logits, axis=axis, keepdims=True)
    e = jnp.exp(logits - m)
    lb = e / jnp.sum(e, axis=axis, keepdims=True)
    return lb[0:1] if axis == 0 else lb


def _stack_group_queries(q, g):
    rows = q.shape[0]
    low = _low_half((rows, LANES))
    zero = jnp.zeros((rows, LANES), q.dtype)
    pa = q[:, g * 2 * LANES:g * 2 * LANES + LANES]
    pb = q[:, g * 2 * LANES + LANES:(g + 1) * 2 * LANES]
    return jnp.concatenate([jnp.where(low, pa, zero), jnp.where(low, zero, pa),
                            jnp.where(low, pb, zero), jnp.where(low, zero, pb)], axis=0)


def _unstack_group(o2, rows):
    low = _low_half((rows, LANES))
    return jnp.concatenate([jnp.where(low, o2[0:rows], o2[rows:2 * rows]),
                            jnp.where(low, o2[2 * rows:3 * rows], o2[3 * rows:4 * rows])], axis=1)


def _log2(n):
    assert n > 0 and n & (n - 1) == 0, n
    return n.bit_length() - 1


def _twice(x):
    return jnp.concatenate([x, x], axis=0)


def _row_slopes(g, row_head):
    s = jnp.full(row_head.shape, SLOPES[g][GROUP - 1], F32)
    for r in range(GROUP - 2, -1, -1):
        s = jnp.where(row_head == r, SLOPES[g][r], s)
    return s


def _block_ranks(score, ids, cand):
    sub = 8
    nrow = score.shape[0]
    in_order = all(r == i for r, i in cand) and nrow % sub == 0
    rank = jnp.zeros(score.shape, F32)
    for row, id_i in cand:
        s_i = score[row:row + 1, :]
        if in_order:
            parts = []
            for v in range(nrow // sub):
                blk = score[v * sub:(v + 1) * sub, :]
                if (v + 1) * sub - 1 <= id_i:
                    parts.append((s_i > blk).astype(F32))
                elif v * sub > id_i:
                    parts.append((s_i >= blk).astype(F32))
                else:
                    parts.append(jnp.where(ids[v * sub:(v + 1) * sub, :] > id_i,
                                           (s_i >= blk).astype(F32), (s_i > blk).astype(F32)))
            rank = rank + jnp.concatenate(parts, axis=0)
        else:
            rank = rank + jnp.where(ids > id_i, (s_i >= score).astype(F32), (s_i > score).astype(F32))
    return rank


def _expand_gates(gates, gexp_ref):
    hi = gates.astype(BF16)
    lo = (gates - hi.astype(F32)).astype(BF16)
    return [_dot(hi, gexp_ref[br]) + _dot(lo, gexp_ref[br]) for br in range(N_BRANCH)]


TOK_Q, TOK_RQ, TOK_RF, TOK_RI, TOK_RG, TOK_GATE, TOK_KVC, TOK_END = 0, 512, 1024, 1536, 2048, 2560, 2688, 2944


def _proj_prompt_body(x_ref, ln_ref, wtok_ref, wft_ref, qg_ref, lbl_ref, pe_ref, gsel_ref, gwin_ref,
                      q_ref, gates_ref, rq_ref, f_ref, rv_ref, rgs_ref, kvc_ref, kvt_ref, wint_ref, att_ref):
    xb = (_row_rms(x_ref[0]) * ln_ref[...]).astype(BF16)

    def tok(lo, hi):
        return _dot_nt(xb, wtok_ref[lo:hi, :])

    zq = tok(TOK_Q, TOK_RQ)
    q_ref[0] = (zq * lax.rsqrt(_head_mean_sq(zq) + EPS) * qg_ref[...]).astype(BF16)
    rq_ref[0] = _silu(tok(TOK_RQ, TOK_RF))
    lb = _lower_bound(lbl_ref[...], 0)
    f_ref[0] = lb + (1.0 - lb) * jax.nn.sigmoid(tok(TOK_RF, TOK_RI))
    rv_ref[0] = tok(TOK_RI, TOK_RG)
    rgs_ref[0] = _silu(tok(TOK_RG, TOK_GATE))
    gates_ref[0] = jax.nn.sigmoid(tok(TOK_GATE, TOK_KVC))
    zc = tok(TOK_KVC, TOK_END) + pe_ref[...]
    kvc_ref[0, 0] = zc[:, 0:LANES]
    kvc_ref[1, 0] = zc[:, LANES:2 * LANES]

    zf = _dot_nt(wft_ref[...], xb)
    d = HEAD_DIM
    ksel = jnp.concatenate([_col_head_norm(zf[256 + g * d:256 + (g + 1) * d], gsel_ref[...])
                            for g in range(N_KV_HEADS)], axis=0)
    kwin = jnp.concatenate([_col_head_norm(zf[512 + g * d:512 + (g + 1) * d], gwin_ref[...])
                            for g in range(N_KV_HEADS)], axis=0)
    vsel = zf[384:512]
    vwin = zf[640:768]
    kvt_ref[0, 0:256] = zf[0:256]
    kvt_ref[0, 256:384] = ksel
    kvt_ref[0, 384:512] = vsel
    wint_ref[0, 0:128] = kwin
    wint_ref[0, 128:256] = vwin
    att_ref[0, 0:128] = ksel.astype(BF16)
    att_ref[0, 128:256] = vsel.astype(BF16)
    att_ref[0, 256:384] = kwin.astype(BF16)
    att_ref[0, 384:512] = vwin.astype(BF16)


def _proj_prompt(x, ln, wtok, wft, qg, lbl, pe_tok, gsel, gwin):
    b, t, dm = x.shape
    tm = PROJ_ROWS
    grid = (b, t // tm)
    row = lambda i, j: (i, j, 0)
    col = lambda i, j: (i, 0, j)
    const2 = lambda i, j: (0, 0)
    out_shape = (
        jax.ShapeDtypeStruct((b, t, ATTN_WIDTH), BF16),
        jax.ShapeDtypeStruct((b, t, LANES), F32),
        jax.ShapeDtypeStruct((b, t, RNN_WIDTH), F32),
        jax.ShapeDtypeStruct((b, t, RNN_WIDTH), F32),
        jax.ShapeDtypeStruct((b, t, RNN_WIDTH), F32),
        jax.ShapeDtypeStruct((b, t, RNN_WIDTH), F32),
        jax.ShapeDtypeStruct((2, b, t, LANES), F32),
        jax.ShapeDtypeStruct((b, 4 * KV_WIDTH, t), F32),
        jax.ShapeDtypeStruct((b, 2 * KV_WIDTH, t), F32),
        jax.ShapeDtypeStruct((b, 4 * KV_WIDTH, t), BF16),
    )
    out_specs = (
        pl.BlockSpec((1, tm, ATTN_WIDTH), row), pl.BlockSpec((1, tm, LANES), row),
        pl.BlockSpec((1, tm, RNN_WIDTH), row), pl.BlockSpec((1, tm, RNN_WIDTH), row),
        pl.BlockSpec((1, tm, RNN_WIDTH), row), pl.BlockSpec((1, tm, RNN_WIDTH), row),
        pl.BlockSpec((2, 1, tm, LANES), lambda i, j: (0, i, j, 0)),
        pl.BlockSpec((1, 4 * KV_WIDTH, tm), col), pl.BlockSpec((1, 2 * KV_WIDTH, tm), col),
        pl.BlockSpec((1, 4 * KV_WIDTH, tm), col),
    )
    in_specs = [
        pl.BlockSpec((1, tm, dm), row), pl.BlockSpec(ln.shape, const2),
        pl.BlockSpec(wtok.shape, const2), pl.BlockSpec(wft.shape, const2),
        pl.BlockSpec(qg.shape, const2), pl.BlockSpec(lbl.shape, const2), pl.BlockSpec(pe_tok.shape, const2),
        pl.BlockSpec(gsel.shape, const2), pl.BlockSpec(gwin.shape, const2),
    ]
    return pl.pallas_call(
        _proj_prompt_body, grid=grid, in_specs=in_specs, out_specs=out_specs, out_shape=out_shape,
        compiler_params=pltpu.CompilerParams(dimension_semantics=("parallel", "parallel"),
                                             vmem_limit_bytes=_vmem_limit(48 << 20)),
        name="proj_prompt",
    )(x, ln, wtok, wft, qg, lbl, pe_tok, gsel, gwin)


def _compress_prompt_body(x_ref, w1_ref, w2_ref, kg_ref, out_ref):
    c = pl.program_id(0)
    nb = out_ref.shape[2]
    acc = jnp.zeros((nb, 2 * CMP_HIDDEN), F32)
    for pos in range(CMP_BLOCK):
        xp = x_ref[0, 0, pl.ds(pos, nb, stride=CMP_BLOCK), :]
        acc = acc + _dot(xp.astype(BF16), w1_ref[0, pos])
    hb = _silu(acc).astype(BF16)
    outs = []
    for g in range(N_KV_HEADS):
        y = _dot(hb[:, g * CMP_HIDDEN:(g + 1) * CMP_HIDDEN], w2_ref[0])
        yn = _row_rms(y) * kg_ref[...]
        outs.append(jnp.where(c == 0, yn, y))
    out_ref[0, 0] = jnp.concatenate(outs, axis=1)


def _compress_prompt(kvc, w1bd, w2dup, kg_dup):
    _, b, t, _ = kvc.shape
    nb = t // CMP_BLOCK
    return pl.pallas_call(
        _compress_prompt_body, grid=(2, b),
        in_specs=[pl.BlockSpec((1, 1, t, LANES), lambda c, i: (c, i, 0, 0)),
                  pl.BlockSpec((1,) + w1bd.shape[1:], lambda c, i: (c, 0, 0, 0)),
                  pl.BlockSpec((1,) + w2dup.shape[1:], lambda c, i: (c, 0, 0)),
                  pl.BlockSpec(kg_dup.shape, lambda c, i: (0, 0))],
        out_specs=pl.BlockSpec((1, 1, nb, 2 * LANES), lambda c, i: (c, i, 0, 0)),
        out_shape=jax.ShapeDtypeStruct((2, b, nb, 2 * LANES), F32),
        compiler_params=pltpu.CompilerParams(dimension_semantics=("arbitrary", "arbitrary"),
                                             vmem_limit_bytes=_vmem_limit(32 << 20)),
        name="compress_prompt",
    )(kvc, w1bd, w2dup, kg_dup)


def _row_max_update(sc_scr, nm_scr, m_scr, mn_scr, ncol, tq):
    rb = ROW_BLOCK
    for i in range(GROUP * tq // rb):
        rows = slice(i * rb, (i + 1) * rb)
        qrows = slice((i * rb) % tq, (i * rb) % tq + rb)
        mx = jnp.full((rb, LANES), NEG, F32)
        for j in range(ncol):
            cols = slice(j * LANES, (j + 1) * LANES)
            v = sc_scr[rows, cols]
            if nm_scr is not None:
                v = v + nm_scr[qrows, cols]
                sc_scr[rows, cols] = v
            mx = jnp.maximum(mx, v)
        mn_scr[rows, :] = jnp.maximum(m_scr[rows, :], jnp.max(mx, axis=-1, keepdims=True))


def _softmax_update(sc_scr, p_scr, m_scr, mn_scr, l_scr, a_scr, ncol, tq):
    rb = ROW_BLOCK
    for i in range(GROUP * tq // rb):
        rows = slice(i * rb, (i + 1) * rb)
        mn = mn_scr[rows, :]
        tot = jnp.zeros((rb, LANES), F32)
        for j in range(ncol):
            cols = slice(j * LANES, (j + 1) * LANES)
            p = jnp.exp(sc_scr[rows, cols] - mn)
            tot = tot + p
            p_scr[rows, cols] = p.astype(BF16)
        alpha = jnp.exp(m_scr[rows, :] - mn)
        l_scr[rows, :] = alpha * l_scr[rows, :] + jnp.sum(tot, axis=-1, keepdims=True)
        a_scr[rows, :] = alpha
        m_scr[rows, :] = mn


def _nsa_prompt_body(q_ref, gates_ref, cmp_ref, att_ref, kfeat_ref, qfeat_ref, gexp_ref, oa_ref,
                     sc_scr, sc2_scr, nm_scr, p_scr, m_scr, mn_scr, l_scr, a_scr, acc_scr, *, seq, topk):
    tq = NSA_Q_ROWS
    tk = min(SEL_KEYS, seq)
    nb = cmp_ref.shape[2]
    t0 = pl.program_id(1) * tq
    wk = min(WINDOW + tq, seq)
    ws = pl.multiple_of(jnp.clip(t0 - WINDOW, 0, seq - wk), LANES)
    q = q_ref[0]
    d = HEAD_DIM
    groups = range(N_KV_HEADS)
    def reset(stats):
        for g in groups:
            m_scr[g] = jnp.full(m_scr.shape[1:], NEG, F32)
            for ref in stats:
                ref[g] = jnp.zeros(ref.shape[1:], F32)

    sc_bufs = (sc_scr, sc2_scr)

    def logits(buf, g, qa, kt_, kf, width):
        sc_bufs[buf][g, :, 0:width] = _dot(qa, jnp.concatenate([kt_, kt_, kf], axis=0))

    def softmax_tiles(buf, ncol, masked):
        sc = sc_bufs[buf]
        for g in groups:
            _row_max_update(sc.at[g], nm_scr if masked else None, m_scr.at[g], mn_scr.at[g], ncol, tq)
        for g in groups:
            _softmax_update(sc.at[g], p_scr.at[g], m_scr.at[g], mn_scr.at[g], l_scr.at[g], a_scr.at[g], ncol, tq)

    qst = [_stack_group_queries(q, g) for g in groups]
    qwin = [jnp.concatenate([qst[g], qfeat_ref[g].astype(BF16)], axis=1) for g in groups]

    kcs = [cmp_ref[0, 0][:, g * LANES:(g + 1) * LANES].astype(BF16) for g in groups]
    vcs = [cmp_ref[1, 0][:, g * LANES:(g + 1) * LANES].astype(BF16) for g in groups]
    st = [_dot_nt(kcs[g], qst[g]) for g in groups]

    reset((l_scr,))
    kfw = kfeat_ref[:, pl.ds(ws, wk)]
    ddw = (t0 - ws) + lax.broadcasted_iota(jnp.int32, (tq, wk), 0) - lax.broadcasted_iota(jnp.int32, (tq, wk), 1)
    nm_scr[:, 0:wk] = jnp.where((ddw >= 0) & (ddw < WINDOW), 0.0, NEG)
    for g in groups:
        logits(0, g, qwin[g], att_ref[0, 2 * KV_WIDTH + g * d:2 * KV_WIDTH + (g + 1) * d, pl.ds(ws, wk)], kfw, wk)

    blk = lax.broadcasted_iota(jnp.int32, (nb, GROUP * tq), 0)
    colq = lax.broadcasted_iota(jnp.int32, (1, GROUP * tq), 1)
    dist = (t0 + (colq & (tq - 1)) - (blk * CMP_BLOCK + (CMP_BLOCK - 1))).astype(F32)
    valid = dist >= 0.0
    ps = []
    for g in groups:
        s = jnp.where(valid, st[g] - _row_slopes(g, colq >> _log2(tq)) * dist, NEG)
        e = jnp.exp(s - jnp.max(s, axis=0, keepdims=True))
        ps.append(jnp.where(valid, e / jnp.sum(e, axis=0, keepdims=True), 0.0))
    o_cmp = [_unstack_group(_dot_tn(ps[g].astype(BF16), vcs[g]), tq) for g in groups]

    softmax_tiles(0, wk // LANES, True)
    o_win = []
    for g in groups:
        vw = att_ref[0, 3 * KV_WIDTH + g * d:3 * KV_WIDTH + (g + 1) * d, pl.ds(ws, wk)]
        o_win.append(_unstack_group(_dot_nt(p_scr[g, :, 0:wk], _twice(vw)) / l_scr[g], tq))

    bj = lax.broadcasted_iota(jnp.int32, (nb, tq), 0)
    cur = (t0 + lax.broadcasted_iota(jnp.int32, (nb, tq), 1)) >> _log2(SEL_BLOCK)
    force = (bj == 0) | (bj == cur)
    qsel = []
    for g in groups:
        p = ps[g]
        imp = p[:, 0:tq] + p[:, tq:2 * tq] + p[:, 2 * tq:3 * tq] + p[:, 3 * tq:4 * tq]
        score = jnp.where(bj <= cur, jnp.where(force, FORCED_SCORE, imp), -1.0)
        rank = _block_ranks(score, bj, [(i, i) for i in range(nb)])
        mneg = jnp.where((rank < topk) & (score >= 0.0), 0.0, NEG)
        mtok = jnp.concatenate([mneg, jnp.zeros((LANES - nb, tq), F32)], axis=0).T
        qsel.append(jnp.concatenate(
            [qst[g], (qfeat_ref[g] + jnp.concatenate([mtok] * GROUP, axis=0)).astype(BF16)], axis=1))

    reset((l_scr, acc_scr))
    n_kt = (t0 + tq - 1) // tk + 1

    def sel_logits(kt, buf):
        s0 = pl.multiple_of(kt * tk, tk)
        kf = kfeat_ref[:, pl.ds(s0, tk)]
        for g in groups:
            logits(buf, g, qsel[g], att_ref[0, g * d:(g + 1) * d, pl.ds(s0, tk)], kf, tk)

    def sel_update(kt, buf, causal):
        s0 = pl.multiple_of(kt * tk, tk)
        if causal:
            dd = (t0 - s0) + lax.broadcasted_iota(jnp.int32, (tq, tk), 0) - lax.broadcasted_iota(jnp.int32, (tq, tk), 1)
            nm_scr[:, 0:tk] = jnp.where(dd >= 0, 0.0, NEG)
        softmax_tiles(buf, tk // LANES, causal)
        for g in groups:
            vt_ = att_ref[0, KV_WIDTH + g * d:KV_WIDTH + (g + 1) * d, pl.ds(s0, tk)]
            acc_scr[g] = a_scr[g] * acc_scr[g] + _dot_nt(p_scr[g, :, 0:tk], _twice(vt_))

    sel_logits(0, 0)

    def full_tile(kt, carry):
        for par in (0, 1):
            @pl.when((kt & 1) == par)
            def _(par=par):
                sel_logits(kt + 1, 1 - par)
                sel_update(kt, par, False)
        return carry

    lax.fori_loop(0, n_kt - 1, full_tile, 0)
    for par in (0, 1):
        @pl.when(((n_kt - 1) & 1) == par)
        def _(par=par):
            sel_update(n_kt - 1, par, True)
    o_sel = [_unstack_group(acc_scr[g] / l_scr[g], tq) for g in groups]

    ge = _expand_gates(gates_ref[0], gexp_ref)
    oa_ref[0] = (ge[0] * jnp.concatenate(o_cmp, axis=1) + ge[1] * jnp.concatenate(o_sel, axis=1)
                 + ge[2] * jnp.concatenate(o_win, axis=1))


def _nsa_prompt(q, gates, cmpkv, att, k_feat, q_feat, gexp):
    b, t, _ = q.shape
    tq = NSA_Q_ROWS
    nb = t // CMP_BLOCK
    assert nb <= HEAD_DIM, "the block mask uses 64 feature lanes"
    cw = max(min(SEL_KEYS, t), min(WINDOW + tq, t))
    body = functools.partial(_nsa_prompt_body, seq=t, topk=min(SEL_TOPK, nb))
    ng = N_KV_HEADS
    stat = pltpu.VMEM((ng, GROUP * tq, LANES), F32)
    return pl.pallas_call(
        body, grid=(b, t // tq),
        in_specs=[pl.BlockSpec((1, tq, ATTN_WIDTH), lambda i, j: (i, j, 0)),
                  pl.BlockSpec((1, tq, LANES), lambda i, j: (i, j, 0)),
                  pl.BlockSpec((2, 1, nb, 2 * LANES), lambda i, j: (0, i, 0, 0)),
                  pl.BlockSpec((1, 4 * KV_WIDTH, t), lambda i, j: (i, 0, 0)),
                  pl.BlockSpec(k_feat.shape, lambda i, j: (0, 0)),
                  pl.BlockSpec(q_feat.shape, lambda i, j: (0, 0, 0)),
                  pl.BlockSpec(gexp.shape, lambda i, j: (0, 0, 0))],
        out_specs=pl.BlockSpec((1, tq, ATTN_WIDTH), lambda i, j: (i, j, 0)),
        out_shape=jax.ShapeDtypeStruct((b, t, ATTN_WIDTH), F32),
        scratch_shapes=[pltpu.VMEM((ng, GROUP * tq, cw), F32), pltpu.VMEM((ng, GROUP * tq, cw), F32),
                        pltpu.VMEM((tq, cw), F32),
                        pltpu.VMEM((ng, GROUP * tq, cw), BF16), stat, stat, stat, stat, stat],
        compiler_params=pltpu.CompilerParams(dimension_semantics=("parallel", "parallel"),
                                             vmem_limit_bytes=_vmem_limit(48 << 20)),
        name="nsa_prompt",
    )(q, gates, cmpkv, att, k_feat, q_feat, gexp)


def _split3(x):
    hi = x.astype(BF16)
    r1 = x - hi.astype(F32)
    mid = r1.astype(BF16)
    lo = (r1 - mid.astype(F32)).astype(BF16)
    return hi, mid, lo


def _hgrn_prompt_body(rq_ref, f_ref, rv_ref, ltri_ref, bd_ref, ind_ref, indt_ref, o_ref, st_ref,
                      s_scr, cum_scr, k_scr, prod_scr):
    c16 = RNN_CHUNK
    nbat = rq_ref.shape[0]

    @pl.when(pl.program_id(0) == 0)
    def _():
        s_scr[...] = jnp.zeros(s_scr.shape, F32)

    ltri = ltri_ref[...]
    for bi in range(nbat):
        f = f_ref[bi]
        hi, mid, lo = _split3(jnp.log(f))
        cum = (_dot(ltri, hi) + _dot(ltri, mid) + _dot(ltri, lo)) * LOG2E
        cum_scr[bi] = cum
        k_scr[bi] = cum - jnp.log2(1.0 - f)

    si = lax.broadcasted_iota(jnp.int32, (c16, c16, RNN_WIDTH), 0)
    ti = lax.broadcasted_iota(jnp.int32, (c16, c16, RNN_WIDTH), 1)
    causal = si <= ti
    npair = RNN_WIDTH // LANES

    def pair_products(c, buf, bi):
        r0 = pl.multiple_of(c * c16, c16)
        cc = cum_scr[bi, pl.ds(r0, c16), :]
        qc = rq_ref[bi, pl.ds(r0, c16), :]
        lk = k_scr[bi, pl.ds(r0, c16), :]
        dec = jnp.exp2(jnp.where(causal, cc[None, :, :] - lk[:, None, :], NEG))
        prod_scr[buf, bi] = (qc[None, :, :] * dec).reshape(c16 * c16, RNN_WIDTH).astype(BF16)

    def recurrence(c, buf, nxt):
        r0 = pl.multiple_of(c * c16, c16)
        rows = range(nbat)
        a1 = [_dot(prod_scr[buf, bi], ind_ref[...]) for bi in rows]
        if nxt is not None:
            for bi in rows:
                pair_products(nxt, 1 - buf, bi)
        cc = [cum_scr[bi, pl.ds(r0, c16), :] for bi in rows]
        last = [x[c16 - 1:c16, :] for x in cc]
        o_int = []
        for bi in rows:
            qd = (rq_ref[bi, pl.ds(r0, c16), :] * jnp.exp2(cc[bi])).astype(BF16)
            o_int.append(jnp.concatenate([_dot_nt(qd[:, p * LANES:(p + 1) * LANES], s_scr[bi, p].astype(BF16))
                                          for p in range(npair)], axis=1))
        a2 = [_dot(a1[bi].astype(BF16), indt_ref[...]).reshape(c16, c16, RNN_WIDTH) for bi in rows]
        for bi in rows:
            kd = jnp.exp2(last[bi] - k_scr[bi, pl.ds(r0, c16), :]).astype(BF16)
            vb = rv_ref[bi, pl.ds(r0, c16), :].astype(BF16)
            dl = jnp.exp2(last[bi])
            for p in range(npair):
                u = _dot_tn(vb[:, p * LANES:(p + 1) * LANES], kd[:, p * LANES:(p + 1) * LANES])
                s_scr[bi, p] = s_scr[bi, p] * dl[:, p * LANES:(p + 1) * LANES] + u * bd_ref[...]
        for bi in rows:
            vc = rv_ref[bi, pl.ds(r0, c16), :]
            o_ref[bi, pl.ds(r0, c16), :] = o_int[bi] + jnp.sum(a2[bi] * vc[:, None, :], axis=0)

    nchunk = rq_ref.shape[1] // c16
    assert nchunk % 2 == 0 and nchunk >= 4
    for bi in range(nbat):
        pair_products(0, 0, bi)

    def two_chunks(j, carry):
        for sub in (0, 1):
            recurrence(2 * j + sub, sub, 2 * j + sub + 1)
        return carry

    lax.fori_loop(0, nchunk // 2 - 1, two_chunks, 0)
    recurrence(nchunk - 2, 0, nchunk - 1)
    recurrence(nchunk - 1, 1, None)
    st_ref[...] = s_scr[...]


def _hgrn_prompt(rq, f, rv, ltri, bd, ind):
    b, t, w = rq.shape
    tc = RNN_ROWS
    npair = w // LANES
    row = lambda j: (0, j, 0)
    const = lambda j: (0, 0)
    indt = ind.T
    return pl.pallas_call(
        _hgrn_prompt_body, grid=(t // tc,),
        in_specs=[pl.BlockSpec((b, tc, w), row), pl.BlockSpec((b, tc, w), row), pl.BlockSpec((b, tc, w), row),
                  pl.BlockSpec(ltri.shape, const), pl.BlockSpec(bd.shape, const),
                  pl.BlockSpec(ind.shape, const), pl.BlockSpec(indt.shape, const)],
        out_specs=(pl.BlockSpec((b, tc, w), row), pl.BlockSpec((b, npair, LANES, LANES), lambda j: (0, 0, 0, 0))),
        out_shape=(jax.ShapeDtypeStruct((b, t, w), F32), jax.ShapeDtypeStruct((b, npair, LANES, LANES), F32)),
        scratch_shapes=[pltpu.VMEM((b, npair, LANES, LANES), F32), pltpu.VMEM((b, tc, w), F32),
                        pltpu.VMEM((b, tc, w), F32), pltpu.VMEM((2, b, RNN_CHUNK * RNN_CHUNK, w), BF16)],
        compiler_params=pltpu.CompilerParams(dimension_semantics=("arbitrary",),
                                             vmem_limit_bytes=_vmem_limit(40 << 20)),
        name="hgrn_prompt",
    )(rq, f, rv, ltri, bd, ind, indt)


def _finish_body(x_ref, oa_ref, orn_ref, rgs_ref, ag_ref, rg_ref, wout_ref, lnm_ref, wup_ref, wdn_ref, y_ref, hn_scr):
    @pl.when(pl.program_id(1) == 0)
    def _():
        oa = oa_ref[...]
        orn = orn_ref[...]
        a_n = oa * lax.rsqrt(_head_mean_sq(oa) + EPS) * ag_ref[...]
        r_n = orn * lax.rsqrt(_head_mean_sq(orn) + EPS) * rg_ref[...] * rgs_ref[...]
        h = (x_ref[...] + _dot(a_n.astype(BF16), wout_ref[0:ATTN_WIDTH, :])
             + _dot(r_n.astype(BF16), wout_ref[ATTN_WIDTH:ATTN_WIDTH + RNN_WIDTH, :]))
        y_ref[...] = h
        hn_scr[...] = (_row_rms(h) * lnm_ref[...]).astype(BF16)

    u = jnp.maximum(_dot(hn_scr[...], wup_ref[...]), 0.0)
    y_ref[...] += _dot((u * u).astype(BF16), wdn_ref[...])


def _finish(x, oa, orn, rgs, ag, rg, wout, lnm, wup, wdn):
    n, dm = x.shape
    tm = FINISH_ROWS
    dff = wup.shape[1]
    row = lambda i, j: (i, 0)
    const = lambda i, j: (0, 0)
    return pl.pallas_call(
        _finish_body, grid=(n // tm, dff // FF_CHUNK),
        in_specs=[pl.BlockSpec((tm, dm), row), pl.BlockSpec((tm, ATTN_WIDTH), row),
                  pl.BlockSpec((tm, RNN_WIDTH), row), pl.BlockSpec((tm, RNN_WIDTH), row),
                  pl.BlockSpec(ag.shape, const), pl.BlockSpec(rg.shape, const),
                  pl.BlockSpec(wout.shape, const), pl.BlockSpec(lnm.shape, const),
                  pl.BlockSpec((dm, FF_CHUNK), lambda i, j: (0, j)),
                  pl.BlockSpec((FF_CHUNK, dm), lambda i, j: (j, 0))],
        out_specs=pl.BlockSpec((tm, dm), row),
        out_shape=jax.ShapeDtypeStruct((n, dm), F32),
        scratch_shapes=[pltpu.VMEM((tm, dm), BF16)],
        compiler_params=pltpu.CompilerParams(dimension_semantics=("parallel", "arbitrary"),
                                             vmem_limit_bytes=_vmem_limit(48 << 20)),
        name="finish",
    )(x, oa, orn, rgs, ag, rg, wout, lnm, wup, wdn)


def _proj_sample_body(xbt_ref, xtb_ref, ln_ref, wtok_ref, wtb_ref, wbt_ref, qg_ref, lbl_ref, gsel_ref, gwin_ref,
                      q_ref, rgs_ref, gates_ref, ztb_ref, zbt_ref):
    xb = (_row_rms(xbt_ref[...]) * ln_ref[...]).astype(BF16)
    xt = (_row_rms(xtb_ref[...]) * ln_ref[...]).astype(BF16)
    zq = _dot_nt(xb, wtok_ref[0:ATTN_WIDTH, :])
    q_ref[...] = (zq * lax.rsqrt(_head_mean_sq(zq) + EPS) * qg_ref[...]).astype(BF16)
    rgs_ref[...] = _silu(_dot_nt(xb, wtok_ref[ATTN_WIDTH:ATTN_WIDTH + RNN_WIDTH, :]))
    gates_ref[...] = jax.nn.sigmoid(_dot_nt(xb, wtok_ref[ATTN_WIDTH + RNN_WIDTH:ATTN_WIDTH + RNN_WIDTH + LANES, :]))

    d = HEAD_DIM
    zt = _dot_nt(wtb_ref[...], xt)
    ztb_ref[0:256] = zt[0:256]
    for g in range(N_KV_HEADS):
        ztb_ref[256 + g * d:256 + (g + 1) * d] = _col_head_norm(zt[256 + g * d:256 + (g + 1) * d], gsel_ref[...])
    ztb_ref[384:512] = zt[384:512]
    ztb_ref[512:1024] = _silu(zt[512:1024])
    lb = _lower_bound(lbl_ref[...], 0)[0]
    ztb_ref[1024:1536] = lb + (1.0 - lb) * jax.nn.sigmoid(zt[1024:1536])
    ztb_ref[1536:2048] = zt[1536:2048]

    zb = _dot_nt(wbt_ref[...], xb)
    for g in range(N_KV_HEADS):
        zbt_ref[g * d:(g + 1) * d] = _col_head_norm(zb[g * d:(g + 1) * d], gsel_ref[...])
        zbt_ref[256 + g * d:256 + (g + 1) * d] = _col_head_norm(zb[256 + g * d:256 + (g + 1) * d], gwin_ref[...])
    zbt_ref[128:256] = zb[128:256]
    zbt_ref[384:512] = zb[384:512]


def _proj_sample(xbt, xtb, ln, wtok, wtb, wbt, qg, lblt, gsel, gwin):
    n, dm = xbt.shape
    tm = PROJ_ROWS
    row = lambda i: (i, 0)
    col = lambda i: (0, i)
    const = lambda i: (0, 0)
    return pl.pallas_call(
        _proj_sample_body, grid=(n // tm,),
        in_specs=[pl.BlockSpec((tm, dm), row), pl.BlockSpec((tm, dm), row), pl.BlockSpec(ln.shape, const),
                  pl.BlockSpec(wtok.shape, const), pl.BlockSpec(wtb.shape, const), pl.BlockSpec(wbt.shape, const),
                  pl.BlockSpec(qg.shape, const), pl.BlockSpec(lblt.shape, lambda i: (0, 0, 0)),
                  pl.BlockSpec(gsel.shape, const), pl.BlockSpec(gwin.shape, const)],
        out_specs=(pl.BlockSpec((tm, ATTN_WIDTH), row), pl.BlockSpec((tm, RNN_WIDTH), row),
                   pl.BlockSpec((tm, LANES), row), pl.BlockSpec((wtb.shape[0], tm), col),
                   pl.BlockSpec((wbt.shape[0], tm), col)),
        out_shape=(jax.ShapeDtypeStruct((n, ATTN_WIDTH), BF16), jax.ShapeDtypeStruct((n, RNN_WIDTH), F32),
                   jax.ShapeDtypeStruct((n, LANES), F32), jax.ShapeDtypeStruct((wtb.shape[0], n), F32),
                   jax.ShapeDtypeStruct((wbt.shape[0], n), F32)),
        compiler_params=pltpu.CompilerParams(dimension_semantics=("parallel",),
                                             vmem_limit_bytes=_vmem_limit(40 << 20)),
        name="proj_sample",
    )(xbt, xtb, ln, wtok, wtb, wbt, qg, lblt, gsel, gwin)


PAGE_ROWS = 4 * KV_WIDTH
SAMPLE_BATCH_PER_STEP = 4
WINDOW_BATCH_PER_STEP = 4


def _nsa_sample_body(pt_ref, cache_ref, q_ref, nkv_ref, w1_ref, pe_ref, w2_ref, kg_ref, e_ref,
                     ocs_ref, cbuf, sbuf, lhs_scr, kk_scr, vv_scr, sem, *, n_pages, past_len, dec_t, topk):
    step_id = pl.program_id(0)
    nsteps = pl.num_programs(0)
    slot = step_id % 2
    d = HEAD_DIM
    nblk = 2 * n_pages
    half_rows = PAGE_ROWS // 2
    bps = q_ref.shape[0]

    def page_copies(st, sl, bi, j):
        pg = pt_ref[(st * bps + bi) * n_pages + j]
        return (pltpu.make_async_copy(cache_ref.at[pg, pl.ds(0, half_rows)], cbuf.at[sl, bi, :, j, :], sem.at[sl]),
                pltpu.make_async_copy(cache_ref.at[pg, pl.ds(half_rows, half_rows)],
                                      sbuf.at[sl, bi, pl.ds(j * half_rows, half_rows)], sem.at[sl]))

    def all_copies(st, sl):
        return [cp for bi in range(bps) for j in range(n_pages) for cp in page_copies(st, sl, bi, j)]

    @pl.when(step_id == 0)
    def _():
        for cp in all_copies(0, 0):
            cp.start()

    @pl.when(step_id + 1 < nsteps)
    def _():
        for cp in all_copies(step_id + 1, 1 - slot):
            cp.start()

    for cp in all_copies(step_id, slot):
        cp.wait()

    low = _low_half((n_pages, LANES))
    rows_b = 4 * n_pages

    def compress(c):
        for dd in range(d):
            rows = []
            for bi in range(bps):
                for g in range(N_KV_HEADS):
                    xg = cbuf[slot, bi, (c * N_KV_HEADS + g) * d + dd]
                    xg = xg + pe_ref[c, dd]
                    rows += [jnp.where(low, xg, 0.0), jnp.where(low, 0.0, xg)]
            lhs_scr[:, dd * LANES:(dd + 1) * LANES] = jnp.concatenate(rows, axis=0).astype(BF16)
        acc = _dot(lhs_scr[...], w1_ref[c])
        return _dot(_silu(acc).astype(BF16), w2_ref[c])

    kc_all = _row_rms(compress(0)) * kg_ref[...]
    vc_all = compress(1)

    nq = GROUP * dec_t
    colq = lax.broadcasted_iota(jnp.int32, (1, nq), 1)
    rowq = lax.broadcasted_iota(jnp.int32, (nq, 1), 0)
    cur_blk = past_len // SEL_BLOCK
    chains = [(bi, g) for bi in range(bps) for g in range(N_KV_HEADS)]
    nch = len(chains)
    step = rowq & (dec_t - 1)
    qpos = past_len + (colq & (dec_t - 1))
    rho_q = lax.broadcasted_iota(jnp.int32, (nblk, 1), 0)
    end = (2 * (rho_q & (n_pages - 1)) + (rho_q >> _log2(n_pages))) * CMP_BLOCK + (CMP_BLOCK - 1)
    dist = (qpos - end).astype(F32)
    valid = dist >= 0.0
    kpos = lax.broadcasted_iota(jnp.int32, (1, past_len), 1)
    dpast = (past_len + step - kpos).astype(F32)
    lane = lax.broadcasted_iota(jnp.int32, (1, LANES), 1)
    per_tile = LANES // dec_t
    offs = [((step_id * bps + bi) & (per_tile - 1)) * dec_t for bi in range(bps)]
    dnews = [(step - (lane - off)).astype(F32) for off in offs]
    oknew = [(lane >= off) & (lane < off + dec_t) for off in offs]

    qst = [_stack_group_queries(q_ref[bi], g) for bi, g in chains]
    kcs = [kc_all[bi * rows_b + g * nblk:bi * rows_b + (g + 1) * nblk].astype(BF16) for bi, g in chains]
    vcs = [vc_all[bi * rows_b + g * nblk:bi * rows_b + (g + 1) * nblk].astype(BF16) for bi, g in chains]
    st = [_dot_nt(kcs[ch], qst[ch]) for ch in range(nch)]
    for ch, (bi, g) in enumerate(chains):
        for pg in range(n_pages):
            kt_ = sbuf[slot, bi, pl.ds(pg * half_rows + g * d, d), :].astype(BF16)
            vt_ = sbuf[slot, bi, pl.ds(pg * half_rows + (N_KV_HEADS + g) * d, d), :].astype(BF16)
            kk_scr[ch, 0:d, pg * PAGE_SIZE:(pg + 1) * PAGE_SIZE] = kt_
            kk_scr[ch, d:2 * d, pg * PAGE_SIZE:(pg + 1) * PAGE_SIZE] = kt_
            vv_scr[ch, 0:d, pg * PAGE_SIZE:(pg + 1) * PAGE_SIZE] = vt_
            vv_scr[ch, d:2 * d, pg * PAGE_SIZE:(pg + 1) * PAGE_SIZE] = vt_
    sp = [_dot(qst[ch], kk_scr[ch]) for ch in range(nch)]
    nk = [_twice(nkv_ref[g * d:(g + 1) * d, :].astype(BF16)) for g in range(N_KV_HEADS)]
    nv = [_twice(nkv_ref[KV_WIDTH + g * d:KV_WIDTH + (g + 1) * d, :].astype(BF16)) for g in range(N_KV_HEADS)]
    sn = [_dot(qst[ch], nk[g]) for ch, (bi, g) in enumerate(chains)]
    ps = []
    for ch, (bi, g) in enumerate(chains):
        s = jnp.where(valid, st[ch] - _row_slopes(g, colq >> _log2(dec_t)) * dist, NEG)
        e = jnp.exp(s - jnp.max(s, axis=0, keepdims=True))
        ps.append(jnp.where(valid, e / jnp.sum(e, axis=0, keepdims=True), 0.0))
    o_cmp = [_unstack_group(_dot_tn(ps[ch].astype(BF16), vcs[ch]), dec_t) for ch in range(nch)]
    imps = []
    for ch in range(nch):
        p = ps[ch]
        imp = p[:, 0:dec_t]
        for r in range(1, GROUP):
            imp = imp + p[:, r * dec_t:(r + 1) * dec_t]
        imps.append(imp)
    imp_all = jnp.concatenate(imps, axis=1)
    rho_a = lax.broadcasted_iota(jnp.int32, imp_all.shape, 0)
    bid_a = 2 * (rho_a & (n_pages - 1)) + (rho_a >> _log2(n_pages))
    score = jnp.where((bid_a == 0) | (bid_a == cur_blk), FORCED_SCORE, imp_all)
    rank = _block_ranks(score, bid_a, [(i, 2 * (i % n_pages) + i // n_pages) for i in range(nblk)])
    rank = rank + jnp.where(bid_a > cur_blk, (FORCED_SCORE >= score).astype(F32), (FORCED_SCORE > score).astype(F32))
    msel = jnp.where((rank < topk) & (score >= 0.0), 1.0, 0.0).astype(BF16)
    mk_all = _dot_tn(msel, e_ref[...])
    pp, pn, den = [], [], []
    for ch, (bi, g) in enumerate(chains):
        okp = jnp.concatenate([mk_all[ch * dec_t:(ch + 1) * dec_t]] * GROUP, axis=0) > 0.5
        slope = _row_slopes(g, rowq >> _log2(dec_t))
        lgp = jnp.where(okp & (dpast >= 0.0), sp[ch] - slope * dpast, NEG)
        lgn = jnp.where(oknew[bi] & (dnews[bi] >= 0.0), sn[ch] - slope * dnews[bi], NEG)
        m = jnp.maximum(jnp.max(lgp, axis=-1, keepdims=True), jnp.max(lgn, axis=-1, keepdims=True))
        pp.append(jnp.exp(lgp - m))
        pn.append(jnp.exp(lgn - m))
        den.append(jnp.sum(pp[ch], axis=-1, keepdims=True) + jnp.sum(pn[ch], axis=-1, keepdims=True))
    o_sel = [_unstack_group((_dot_nt(pp[ch].astype(BF16), vv_scr[ch]) + _dot_nt(pn[ch].astype(BF16), nv[g]))
                            / den[ch], dec_t) for ch, (bi, g) in enumerate(chains)]

    for bi in range(bps):
        ocs_ref[bi, 0] = jnp.concatenate(o_cmp[bi * N_KV_HEADS:(bi + 1) * N_KV_HEADS], axis=1)
        ocs_ref[bi, 1] = jnp.concatenate(o_sel[bi * N_KV_HEADS:(bi + 1) * N_KV_HEADS], axis=1)


def _nsa_sample(page_flat, cache, q, nkv, w1r, pe_t, w2dup, kg_dup, e_perm, past_len, topk):
    nbatch, dec_t, _ = q.shape
    n_pages = past_len // PAGE_SIZE
    bps = SAMPLE_BATCH_PER_STEP
    nch = bps * N_KV_HEADS
    body = functools.partial(_nsa_sample_body, n_pages=n_pages, past_len=past_len, dec_t=dec_t, topk=topk)
    grid_spec = pltpu.PrefetchScalarGridSpec(
        num_scalar_prefetch=1, grid=(nbatch // bps,),
        in_specs=[pl.BlockSpec(memory_space=pl.ANY),
                  pl.BlockSpec((bps, dec_t, ATTN_WIDTH), lambda i, pt: (i, 0, 0)),
                  pl.BlockSpec((2 * KV_WIDTH, LANES), lambda i, pt: (0, i * bps * dec_t // LANES)),
                  pl.BlockSpec(w1r.shape, lambda i, pt: (0, 0, 0)),
                  pl.BlockSpec(pe_t.shape, lambda i, pt: (0, 0, 0, 0)),
                  pl.BlockSpec(w2dup.shape, lambda i, pt: (0, 0, 0)),
                  pl.BlockSpec(kg_dup.shape, lambda i, pt: (0, 0)),
                  pl.BlockSpec(e_perm.shape, lambda i, pt: (0, 0))],
        out_specs=pl.BlockSpec((bps, 2, dec_t, ATTN_WIDTH), lambda i, pt: (i, 0, 0, 0)),
        scratch_shapes=[pltpu.VMEM((2, bps, PAGE_ROWS // 2, n_pages, PAGE_SIZE), F32),
                        pltpu.VMEM((2, bps, n_pages * PAGE_ROWS // 2, PAGE_SIZE), F32),
                        pltpu.VMEM((bps * 4 * n_pages, HEAD_DIM * LANES), BF16),
                        pltpu.VMEM((nch, 2 * HEAD_DIM, past_len), BF16),
                        pltpu.VMEM((nch, 2 * HEAD_DIM, past_len), BF16),
                        pltpu.SemaphoreType.DMA((2,))])
    return pl.pallas_call(
        body, grid_spec=grid_spec,
        out_shape=jax.ShapeDtypeStruct((nbatch, 2, dec_t, ATTN_WIDTH), F32),
        compiler_params=pltpu.CompilerParams(dimension_semantics=("arbitrary",),
                                             vmem_limit_bytes=_vmem_limit(56 << 20)),
        name="nsa_sample",
    )(page_flat, cache, q, nkv, w1r, pe_t, w2dup, kg_dup, e_perm)


def _win_sample_body(win_ref, q_ref, nw_ref, ocs_ref, gates_ref, gexp_ref, oa_ref, wout_ref, *, past_len, dec_t):
    d = HEAD_DIM
    wbuf = win_ref.shape[2]
    nq = GROUP * dec_t
    rowq = lax.broadcasted_iota(jnp.int32, (nq, 1), 0)
    step = rowq & (dec_t - 1)
    kpos = past_len - wbuf + lax.broadcasted_iota(jnp.int32, (1, wbuf), 1)
    dpast = (past_len + step - kpos).astype(F32)
    okp = (dpast >= 0.0) & (dpast < WINDOW)
    nbat = win_ref.shape[0]
    lane = lax.broadcasted_iota(jnp.int32, (1, LANES), 1)
    per_tile = LANES // dec_t
    offs = [((pl.program_id(0) * nbat + bi) & (per_tile - 1)) * dec_t for bi in range(nbat)]
    dnews = [(step - (lane - off)).astype(F32) for off in offs]
    okns = [(lane >= off) & (lane < off + dec_t) & (dn >= 0.0) & (dn < WINDOW) for off, dn in zip(offs, dnews)]
    chains = [(bi, g) for bi in range(nbat) for g in range(N_KV_HEADS)]
    qst = [_stack_group_queries(q_ref[bi], g) for bi, g in chains]
    sp = [_dot(qst[ch], _twice(win_ref[bi, g * d:(g + 1) * d, :].astype(BF16))) for ch, (bi, g) in enumerate(chains)]
    nk = [_twice(nw_ref[g * d:(g + 1) * d, :].astype(BF16)) for g in range(N_KV_HEADS)]
    nvs = [_twice(nw_ref[KV_WIDTH + g * d:KV_WIDTH + (g + 1) * d, :].astype(BF16)) for g in range(N_KV_HEADS)]
    sn = [_dot(qst[ch], nk[g]) for ch, (bi, g) in enumerate(chains)]
    pp, pn, den = [], [], []
    for ch, (bi, g) in enumerate(chains):
        slope = _row_slopes(g, rowq >> _log2(dec_t))
        lgp = jnp.where(okp, sp[ch] - slope * dpast, NEG)
        lgn = jnp.where(okns[bi], sn[ch] - slope * dnews[bi], NEG)
        m = jnp.maximum(jnp.max(lgp, axis=-1, keepdims=True), jnp.max(lgn, axis=-1, keepdims=True))
        pp.append(jnp.exp(lgp - m))
        pn.append(jnp.exp(lgn - m))
        den.append(jnp.sum(pp[ch], axis=-1, keepdims=True) + jnp.sum(pn[ch], axis=-1, keepdims=True))
    o_win = []
    for ch, (bi, g) in enumerate(chains):
        vt_ = win_ref[bi, KV_WIDTH + g * d:KV_WIDTH + (g + 1) * d, :].astype(BF16)
        o2 = _dot_nt(pp[ch].astype(BF16), _twice(vt_)) + _dot_nt(pn[ch].astype(BF16), nvs[g])
        o_win.append(_unstack_group(o2 / den[ch], dec_t))
    for bi in range(nbat):
        ge = _expand_gates(gates_ref[bi], gexp_ref)
        oa_ref[bi] = (ge[0] * ocs_ref[bi, 0] + ge[1] * ocs_ref[bi, 1]
                      + ge[2] * jnp.concatenate(o_win[bi * N_KV_HEADS:(bi + 1) * N_KV_HEADS], axis=1))

        rolled = pltpu.roll(win_ref[bi], wbuf - dec_t, 1)
        newr = pltpu.roll(nw_ref[...], (LANES - dec_t - offs[bi]) & (LANES - 1), 1)
        wout_ref[bi, :, 0:wbuf - LANES] = rolled[:, 0:wbuf - LANES]
        wout_ref[bi, :, wbuf - LANES:wbuf] = jnp.where(lane >= LANES - dec_t, newr, rolled[:, wbuf - LANES:wbuf])


def _win_sample(win, q, nw, ocs, gates, gexp, past_len):
    nbatch, feat, wbuf = win.shape
    dec_t = q.shape[1]
    body = functools.partial(_win_sample_body, past_len=past_len, dec_t=dec_t)
    b3 = lambda i: (i, 0, 0)
    bps = WINDOW_BATCH_PER_STEP
    return pl.pallas_call(
        body, grid=(nbatch // bps,),
        in_specs=[pl.BlockSpec((bps, feat, wbuf), b3), pl.BlockSpec((bps, dec_t, ATTN_WIDTH), b3),
                  pl.BlockSpec((feat, LANES), lambda i: (1, i * bps * dec_t // LANES)),
                  pl.BlockSpec((bps, 2, dec_t, ATTN_WIDTH), lambda i: (i, 0, 0, 0)),
                  pl.BlockSpec((bps, dec_t, LANES), b3), pl.BlockSpec(gexp.shape, lambda i: (0, 0, 0))],
        out_specs=(pl.BlockSpec((bps, dec_t, ATTN_WIDTH), b3), pl.BlockSpec((bps, feat, wbuf), b3)),
        out_shape=(jax.ShapeDtypeStruct((nbatch, dec_t, ATTN_WIDTH), F32),
                   jax.ShapeDtypeStruct((nbatch, feat, wbuf), F32)),
        compiler_params=pltpu.CompilerParams(dimension_semantics=("parallel",),
                                             vmem_limit_bytes=_vmem_limit(24 << 20)),
        name="win_sample",
    )(win, q, nw, ocs, gates, gexp)


def _hgrn_sample_body(q_ref, f_ref, v_ref, s_ref, o_ref, so_ref, *, dec_t):
    nb = s_ref.shape[3]
    o_ref[...] = jnp.zeros(o_ref.shape, F32)

    sub = 8

    def per_tile(i, carry):
        r0 = pl.multiple_of(i * sub, sub)
        f_t = [f_ref[pl.ds(r0, sub), pl.ds(t * nb, nb)] for t in range(dec_t)]
        q_t = [q_ref[pl.ds(r0, sub), pl.ds(t * nb, nb)] for t in range(dec_t)]
        for j in range(sub):
            s = s_ref[0, r0 + j]
            for t in range(dec_t):
                cols = pl.ds(t * nb, nb)
                fr = f_t[t][j:j + 1, :]
                s = fr * s + (1.0 - fr) * v_ref[:, cols]
                o_ref[:, cols] = o_ref[:, cols] + s * q_t[t][j:j + 1, :]
            so_ref[0, r0 + j] = s
        return carry

    lax.fori_loop(0, s_ref.shape[1] // sub, per_tile, 0)


def _hgrn_sample(ztb, state, dec_t):
    nh, dk, dv, nb = state.shape
    n = ztb.shape[1]
    body = functools.partial(_hgrn_sample_body, dec_t=dec_t)
    q0, f0, v0 = 512 // dk, 1024 // dk, 1536 // dk
    return pl.pallas_call(
        body, grid=(nh,),
        in_specs=[pl.BlockSpec((dk, n), lambda h: (q0 + h, 0)), pl.BlockSpec((dk, n), lambda h: (f0 + h, 0)),
                  pl.BlockSpec((dv, n), lambda h: (v0 + h, 0)),
                  pl.BlockSpec((1, dk, dv, nb), lambda h: (h, 0, 0, 0))],
        out_specs=(pl.BlockSpec((dv, n), lambda h: (h, 0)), pl.BlockSpec((1, dk, dv, nb), lambda h: (h, 0, 0, 0))),
        out_shape=(jax.ShapeDtypeStruct((nh * dv, n), F32), jax.ShapeDtypeStruct(state.shape, F32)),
        compiler_params=pltpu.CompilerParams(dimension_semantics=("parallel",),
                                             vmem_limit_bytes=_vmem_limit(24 << 20)),
        name="hgrn_sample",
    )(ztb, ztb, ztb, state)


def _gate_expander():
    m = np.zeros((N_BRANCH, LANES, ATTN_WIDTH), np.float32)
    for br in range(N_BRANCH):
        for h in range(N_ATTN_HEADS):
            m[br, h * N_BRANCH + br, h * HEAD_DIM:(h + 1) * HEAD_DIM] = 1.0
    return jnp.asarray(m, BF16)


def _block_expander(block_ids, n_keys):
    key_blk = np.arange(n_keys) // SEL_BLOCK
    return jnp.asarray((np.asarray(block_ids)[:, None] == key_blk[None, :]).astype(np.float32), BF16)


def _key_features(n_keys):
    s = np.arange(n_keys)
    m = np.zeros((LANES, n_keys), np.float32)
    m[0:HEAD_DIM] = (np.arange(HEAD_DIM)[:, None] == (s // SEL_BLOCK)[None, :])
    m[HEAD_DIM] = s // SEL_BLOCK
    m[HEAD_DIM + 1] = s % SEL_BLOCK
    return jnp.asarray(m, BF16)


def _query_slope_features(rows):
    m = np.zeros((N_KV_HEADS, GROUP * rows, LANES), np.float32)
    for g in range(N_KV_HEADS):
        for r in range(GROUP):
            m[g, r * rows:(r + 1) * rows, HEAD_DIM] = SLOPES[g][r] * SEL_BLOCK
            m[g, r * rows:(r + 1) * rows, HEAD_DIM + 1] = SLOPES[g][r]
    return jnp.asarray(m, F32)


def _chunk_lower_tri(n, c):
    i = np.arange(n)
    return jnp.asarray(((i[:, None] // c == i[None, :] // c) & (i[None, :] <= i[:, None])).astype(np.float32), BF16)


def _head_indicator():
    m = np.zeros((RNN_WIDTH, LANES), np.float32)
    m[np.arange(RNN_WIDTH), np.arange(RNN_WIDTH) // RNN_DK] = 1.0
    return jnp.asarray(m, BF16)


def _head_block_diag(n):
    i = np.arange(n)
    return jnp.asarray((i[:, None] // RNN_DV == i[None, :] // RNN_DK).astype(np.float32), F32)


def kernel(x_prompt, x_sample, cache_kv, cache_win, state_rnn, page_table, ln_mix, w_in, q_norm, k_norm, cmp_pe,
           cmp_w1, cmp_w2, attn_out_norm, rnn_lb_logits, rnn_out_norm, w_out, ln_mlp, w_up, w_down):
    assert w_in.shape[0] == 1, "single layer"
    b, t, dm = x_prompt.shape
    nbatch, dec_t, _ = x_sample.shape
    n_pool = cache_kv.shape[1]
    n_pages = page_table.shape[1]
    past_len = n_pages * PAGE_SIZE
    wbuf = cache_win.shape[2]
    assert t % PROJ_ROWS == 0 and t % RNN_ROWS == 0 and t % min(SEL_KEYS, t) == 0 and t >= WINDOW
    assert (b * t) % FINISH_ROWS == 0 and (nbatch * dec_t) % FINISH_ROWS == 0 and w_up.shape[2] % FF_CHUNK == 0
    assert (nbatch * dec_t) % PROJ_ROWS == 0 and nbatch == LANES and dec_t <= 8
    assert past_len % SEL_BLOCK == 0 and wbuf == WINDOW and wbuf >= LANES
    assert LANES % dec_t == 0 and (LANES // dec_t) % max(SAMPLE_BATCH_PER_STEP, WINDOW_BATCH_PER_STEP) == 0

    w = w_in[0]
    c_kv, c_gate, c_rq, c_rf, c_ri, c_rg = ATTN_WIDTH, ATTN_WIDTH + 6 * KV_WIDTH, 1304, 1816, 2328, 2840
    gate_cols = jnp.pad(w[:, c_gate:c_rq], ((0, 0), (0, LANES - N_ATTN_HEADS * N_BRANCH)))
    wtok = jnp.concatenate([w[:, 0:ATTN_WIDTH], w[:, c_rq:], gate_cols, w[:, c_kv:c_kv + 2 * KV_WIDTH]],
                           axis=1).T.astype(BF16)
    wft = w[:, c_kv:c_gate].T.astype(BF16)
    wtok_s = jnp.concatenate([w[:, 0:ATTN_WIDTH], w[:, c_rg:], gate_cols], axis=1).T.astype(BF16)
    wtb_s = jnp.concatenate([w[:, c_kv:c_kv + 4 * KV_WIDTH], w[:, c_rq:c_rg]], axis=1).T.astype(BF16)
    wbt_s = w[:, c_kv + 2 * KV_WIDTH:c_gate].T.astype(BF16)
    ln = ln_mix[0][None, :]
    qg = (jnp.tile(q_norm[0], N_ATTN_HEADS) * SCALE)[None, :]
    lbl = rnn_lb_logits.astype(F32)
    lblt = jnp.broadcast_to(lbl[:, :, None], lbl.shape + (PROJ_ROWS,))
    gsel = jnp.broadcast_to(k_norm[0, 1][:, None], (HEAD_DIM, PROJ_ROWS))
    gwin = jnp.broadcast_to(k_norm[0, 2][:, None], (HEAD_DIM, PROJ_ROWS))
    kg_dup = jnp.tile(k_norm[0, 0], 2)[None, :]
    pe = cmp_pe[0]
    pe_tok = jnp.tile(jnp.concatenate([jnp.tile(pe[0], (1, N_KV_HEADS)), jnp.tile(pe[1], (1, N_KV_HEADS))], axis=1),
                      (PROJ_ROWS // CMP_BLOCK, 1))
    w1 = cmp_w1[0].reshape(2, CMP_BLOCK, HEAD_DIM, CMP_HIDDEN)
    zeros = jnp.zeros_like(w1)
    w1bd = jnp.concatenate([jnp.concatenate([w1, zeros], axis=3), jnp.concatenate([zeros, w1], axis=3)],
                           axis=2).astype(BF16)
    w1r = jnp.tile(w1.transpose(0, 2, 1, 3), (1, 1, 2, 1)).astype(BF16)
    w1r = w1r.reshape(2, HEAD_DIM * LANES, CMP_HIDDEN)
    pe_t = jnp.tile(pe.transpose(0, 2, 1), (1, 1, 2))[:, :, None, :]
    w2dup = jnp.tile(cmp_w2[0], (1, 1, 2)).astype(BF16)
    ag = attn_out_norm[0][None, :]
    rg = rnn_out_norm[0][None, :]
    wout = w_out[0].astype(BF16)
    lnm = ln_mlp[0][None, :]
    wup = w_up[0].astype(BF16)
    wdn = w_down[0].astype(BF16)
    gexp = _gate_expander()

    (q_p, gates_p, rq_p, f_p, rv_p, rgs_p, kvc_p, kvt_p, wint_p, att_p) = _proj_prompt(
        x_prompt, ln, wtok, wft, qg, lbl, pe_tok, gsel, gwin)
    cmp_p = _compress_prompt(kvc_p, w1bd, w2dup, kg_dup)
    nb_p = t // CMP_BLOCK
    oa_p = _nsa_prompt(q_p, gates_p, cmp_p, att_p, _key_features(t), _query_slope_features(NSA_Q_ROWS), gexp)
    orn_p, st_p = _hgrn_prompt(rq_p, f_p, rv_p, _chunk_lower_tri(RNN_ROWS, RNN_CHUNK),
                               _head_block_diag(LANES), _head_indicator())
    y_p = _finish(x_prompt.reshape(b * t, dm), oa_p.reshape(b * t, ATTN_WIDTH), orn_p.reshape(b * t, RNN_WIDTH),
                  rgs_p.reshape(b * t, RNN_WIDTH), ag, rg, wout, lnm, wup, wdn).reshape(b, t, dm)
    kv_prompt = kvt_p.reshape(1, b, 4, N_KV_HEADS, HEAD_DIM, t).transpose(0, 1, 5, 2, 3, 4)
    wlen = min(WINDOW, t)
    win_prompt = wint_p[:, :, t - wlen:].reshape(1, b, 2, N_KV_HEADS, HEAD_DIM, wlen).transpose(0, 1, 5, 2, 3, 4)
    hh = LANES // RNN_DV
    st5 = st_p.reshape(b, RNN_WIDTH // LANES, hh, RNN_DV, hh, RNN_DK)
    rnn_prompt = jnp.stack([st5[:, :, i, :, i, :] for i in range(hh)], axis=2)
    rnn_prompt = rnn_prompt.reshape(b, N_RNN_HEADS, RNN_DV, RNN_DK).transpose(0, 1, 3, 2)[None]

    n_s = nbatch * dec_t
    xbt = x_sample.reshape(n_s, dm)
    xtb = x_sample.transpose(1, 0, 2).reshape(n_s, dm)
    q_s, rgs_s, gates_s, ztb, zbt = _proj_sample(xbt, xtb, ln, wtok_s, wtb_s, wbt_s, qg, lblt, gsel, gwin)
    kv_sample = ztb[0:4 * KV_WIDTH].reshape(4, N_KV_HEADS, HEAD_DIM, dec_t, nbatch).transpose(4, 3, 0, 1, 2)[None]
    cache = cache_kv[0].transpose(0, 2, 3, 4, 1).reshape(n_pool, PAGE_ROWS, PAGE_SIZE)
    nblk_s = past_len // CMP_BLOCK
    rho = np.arange(nblk_s)
    e_perm = _block_expander(2 * (rho % n_pages) + rho // n_pages, past_len)
    ns_s = -(-(past_len + dec_t) // SEL_BLOCK)
    q_s3 = q_s.reshape(nbatch, dec_t, ATTN_WIDTH)
    ocs = _nsa_sample(page_table.reshape(-1), cache, q_s3, zbt, w1r, pe_t, w2dup, kg_dup,
                      e_perm, past_len, min(SEL_TOPK, ns_s))
    win = cache_win[0].transpose(0, 2, 3, 4, 1).reshape(nbatch, 2 * KV_WIDTH, wbuf)
    oa_s, win_new = _win_sample(win, q_s3, zbt, ocs, gates_s.reshape(nbatch, dec_t, LANES), gexp, past_len)
    win_sample = win_new.reshape(1, nbatch, 2, N_KV_HEADS, HEAD_DIM, wbuf).transpose(0, 1, 5, 2, 3, 4)
    state = state_rnn[0].transpose(1, 2, 3, 0)
    orn_t, state_new = _hgrn_sample(ztb, state, dec_t)
    rnn_sample = state_new.transpose(3, 0, 1, 2)[None]
    orn_s = orn_t.reshape(RNN_WIDTH, dec_t, nbatch).transpose(2, 1, 0).reshape(n_s, RNN_WIDTH)
    y_s = _finish(xbt, oa_s.reshape(n_s, ATTN_WIDTH), orn_s, rgs_s, ag, rg, wout, lnm, wup, wdn).reshape(nbatch, dec_t, dm)

    return (y_p, y_s, kv_prompt, kv_sample, win_prompt, win_sample, rnn_prompt, rnn_sample)
```

```python
import functools

import numpy as np
import jax
import jax.numpy as jnp
from jax import lax
from jax.experimental import pallas as pl
from jax.experimental.pallas import tpu as pltpu

F32 = jnp.float32
BF16 = jnp.bfloat16

HEAD_DIM = 64
N_ATTN_HEADS = 8
N_KV_HEADS = 2
GROUP = N_ATTN_HEADS // N_KV_HEADS
N_RNN_HEADS = 8
RNN_DK = 64
RNN_DV = 64
ATTN_WIDTH = N_ATTN_HEADS * HEAD_DIM
RNN_WIDTH = N_RNN_HEADS * RNN_DV
KV_WIDTH = N_KV_HEADS * HEAD_DIM
N_BRANCH = 3
CMP_BLOCK = 64
SEL_BLOCK = 64
SEL_TOPK = 16
WINDOW = 512
CMP_HIDDEN = 128
PAGE_SIZE = 128
SCALE = HEAD_DIM ** -0.5
EPS = 1e-6
NEG = -1e30
LOG2E = 1.4426950408889634
FORCED_SCORE = GROUP + 1.0
SLOPES = [[2.0 ** (-(g * GROUP + r + 1)) for r in range(GROUP)] for g in range(N_KV_HEADS)]

LANES = 128
VMEM_BYTES_V7X = 64 * 1024 * 1024

PROJ_ROWS = 512
NSA_Q_ROWS = 128
SEL_KEYS = 512
ROW_BLOCK = 16
RNN_ROWS = 512
CUMSUM_ROWS = 256
RNN_CHUNK = 16
FF_CHUNK = 2048
FINISH_ROWS = 512

NT = (((1,), (1,)), ((), ()))
TN = (((0,), (0,)), ((), ()))


def _vmem_limit(nbytes):
    return int(min(VMEM_BYTES_V7X - (8 << 20), max(nbytes, 16 << 20)))


def _dot(a, b):
    return jnp.dot(a, b, preferred_element_type=F32)


def _dot_nt(a, b):
    return lax.dot_general(a, b, NT, preferred_element_type=F32)


def _dot_tn(a, b):
    return lax.dot_general(a, b, TN, preferred_element_type=F32)


def _low_half(shape):
    lane = lax.broadcasted_iota(jnp.int32, shape, len(shape) - 1)
    return (lane & HEAD_DIM) == 0


def _head_mean_sq(x):
    outs = []
    for j in range(x.shape[-1] // LANES):
        blk = x[:, j * LANES:(j + 1) * LANES]
        sq = blk * blk
        low = _low_half(blk.shape)
        s_lo = jnp.sum(jnp.where(low, sq, 0.0), axis=-1, keepdims=True)
        s_hi = jnp.sum(jnp.where(low, 0.0, sq), axis=-1, keepdims=True)
        outs.append(jnp.where(low, s_lo, s_hi))
    return jnp.concatenate(outs, axis=-1) * (1.0 / HEAD_DIM)


def _row_rms(x):
    return x * lax.rsqrt(jnp.mean(x * x, axis=-1, keepdims=True) + EPS)


def _col_head_norm(rows, gain):
    ms = jnp.mean(rows * rows, axis=0, keepdims=True)
    return rows * lax.rsqrt(ms + EPS) * gain


def _silu(z):
    return z * jax.nn.sigmoid(z)


def _lower_bound(logits, axis):
    m = jnp.max(logits, axis=axis, keepdims=True)
    e = jnp.exp(logits - m)
    lb = e / jnp.sum(e, axis=axis, keepdims=True)
    return lb[0:1] if axis == 0 else lb


def _stack_group_queries(q, g):
    rows = q.shape[0]
    low = _low_half((rows, LANES))
    zero = jnp.zeros((rows, LANES), q.dtype)
    pa = q[:, g * 2 * LANES:g * 2 * LANES + LANES]
    pb = q[:, g * 2 * LANES + LANES:(g + 1) * 2 * LANES]
    return jnp.concatenate([jnp.where(low, pa, zero), jnp.where(low, zero, pa),
                            jnp.where(low, pb, zero), jnp.where(low, zero, pb)], axis=0)


def _unstack_group(o2, rows):
    low = _low_half((rows, LANES))
    return jnp.concatenate([jnp.where(low, o2[0:rows], o2[rows:2 * rows]),
                            jnp.where(low, o2[2 * rows:3 * rows], o2[3 * rows:4 * rows])], axis=1)


def _log2(n):
    assert n > 0 and n & (n - 1) == 0, n
    return n.bit_length() - 1


def _twice(x):
    return jnp.concatenate([x, x], axis=0)


def _row_slopes(g, row_head):
    s = jnp.full(row_head.shape, SLOPES[g][GROUP - 1], F32)
    for r in range(GROUP - 2, -1, -1):
        s = jnp.where(row_head == r, SLOPES[g][r], s)
    return s


def _block_ranks(score, ids, cand):
    sub = 8
    nrow = score.shape[0]
    in_order = all(r == i for r, i in cand) and nrow % sub == 0
    rank = jnp.zeros(score.shape, F32)
    for row, id_i in cand:
        s_i = score[row:row + 1, :]
        if in_order:
            parts = []
            for v in range(nrow // sub):
                blk = score[v * sub:(v + 1) * sub, :]
                if (v + 1) * sub - 1 <= id_i:
                    parts.append((s_i > blk).astype(F32))
                elif v * sub > id_i:
                    parts.append((s_i >= blk).astype(F32))
                else:
                    parts.append(jnp.where(ids[v * sub:(v + 1) * sub, :] > id_i,
                                           (s_i >= blk).astype(F32), (s_i > blk).astype(F32)))
            rank = rank + jnp.concatenate(parts, axis=0)
        else:
            rank = rank + jnp.where(ids > id_i, (s_i >= score).astype(F32), (s_i > score).astype(F32))
    return rank


def _expand_gates(gates, gexp_ref):
    hi = gates.astype(BF16)
    lo = (gates - hi.astype(F32)).astype(BF16)
    return [_dot(hi, gexp_ref[br]) + _dot(lo, gexp_ref[br]) for br in range(N_BRANCH)]


TOK_Q, TOK_RQ, TOK_RF, TOK_RI, TOK_RG, TOK_GATE, TOK_KVC, TOK_END = 0, 512, 1024, 1536, 2048, 2560, 2688, 2944


def _proj_prompt_body(x_ref, ln_ref, wtok_ref, wft_ref, qg_ref, lbl_ref, pe_ref, gsel_ref, gwin_ref,
                      q_ref, gates_ref, rq_ref, f_ref, rv_ref, rgs_ref, kvc_ref, kvt_ref, wint_ref, att_ref):
    xb = (_row_rms(x_ref[0]) * ln_ref[...]).astype(BF16)

    def tok(lo, hi):
        return _dot_nt(xb, wtok_ref[lo:hi, :])

    zq = tok(TOK_Q, TOK_RQ)
    q_ref[0] = (zq * lax.rsqrt(_head_mean_sq(zq) + EPS) * qg_ref[...]).astype(BF16)
    rq_ref[0] = _silu(tok(TOK_RQ, TOK_RF))
    lb = _lower_bound(lbl_ref[...], 0)
    f_ref[0] = lb + (1.0 - lb) * jax.nn.sigmoid(tok(TOK_RF, TOK_RI))
    rv_ref[0] = tok(TOK_RI, TOK_RG)
    rgs_ref[0] = _silu(tok(TOK_RG, TOK_GATE))
    gates_ref[0] = jax.nn.sigmoid(tok(TOK_GATE, TOK_KVC))
    zc = tok(TOK_KVC, TOK_END) + pe_ref[...]
    kvc_ref[0, 0] = zc[:, 0:LANES]
    kvc_ref[1, 0] = zc[:, LANES:2 * LANES]

    zf = _dot_nt(wft_ref[...], xb)
    d = HEAD_DIM
    ksel = jnp.concatenate([_col_head_norm(zf[256 + g * d:256 + (g + 1) * d], gsel_ref[...])
                            for g in range(N_KV_HEADS)], axis=0)
    kwin = jnp.concatenate([_col_head_norm(zf[512 + g * d:512 + (g + 1) * d], gwin_ref[...])
                            for g in range(N_KV_HEADS)], axis=0)
    vsel = zf[384:512]
    vwin = zf[640:768]
    kvt_ref[0, 0:256] = zf[0:256]
    kvt_ref[0, 256:384] = ksel
    kvt_ref[0, 384:512] = vsel
    wint_ref[0, 0:128] = kwin
    wint_ref[0, 128:256] = vwin
    att_ref[0, 0:128] = ksel.astype(BF16)
    att_ref[0, 128:256] = vsel.astype(BF16)
    att_ref[0, 256:384] = kwin.astype(BF16)
    att_ref[0, 384:512] = vwin.astype(BF16)


def _proj_prompt(x, ln, wtok, wft, qg, lbl, pe_tok, gsel, gwin):
    b, t, dm = x.shape
    tm = PROJ_ROWS
    grid = (b, t // tm)
    row = lambda i, j: (i, j, 0)
    col = lambda i, j: (i, 0, j)
    const2 = lambda i, j: (0, 0)
    out_shape = (
        jax.ShapeDtypeStruct((b, t, ATTN_WIDTH), BF16),
        jax.ShapeDtypeStruct((b, t, LANES), F32),
        jax.ShapeDtypeStruct((b, t, RNN_WIDTH), F32),
        jax.ShapeDtypeStruct((b, t, RNN_WIDTH), F32),
        jax.ShapeDtypeStruct((b, t, RNN_WIDTH), F32),
        jax.ShapeDtypeStruct((b, t, RNN_WIDTH), F32),
        jax.ShapeDtypeStruct((2, b, t, LANES), F32),
        jax.ShapeDtypeStruct((b, 4 * KV_WIDTH, t), F32),
        jax.ShapeDtypeStruct((b, 2 * KV_WIDTH, t), F32),
        jax.ShapeDtypeStruct((b, 4 * KV_WIDTH, t), BF16),
    )
    out_specs = (
        pl.BlockSpec((1, tm, ATTN_WIDTH), row), pl.BlockSpec((1, tm, LANES), row),
        pl.BlockSpec((1, tm, RNN_WIDTH), row), pl.BlockSpec((1, tm, RNN_WIDTH), row),
        pl.BlockSpec((1, tm, RNN_WIDTH), row), pl.BlockSpec((1, tm, RNN_WIDTH), row),
        pl.BlockSpec((2, 1, tm, LANES), lambda i, j: (0, i, j, 0)),
        pl.BlockSpec((1, 4 * KV_WIDTH, tm), col), pl.BlockSpec((1, 2 * KV_WIDTH, tm), col),
        pl.BlockSpec((1, 4 * KV_WIDTH, tm), col),
    )
    in_specs = [
        pl.BlockSpec((1, tm, dm), row), pl.BlockSpec(ln.shape, const2),
        pl.BlockSpec(wtok.shape, const2), pl.BlockSpec(wft.shape, const2),
        pl.BlockSpec(qg.shape, const2), pl.BlockSpec(lbl.shape, const2), pl.BlockSpec(pe_tok.shape, const2),
        pl.BlockSpec(gsel.shape, const2), pl.BlockSpec(gwin.shape, const2),
    ]
    return pl.pallas_call(
        _proj_prompt_body, grid=grid, in_specs=in_specs, out_specs=out_specs, out_shape=out_shape,
        compiler_params=pltpu.CompilerParams(dimension_semantics=("parallel", "parallel"),
                                             vmem_limit_bytes=_vmem_limit(48 << 20)),
        name="proj_prompt",
    )(x, ln, wtok, wft, qg, lbl, pe_tok, gsel, gwin)


def _compress_prompt_body(x_ref, w1_ref, w2_ref, kg_ref, out_ref):
    c = pl.program_id(0)
    nb = out_ref.shape[2]
    acc = jnp.zeros((nb, 2 * CMP_HIDDEN), F32)
    for pos in range(CMP_BLOCK):
        xp = x_ref[0, 0, pl.ds(pos, nb, stride=CMP_BLOCK), :]
        acc = acc + _dot(xp.astype(BF16), w1_ref[0, pos])
    hb = _silu(acc).astype(BF16)
    outs = []
    for g in range(N_KV_HEADS):
        y = _dot(hb[:, g * CMP_HIDDEN:(g + 1) * CMP_HIDDEN], w2_ref[0])
        yn = _row_rms(y) * kg_ref[...]
        outs.append(jnp.where(c == 0, yn, y))
    out_ref[0, 0] = jnp.concatenate(outs, axis=1)


def _compress_prompt(kvc, w1bd, w2dup, kg_dup):
    _, b, t, _ = kvc.shape
    nb = t // CMP_BLOCK
    return pl.pallas_call(
        _compress_prompt_body, grid=(2, b),
        in_specs=[pl.BlockSpec((1, 1, t, LANES), lambda c, i: (c, i, 0, 0)),
                  pl.BlockSpec((1,) + w1bd.shape[1:], lambda c, i: (c, 0, 0, 0)),
                  pl.BlockSpec((1,) + w2dup.shape[1:], lambda c, i: (c, 0, 0)),
                  pl.BlockSpec(kg_dup.shape, lambda c, i: (0, 0))],
        out_specs=pl.BlockSpec((1, 1, nb, 2 * LANES), lambda c, i: (c, i, 0, 0)),
        out_shape=jax.ShapeDtypeStruct((2, b, nb, 2 * LANES), F32),
        compiler_params=pltpu.CompilerParams(dimension_semantics=("arbitrary", "arbitrary"),
                                             vmem_limit_bytes=_vmem_limit(32 << 20)),
        name="compress_prompt",
    )(kvc, w1bd, w2dup, kg_dup)


def _row_max_update(sc_scr, nm_scr, m_scr, mn_scr, ncol, tq):
    rb = ROW_BLOCK
    for i in range(GROUP * tq // rb):
        rows = slice(i * rb, (i + 1) * rb)
        qrows = slice((i * rb) % tq, (i * rb) % tq + rb)
        mx = jnp.full((rb, LANES), NEG, F32)
        for j in range(ncol):
            cols = slice(j * LANES, (j + 1) * LANES)
            v = sc_scr[rows, cols]
            if nm_scr is not None:
                v = v + nm_scr[qrows, cols]
                sc_scr[rows, cols] = v
            mx = jnp.maximum(mx, v)
        mn_scr[rows, :] = jnp.maximum(m_scr[rows, :], jnp.max(mx, axis=-1, keepdims=True))


def _softmax_update(sc_scr, p_scr, m_scr, mn_scr, l_scr, a_scr, ncol, tq):
    rb = ROW_BLOCK
    for i in range(GROUP * tq // rb):
        rows = slice(i * rb, (i + 1) * rb)
        mn = mn_scr[rows, :]
        tot = jnp.zeros((rb, LANES), F32)
        for j in range(ncol):
            cols = slice(j * LANES, (j + 1) * LANES)
            p = jnp.exp(sc_scr[rows, cols] - mn)
            tot = tot + p
            p_scr[rows, cols] = p.astype(BF16)
        alpha = jnp.exp(m_scr[rows, :] - mn)
        l_scr[rows, :] = alpha * l_scr[rows, :] + jnp.sum(tot, axis=-1, keepdims=True)
        a_scr[rows, :] = alpha
        m_scr[rows, :] = mn


def _nsa_prompt_body(q_ref, gates_ref, cmp_ref, att_ref, kfeat_ref, qfeat_ref, gexp_ref, oa_ref,
                     sc_scr, sc2_scr, nm_scr, p_scr, m_scr, mn_scr, l_scr, a_scr, acc_scr, *, seq, topk):
    tq = NSA_Q_ROWS
    tk = min(SEL_KEYS, seq)
    nb = cmp_ref.shape[2]
    t0 = pl.program_id(1) * tq
    wk = min(WINDOW + tq, seq)
    ws = pl.multiple_of(jnp.clip(t0 - WINDOW, 0, seq - wk), LANES)
    q = q_ref[0]
    d = HEAD_DIM
    groups = range(N_KV_HEADS)
    def reset(stats):
        for g in groups:
            m_scr[g] = jnp.full(m_scr.shape[1:], NEG, F32)
            for ref in stats:
                ref[g] = jnp.zeros(ref.shape[1:], F32)

    sc_bufs = (sc_scr, sc2_scr)

    def logits(buf, g, qa, kt_, kf, width):
        sc_bufs[buf][g, :, 0:width] = _dot(qa, jnp.concatenate([kt_, kt_, kf], axis=0))

    def softmax_tiles(buf, ncol, masked):
        sc = sc_bufs[buf]
        for g in groups:
            _row_max_update(sc.at[g], nm_scr if masked else None, m_scr.at[g], mn_scr.at[g], ncol, tq)
        for g in groups:
            _softmax_update(sc.at[g], p_scr.at[g], m_scr.at[g], mn_scr.at[g], l_scr.at[g], a_scr.at[g], ncol, tq)

    qst = [_stack_group_queries(q, g) for g in groups]
    qwin = [jnp.concatenate([qst[g], qfeat_ref[g].astype(BF16)], axis=1) for g in groups]

    kcs = [cmp_ref[0, 0][:, g * LANES:(g + 1) * LANES].astype(BF16) for g in groups]
    vcs = [cmp_ref[1, 0][:, g * LANES:(g + 1) * LANES].astype(BF16) for g in groups]
    st = [_dot_nt(kcs[g], qst[g]) for g in groups]

    reset((l_scr,))
    kfw = kfeat_ref[:, pl.ds(ws, wk)]
    ddw = (t0 - ws) + lax.broadcasted_iota(jnp.int32, (tq, wk), 0) - lax.broadcasted_iota(jnp.int32, (tq, wk), 1)
    nm_scr[:, 0:wk] = jnp.where((ddw >= 0) & (ddw < WINDOW), 0.0, NEG)
    for g in groups:
        logits(0, g, qwin[g], att_ref[0, 2 * KV_WIDTH + g * d:2 * KV_WIDTH + (g + 1) * d, pl.ds(ws, wk)], kfw, wk)

    blk = lax.broadcasted_iota(jnp.int32, (nb, GROUP * tq), 0)
    colq = lax.broadcasted_iota(jnp.int32, (1, GROUP * tq), 1)
    dist = (t0 + (colq & (tq - 1)) - (blk * CMP_BLOCK + (CMP_BLOCK - 1))).astype(F32)
    valid = dist >= 0.0
    ps = []
    for g in groups:
        s = jnp.where(valid, st[g] - _row_slopes(g, colq >> _log2(tq)) * dist, NEG)
        e = jnp.exp(s - jnp.max(s, axis=0, keepdims=True))
        ps.append(jnp.where(valid, e / jnp.sum(e, axis=0, keepdims=True), 0.0))
    o_cmp = [_unstack_group(_dot_tn(ps[g].astype(BF16), vcs[g]), tq) for g in groups]

    softmax_tiles(0, wk // LANES, True)
    o_win = []
    for g in groups:
        vw = att_ref[0, 3 * KV_WIDTH + g * d:3 * KV_WIDTH + (g + 1) * d, pl.ds(ws, wk)]
        o_win.append(_unstack_group(_dot_nt(p_scr[g, :, 0:wk], _twice(vw)) / l_scr[g], tq))

    bj = lax.broadcasted_iota(jnp.int32, (nb, tq), 0)
    cur = (t0 + lax.broadcasted_iota(jnp.int32, (nb, tq), 1)) >> _log2(SEL_BLOCK)
    force = (bj == 0) | (bj == cur)
    qsel = []
    for g in groups:
        p = ps[g]
        imp = p[:, 0:tq] + p[:, tq:2 * tq] + p[:, 2 * tq:3 * tq] + p[:, 3 * tq:4 * tq]
        score = jnp.where(bj <= cur, jnp.where(force, FORCED_SCORE, imp), -1.0)
        rank = _block_ranks(score, bj, [(i, i) for i in range(nb)])
        mneg = jnp.where((rank < topk) & (score >= 0.0), 0.0, NEG)
        mtok = jnp.concatenate([mneg, jnp.zeros((LANES - nb, tq), F32)], axis=0).T
        qsel.append(jnp.concatenate(
            [qst[g], (qfeat_ref[g] + jnp.concatenate([mtok] * GROUP, axis=0)).astype(BF16)], axis=1))

    reset((l_scr, acc_scr))
    n_kt = (t0 + tq - 1) // tk + 1

    def sel_logits(kt, buf):
        s0 = pl.multiple_of(kt * tk, tk)
        kf = kfeat_ref[:, pl.ds(s0, tk)]
        for g in groups:
            logits(buf, g, qsel[g], att_ref[0, g * d:(g + 1) * d, pl.ds(s0, tk)], kf, tk)

    def sel_update(kt, buf, causal):
        s0 = pl.multiple_of(kt * tk, tk)
        if causal:
            dd = (t0 - s0) + lax.broadcasted_iota(jnp.int32, (tq, tk), 0) - lax.broadcasted_iota(jnp.int32, (tq, tk), 1)
            nm_scr[:, 0:tk] = jnp.where(dd >= 0, 0.0, NEG)
        softmax_tiles(buf, tk // LANES, causal)
        for g in groups:
            vt_ = att_ref[0, KV_WIDTH + g * d:KV_WIDTH + (g + 1) * d, pl.ds(s0, tk)]
            acc_scr[g] = a_scr[g] * acc_scr[g] + _dot_nt(p_scr[g, :, 0:tk], _twice(vt_))

    sel_logits(0, 0)

    def full_tile(kt, carry):
        for par in (0, 1):
            @pl.when((kt & 1) == par)
            def _(par=par):
                sel_logits(kt + 1, 1 - par)
                sel_update(kt, par, False)
        return carry

    lax.fori_loop(0, n_kt - 1, full_tile, 0)
    for par in (0, 1):
        @pl.when(((n_kt - 1) & 1) == par)
        def _(par=par):
            sel_update(n_kt - 1, par, True)
    o_sel = [_unstack_group(acc_scr[g] / l_scr[g], tq) for g in groups]

    ge = _expand_gates(gates_ref[0], gexp_ref)
    oa_ref[0] = (ge[0] * jnp.concatenate(o_cmp, axis=1) + ge[1] * jnp.concatenate(o_sel, axis=1)
                 + ge[2] * jnp.concatenate(o_win, axis=1))


def _nsa_prompt(q, gates, cmpkv, att, k_feat, q_feat, gexp):
    b, t, _ = q.shape
    tq = NSA_Q_ROWS
    nb = t // CMP_BLOCK
    assert nb <= HEAD_DIM, "the block mask uses 64 feature lanes"
    cw = max(min(SEL_KEYS, t), min(WINDOW + tq, t))
    body = functools.partial(_nsa_prompt_body, seq=t, topk=min(SEL_TOPK, nb))
    ng = N_KV_HEADS
    stat = pltpu.VMEM((ng, GROUP * tq, LANES), F32)
    return pl.pallas_call(
        body, grid=(b, t // tq),
        in_specs=[pl.BlockSpec((1, tq, ATTN_WIDTH), lambda i, j: (i, j, 0)),
                  pl.BlockSpec((1, tq, LANES), lambda i, j: (i, j, 0)),
                  pl.BlockSpec((2, 1, nb, 2 * LANES), lambda i, j: (0, i, 0, 0)),
                  pl.BlockSpec((1, 4 * KV_WIDTH, t), lambda i, j: (i, 0, 0)),
                  pl.BlockSpec(k_feat.shape, lambda i, j: (0, 0)),
                  pl.BlockSpec(q_feat.shape, lambda i, j: (0, 0, 0)),
                  pl.BlockSpec(gexp.shape, lambda i, j: (0, 0, 0))],
        out_specs=pl.BlockSpec((1, tq, ATTN_WIDTH), lambda i, j: (i, j, 0)),
        out_shape=jax.ShapeDtypeStruct((b, t, ATTN_WIDTH), F32),
        scratch_shapes=[pltpu.VMEM((ng, GROUP * tq, cw), F32), pltpu.VMEM((ng, GROUP * tq, cw), F32),
                        pltpu.VMEM((tq, cw), F32),
                        pltpu.VMEM((ng, GROUP * tq, cw), BF16), stat, stat, stat, stat, stat],
        compiler_params=pltpu.CompilerParams(dimension_semantics=("parallel", "parallel"),
                                             vmem_limit_bytes=_vmem_limit(48 << 20)),
        name="nsa_prompt",
    )(q, gates, cmpkv, att, k_feat, q_feat, gexp)


def _split3(x):
    hi = x.astype(BF16)
    r1 = x - hi.astype(F32)
    mid = r1.astype(BF16)
    lo = (r1 - mid.astype(F32)).astype(BF16)
    return hi, mid, lo


def _hgrn_prompt_body(rq_ref, f_ref, rv_ref, ltri_ref, bd_ref, ind_ref, indt_ref, o_ref, st_ref,
                      s_scr, cum_scr, k_scr, prod_scr):
    c16 = RNN_CHUNK
    nbat = rq_ref.shape[0]

    @pl.when(pl.program_id(0) == 0)
    def _():
        s_scr[...] = jnp.zeros(s_scr.shape, F32)

    ltri = ltri_ref[...]
    nl = ltri.shape[0]
    for bi in range(nbat):
        for r in range(0, rq_ref.shape[1], nl):
            f = f_ref[bi, r:r + nl, :]
            hi, mid, lo = _split3(jnp.log(f))
            cum = (_dot(ltri, hi) + _dot(ltri, mid) + _dot(ltri, lo)) * LOG2E
            cum_scr[bi, r:r + nl, :] = cum
            k_scr[bi, r:r + nl, :] = cum - jnp.log2(1.0 - f)

    si = lax.broadcasted_iota(jnp.int32, (c16, c16, RNN_WIDTH), 0)
    ti = lax.broadcasted_iota(jnp.int32, (c16, c16, RNN_WIDTH), 1)
    causal = si <= ti
    npair = RNN_WIDTH // LANES

    def pair_products(c, buf, bi):
        r0 = pl.multiple_of(c * c16, c16)
        cc = cum_scr[bi, pl.ds(r0, c16), :]
        qc = rq_ref[bi, pl.ds(r0, c16), :]
        lk = k_scr[bi, pl.ds(r0, c16), :]
        dec = jnp.exp2(jnp.where(causal, cc[None, :, :] - lk[:, None, :], NEG))
        prod_scr[buf, bi] = (qc[None, :, :] * dec).reshape(c16 * c16, RNN_WIDTH).astype(BF16)

    def recurrence(c, buf, nxt):
        r0 = pl.multiple_of(c * c16, c16)
        rows = range(nbat)
        a1 = [_dot(prod_scr[buf, bi], ind_ref[...]) for bi in rows]
        if nxt is not None:
            for bi in rows:
                pair_products(nxt, 1 - buf, bi)
        cc = [cum_scr[bi, pl.ds(r0, c16), :] for bi in rows]
        last = [x[c16 - 1:c16, :] for x in cc]
        o_int = []
        for bi in rows:
            qd = (rq_ref[bi, pl.ds(r0, c16), :] * jnp.exp2(cc[bi])).astype(BF16)
            o_int.append(jnp.concatenate([_dot_nt(qd[:, p * LANES:(p + 1) * LANES], s_scr[bi, p].astype(BF16))
                                          for p in range(npair)], axis=1))
        a2 = [_dot(a1[bi].astype(BF16), indt_ref[...]).reshape(c16, c16, RNN_WIDTH) for bi in rows]
        for bi in rows:
            kd = jnp.exp2(last[bi] - k_scr[bi, pl.ds(r0, c16), :]).astype(BF16)
            vb = rv_ref[bi, pl.ds(r0, c16), :].astype(BF16)
            dl = jnp.exp2(last[bi])
            for p in range(npair):
                u = _dot_tn(vb[:, p * LANES:(p + 1) * LANES], kd[:, p * LANES:(p + 1) * LANES])
                s_scr[bi, p] = s_scr[bi, p] * dl[:, p * LANES:(p + 1) * LANES] + u * bd_ref[...]
        for bi in rows:
            vc = rv_ref[bi, pl.ds(r0, c16), :]
            o_ref[bi, pl.ds(r0, c16), :] = o_int[bi] + jnp.sum(a2[bi] * vc[:, None, :], axis=0)

    nchunk = rq_ref.shape[1] // c16
    assert nchunk % 2 == 0 and nchunk >= 4
    for bi in range(nbat):
        pair_products(0, 0, bi)

    def two_chunks(j, carry):
        for sub in (0, 1):
            recurrence(2 * j + sub, sub, 2 * j + sub + 1)
        return carry

    lax.fori_loop(0, nchunk // 2 - 1, two_chunks, 0)
    recurrence(nchunk - 2, 0, nchunk - 1)
    recurrence(nchunk - 1, 1, None)
    st_ref[...] = s_scr[...]


def _hgrn_prompt(rq, f, rv, ltri, bd, ind):
    b, t, w = rq.shape
    tc = RNN_ROWS
    npair = w // LANES
    row = lambda j: (0, j, 0)
    const = lambda j: (0, 0)
    indt = ind.T
    return pl.pallas_call(
        _hgrn_prompt_body, grid=(t // tc,),
        in_specs=[pl.BlockSpec((b, tc, w), row), pl.BlockSpec((b, tc, w), row), pl.BlockSpec((b, tc, w), row),
                  pl.BlockSpec(ltri.shape, const), pl.BlockSpec(bd.shape, const),
                  pl.BlockSpec(ind.shape, const), pl.BlockSpec(indt.shape, const)],
        out_specs=(pl.BlockSpec((b, tc, w), row), pl.BlockSpec((b, npair, LANES, LANES), lambda j: (0, 0, 0, 0))),
        out_shape=(jax.ShapeDtypeStruct((b, t, w), F32), jax.ShapeDtypeStruct((b, npair, LANES, LANES), F32)),
        scratch_shapes=[pltpu.VMEM((b, npair, LANES, LANES), F32), pltpu.VMEM((b, tc, w), F32),
                        pltpu.VMEM((b, tc, w), F32), pltpu.VMEM((2, b, RNN_CHUNK * RNN_CHUNK, w), BF16)],
        compiler_params=pltpu.CompilerParams(dimension_semantics=("arbitrary",),
                                             vmem_limit_bytes=_vmem_limit(52 << 20)),
        name="hgrn_prompt",
    )(rq, f, rv, ltri, bd, ind, indt)


def _finish_body(x_ref, oa_ref, orn_ref, rgs_ref, ag_ref, rg_ref, wout_ref, lnm_ref, wup_ref, wdn_ref, y_ref, hn_scr):
    @pl.when(pl.program_id(1) == 0)
    def _():
        oa = oa_ref[...]
        orn = orn_ref[...]
        a_n = oa * lax.rsqrt(_head_mean_sq(oa) + EPS) * ag_ref[...]
        r_n = orn * lax.rsqrt(_head_mean_sq(orn) + EPS) * rg_ref[...] * rgs_ref[...]
        h = (x_ref[...] + _dot(a_n.astype(BF16), wout_ref[0:ATTN_WIDTH, :])
             + _dot(r_n.astype(BF16), wout_ref[ATTN_WIDTH:ATTN_WIDTH + RNN_WIDTH, :]))
        y_ref[...] = h
        hn_scr[...] = (_row_rms(h) * lnm_ref[...]).astype(BF16)

    u = jnp.maximum(_dot(hn_scr[...], wup_ref[...]), 0.0)
    y_ref[...] += _dot((u * u).astype(BF16), wdn_ref[...])


def _finish(x, oa, orn, rgs, ag, rg, wout, lnm, wup, wdn):
    n, dm = x.shape
    tm = FINISH_ROWS
    dff = wup.shape[1]
    row = lambda i, j: (i, 0)
    const = lambda i, j: (0, 0)
    return pl.pallas_call(
        _finish_body, grid=(n // tm, dff // FF_CHUNK),
        in_specs=[pl.BlockSpec((tm, dm), row), pl.BlockSpec((tm, ATTN_WIDTH), row),
                  pl.BlockSpec((tm, RNN_WIDTH), row), pl.BlockSpec((tm, RNN_WIDTH), row),
                  pl.BlockSpec(ag.shape, const), pl.BlockSpec(rg.shape, const),
                  pl.BlockSpec(wout.shape, const), pl.BlockSpec(lnm.shape, const),
                  pl.BlockSpec((dm, FF_CHUNK), lambda i, j: (0, j)),
                  pl.BlockSpec((FF_CHUNK, dm), lambda i, j: (j, 0))],
        out_specs=pl.BlockSpec((tm, dm), row),
        out_shape=jax.ShapeDtypeStruct((n, dm), F32),
        scratch_shapes=[pltpu.VMEM((tm, dm), BF16)],
        compiler_params=pltpu.CompilerParams(dimension_semantics=("parallel", "arbitrary"),
                                             vmem_limit_bytes=_vmem_limit(48 << 20)),
        name="finish",
    )(x, oa, orn, rgs, ag, rg, wout, lnm, wup, wdn)


def _proj_sample_body(xbt_ref, xtb_ref, ln_ref, wtok_ref, wtb_ref, wbt_ref, qg_ref, lbl_ref, gsel_ref, gwin_ref,
                      q_ref, rgs_ref, gates_ref, ztb_ref, zbt_ref):
    xb = (_row_rms(xbt_ref[...]) * ln_ref[...]).astype(BF16)
    xt = (_row_rms(xtb_ref[...]) * ln_ref[...]).astype(BF16)
    zq = _dot_nt(xb, wtok_ref[0:ATTN_WIDTH, :])
    q_ref[...] = (zq * lax.rsqrt(_head_mean_sq(zq) + EPS) * qg_ref[...]).astype(BF16)
    rgs_ref[...] = _silu(_dot_nt(xb, wtok_ref[ATTN_WIDTH:ATTN_WIDTH + RNN_WIDTH, :]))
    gates_ref[...] = jax.nn.sigmoid(_dot_nt(xb, wtok_ref[ATTN_WIDTH + RNN_WIDTH:ATTN_WIDTH + RNN_WIDTH + LANES, :]))

    d = HEAD_DIM
    zt = _dot_nt(wtb_ref[...], xt)
    ztb_ref[0:256] = zt[0:256]
    for g in range(N_KV_HEADS):
        ztb_ref[256 + g * d:256 + (g + 1) * d] = _col_head_norm(zt[256 + g * d:256 + (g + 1) * d], gsel_ref[...])
    ztb_ref[384:512] = zt[384:512]
    ztb_ref[512:1024] = _silu(zt[512:1024])
    lb = _lower_bound(lbl_ref[...], 0)[0]
    ztb_ref[1024:1536] = lb + (1.0 - lb) * jax.nn.sigmoid(zt[1024:1536])
    ztb_ref[1536:2048] = zt[1536:2048]

    zb = _dot_nt(wbt_ref[...], xb)
    for g in range(N_KV_HEADS):
        zbt_ref[g * d:(g + 1) * d] = _col_head_norm(zb[g * d:(g + 1) * d], gsel_ref[...])
        zbt_ref[256 + g * d:256 + (g + 1) * d] = _col_head_norm(zb[256 + g * d:256 + (g + 1) * d], gwin_ref[...])
    zbt_ref[128:256] = zb[128:256]
    zbt_ref[384:512] = zb[384:512]


def _proj_sample(xbt, xtb, ln, wtok, wtb, wbt, qg, lblt, gsel, gwin):
    n, dm = xbt.shape
    tm = PROJ_ROWS
    row = lambda i: (i, 0)
    col = lambda i: (0, i)
    const = lambda i: (0, 0)
    return pl.pallas_call(
        _proj_sample_body, grid=(n // tm,),
        in_specs=[pl.BlockSpec((tm, dm), row), pl.BlockSpec((tm, dm), row), pl.BlockSpec(ln.shape, const),
                  pl.BlockSpec(wtok.shape, const), pl.BlockSpec(wtb.shape, const), pl.BlockSpec(wbt.shape, const),
                  pl.BlockSpec(qg.shape, const), pl.BlockSpec(lblt.shape, lambda i: (0, 0, 0)),
                  pl.BlockSpec(gsel.shape, const), pl.BlockSpec(gwin.shape, const)],
        out_specs=(pl.BlockSpec((tm, ATTN_WIDTH), row), pl.BlockSpec((tm, RNN_WIDTH), row),
                   pl.BlockSpec((tm, LANES), row), pl.BlockSpec((wtb.shape[0], tm), col),
                   pl.BlockSpec((wbt.shape[0], tm), col)),
        out_shape=(jax.ShapeDtypeStruct((n, ATTN_WIDTH), BF16), jax.ShapeDtypeStruct((n, RNN_WIDTH), F32),
                   jax.ShapeDtypeStruct((n, LANES), F32), jax.ShapeDtypeStruct((wtb.shape[0], n), F32),
                   jax.ShapeDtypeStruct((wbt.shape[0], n), F32)),
        compiler_params=pltpu.CompilerParams(dimension_semantics=("parallel",),
                                             vmem_limit_bytes=_vmem_limit(40 << 20)),
        name="proj_sample",
    )(xbt, xtb, ln, wtok, wtb, wbt, qg, lblt, gsel, gwin)


PAGE_ROWS = 4 * KV_WIDTH
SAMPLE_BATCH_PER_STEP = 4
WINDOW_BATCH_PER_STEP = 8


def _nsa_sample_body(pt_ref, cache_ref, q_ref, nkv_ref, w1_ref, pe_ref, w2_ref, kg_ref, e_ref,
                     ocs_ref, cbuf, sbuf, lhs_scr, kk_scr, vv_scr, sem, *, n_pages, past_len, dec_t, topk):
    step_id = pl.program_id(0)
    nsteps = pl.num_programs(0)
    slot = step_id % 2
    d = HEAD_DIM
    nblk = 2 * n_pages
    half_rows = PAGE_ROWS // 2
    bps = q_ref.shape[0]

    def page_copies(st, sl, bi, j):
        pg = pt_ref[(st * bps + bi) * n_pages + j]
        return (pltpu.make_async_copy(cache_ref.at[pg, pl.ds(0, half_rows)], cbuf.at[sl, bi, :, j, :], sem.at[sl]),
                pltpu.make_async_copy(cache_ref.at[pg, pl.ds(half_rows, half_rows)],
                                      sbuf.at[sl, bi, pl.ds(j * half_rows, half_rows)], sem.at[sl]))

    def all_copies(st, sl):
        return [cp for bi in range(bps) for j in range(n_pages) for cp in page_copies(st, sl, bi, j)]

    @pl.when(step_id == 0)
    def _():
        for cp in all_copies(0, 0):
            cp.start()

    @pl.when(step_id + 1 < nsteps)
    def _():
        for cp in all_copies(step_id + 1, 1 - slot):
            cp.start()

    for cp in all_copies(step_id, slot):
        cp.wait()

    low = _low_half((n_pages, LANES))
    rows_b = 4 * n_pages

    def compress(c):
        for dd in range(d):
            rows = []
            for bi in range(bps):
                for g in range(N_KV_HEADS):
                    xg = cbuf[slot, bi, (c * N_KV_HEADS + g) * d + dd]
                    xg = xg + pe_ref[c, dd]
                    rows += [jnp.where(low, xg, 0.0), jnp.where(low, 0.0, xg)]
            lhs_scr[:, dd * LANES:(dd + 1) * LANES] = jnp.concatenate(rows, axis=0).astype(BF16)
        acc = _dot(lhs_scr[...], w1_ref[c])
        return _dot(_silu(acc).astype(BF16), w2_ref[c])

    kc_all = _row_rms(compress(0)) * kg_ref[...]
    vc_all = compress(1)

    nq = GROUP * dec_t
    colq = lax.broadcasted_iota(jnp.int32, (1, nq), 1)
    rowq = lax.broadcasted_iota(jnp.int32, (nq, 1), 0)
    cur_blk = past_len // SEL_BLOCK
    chains = [(bi, g) for bi in range(bps) for g in range(N_KV_HEADS)]
    nch = len(chains)
    step = rowq & (dec_t - 1)
    qpos = past_len + (colq & (dec_t - 1))
    rho_q = lax.broadcasted_iota(jnp.int32, (nblk, 1), 0)
    end = (2 * (rho_q & (n_pages - 1)) + (rho_q >> _log2(n_pages))) * CMP_BLOCK + (CMP_BLOCK - 1)
    dist = (qpos - end).astype(F32)
    valid = dist >= 0.0
    kpos = lax.broadcasted_iota(jnp.int32, (1, past_len), 1)
    dpast = (past_len + step - kpos).astype(F32)
    lane = lax.broadcasted_iota(jnp.int32, (1, LANES), 1)
    per_tile = LANES // dec_t
    offs = [((step_id * bps + bi) & (per_tile - 1)) * dec_t for bi in range(bps)]
    dnews = [(step - (lane - off)).astype(F32) for off in offs]
    oknew = [(lane >= off) & (lane < off + dec_t) for off in offs]

    qst = [_stack_group_queries(q_ref[bi], g) for bi, g in chains]
    kcs = [kc_all[bi * rows_b + g * nblk:bi * rows_b + (g + 1) * nblk].astype(BF16) for bi, g in chains]
    vcs = [vc_all[bi * rows_b + g * nblk:bi * rows_b + (g + 1) * nblk].astype(BF16) for bi, g in chains]
    st = [_dot_nt(kcs[ch], qst[ch]) for ch in range(nch)]
    for ch, (bi, g) in enumerate(chains):
        for pg in range(n_pages):
            kt_ = sbuf[slot, bi, pl.ds(pg * half_rows + g * d, d), :].astype(BF16)
            vt_ = sbuf[slot, bi, pl.ds(pg * half_rows + (N_KV_HEADS + g) * d, d), :].astype(BF16)
            kk_scr[ch, 0:d, pg * PAGE_SIZE:(pg + 1) * PAGE_SIZE] = kt_
            kk_scr[ch, d:2 * d, pg * PAGE_SIZE:(pg + 1) * PAGE_SIZE] = kt_
            vv_scr[ch, 0:d, pg * PAGE_SIZE:(pg + 1) * PAGE_SIZE] = vt_
            vv_scr[ch, d:2 * d, pg * PAGE_SIZE:(pg + 1) * PAGE_SIZE] = vt_
    sp = [_dot(qst[ch], kk_scr[ch]) for ch in range(nch)]
    nk = [_twice(nkv_ref[g * d:(g + 1) * d, :].astype(BF16)) for g in range(N_KV_HEADS)]
    nv = [_twice(nkv_ref[KV_WIDTH + g * d:KV_WIDTH + (g + 1) * d, :].astype(BF16)) for g in range(N_KV_HEADS)]
    sn = [_dot(qst[ch], nk[g]) for ch, (bi, g) in enumerate(chains)]
    ps = []
    for ch, (bi, g) in enumerate(chains):
        s = jnp.where(valid, st[ch] - _row_slopes(g, colq >> _log2(dec_t)) * dist, NEG)
        e = jnp.exp(s - jnp.max(s, axis=0, keepdims=True))
        ps.append(jnp.where(valid, e / jnp.sum(e, axis=0, keepdims=True), 0.0))
    o_cmp = [_unstack_group(_dot_tn(ps[ch].astype(BF16), vcs[ch]), dec_t) for ch in range(nch)]
    imps = []
    for ch in range(nch):
        p = ps[ch]
        imp = p[:, 0:dec_t]
        for r in range(1, GROUP):
            imp = imp + p[:, r * dec_t:(r + 1) * dec_t]
        imps.append(imp)
    imp_all = jnp.concatenate(imps, axis=1)
    rho_a = lax.broadcasted_iota(jnp.int32, imp_all.shape, 0)
    bid_a = 2 * (rho_a & (n_pages - 1)) + (rho_a >> _log2(n_pages))
    score = jnp.where((bid_a == 0) | (bid_a == cur_blk), FORCED_SCORE, imp_all)
    rank = _block_ranks(score, bid_a, [(i, 2 * (i % n_pages) + i // n_pages) for i in range(nblk)])
    rank = rank + jnp.where(bid_a > cur_blk, (FORCED_SCORE >= score).astype(F32), (FORCED_SCORE > score).astype(F32))
    msel = jnp.where((rank < topk) & (score >= 0.0), 1.0, 0.0).astype(BF16)
    mk_all = _dot_tn(msel, e_ref[...])
    pp, pn, den = [], [], []
    for ch, (bi, g) in enumerate(chains):
        okp = jnp.concatenate([mk_all[ch * dec_t:(ch + 1) * dec_t]] * GROUP, axis=0) > 0.5
        slope = _row_slopes(g, rowq >> _log2(dec_t))
        lgp = jnp.where(okp & (dpast >= 0.0), sp[ch] - slope * dpast, NEG)
        lgn = jnp.where(oknew[bi] & (dnews[bi] >= 0.0), sn[ch] - slope * dnews[bi], NEG)
        m = jnp.maximum(jnp.max(lgp, axis=-1, keepdims=True), jnp.max(lgn, axis=-1, keepdims=True))
        pp.append(jnp.exp(lgp - m))
        pn.append(jnp.exp(lgn - m))
        den.append(jnp.sum(pp[ch], axis=-1, keepdims=True) + jnp.sum(pn[ch], axis=-1, keepdims=True))
    o_sel = [_unstack_group((_dot_nt(pp[ch].astype(BF16), vv_scr[ch]) + _dot_nt(pn[ch].astype(BF16), nv[g]))
                            / den[ch], dec_t) for ch, (bi, g) in enumerate(chains)]

    for bi in range(bps):
        ocs_ref[bi, 0] = jnp.concatenate(o_cmp[bi * N_KV_HEADS:(bi + 1) * N_KV_HEADS], axis=1)
        ocs_ref[bi, 1] = jnp.concatenate(o_sel[bi * N_KV_HEADS:(bi + 1) * N_KV_HEADS], axis=1)


def _nsa_sample(page_flat, cache, q, nkv, w1r, pe_t, w2dup, kg_dup, e_perm, past_len, topk):
    nbatch, dec_t, _ = q.shape
    n_pages = past_len // PAGE_SIZE
    bps = SAMPLE_BATCH_PER_STEP
    nch = bps * N_KV_HEADS
    body = functools.partial(_nsa_sample_body, n_pages=n_pages, past_len=past_len, dec_t=dec_t, topk=topk)
    grid_spec = pltpu.PrefetchScalarGridSpec(
        num_scalar_prefetch=1, grid=(nbatch // bps,),
        in_specs=[pl.BlockSpec(memory_space=pl.ANY),
                  pl.BlockSpec((bps, dec_t, ATTN_WIDTH), lambda i, pt: (i, 0, 0)),
                  pl.BlockSpec((2 * KV_WIDTH, LANES), lambda i, pt: (0, i * bps * dec_t // LANES)),
                  pl.BlockSpec(w1r.shape, lambda i, pt: (0, 0, 0)),
                  pl.BlockSpec(pe_t.shape, lambda i, pt: (0, 0, 0, 0)),
                  pl.BlockSpec(w2dup.shape, lambda i, pt: (0, 0, 0)),
                  pl.BlockSpec(kg_dup.shape, lambda i, pt: (0, 0)),
                  pl.BlockSpec(e_perm.shape, lambda i, pt: (0, 0))],
        out_specs=pl.BlockSpec((bps, 2, dec_t, ATTN_WIDTH), lambda i, pt: (i, 0, 0, 0)),
        scratch_shapes=[pltpu.VMEM((2, bps, PAGE_ROWS // 2, n_pages, PAGE_SIZE), F32),
                        pltpu.VMEM((2, bps, n_pages * PAGE_ROWS // 2, PAGE_SIZE), F32),
                        pltpu.VMEM((bps * 4 * n_pages, HEAD_DIM * LANES), BF16),
                        pltpu.VMEM((nch, 2 * HEAD_DIM, past_len), BF16),
                        pltpu.VMEM((nch, 2 * HEAD_DIM, past_len), BF16),
                        pltpu.SemaphoreType.DMA((2,))])
    return pl.pallas_call(
        body, grid_spec=grid_spec,
        out_shape=jax.ShapeDtypeStruct((nbatch, 2, dec_t, ATTN_WIDTH), F32),
        compiler_params=pltpu.CompilerParams(dimension_semantics=("arbitrary",),
                                             vmem_limit_bytes=_vmem_limit(56 << 20)),
        name="nsa_sample",
    )(page_flat, cache, q, nkv, w1r, pe_t, w2dup, kg_dup, e_perm)


def _win_sample_body(win_ref, q_ref, nw_ref, ocs_ref, gates_ref, gexp_ref, oa_ref, wout_ref, *, past_len, dec_t):
    d = HEAD_DIM
    wbuf = win_ref.shape[2]
    nq = GROUP * dec_t
    rowq = lax.broadcasted_iota(jnp.int32, (nq, 1), 0)
    step = rowq & (dec_t - 1)
    kpos = past_len - wbuf + lax.broadcasted_iota(jnp.int32, (1, wbuf), 1)
    dpast = (past_len + step - kpos).astype(F32)
    okp = (dpast >= 0.0) & (dpast < WINDOW)
    nbat = win_ref.shape[0]
    lane = lax.broadcasted_iota(jnp.int32, (1, LANES), 1)
    per_tile = LANES // dec_t
    offs = [((pl.program_id(0) * nbat + bi) & (per_tile - 1)) * dec_t for bi in range(nbat)]
    dnews = [(step - (lane - off)).astype(F32) for off in offs]
    okns = [(lane >= off) & (lane < off + dec_t) & (dn >= 0.0) & (dn < WINDOW) for off, dn in zip(offs, dnews)]
    chains = [(bi, g) for bi in range(nbat) for g in range(N_KV_HEADS)]
    qst = [_stack_group_queries(q_ref[bi], g) for bi, g in chains]
    sp = [_dot(qst[ch], _twice(win_ref[bi, g * d:(g + 1) * d, :].astype(BF16))) for ch, (bi, g) in enumerate(chains)]
    nk = [_twice(nw_ref[g * d:(g + 1) * d, :].astype(BF16)) for g in range(N_KV_HEADS)]
    nvs = [_twice(nw_ref[KV_WIDTH + g * d:KV_WIDTH + (g + 1) * d, :].astype(BF16)) for g in range(N_KV_HEADS)]
    sn = [_dot(qst[ch], nk[g]) for ch, (bi, g) in enumerate(chains)]
    pp, pn, den = [], [], []
    for ch, (bi, g) in enumerate(chains):
        slope = _row_slopes(g, rowq >> _log2(dec_t))
        lgp = jnp.where(okp, sp[ch] - slope * dpast, NEG)
        lgn = jnp.where(okns[bi], sn[ch] - slope * dnews[bi], NEG)
        m = jnp.maximum(jnp.max(lgp, axis=-1, keepdims=True), jnp.max(lgn, axis=-1, keepdims=True))
        pp.append(jnp.exp(lgp - m))
        pn.append(jnp.exp(lgn - m))
        den.append(jnp.sum(pp[ch], axis=-1, keepdims=True) + jnp.sum(pn[ch], axis=-1, keepdims=True))
    o_win = []
    for ch, (bi, g) in enumerate(chains):
        vt_ = win_ref[bi, KV_WIDTH + g * d:KV_WIDTH + (g + 1) * d, :].astype(BF16)
        o2 = _dot_nt(pp[ch].astype(BF16), _twice(vt_)) + _dot_nt(pn[ch].astype(BF16), nvs[g])
        o_win.append(_unstack_group(o2 / den[ch], dec_t))
    for bi in range(nbat):
        ge = _expand_gates(gates_ref[bi], gexp_ref)
        oa_ref[bi] = (ge[0] * ocs_ref[bi, 0] + ge[1] * ocs_ref[bi, 1]
                      + ge[2] * jnp.concatenate(o_win[bi * N_KV_HEADS:(bi + 1) * N_KV_HEADS], axis=1))

        rolled = pltpu.roll(win_ref[bi], wbuf - dec_t, 1)
        newr = pltpu.roll(nw_ref[...], (LANES - dec_t - offs[bi]) & (LANES - 1), 1)
        wout_ref[bi, :, 0:wbuf - LANES] = rolled[:, 0:wbuf - LANES]
        wout_ref[bi, :, wbuf - LANES:wbuf] = jnp.where(lane >= LANES - dec_t, newr, rolled[:, wbuf - LANES:wbuf])


def _win_sample(win, q, nw, ocs, gates, gexp, past_len):
    nbatch, feat, wbuf = win.shape
    dec_t = q.shape[1]
    body = functools.partial(_win_sample_body, past_len=past_len, dec_t=dec_t)
    b3 = lambda i: (i, 0, 0)
    bps = WINDOW_BATCH_PER_STEP
    return pl.pallas_call(
        body, grid=(nbatch // bps,),
        in_specs=[pl.BlockSpec((bps, feat, wbuf), b3), pl.BlockSpec((bps, dec_t, ATTN_WIDTH), b3),
                  pl.BlockSpec((feat, LANES), lambda i: (1, i * bps * dec_t // LANES)),
                  pl.BlockSpec((bps, 2, dec_t, ATTN_WIDTH), lambda i: (i, 0, 0, 0)),
                  pl.BlockSpec((bps, dec_t, LANES), b3), pl.BlockSpec(gexp.shape, lambda i: (0, 0, 0))],
        out_specs=(pl.BlockSpec((bps, dec_t, ATTN_WIDTH), b3), pl.BlockSpec((bps, feat, wbuf), b3)),
        out_shape=(jax.ShapeDtypeStruct((nbatch, dec_t, ATTN_WIDTH), F32),
                   jax.ShapeDtypeStruct((nbatch, feat, wbuf), F32)),
        compiler_params=pltpu.CompilerParams(dimension_semantics=("parallel",),
                                             vmem_limit_bytes=_vmem_limit(24 << 20)),
        name="win_sample",
    )(win, q, nw, ocs, gates, gexp)


def _hgrn_sample_body(q_ref, f_ref, v_ref, s_ref, o_ref, so_ref, *, dec_t):
    nb = s_ref.shape[3]
    o_ref[...] = jnp.zeros(o_ref.shape, F32)

    sub = 8

    def per_tile(i, carry):
        r0 = pl.multiple_of(i * sub, sub)
        f_t = [f_ref[pl.ds(r0, sub), pl.ds(t * nb, nb)] for t in range(dec_t)]
        q_t = [q_ref[pl.ds(r0, sub), pl.ds(t * nb, nb)] for t in range(dec_t)]
        for j in range(sub):
            s = s_ref[0, r0 + j]
            for t in range(dec_t):
                cols = pl.ds(t * nb, nb)
                fr = f_t[t][j:j + 1, :]
                s = fr * s + (1.0 - fr) * v_ref[:, cols]
                o_ref[:, cols] = o_ref[:, cols] + s * q_t[t][j:j + 1, :]
            so_ref[0, r0 + j] = s
        return carry

    lax.fori_loop(0, s_ref.shape[1] // sub, per_tile, 0)


def _hgrn_sample(ztb, state, dec_t):
    nh, dk, dv, nb = state.shape
    n = ztb.shape[1]
    body = functools.partial(_hgrn_sample_body, dec_t=dec_t)
    q0, f0, v0 = 512 // dk, 1024 // dk, 1536 // dk
    return pl.pallas_call(
        body, grid=(nh,),
        in_specs=[pl.BlockSpec((dk, n), lambda h: (q0 + h, 0)), pl.BlockSpec((dk, n), lambda h: (f0 + h, 0)),
                  pl.BlockSpec((dv, n), lambda h: (v0 + h, 0)),
                  pl.BlockSpec((1, dk, dv, nb), lambda h: (h, 0, 0, 0))],
        out_specs=(pl.BlockSpec((dv, n), lambda h: (h, 0)), pl.BlockSpec((1, dk, dv, nb), lambda h: (h, 0, 0, 0))),
        out_shape=(jax.ShapeDtypeStruct((nh * dv, n), F32), jax.ShapeDtypeStruct(state.shape, F32)),
        compiler_params=pltpu.CompilerParams(dimension_semantics=("parallel",),
                                             vmem_limit_bytes=_vmem_limit(24 << 20)),
        name="hgrn_sample",
    )(ztb, ztb, ztb, state)


def _gate_expander():
    m = np.zeros((N_BRANCH, LANES, ATTN_WIDTH), np.float32)
    for br in range(N_BRANCH):
        for h in range(N_ATTN_HEADS):
            m[br, h * N_BRANCH + br, h * HEAD_DIM:(h + 1) * HEAD_DIM] = 1.0
    return jnp.asarray(m, BF16)


def _block_expander(block_ids, n_keys):
    key_blk = np.arange(n_keys) // SEL_BLOCK
    return jnp.asarray((np.asarray(block_ids)[:, None] == key_blk[None, :]).astype(np.float32), BF16)


def _key_features(n_keys):
    s = np.arange(n_keys)
    m = np.zeros((LANES, n_keys), np.float32)
    m[0:HEAD_DIM] = (np.arange(HEAD_DIM)[:, None] == (s // SEL_BLOCK)[None, :])
    m[HEAD_DIM] = s // SEL_BLOCK
    m[HEAD_DIM + 1] = s % SEL_BLOCK
    return jnp.asarray(m, BF16)


def _query_slope_features(rows):
    m = np.zeros((N_KV_HEADS, GROUP * rows, LANES), np.float32)
    for g in range(N_KV_HEADS):
        for r in range(GROUP):
            m[g, r * rows:(r + 1) * rows, HEAD_DIM] = SLOPES[g][r] * SEL_BLOCK
            m[g, r * rows:(r + 1) * rows, HEAD_DIM + 1] = SLOPES[g][r]
    return jnp.asarray(m, F32)


def _chunk_lower_tri(n, c):
    i = np.arange(n)
    return jnp.asarray(((i[:, None] // c == i[None, :] // c) & (i[None, :] <= i[:, None])).astype(np.float32), BF16)


def _head_indicator():
    m = np.zeros((RNN_WIDTH, LANES), np.float32)
    m[np.arange(RNN_WIDTH), np.arange(RNN_WIDTH) // RNN_DK] = 1.0
    return jnp.asarray(m, BF16)


def _head_block_diag(n):
    i = np.arange(n)
    return jnp.asarray((i[:, None] // RNN_DV == i[None, :] // RNN_DK).astype(np.float32), F32)


def kernel(x_prompt, x_sample, cache_kv, cache_win, state_rnn, page_table, ln_mix, w_in, q_norm, k_norm, cmp_pe,
           cmp_w1, cmp_w2, attn_out_norm, rnn_lb_logits, rnn_out_norm, w_out, ln_mlp, w_up, w_down):
    assert w_in.shape[0] == 1, "single layer"
    b, t, dm = x_prompt.shape
    nbatch, dec_t, _ = x_sample.shape
    n_pool = cache_kv.shape[1]
    n_pages = page_table.shape[1]
    past_len = n_pages * PAGE_SIZE
    wbuf = cache_win.shape[2]
    assert t % PROJ_ROWS == 0 and t % RNN_ROWS == 0 and t % min(SEL_KEYS, t) == 0 and t >= WINDOW
    assert (b * t) % FINISH_ROWS == 0 and (nbatch * dec_t) % FINISH_ROWS == 0 and w_up.shape[2] % FF_CHUNK == 0
    assert (nbatch * dec_t) % PROJ_ROWS == 0 and nbatch == LANES and dec_t <= 8
    assert past_len % SEL_BLOCK == 0 and wbuf == WINDOW and wbuf >= LANES
    assert LANES % dec_t == 0 and (LANES // dec_t) % max(SAMPLE_BATCH_PER_STEP, WINDOW_BATCH_PER_STEP) == 0

    w = w_in[0]
    c_kv, c_gate, c_rq, c_rf, c_ri, c_rg = ATTN_WIDTH, ATTN_WIDTH + 6 * KV_WIDTH, 1304, 1816, 2328, 2840
    gate_cols = jnp.pad(w[:, c_gate:c_rq], ((0, 0), (0, LANES - N_ATTN_HEADS * N_BRANCH)))
    wtok = jnp.concatenate([w[:, 0:ATTN_WIDTH], w[:, c_rq:], gate_cols, w[:, c_kv:c_kv + 2 * KV_WIDTH]],
                           axis=1).T.astype(BF16)
    wft = w[:, c_kv:c_gate].T.astype(BF16)
    wtok_s = jnp.concatenate([w[:, 0:ATTN_WIDTH], w[:, c_rg:], gate_cols], axis=1).T.astype(BF16)
    wtb_s = jnp.concatenate([w[:, c_kv:c_kv + 4 * KV_WIDTH], w[:, c_rq:c_rg]], axis=1).T.astype(BF16)
    wbt_s = w[:, c_kv + 2 * KV_WIDTH:c_gate].T.astype(BF16)
    ln = ln_mix[0][None, :]
    qg = (jnp.tile(q_norm[0], N_ATTN_HEADS) * SCALE)[None, :]
    lbl = rnn_lb_logits.astype(F32)
    lblt = jnp.broadcast_to(lbl[:, :, None], lbl.shape + (PROJ_ROWS,))
    gsel = jnp.broadcast_to(k_norm[0, 1][:, None], (HEAD_DIM, PROJ_ROWS))
    gwin = jnp.broadcast_to(k_norm[0, 2][:, None], (HEAD_DIM, PROJ_ROWS))
    kg_dup = jnp.tile(k_norm[0, 0], 2)[None, :]
    pe = cmp_pe[0]
    pe_tok = jnp.tile(jnp.concatenate([jnp.tile(pe[0], (1, N_KV_HEADS)), jnp.tile(pe[1], (1, N_KV_HEADS))], axis=1),
                      (PROJ_ROWS // CMP_BLOCK, 1))
    w1 = cmp_w1[0].reshape(2, CMP_BLOCK, HEAD_DIM, CMP_HIDDEN)
    zeros = jnp.zeros_like(w1)
    w1bd = jnp.concatenate([jnp.concatenate([w1, zeros], axis=3), jnp.concatenate([zeros, w1], axis=3)],
                           axis=2).astype(BF16)
    w1r = jnp.tile(w1.transpose(0, 2, 1, 3), (1, 1, 2, 1)).astype(BF16)
    w1r = w1r.reshape(2, HEAD_DIM * LANES, CMP_HIDDEN)
    pe_t = jnp.tile(pe.transpose(0, 2, 1), (1, 1, 2))[:, :, None, :]
    w2dup = jnp.tile(cmp_w2[0], (1, 1, 2)).astype(BF16)
    ag = attn_out_norm[0][None, :]
    rg = rnn_out_norm[0][None, :]
    wout = w_out[0].astype(BF16)
    lnm = ln_mlp[0][None, :]
    wup = w_up[0].astype(BF16)
    wdn = w_down[0].astype(BF16)
    gexp = _gate_expander()

    (q_p, gates_p, rq_p, f_p, rv_p, rgs_p, kvc_p, kvt_p, wint_p, att_p) = _proj_prompt(
        x_prompt, ln, wtok, wft, qg, lbl, pe_tok, gsel, gwin)
    cmp_p = _compress_prompt(kvc_p, w1bd, w2dup, kg_dup)
    nb_p = t // CMP_BLOCK
    oa_p = _nsa_prompt(q_p, gates_p, cmp_p, att_p, _key_features(t), _query_slope_features(NSA_Q_ROWS), gexp)
    orn_p, st_p = _hgrn_prompt(rq_p, f_p, rv_p, _chunk_lower_tri(CUMSUM_ROWS, RNN_CHUNK),
                               _head_block_diag(LANES), _head_indicator())
    y_p = _finish(x_prompt.reshape(b * t, dm), oa_p.reshape(b * t, ATTN_WIDTH), orn_p.reshape(b * t, RNN_WIDTH),
                  rgs_p.reshape(b * t, RNN_WIDTH), ag, rg, wout, lnm, wup, wdn).reshape(b, t, dm)
    kv_prompt = kvt_p.reshape(1, b, 4, N_KV_HEADS, HEAD_DIM, t).transpose(0, 1, 5, 2, 3, 4)
    wlen = min(WINDOW, t)
    win_prompt = wint_p[:, :, t - wlen:].reshape(1, b, 2, N_KV_HEADS, HEAD_DIM, wlen).transpose(0, 1, 5, 2, 3, 4)
    hh = LANES // RNN_DV
    st5 = st_p.reshape(b, RNN_WIDTH // LANES, hh, RNN_DV, hh, RNN_DK)
    rnn_prompt = jnp.stack([st5[:, :, i, :, i, :] for i in range(hh)], axis=2)
    rnn_prompt = rnn_prompt.reshape(b, N_RNN_HEADS, RNN_DV, RNN_DK).transpose(0, 1, 3, 2)[None]

    n_s = nbatch * dec_t
    xbt = x_sample.reshape(n_s, dm)
    xtb = x_sample.transpose(1, 0, 2).reshape(n_s, dm)
    q_s, rgs_s, gates_s, ztb, zbt = _proj_sample(xbt, xtb, ln, wtok_s, wtb_s, wbt_s, qg, lblt, gsel, gwin)
    kv_sample = ztb[0:4 * KV_WIDTH].reshape(4, N_KV_HEADS, HEAD_DIM, dec_t, nbatch).transpose(4, 3, 0, 1, 2)[None]
    cache = cache_kv[0].transpose(0, 2, 3, 4, 1).reshape(n_pool, PAGE_ROWS, PAGE_SIZE)
    nblk_s = past_len // CMP_BLOCK
    rho = np.arange(nblk_s)
    e_perm = _block_expander(2 * (rho % n_pages) + rho // n_pages, past_len)
    ns_s = -(-(past_len + dec_t) // SEL_BLOCK)
    q_s3 = q_s.reshape(nbatch, dec_t, ATTN_WIDTH)
    ocs = _nsa_sample(page_table.reshape(-1), cache, q_s3, zbt, w1r, pe_t, w2dup, kg_dup,
                      e_perm, past_len, min(SEL_TOPK, ns_s))
    win = cache_win[0].transpose(0, 2, 3, 4, 1).reshape(nbatch, 2 * KV_WIDTH, wbuf)
    oa_s, win_new = _win_sample(win, q_s3, zbt, ocs, gates_s.reshape(nbatch, dec_t, LANES), gexp, past_len)
    win_sample = win_new.reshape(1, nbatch, 2, N_KV_HEADS, HEAD_DIM, wbuf).transpose(0, 1, 5, 2, 3, 4)
    state = state_rnn[0].transpose(1, 2, 3, 0)
    orn_t, state_new = _hgrn_sample(ztb, state, dec_t)
    rnn_sample = state_new.transpose(3, 0, 1, 2)[None]
    orn_s = orn_t.reshape(RNN_WIDTH, dec_t, nbatch).transpose(2, 1, 0).reshape(n_s, RNN_WIDTH)
    y_s = _finish(xbt, oa_s.reshape(n_s, ATTN_WIDTH), orn_s, rgs_s, ag, rg, wout, lnm, wup, wdn).reshape(nbatch, dec_t, dm)

    return (y_p, y_s, kv_prompt, kv_sample, win_prompt, win_sample, rnn_prompt, rnn_sample)
```
